```python
import math
import jax
import jax.numpy as jnp
from jax import lax
import numpy as np

D_MODEL = 1024
BATCH = 4
SEQ = 4096
DEPTH = 4

SSD_HEADS = 16
SSD_HEAD_DIM = 64
SSD_WIDTH = SSD_HEADS * SSD_HEAD_DIM
SSD_GROUPS = 2
SSD_STATE = 128
SSD_BC = SSD_GROUPS * SSD_STATE
SSD_XBC = SSD_WIDTH + 2 * SSD_BC
SSD_CONV = 4
SSD_CHUNK = 128
GDN_HEADS = 4
GDN_HEAD_DIM = 128
GDN_WIDTH = GDN_HEADS * GDN_HEAD_DIM
GDN_CONV = 4
GDN_CHUNK = 64
HG_HEADS = 4
HG_HEAD_DIM = 128
HG_WIDTH = HG_HEADS * HG_HEAD_DIM
HG_CHUNK = 16
D_MIX = SSD_WIDTH + GDN_WIDTH + HG_WIDTH
IN_COLS = SSD_WIDTH + SSD_XBC + SSD_HEADS + 4 * GDN_WIDTH + 2 * GDN_HEADS + 4 * HG_WIDTH
MOE_GROUPS = 4
EXPERTS_PER_GROUP = 4
N_EXPERTS = MOE_GROUPS * EXPERTS_PER_GROUP
MOE_TOP_K = 2
D_EXPERT = 256
DN_ALPHA = (2 * DEPTH) ** 0.25
DN_BETA = (8 * DEPTH) ** -0.25

kernel_name = 'hymba_style_ssd_gdn_hgrn2_hmoe_deepnorm'


def _in_proj_splits():
    sizes = (SSD_WIDTH, SSD_XBC, SSD_HEADS, 3 * GDN_WIDTH, GDN_WIDTH, GDN_HEADS, GDN_HEADS,
             HG_WIDTH, HG_WIDTH, HG_WIDTH, HG_WIDTH)
    pts, acc = [], 0
    for s in sizes[:-1]:
        acc += s
        pts.append(acc)
    return pts


def _layer_norm(x, g, b, eps=1e-5):
    xf = x.astype(jnp.float32)
    mu = jnp.mean(xf, axis=-1, keepdims=True)
    var = jnp.mean(jnp.square(xf - mu), axis=-1, keepdims=True)
    return ((xf - mu) * lax.rsqrt(var + eps) * g.astype(jnp.float32) + b.astype(jnp.float32)).astype(x.dtype)


def _rms_normalize(x, eps=1e-6):
    xf = x.astype(jnp.float32)
    return xf * lax.rsqrt(jnp.mean(xf * xf, axis=-1, keepdims=True) + eps)


def _l2_normalize(x, eps=1e-6):
    return x * lax.rsqrt(jnp.sum(x * x, axis=-1, keepdims=True) + eps)


def _causal_conv(x, w):
    k = w.shape[0]
    seq = x.shape[1]
    xp = jnp.pad(x, ((0, 0), (k - 1, 0), (0, 0)))
    out = xp[:, 0:seq, :] * w[0]
    for i in range(1, k):
        out = out + xp[:, i:i + seq, :] * w[i]
    return out


def ssd_mixer(z, xbc, dt_raw, conv_w, conv_b, dt_bias, a_log, d_skip, norm_w):
    f32 = jnp.float32
    bsz, seq, _ = z.shape
    nc, cl = seq // SSD_CHUNK, SSD_CHUNK
    g, hg, p, n = SSD_GROUPS, SSD_HEADS // SSD_GROUPS, SSD_HEAD_DIM, SSD_STATE
    xbc = jax.nn.silu(_causal_conv(xbc, conv_w) + conv_b).astype(f32)
    xs, bm, cm = jnp.split(xbc, [SSD_WIDTH, SSD_WIDTH + SSD_BC], axis=-1)
    xs = xs.reshape(bsz, nc, cl, g, hg, p)
    bm = bm.reshape(bsz, nc, cl, g, n)
    cm = cm.reshape(bsz, nc, cl, g, n)
    dt = jax.nn.softplus(dt_raw.astype(f32) + dt_bias.astype(f32)).reshape(bsz, nc, cl, g, hg)
    a = -jnp.exp(a_log.astype(f32)).reshape(g, hg)
    acum = jnp.cumsum(dt * a, axis=2)
    xdt = xs * dt[..., None]
    causal = jnp.tril(jnp.ones((cl, cl), dtype=bool))[:, :, None, None]
    seg = acum[:, :, :, None] - acum[:, :, None, :]
    decay = jnp.exp(jnp.where(causal, seg, -jnp.inf))
    cb = jnp.einsum('bclgn,bcsgn->bclsg', cm, bm)
    y_diag = jnp.einsum('bclsgh,bcsghp->bclghp', cb[..., None] * decay, xdt)
    to_end = jnp.exp(acum[:, :, -1:] - acum)
    chunk_states = jnp.einsum('bclgn,bclghp->bcghpn', bm, xdt * to_end[..., None])
    chunk_decay = jnp.exp(acum[:, :, -1])

    def step(state, inp):
        cs, cd = inp
        return state * cd[..., None, None] + cs, state

    init = jnp.zeros((bsz, g, hg, p, n), f32)
    _, prev = lax.scan(step, init, (jnp.moveaxis(chunk_states, 1, 0), jnp.moveaxis(chunk_decay, 1, 0)))
    prev = jnp.moveaxis(prev, 0, 1)
    y_off = jnp.einsum('bclgn,bcghpn->bclghp', cm, prev) * jnp.exp(acum)[..., None]
    y = y_diag + y_off + xs * d_skip.astype(f32).reshape(g, hg, 1)
    y = y.reshape(bsz, seq, SSD_WIDTH) * jax.nn.silu(z.astype(f32))
    y = _rms_normalize(y.reshape(bsz, seq, g, SSD_WIDTH // g)).reshape(bsz, seq, SSD_WIDTH)
    return (y * norm_w.astype(f32)).astype(z.dtype)


def gdn_mixer(qkv, gate, b_raw, a_raw, conv_w, dt_bias, a_log, norm_w):
    f32 = jnp.float32
    bsz, seq, _ = qkv.shape
    nh, dh, cl = GDN_HEADS, GDN_HEAD_DIM, GDN_CHUNK
    nc = seq // cl
    qkv = jax.nn.silu(_causal_conv(qkv, conv_w)).astype(f32)
    q, k, v = jnp.split(qkv, 3, axis=-1)

    def heads(t):
        return t.reshape(bsz, nc, cl, nh, dh).transpose(0, 3, 1, 2, 4)

    q = _l2_normalize(heads(q)) * (dh ** -0.5)
    k = _l2_normalize(heads(k))
    v = heads(v)
    beta = jax.nn.sigmoid(b_raw.astype(f32)).reshape(bsz, nc, cl, nh).transpose(0, 3, 1, 2)
    log_a = -jnp.exp(a_log.astype(f32)) * jax.nn.softplus(a_raw.astype(f32) + dt_bias.astype(f32))
    gcum = jnp.cumsum(log_a.reshape(bsz, nc, cl, nh).transpose(0, 3, 1, 2), axis=-1)
    incl = jnp.tril(jnp.ones((cl, cl), dtype=bool))
    strict = jnp.tril(jnp.ones((cl, cl), dtype=bool), -1)
    decay = jnp.exp(jnp.where(incl, gcum[..., :, None] - gcum[..., None, :], -jnp.inf))
    kb = k * beta[..., None]
    m = jnp.where(strict, jnp.einsum('bhnid,bhnjd->bhnij', kb, k) * decay, 0.0)
    a_mat = m + jnp.eye(cl, dtype=f32)
    rhs = jnp.concatenate([v * beta[..., None], kb * jnp.exp(gcum)[..., None]], axis=-1)
    sol = lax.linalg.triangular_solve(a_mat, rhs, left_side=True, lower=True, unit_diagonal=True)
    u, w = jnp.split(sol, 2, axis=-1)
    qk = jnp.einsum('bhnid,bhnjd->bhnij', q, k) * decay
    qg = q * jnp.exp(gcum)[..., None]
    kd = k * jnp.exp(gcum[..., -1:] - gcum)[..., None]
    gl = jnp.exp(gcum[..., -1])

    def step(state, inp):
        u_c, w_c, qk_c, qg_c, kd_c, gl_c = inp
        v_new = u_c - jnp.einsum('bhcd,bhde->bhce', w_c, state)
        o = jnp.einsum('bhcd,bhde->bhce', qg_c, state) + jnp.einsum('bhij,bhje->bhie', qk_c, v_new)
        state = state * gl_c[..., None, None] + jnp.einsum('bhcd,bhce->bhde', kd_c, v_new)
        return state, o

    init = jnp.zeros((bsz, nh, dh, dh), f32)
    _, o = lax.scan(step, init, tuple(jnp.moveaxis(t, 2, 0) for t in (u, w, qk, qg, kd, gl)))
    o = jnp.moveaxis(o, 0, 2).transpose(0, 2, 3, 1, 4).reshape(bsz, seq, nh, dh)
    o = _rms_normalize(o) * norm_w.astype(f32) * jax.nn.silu(gate.astype(f32).reshape(bsz, seq, nh, dh))
    return o.reshape(bsz, seq, GDN_WIDTH).astype(qkv.dtype)


def hgrn2_mixer(q_raw, f_raw, i_raw, gate, lb, norm_w):
    f32 = jnp.float32
    bsz, seq, _ = q_raw.shape
    nh, dk, cl = HG_HEADS, HG_HEAD_DIM, HG_CHUNK
    nc = seq // cl
    f_raw = f_raw.astype(f32)
    lb = lb.astype(f32)
    log_f = jnp.logaddexp(jnp.log(lb), jnp.log1p(-lb) + jax.nn.log_sigmoid(f_raw))
    key_in = (1.0 - lb) * jax.nn.sigmoid(-f_raw)

    def heads(t):
        return t.astype(f32).reshape(bsz, nc, cl, nh, dk).transpose(0, 3, 1, 2, 4)

    q = heads(jax.nn.silu(q_raw.astype(f32)))
    k = heads(key_in)
    v = heads(i_raw)
    bcum = jnp.cumsum(heads(log_f), axis=3)
    causal = jnp.tril(jnp.ones((cl, cl), dtype=bool))[:, :, None]
    seg = bcum[:, :, :, :, None, :] - bcum[:, :, :, None, :, :]
    decay = jnp.exp(jnp.where(causal, seg, -jnp.inf))
    scores = jnp.einsum('bhntsd,bhnsd->bhnts', q[:, :, :, :, None, :] * decay, k)
    o_intra = jnp.einsum('bhnts,bhnse->bhnte', scores, v)
    qg = q * jnp.exp(bcum)
    kd = k * jnp.exp(bcum[:, :, :, -1:] - bcum)
    gl = jnp.exp(bcum[:, :, :, -1])

    def step(state, inp):
        qg_c, kd_c, v_c, gl_c = inp
        o = jnp.einsum('bhtd,bhde->bhte', qg_c, state)
        state = state * gl_c[..., None] + jnp.einsum('bhsd,bhse->bhde', kd_c, v_c)
        return state, o

    init = jnp.zeros((bsz, nh, dk, dk), f32)
    _, o_inter = lax.scan(step, init, tuple(jnp.moveaxis(t, 2, 0) for t in (qg, kd, v, gl)))
    o = o_intra + jnp.moveaxis(o_inter, 0, 2)
    o = o.transpose(0, 2, 3, 1, 4).reshape(bsz, seq, nh, dk)
    o = _rms_normalize(o) * norm_w.astype(f32) * jax.nn.silu(gate.astype(f32).reshape(bsz, seq, nh, dk))
    return o.reshape(bsz, seq, HG_WIDTH).astype(q_raw.dtype)


def hier_moe(x, w_rg, b_rg, w_re, b_re, w_gu, w_dn):
    f32 = jnp.float32
    bsz, seq, d = x.shape
    t = x.reshape(-1, d)
    pg = jax.nn.softmax((t @ w_rg).astype(f32) + b_rg.astype(f32), axis=-1)
    g_p, g_idx = lax.top_k(pg, 1)
    le = ((t @ w_re).astype(f32) + b_re.astype(f32)).reshape(-1, MOE_GROUPS, EXPERTS_PER_GROUP)
    le = jnp.take_along_axis(le, g_idx[:, :, None], axis=1)[:, 0]
    pe = jax.nn.softmax(le, axis=-1)
    e_p, e_idx = lax.top_k(pe, MOE_TOP_K)
    wts = g_p * e_p / jnp.sum(e_p, axis=-1, keepdims=True)
    eid = g_idx * EXPERTS_PER_GROUP + e_idx
    combine = jnp.sum(jax.nn.one_hot(eid, N_EXPERTS, dtype=f32) * wts[..., None], axis=1)
    gu = jnp.einsum('td,edf->tef', t, w_gu)
    g_half, u_half = jnp.split(gu, 2, axis=-1)
    h = jax.nn.silu(g_half) * u_half * combine[..., None].astype(t.dtype)
    y = jnp.einsum('tef,efd->td', h, w_dn)
    return y.reshape(bsz, seq, d)


def setup_inputs(seed: int = 0) -> dict:
    key = jax.random.key(seed)
    ks = jax.random.split(key, 32)
    f32 = jnp.float32

    def nrm(k, shape, s):
        return jax.random.normal(k, shape, f32) * s

    def dt_bias_init(k, n_heads):
        u = jax.random.uniform(k, (DEPTH, n_heads), f32)
        dt = jnp.exp(u * (math.log(0.1) - math.log(0.001)) + math.log(0.001))
        dt = jnp.maximum(dt, 1e-4)
        return dt + jnp.log(-jnp.expm1(-dt))

    return {
        'x': nrm(ks[0], (BATCH, SEQ, D_MODEL), 1.0),
        'w_in': nrm(ks[1], (DEPTH, D_MODEL, IN_COLS), D_MODEL ** -0.5),
        'ssd_conv_w': nrm(ks[2], (DEPTH, SSD_CONV, SSD_XBC), SSD_CONV ** -0.5),
        'ssd_conv_b': nrm(ks[3], (DEPTH, SSD_XBC), 0.02),
        'ssd_dt_bias': dt_bias_init(ks[4], SSD_HEADS),
        'ssd_a_log': jnp.log(jax.random.uniform(ks[5], (DEPTH, SSD_HEADS), f32, 1.0, 16.0)),
        'ssd_d': 1.0 + nrm(ks[6], (DEPTH, SSD_HEADS), 0.02),
        'ssd_norm_w': 1.0 + nrm(ks[7], (DEPTH, SSD_WIDTH), 0.02),
        'gdn_conv_w': nrm(ks[8], (DEPTH, GDN_CONV, 3 * GDN_WIDTH), GDN_CONV ** -0.5),
        'gdn_dt_bias': dt_bias_init(ks[9], GDN_HEADS),
        'gdn_a_log': jnp.log(jax.random.uniform(ks[10], (DEPTH, GDN_HEADS), f32, 1.0, 16.0)),
        'gdn_norm_w': 1.0 + nrm(ks[11], (DEPTH, GDN_HEAD_DIM), 0.02),
        'hg_lb_logits': nrm(ks[12], (DEPTH, HG_WIDTH), 0.1),
        'hg_norm_w': 1.0 + nrm(ks[13], (DEPTH, HG_HEAD_DIM), 0.02),
        'w_out': nrm(ks[14], (DEPTH, D_MIX, D_MODEL), DN_BETA * D_MIX ** -0.5),
        'ln1_g': 1.0 + nrm(ks[15], (DEPTH, D_MODEL), 0.02),
        'ln1_b': nrm(ks[16], (DEPTH, D_MODEL), 0.02),
        'w_router_group': nrm(ks[17], (DEPTH, D_MODEL, MOE_GROUPS), D_MODEL ** -0.5),
        'b_router_group': nrm(ks[18], (DEPTH, MOE_GROUPS), 0.01),
        'w_router_expert': nrm(ks[19], (DEPTH, D_MODEL, N_EXPERTS), D_MODEL ** -0.5),
        'b_router_expert': nrm(ks[20], (DEPTH, N_EXPERTS), 0.01),
        'w_expert_gate_up': nrm(ks[21], (DEPTH, N_EXPERTS, D_MODEL, 2 * D_EXPERT), D_MODEL ** -0.5),
        'w_expert_down': nrm(ks[22], (DEPTH, N_EXPERTS, D_EXPERT, D_MODEL), DN_BETA * D_EXPERT ** -0.5),
        'ln2_g': 1.0 + nrm(ks[23], (DEPTH, D_MODEL), 0.02),
        'ln2_b': nrm(ks[24], (DEPTH, D_MODEL), 0.02),
    }


def reference(x, w_in, ssd_conv_w, ssd_conv_b, ssd_dt_bias, ssd_a_log, ssd_d, ssd_norm_w,
              gdn_conv_w, gdn_dt_bias, gdn_a_log, gdn_norm_w, hg_lb_logits, hg_norm_w, w_out,
              ln1_g, ln1_b, w_router_group, b_router_group, w_router_expert, b_router_expert,
              w_expert_gate_up, w_expert_down, ln2_g, ln2_b):
    splits = _in_proj_splits()
    lb_cum = jnp.cumsum(jax.nn.softmax(hg_lb_logits.astype(jnp.float32), axis=0), axis=0)
    lb_all = lb_cum - lb_cum[0:1]
    h = x
    for l in range(DEPTH):
        proj = jnp.einsum('bld,dc->blc', h, w_in[l])
        (z, xbc, dt_raw, qkv, gdn_gate, gdn_b, gdn_a,
         hg_q, hg_f, hg_i, hg_gate) = jnp.split(proj, splits, axis=-1)
        y_ssd = ssd_mixer(z, xbc, dt_raw, ssd_conv_w[l], ssd_conv_b[l], ssd_dt_bias[l],
                          ssd_a_log[l], ssd_d[l], ssd_norm_w[l])
        y_gdn = gdn_mixer(qkv, gdn_gate, gdn_b, gdn_a, gdn_conv_w[l], gdn_dt_bias[l],
                          gdn_a_log[l], gdn_norm_w[l])
        y_hg = hgrn2_mixer(hg_q, hg_f, hg_i, hg_gate, lb_all[l], hg_norm_w[l])
        mix = jnp.einsum('blc,cd->bld', jnp.concatenate([y_ssd, y_gdn, y_hg], axis=-1), w_out[l])
        h = _layer_norm(DN_ALPHA * h + mix, ln1_g[l], ln1_b[l])
        ffn = hier_moe(h, w_router_group[l], b_router_group[l], w_router_expert[l],
                       b_router_expert[l], w_expert_gate_up[l], w_expert_down[l])
        h = _layer_norm(DN_ALPHA * h + ffn, ln2_g[l], ln2_b[l])
    return h
```

```python
import functools

import jax
import jax.numpy as jnp
from jax import lax
from jax.experimental import pallas as pl
from jax.experimental.pallas import tpu as pltpu

F32 = jnp.float32
BF16 = jnp.bfloat16
HIGHEST = lax.Precision.HIGHEST

D_MODEL = 1024
DEPTH = 4
SSD_HEADS = 16
SSD_HEAD_DIM = 64
SSD_WIDTH = 1024
SSD_GROUPS = 2
SSD_STATE = 128
SSD_BC = 256
SSD_XBC = 1536
SSD_CONV = 4
GDN_HEADS = 4
GDN_HEAD_DIM = 128
GDN_WIDTH = 512
GDN_CONV = 4
GDN_CHUNK = 64
HG_HEADS = 4
HG_HEAD_DIM = 128
HG_WIDTH = 512
D_MIX = 2048
MOE_GROUPS = 4
EXPERTS_PER_GROUP = 4
N_EXPERTS = 16
D_EXPERT = 256
DN_ALPHA = (2 * DEPTH) ** 0.25

LANES = 128
SUBLANES = 8
ROWS = 128
VMEM_LIMIT = 48 * 1024 * 1024

OFF_XBC = 0
OFF_QKV = 1536
OFF_Z = 3072
OFF_GGATE = 4096
OFF_HQ = 4608
OFF_HF = 5120
OFF_HI = 5632
OFF_HGATE = 6144
MAIN_COLS = 6656
SM_DT = 0
SM_B = 16
SM_A = 20


def _dot(a, b):
    return jnp.dot(a.astype(BF16), b.astype(BF16), preferred_element_type=F32)


def _dot_nt(a, b):
    return lax.dot_general(a.astype(BF16), b.astype(BF16), (((1,), (1,)), ((), ())),
                           preferred_element_type=F32)


def _dot_tn(a, b):
    return lax.dot_general(a.astype(BF16), b.astype(BF16), (((0,), (0,)), ((), ())),
                           preferred_element_type=F32)


def _dot_hi(a, b):
    return jnp.dot(a, b, precision=HIGHEST, preferred_element_type=F32)


def _dot2(a, b01):
    hi = a.astype(BF16)
    lo = (a - hi.astype(F32)).astype(BF16)
    b = b01.astype(BF16)
    return (jnp.dot(hi, b, preferred_element_type=F32) + jnp.dot(lo, b, preferred_element_type=F32))


def _sigmoid(x):
    return 1.0 / (1.0 + jnp.exp(-x))


def _silu(x):
    return x * _sigmoid(x)


def _softplus(x):
    return jnp.maximum(x, 0.0) + jnp.log1p(jnp.exp(-jnp.abs(x)))


def _iota2(shape, dim):
    return lax.broadcasted_iota(jnp.int32, shape, dim)


def _params(sem):
    return pltpu.CompilerParams(dimension_semantics=sem, vmem_limit_bytes=VMEM_LIMIT)


def _matmul_kernel(x_ref, w_ref, o_ref):
    o_ref[...] = jnp.dot(x_ref[...], w_ref[...], preferred_element_type=F32)


def _matmul_hi_kernel(x_ref, w_ref, o_ref):
    o_ref[...] = _dot_hi(x_ref[...], w_ref[...])


def _matmul(x, w, tm, tn, hi=False):
    t, k = x.shape
    n = w.shape[1]
    return pl.pallas_call(
        _matmul_hi_kernel if hi else _matmul_kernel,
        out_shape=jax.ShapeDtypeStruct((t, n), F32),
        grid=(t // tm, n // tn),
        in_specs=[pl.BlockSpec((tm, k), lambda i, j: (i, 0)),
                  pl.BlockSpec((k, tn), lambda i, j: (0, j))],
        out_specs=pl.BlockSpec((tm, tn), lambda i, j: (i, j)),
        compiler_params=_params(("parallel", "arbitrary")),
        name="in_proj_hi" if hi else "in_proj",
    )(x, w)


def _causal_conv(x, xpad_ref, cw, first):
    rows = x.shape[0]

    @pl.when(first)
    def _():
        xpad_ref[0:SUBLANES, :] = jnp.zeros((SUBLANES, x.shape[1]), F32)

    xpad_ref[SUBLANES:SUBLANES + rows, :] = x
    k = cw.shape[0]
    acc = x * cw[k - 1:k, :]
    for i in range(k - 1):
        off = SUBLANES - (k - 1) + i
        acc = acc + xpad_ref[off:off + rows, :] * cw[i:i + 1, :]
    xpad_ref[0:SUBLANES, :] = xpad_ref[rows:rows + SUBLANES, :]
    return acc


def _ssd_kernel(z_ref, xbc_ref, sm_ref, cw_ref, cb_ref, hp_ref, hpt_ref, dsk_ref, nw_ref,
                y_ref, xpad_ref, st_ref):
    c = pl.program_id(1)

    @pl.when(c == 0)
    def _():
        st_ref[...] = jnp.zeros(st_ref.shape, F32)

    xa = _causal_conv(xbc_ref[...], xpad_ref, cw_ref[...], c == 0) + cb_ref[...]
    xa = _silu(xa)
    xs = xa[:, 0:SSD_WIDTH]

    row = _iota2((ROWS, ROWS), 0)
    col = _iota2((ROWS, ROWS), 1)
    causal = row >= col
    tril = causal.astype(F32)
    triu = (row <= col).astype(F32)

    sm = sm_ref[...]
    dt = _softplus(sm + hp_ref[0:1, :])
    da = dt * (-jnp.exp(hp_ref[1:2, :]))
    acum = _dot_hi(tril, da)
    smt = sm.T
    dtt = _softplus(smt[0:SSD_HEADS, :] + hpt_ref[0:SSD_HEADS, 0:1])
    dat = dtt * (-jnp.exp(hpt_ref[0:SSD_HEADS, 1:2]))
    acumt = _dot_hi(dat, triu)

    hrow = _iota2((LANES, SSD_WIDTH), 0)
    hcol = _iota2((LANES, SSD_WIDTH), 1)
    expand = ((hcol >> 6) == hrow).astype(F32)
    dtx = _dot2(dt, expand)
    eax = _dot2(jnp.exp(acum), expand)
    tex = _dot2(jnp.exp(acum[ROWS - 1:ROWS, :] - acum), expand)
    xdt = xs * dtx
    xdt_b = xdt.astype(BF16)
    xend_b = (xdt * tex).astype(BF16)
    lane_lo = _iota2((ROWS, LANES), 1) < SSD_HEAD_DIM

    hg = SSD_HEADS // SSD_GROUPS
    gw = SSD_WIDTH // SSD_GROUPS
    y_parts = []
    for g in range(SSD_GROUPS):
        bm = xa[:, SSD_WIDTH + g * SSD_STATE:SSD_WIDTH + (g + 1) * SSD_STATE]
        cm = xa[:, SSD_WIDTH + SSD_BC + g * SSD_STATE:SSD_WIDTH + SSD_BC + (g + 1) * SSD_STATE]
        cm_b = cm.astype(BF16)
        cb = _dot_nt(cm_b, bm)
        yd = []
        for pair in range(hg // 2):
            res = []
            for sub in range(2):
                h = g * hg + pair * 2 + sub
                seg = acum[:, h:h + 1] - acumt[h:h + 1, :]
                lmat = cb * jnp.exp(jnp.where(causal, seg, -jnp.inf))
                c0 = (g * hg + pair * 2) * SSD_HEAD_DIM
                res.append(jnp.dot(lmat.astype(BF16), xdt_b[:, c0:c0 + LANES],
                                   preferred_element_type=F32))
            yd.append(jnp.where(lane_lo, res[0], res[1]))
        yd = jnp.concatenate(yd, axis=1)
        st = st_ref[g]
        yoff = jnp.dot(cm_b, st.astype(BF16), preferred_element_type=F32) * eax[:, g * gw:(g + 1) * gw]
        st_ref[g] = (st * eax[ROWS - 1:ROWS, g * gw:(g + 1) * gw]
                     + jnp.dot(bm.T.astype(BF16), xend_b[:, g * gw:(g + 1) * gw],
                               preferred_element_type=F32))
        y = yd + yoff + xs[:, g * gw:(g + 1) * gw] * dsk_ref[:, g * gw:(g + 1) * gw]
        y = y * _silu(z_ref[:, g * gw:(g + 1) * gw])
        ms = jnp.mean(y * y, axis=-1, keepdims=True)
        y_parts.append(y * lax.rsqrt(ms + 1e-6) * nw_ref[:, g * gw:(g + 1) * gw])
    y_ref[...] = jnp.concatenate(y_parts, axis=1).astype(y_ref.dtype)


def _ssd_mixer(proj, small, nb, seq, conv_w, conv_b, dt_bias, a_log, d_skip, norm_w):
    nblk = seq // ROWS
    hp = jnp.zeros((SUBLANES, LANES), F32)
    hp = hp.at[0, SM_DT:SM_DT + SSD_HEADS].set(dt_bias).at[1, SM_DT:SM_DT + SSD_HEADS].set(a_log)
    hpt = hp.T
    dsk = jnp.repeat(d_skip, SSD_HEAD_DIM)[None, :]
    rowmap = lambda b, c: (b * nblk + c, 0)
    const = lambda b, c: (0, 0)
    return pl.pallas_call(
        _ssd_kernel,
        out_shape=jax.ShapeDtypeStruct((nb * seq, SSD_WIDTH), BF16),
        grid=(nb, nblk),
        in_specs=[
            pl.BlockSpec((ROWS, SSD_WIDTH), lambda b, c: (b * nblk + c, OFF_Z // SSD_WIDTH)),
            pl.BlockSpec((ROWS, SSD_XBC), lambda b, c: (b * nblk + c, OFF_XBC // SSD_XBC)),
            pl.BlockSpec((ROWS, LANES), rowmap),
            pl.BlockSpec((SSD_CONV, SSD_XBC), const),
            pl.BlockSpec((1, SSD_XBC), const),
            pl.BlockSpec((SUBLANES, LANES), const),
            pl.BlockSpec((LANES, SUBLANES), const),
            pl.BlockSpec((1, SSD_WIDTH), const),
            pl.BlockSpec((1, SSD_WIDTH), const),
        ],
        out_specs=pl.BlockSpec((ROWS, SSD_WIDTH), rowmap),
        scratch_shapes=[pltpu.VMEM((ROWS + SUBLANES, SSD_XBC), F32),
                        pltpu.VMEM((SSD_GROUPS, SSD_STATE, SSD_WIDTH // SSD_GROUPS), F32)],
        compiler_params=_params(("parallel", "arbitrary")),
        name="ssd_mixer",
    )(proj, proj, small, conv_w, conv_b[None, :], hp, hpt, dsk, norm_w[None, :])


def _gdn_kernel(qkv_ref, gate_ref, sm_ref, cw_ref, hp_ref, hpt_ref, nw_ref, y_ref, xpad_ref, st_ref):
    c = pl.program_id(1)

    @pl.when(c == 0)
    def _():
        st_ref[...] = jnp.zeros(st_ref.shape, F32)

    xa = _silu(_causal_conv(qkv_ref[...], xpad_ref, cw_ref[...], c == 0))

    row = _iota2((ROWS, ROWS), 0)
    col = _iota2((ROWS, ROWS), 1)
    same = (row >> 6) == (col >> 6)
    tril = ((row >= col) & same).astype(F32)
    triu = ((row <= col) & same).astype(F32)

    sm = sm_ref[...]
    la = -jnp.exp(hp_ref[1:2, :]) * _softplus(sm + hp_ref[0:1, :])
    gcum = _dot_hi(tril, la)
    smt = sm.T
    lat = (-jnp.exp(hpt_ref[SM_B:SM_B + SUBLANES, 1:2])
           * _softplus(smt[SM_B:SM_B + SUBLANES, :] + hpt_ref[SM_B:SM_B + SUBLANES, 0:1]))
    gcumt = _dot_hi(lat, triu)
    beta_all = _sigmoid(sm)

    ci = _iota2((GDN_CHUNK, GDN_CHUNK), 0)
    cj = _iota2((GDN_CHUNK, GDN_CHUNK), 1)
    incl = ci >= cj
    strict = ci > cj
    eye = (ci == cj).astype(F32)
    dh = GDN_HEAD_DIM
    outs = []
    for h in range(GDN_HEADS):
        st = st_ref[h]
        o_rows = []
        for sc in range(ROWS // GDN_CHUNK):
            r0 = sc * GDN_CHUNK
            r1 = r0 + GDN_CHUNK
            q = xa[r0:r1, h * dh:(h + 1) * dh]
            k = xa[r0:r1, GDN_WIDTH + h * dh:GDN_WIDTH + (h + 1) * dh]
            v = xa[r0:r1, 2 * GDN_WIDTH + h * dh:2 * GDN_WIDTH + (h + 1) * dh]
            q = q * lax.rsqrt(jnp.sum(q * q, axis=-1, keepdims=True) + 1e-6) * (dh ** -0.5)
            k = k * lax.rsqrt(jnp.sum(k * k, axis=-1, keepdims=True) + 1e-6)
            g = gcum[r0:r1, SM_A + h:SM_A + h + 1]
            gt = gcumt[SM_A - SM_B + h:SM_A - SM_B + h + 1, r0:r1]
            beta = beta_all[r0:r1, SM_B + h:SM_B + h + 1]
            decay = jnp.exp(jnp.where(incl, g - gt, -jnp.inf))
            kb = k * beta
            x = -jnp.where(strict, _dot_nt(kb, k) * decay, 0.0)
            tinv = eye + x
            p = x
            for _ in range(5):
                p = _dot_hi(p, p)
                tinv = _dot_hi(tinv, eye + p)
            eg = jnp.exp(g)
            rhs = jnp.concatenate([v * beta, kb * eg], axis=1)
            sol = _dot_hi(tinv, rhs)
            u = sol[:, 0:dh]
            w = sol[:, dh:2 * dh]
            qk = _dot_nt(q, k) * decay
            glast = g[GDN_CHUNK - 1:GDN_CHUNK, :]
            kd = k * jnp.exp(glast - g)
            v_new = u - _dot(w, st)
            o_rows.append(_dot(q * eg, st) + _dot(qk, v_new))
            st = st * jnp.exp(glast) + _dot_tn(kd, v_new)
        st_ref[h] = st
        o = jnp.concatenate(o_rows, axis=0)
        o = o * lax.rsqrt(jnp.mean(o * o, axis=-1, keepdims=True) + 1e-6)
        outs.append(o * nw_ref[...] * _silu(gate_ref[:, h * dh:(h + 1) * dh]))
    y_ref[...] = jnp.concatenate(outs, axis=1).astype(y_ref.dtype)


def _gdn_mixer(proj, small, nb, seq, conv_w, dt_bias, a_log, norm_w):
    nblk = seq // ROWS
    hp = jnp.zeros((SUBLANES, LANES), F32)
    hp = hp.at[0, SM_A:SM_A + GDN_HEADS].set(dt_bias).at[1, SM_A:SM_A + GDN_HEADS].set(a_log)
    hpt = hp.T
    rowmap = lambda b, c: (b * nblk + c, 0)
    const = lambda b, c: (0, 0)
    return pl.pallas_call(
        _gdn_kernel,
        out_shape=jax.ShapeDtypeStruct((nb * seq, GDN_WIDTH), BF16),
        grid=(nb, nblk),
        in_specs=[
            pl.BlockSpec((ROWS, 3 * GDN_WIDTH), lambda b, c: (b * nblk + c, OFF_QKV // (3 * GDN_WIDTH))),
            pl.BlockSpec((ROWS, GDN_WIDTH), lambda b, c: (b * nblk + c, OFF_GGATE // GDN_WIDTH)),
            pl.BlockSpec((ROWS, LANES), rowmap),
            pl.BlockSpec((GDN_CONV, 3 * GDN_WIDTH), const),
            pl.BlockSpec((SUBLANES, LANES), const),
            pl.BlockSpec((LANES, SUBLANES), const),
            pl.BlockSpec((1, GDN_HEAD_DIM), const),
        ],
        out_specs=pl.BlockSpec((ROWS, GDN_WIDTH), rowmap),
        scratch_shapes=[pltpu.VMEM((ROWS + SUBLANES, 3 * GDN_WIDTH), F32),
                        pltpu.VMEM((GDN_HEADS, GDN_HEAD_DIM, GDN_HEAD_DIM), F32)],
        compiler_params=_params(("parallel", "arbitrary")),
        name="gdn_mixer",
    )(proj, proj, small, conv_w, hp, hpt, norm_w[None, :])


def _hgrn_kernel(q_ref, f_ref, i_ref, gate_ref, lb_ref, nw_ref, y_ref, st_ref):
    c = pl.program_id(1)

    @pl.when(c == 0)
    def _():
        st_ref[...] = jnp.zeros(st_ref.shape, F32)

    row = _iota2((ROWS, ROWS), 0)
    col = _iota2((ROWS, ROWS), 1)
    tril = (row >= col).astype(F32)
    ones_b = jnp.ones((LANES, LANES), BF16)
    rmod = _iota2((ROWS, LANES), 0) & (SUBLANES - 1)
    rfull = _iota2((ROWS, LANES), 0)
    dk = HG_HEAD_DIM
    outs = []
    for h in range(HG_HEADS):
        sl = slice(h * dk, (h + 1) * dk)
        lb = lb_ref[:, sl]
        fr = f_ref[:, sl]
        log_sig = jnp.minimum(fr, 0.0) - jnp.log1p(jnp.exp(-jnp.abs(fr)))
        a = jnp.log(lb)
        y = jnp.log1p(-lb) + log_sig
        log_f = jnp.maximum(a, y) + jnp.log1p(jnp.exp(-jnp.abs(a - y)))
        k = (1.0 - lb) * _sigmoid(-fr)
        q = _silu(q_ref[:, sl])
        v = i_ref[:, sl]
        b = _dot_hi(tril, log_f)

        st = st_ref[h]
        o = _dot_nt(q * jnp.exp(b), st)
        blast = b[ROWS - 1:ROWS, :]
        st_ref[h] = st * jnp.exp(blast) + _dot_tn(v, k * jnp.exp(blast - b))

        pmat = jnp.zeros((ROWS, ROWS), F32)
        m = SUBLANES
        while m < ROWS:
            ref = jnp.concatenate(
                [jnp.broadcast_to(b[p0 + m - 1:p0 + m, :], (2 * m, dk)) for p0 in range(0, ROWS, 2 * m)], axis=0)
            right = (rfull & m) != 0
            qs = q * jnp.exp(jnp.where(right, b - ref, -jnp.inf))
            ks = k * jnp.exp(jnp.where(right, -jnp.inf, ref - b))
            parent = ~(2 * m - 1)
            pmat = pmat + jnp.where((row & parent) == (col & parent), _dot_nt(qs, ks), 0.0)
            m *= 2
        o = o + _dot(pmat, v)

        for d in range(SUBLANES):
            if d == 0:
                e = q * k
                vd = v
            else:
                kd = pltpu.roll(k, d, 0)
                bd = pltpu.roll(b, d, 0)
                vd = pltpu.roll(v, d, 0)
                e = q * kd * jnp.exp(jnp.where(rmod >= d, b - bd, -jnp.inf))
            score = jnp.dot(e.astype(BF16), ones_b, preferred_element_type=F32)
            o = o + score * vd

        o = o * lax.rsqrt(jnp.mean(o * o, axis=-1, keepdims=True) + 1e-6)
        outs.append(o * nw_ref[...] * _silu(gate_ref[:, sl]))
    y_ref[...] = jnp.concatenate(outs, axis=1).astype(y_ref.dtype)


def _hgrn_mixer(proj, nb, seq, lb, norm_w):
    nblk = seq // ROWS
    const = lambda b, c: (0, 0)

    def colspec(off):
        return pl.BlockSpec((ROWS, HG_WIDTH), lambda b, c: (b * nblk + c, off // HG_WIDTH))

    return pl.pallas_call(
        _hgrn_kernel,
        out_shape=jax.ShapeDtypeStruct((nb * seq, HG_WIDTH), BF16),
        grid=(nb, nblk),
        in_specs=[colspec(OFF_HQ), colspec(OFF_HF), colspec(OFF_HI), colspec(OFF_HGATE),
                  pl.BlockSpec((1, HG_WIDTH), const),
                  pl.BlockSpec((1, HG_HEAD_DIM), const)],
        out_specs=pl.BlockSpec((ROWS, HG_WIDTH), lambda b, c: (b * nblk + c, 0)),
        scratch_shapes=[pltpu.VMEM((HG_HEADS, HG_HEAD_DIM, HG_HEAD_DIM), F32)],
        compiler_params=_params(("parallel", "arbitrary")),
        name="hgrn2_mixer",
    )(proj, proj, proj, proj, lb[None, :], norm_w[None, :])


def _layer_norm(x, g, b):
    mu = jnp.mean(x, axis=-1, keepdims=True)
    xc = x - mu
    var = jnp.mean(xc * xc, axis=-1, keepdims=True)
    return xc * lax.rsqrt(var + 1e-5) * g + b


def _masked_max(x, mask):
    return jnp.max(jnp.where(mask, x, -jnp.inf), axis=-1, keepdims=True)


def _first_lane(cond, lane):
    return jnp.min(jnp.where(cond, lane, float(LANES)), axis=-1, keepdims=True)


def _outproj_kernel(ys_ref, yg_ref, yh_ref, h_ref, w_ref, g_ref, b_ref, wr_ref, br_ref,
                    h1_ref, h1b_ref, comb_ref):
    mix = jnp.dot(ys_ref[...], w_ref[0:SSD_WIDTH, :], preferred_element_type=F32)
    mix = mix + jnp.dot(yg_ref[...], w_ref[SSD_WIDTH:SSD_WIDTH + GDN_WIDTH, :], preferred_element_type=F32)
    mix = mix + jnp.dot(yh_ref[...], w_ref[SSD_WIDTH + GDN_WIDTH:D_MIX, :], preferred_element_type=F32)
    h1 = _layer_norm(DN_ALPHA * h_ref[...] + mix, g_ref[...], b_ref[...])
    h1_ref[...] = h1
    h1b_ref[...] = h1.astype(BF16)

    logits = _dot_hi(h1, wr_ref[...]) + br_ref[...]
    lane_i = _iota2(logits.shape, 1)
    lane = lane_i.astype(F32)
    gmask = lane_i < MOE_GROUPS
    gmax = _masked_max(logits, gmask)
    gexp = jnp.where(gmask, jnp.exp(logits - gmax), 0.0)
    gprob = gexp / jnp.sum(gexp, axis=-1, keepdims=True)
    g_p = _masked_max(gprob, gmask)
    g_idx = _first_lane(gmask & (gprob == g_p), lane)
    egroup = ((lane_i - MOE_GROUPS) >> 2).astype(F32)
    emask = (lane_i >= MOE_GROUPS) & (lane_i < MOE_GROUPS + N_EXPERTS) & (egroup == g_idx)
    emax = _masked_max(logits, emask)
    eexp = jnp.where(emask, jnp.exp(logits - emax), 0.0)
    eprob = eexp / jnp.sum(eexp, axis=-1, keepdims=True)
    p1 = _masked_max(eprob, emask)
    i1 = _first_lane(emask & (eprob == p1), lane)
    emask2 = emask & (lane != i1)
    p2 = _masked_max(eprob, emask2)
    i2 = _first_lane(emask2 & (eprob == p2), lane)
    denom = p1 + p2
    w1 = g_p * p1 / denom
    w2 = g_p * p2 / denom
    comb_ref[...] = (jnp.where(lane == i1 - MOE_GROUPS, w1, 0.0)
                     + jnp.where(lane == i2 - MOE_GROUPS, w2, 0.0))


def _outproj(ys, yg, yh, h, w_out, ln_g, ln_b, w_router, b_router, tm):
    t = h.shape[0]
    rowmap = lambda i: (i, 0)
    const = lambda i: (0, 0)
    return pl.pallas_call(
        _outproj_kernel,
        out_shape=(jax.ShapeDtypeStruct((t, D_MODEL), F32),
                   jax.ShapeDtypeStruct((t, D_MODEL), BF16),
                   jax.ShapeDtypeStruct((t, LANES), F32)),
        grid=(t // tm,),
        in_specs=[pl.BlockSpec((tm, SSD_WIDTH), rowmap),
                  pl.BlockSpec((tm, GDN_WIDTH), rowmap),
                  pl.BlockSpec((tm, HG_WIDTH), rowmap),
                  pl.BlockSpec((tm, D_MODEL), rowmap),
                  pl.BlockSpec((D_MIX, D_MODEL), const),
                  pl.BlockSpec((1, D_MODEL), const),
                  pl.BlockSpec((1, D_MODEL), const),
                  pl.BlockSpec((D_MODEL, LANES), const),
                  pl.BlockSpec((1, LANES), const)],
        out_specs=(pl.BlockSpec((tm, D_MODEL), rowmap),
                   pl.BlockSpec((tm, D_MODEL), rowmap),
                   pl.BlockSpec((tm, LANES), rowmap)),
        compiler_params=_params(("parallel",)),
        name="out_proj_ln_router",
    )(ys, yg, yh, h, w_out, ln_g[None, :], ln_b[None, :], w_router, b_router)


def _moe_kernel(hb_ref, h_ref, comb_ref, wgu_ref, wdn_ref, g_ref, b_ref, o_ref, ob_ref, acc_ref):
    e = pl.program_id(1)

    @pl.when(e == 0)
    def _():
        acc_ref[...] = jnp.zeros(acc_ref.shape, F32)

    gu = jnp.dot(hb_ref[...], wgu_ref[0], preferred_element_type=F32)
    comb = comb_ref[...]
    lane = _iota2(comb.shape, 1)
    cw = jnp.sum(jnp.where(lane == e, comb, 0.0), axis=-1, keepdims=True)
    hmid = _silu(gu[:, 0:D_EXPERT]) * gu[:, D_EXPERT:2 * D_EXPERT] * cw
    acc_ref[...] += jnp.dot(hmid.astype(BF16), wdn_ref[0], preferred_element_type=F32)

    @pl.when(e == N_EXPERTS - 1)
    def _():
        h2 = _layer_norm(DN_ALPHA * h_ref[...] + acc_ref[...], g_ref[...], b_ref[...])
        o_ref[...] = h2
        ob_ref[...] = h2.astype(BF16)


def _moe(hb, h, comb, w_gu, w_dn, ln_g, ln_b, tm):
    t = h.shape[0]
    rowmap = lambda i, e: (i, 0)
    const = lambda i, e: (0, 0)
    return pl.pallas_call(
        _moe_kernel,
        out_shape=(jax.ShapeDtypeStruct((t, D_MODEL), F32),
                   jax.ShapeDtypeStruct((t, D_MODEL), BF16)),
        grid=(t // tm, N_EXPERTS),
        in_specs=[pl.BlockSpec((tm, D_MODEL), rowmap),
                  pl.BlockSpec((tm, D_MODEL), rowmap),
                  pl.BlockSpec((tm, LANES), rowmap),
                  pl.BlockSpec((1, D_MODEL, 2 * D_EXPERT), lambda i, e: (e, 0, 0)),
                  pl.BlockSpec((1, D_EXPERT, D_MODEL), lambda i, e: (e, 0, 0)),
                  pl.BlockSpec((1, D_MODEL), const),
                  pl.BlockSpec((1, D_MODEL), const)],
        out_specs=(pl.BlockSpec((tm, D_MODEL), rowmap),
                   pl.BlockSpec((tm, D_MODEL), rowmap)),
        scratch_shapes=[pltpu.VMEM((tm, D_MODEL), F32)],
        compiler_params=_params(("parallel", "arbitrary")),
        name="moe_ln",
    )(hb, h, comb, w_gu, w_dn, ln_g[None, :], ln_b[None, :])


def _split_w_in(w):
    sizes = (SSD_WIDTH, SSD_XBC, SSD_HEADS, 3 * GDN_WIDTH, GDN_WIDTH, GDN_HEADS, GDN_HEADS,
             HG_WIDTH, HG_WIDTH, HG_WIDTH, HG_WIDTH)
    parts, acc = [], 0
    for s in sizes:
        parts.append(w[:, acc:acc + s])
        acc += s
    z, xbc, dt, qkv, ggate, gb, ga, hq, hf, hi, hgate = parts
    main = jnp.concatenate([xbc, qkv, z, ggate, hq, hf, hi, hgate], axis=1).astype(BF16)
    small = jnp.concatenate([dt, gb, ga], axis=1)
    small = jnp.pad(small, ((0, 0), (0, LANES - small.shape[1])))
    return main, small


def kernel(x, w_in, ssd_conv_w, ssd_conv_b, ssd_dt_bias, ssd_a_log, ssd_d, ssd_norm_w, gdn_conv_w, gdn_dt_bias, gdn_a_log, gdn_norm_w, hg_lb_logits, hg_norm_w, w_out, ln1_g, ln1_b, w_router_group, b_router_group, w_router_expert, b_router_expert, w_expert_gate_up, w_expert_down, ln2_g, ln2_b):
    nb, seq, d = x.shape
    t = nb * seq
    lb_cum = jnp.cumsum(jax.nn.softmax(hg_lb_logits.astype(F32), axis=0), axis=0)
    lb_all = lb_cum - lb_cum[0:1]
    h = x.reshape(t, d)
    hb = h.astype(BF16)
    for l in range(DEPTH):
        w_main, w_small = _split_w_in(w_in[l])
        proj = _matmul(hb, w_main, 512, MAIN_COLS // 2)
        small = _matmul(h, w_small, 1024, LANES, hi=True)
        y_ssd = _ssd_mixer(proj, small, nb, seq, ssd_conv_w[l], ssd_conv_b[l], ssd_dt_bias[l],
                           ssd_a_log[l], ssd_d[l], ssd_norm_w[l])
        y_gdn = _gdn_mixer(proj, small, nb, seq, gdn_conv_w[l], gdn_dt_bias[l], gdn_a_log[l], gdn_norm_w[l])
        y_hg = _hgrn_mixer(proj, nb, seq, lb_all[l], hg_norm_w[l])
        w_router = jnp.pad(jnp.concatenate([w_router_group[l], w_router_expert[l]], axis=1),
                           ((0, 0), (0, LANES - MOE_GROUPS - N_EXPERTS)))
        b_router = jnp.pad(jnp.concatenate([b_router_group[l], b_router_expert[l]]),
                           (0, LANES - MOE_GROUPS - N_EXPERTS))[None, :]
        h1, h1b, comb = _outproj(y_ssd, y_gdn, y_hg, h, w_out[l].astype(BF16), ln1_g[l], ln1_b[l],
                                 w_router, b_router, 512)
        h, hb = _moe(h1b, h1, comb, w_expert_gate_up[l].astype(BF16), w_expert_down[l].astype(BF16),
                     ln2_g[l], ln2_b[l], 512)
    return h.reshape(nb, seq, d)
```

```python
import functools

import jax
import jax.numpy as jnp
from jax import lax
from jax.experimental import pallas as pl
from jax.experimental.pallas import tpu as pltpu

F32 = jnp.float32
BF16 = jnp.bfloat16
HIGHEST = lax.Precision.HIGHEST

D_MODEL = 1024
DEPTH = 4
SSD_HEADS = 16
SSD_HEAD_DIM = 64
SSD_WIDTH = 1024
SSD_GROUPS = 2
SSD_STATE = 128
SSD_BC = 256
SSD_XBC = 1536
SSD_CONV = 4
GDN_HEADS = 4
GDN_HEAD_DIM = 128
GDN_WIDTH = 512
GDN_CONV = 4
GDN_CHUNK = 64
HG_HEADS = 4
HG_HEAD_DIM = 128
HG_WIDTH = 512
D_MIX = 2048
MOE_GROUPS = 4
EXPERTS_PER_GROUP = 4
N_EXPERTS = 16
D_EXPERT = 256
DN_ALPHA = (2 * DEPTH) ** 0.25

LANES = 128
SUBLANES = 8
ROWS = 128
VMEM_LIMIT = 48 * 1024 * 1024

OFF_XBC = 0
OFF_QKV = 1536
OFF_Z = 3072
OFF_GGATE = 4096
OFF_HQ = 4608
OFF_HF = 5120
OFF_HI = 5632
OFF_HGATE = 6144
MAIN_COLS = 6656
SM_DT = 0
SM_B = 16
SM_A = 20


def _dot(a, b):
    return jnp.dot(a.astype(BF16), b.astype(BF16), preferred_element_type=F32)


def _dot_nt(a, b):
    return lax.dot_general(a.astype(BF16), b.astype(BF16), (((1,), (1,)), ((), ())),
                           preferred_element_type=F32)


def _dot_tn(a, b):
    return lax.dot_general(a.astype(BF16), b.astype(BF16), (((0,), (0,)), ((), ())),
                           preferred_element_type=F32)


def _dot_hi(a, b):
    return jnp.dot(a, b, precision=HIGHEST, preferred_element_type=F32)


def _split(a):
    hi = a.astype(BF16)
    return hi, (a - hi.astype(F32)).astype(BF16)


def _dot3(a, b):
    (ah, al), (bh, bl) = a, b
    return (jnp.dot(ah, bh, preferred_element_type=F32) + jnp.dot(ah, bl, preferred_element_type=F32)
            + jnp.dot(al, bh, preferred_element_type=F32))


def _dot2(a, b01):
    hi = a.astype(BF16)
    lo = (a - hi.astype(F32)).astype(BF16)
    b = b01.astype(BF16)
    return (jnp.dot(hi, b, preferred_element_type=F32) + jnp.dot(lo, b, preferred_element_type=F32))


def _sigmoid(x):
    return 1.0 / (1.0 + jnp.exp(-x))


def _silu(x):
    return x * _sigmoid(x)


def _softplus(x):
    return jnp.maximum(x, 0.0) + jnp.log1p(jnp.exp(-jnp.abs(x)))


def _iota2(shape, dim):
    return lax.broadcasted_iota(jnp.int32, shape, dim)


def _params(sem):
    return pltpu.CompilerParams(dimension_semantics=sem, vmem_limit_bytes=VMEM_LIMIT)


def _matmul_kernel(x_ref, w_ref, o_ref):
    o_ref[...] = jnp.dot(x_ref[...], w_ref[...], preferred_element_type=F32)


def _matmul_hi_kernel(x_ref, w_ref, o_ref):
    o_ref[...] = _dot_hi(x_ref[...], w_ref[...])


def _matmul(x, w, tm, tn, hi=False):
    t, k = x.shape
    n = w.shape[1]
    return pl.pallas_call(
        _matmul_hi_kernel if hi else _matmul_kernel,
        out_shape=jax.ShapeDtypeStruct((t, n), F32),
        grid=(t // tm, n // tn),
        in_specs=[pl.BlockSpec((tm, k), lambda i, j: (i, 0)),
                  pl.BlockSpec((k, tn), lambda i, j: (0, j))],
        out_specs=pl.BlockSpec((tm, tn), lambda i, j: (i, j)),
        compiler_params=_params(("parallel", "arbitrary")),
        name="in_proj_hi" if hi else "in_proj",
    )(x, w)


def _causal_conv(x, xpad_ref, cw, first):
    rows = x.shape[0]

    @pl.when(first)
    def _():
        xpad_ref[0:SUBLANES, :] = jnp.zeros((SUBLANES, x.shape[1]), F32)

    xpad_ref[SUBLANES:SUBLANES + rows, :] = x
    k = cw.shape[0]
    acc = x * cw[k - 1:k, :]
    for i in range(k - 1):
        off = SUBLANES - (k - 1) + i
        acc = acc + xpad_ref[off:off + rows, :] * cw[i:i + 1, :]
    xpad_ref[0:SUBLANES, :] = xpad_ref[rows:rows + SUBLANES, :]
    return acc


def _ssd_kernel(z_ref, xbc_ref, sm_ref, cw_ref, cb_ref, hp_ref, hpt_ref, dsk_ref, nw_ref,
                y_ref, xpad_ref, st_ref):
    c = pl.program_id(1)

    @pl.when(c == 0)
    def _():
        st_ref[...] = jnp.zeros(st_ref.shape, F32)

    xa = _causal_conv(xbc_ref[...], xpad_ref, cw_ref[...], c == 0) + cb_ref[...]
    xa = _silu(xa)
    xs = xa[:, 0:SSD_WIDTH]

    row = _iota2((ROWS, ROWS), 0)
    col = _iota2((ROWS, ROWS), 1)
    causal = row >= col
    tril = causal.astype(F32)
    triu = (row <= col).astype(F32)

    sm = sm_ref[...]
    dt = _softplus(sm + hp_ref[0:1, :])
    da = dt * (-jnp.exp(hp_ref[1:2, :]))
    acum = _dot_hi(tril, da)
    smt = sm.T
    dtt = _softplus(smt[0:SSD_HEADS, :] + hpt_ref[0:SSD_HEADS, 0:1])
    dat = dtt * (-jnp.exp(hpt_ref[0:SSD_HEADS, 1:2]))
    acumt = _dot_hi(dat, triu)

    hrow = _iota2((LANES, SSD_WIDTH), 0)
    hcol = _iota2((LANES, SSD_WIDTH), 1)
    expand = ((hcol >> 6) == hrow).astype(F32)
    dtx = _dot2(dt, expand)
    eax = _dot2(jnp.exp(acum), expand)
    tex = _dot2(jnp.exp(acum[ROWS - 1:ROWS, :] - acum), expand)
    xdt = xs * dtx
    xdt_b = xdt.astype(BF16)
    xend_b = (xdt * tex).astype(BF16)
    lane_lo = _iota2((ROWS, LANES), 1) < SSD_HEAD_DIM

    hg = SSD_HEADS // SSD_GROUPS
    gw = SSD_WIDTH // SSD_GROUPS
    y_parts = []
    for g in range(SSD_GROUPS):
        bm = xa[:, SSD_WIDTH + g * SSD_STATE:SSD_WIDTH + (g + 1) * SSD_STATE]
        cm = xa[:, SSD_WIDTH + SSD_BC + g * SSD_STATE:SSD_WIDTH + SSD_BC + (g + 1) * SSD_STATE]
        cm_b = cm.astype(BF16)
        cb = _dot_nt(cm_b, bm)
        yd = []
        for pair in range(hg // 2):
            res = []
            for sub in range(2):
                h = g * hg + pair * 2 + sub
                seg = acum[:, h:h + 1] - acumt[h:h + 1, :]
                lmat = cb * jnp.exp(jnp.where(causal, seg, -jnp.inf))
                c0 = (g * hg + pair * 2) * SSD_HEAD_DIM
                res.append(jnp.dot(lmat.astype(BF16), xdt_b[:, c0:c0 + LANES],
                                   preferred_element_type=F32))
            yd.append(jnp.where(lane_lo, res[0], res[1]))
        yd = jnp.concatenate(yd, axis=1)
        st = st_ref[g]
        yoff = jnp.dot(cm_b, st.astype(BF16), preferred_element_type=F32) * eax[:, g * gw:(g + 1) * gw]
        st_ref[g] = (st * eax[ROWS - 1:ROWS, g * gw:(g + 1) * gw]
                     + jnp.dot(bm.T.astype(BF16), xend_b[:, g * gw:(g + 1) * gw],
                               preferred_element_type=F32))
        y = yd + yoff + xs[:, g * gw:(g + 1) * gw] * dsk_ref[:, g * gw:(g + 1) * gw]
        y = y * _silu(z_ref[:, g * gw:(g + 1) * gw])
        ms = jnp.mean(y * y, axis=-1, keepdims=True)
        y_parts.append(y * lax.rsqrt(ms + 1e-6) * nw_ref[:, g * gw:(g + 1) * gw])
    y_ref[...] = jnp.concatenate(y_parts, axis=1).astype(y_ref.dtype)


def _ssd_mixer(proj, small, nb, seq, conv_w, conv_b, dt_bias, a_log, d_skip, norm_w):
    nblk = seq // ROWS
    hp = jnp.zeros((SUBLANES, LANES), F32)
    hp = hp.at[0, SM_DT:SM_DT + SSD_HEADS].set(dt_bias).at[1, SM_DT:SM_DT + SSD_HEADS].set(a_log)
    hpt = hp.T
    dsk = jnp.repeat(d_skip, SSD_HEAD_DIM)[None, :]
    rowmap = lambda b, c: (b * nblk + c, 0)
    const = lambda b, c: (0, 0)
    return pl.pallas_call(
        _ssd_kernel,
        out_shape=jax.ShapeDtypeStruct((nb * seq, SSD_WIDTH), BF16),
        grid=(nb, nblk),
        in_specs=[
            pl.BlockSpec((ROWS, SSD_WIDTH), lambda b, c: (b * nblk + c, OFF_Z // SSD_WIDTH)),
            pl.BlockSpec((ROWS, SSD_XBC), lambda b, c: (b * nblk + c, OFF_XBC // SSD_XBC)),
            pl.BlockSpec((ROWS, LANES), rowmap),
            pl.BlockSpec((SSD_CONV, SSD_XBC), const),
            pl.BlockSpec((1, SSD_XBC), const),
            pl.BlockSpec((SUBLANES, LANES), const),
            pl.BlockSpec((LANES, SUBLANES), const),
            pl.BlockSpec((1, SSD_WIDTH), const),
            pl.BlockSpec((1, SSD_WIDTH), const),
        ],
        out_specs=pl.BlockSpec((ROWS, SSD_WIDTH), rowmap),
        scratch_shapes=[pltpu.VMEM((ROWS + SUBLANES, SSD_XBC), F32),
                        pltpu.VMEM((SSD_GROUPS, SSD_STATE, SSD_WIDTH // SSD_GROUPS), F32)],
        compiler_params=_params(("parallel", "arbitrary")),
        name="ssd_mixer",
    )(proj, proj, small, conv_w, conv_b[None, :], hp, hpt, dsk, norm_w[None, :])


def _gdn_kernel(qkv_ref, gate_ref, sm_ref, cw_ref, hp_ref, hpt_ref, nw_ref, y_ref, xpad_ref, st_ref):
    c = pl.program_id(1)

    @pl.when(c == 0)
    def _():
        st_ref[...] = jnp.zeros(st_ref.shape, F32)

    xa = _silu(_causal_conv(qkv_ref[...], xpad_ref, cw_ref[...], c == 0))

    row = _iota2((ROWS, ROWS), 0)
    col = _iota2((ROWS, ROWS), 1)
    incl = row >= col
    strict = row > col
    eye = (row == col).astype(F32)
    tril = incl.astype(F32)
    triu = (row <= col).astype(F32)

    sm = sm_ref[...]
    la = -jnp.exp(hp_ref[1:2, :]) * _softplus(sm + hp_ref[0:1, :])
    gcum = _dot_hi(tril, la)
    smt = sm.T
    lat = (-jnp.exp(hpt_ref[SM_B:SM_B + SUBLANES, 1:2])
           * _softplus(smt[SM_B:SM_B + SUBLANES, :] + hpt_ref[SM_B:SM_B + SUBLANES, 0:1]))
    gcumt = _dot_hi(lat, triu)
    beta_all = _sigmoid(sm)

    dh = GDN_HEAD_DIM
    heads = range(GDN_HEADS)
    qs, ks, xs, rhss, decays, gs = [], [], [], [], [], []
    for h in heads:
        q = xa[:, h * dh:(h + 1) * dh]
        k = xa[:, GDN_WIDTH + h * dh:GDN_WIDTH + (h + 1) * dh]
        v = xa[:, 2 * GDN_WIDTH + h * dh:2 * GDN_WIDTH + (h + 1) * dh]
        q = q * lax.rsqrt(jnp.sum(q * q, axis=-1, keepdims=True) + 1e-6) * (dh ** -0.5)
        k = k * lax.rsqrt(jnp.sum(k * k, axis=-1, keepdims=True) + 1e-6)
        g = gcum[:, SM_A + h:SM_A + h + 1]
        gt = gcumt[SM_A - SM_B + h:SM_A - SM_B + h + 1, :]
        beta = beta_all[:, SM_B + h:SM_B + h + 1]
        decay = jnp.exp(jnp.where(incl, g - gt, -jnp.inf))
        kb = k * beta
        xs.append(-jnp.where(strict, _dot_nt(kb, k) * decay, 0.0))
        rhss.append(jnp.concatenate([v * beta, kb * jnp.exp(g)], axis=1))
        qs.append(q)
        ks.append(k)
        decays.append(decay)
        gs.append(g)
    ps = [_split(x) for x in xs]
    ts = [_split(eye + x) for x in xs]
    levels = ROWS.bit_length() - 2
    for j in range(levels):
        pf = [_dot3(p, p) for p in ps]
        ps = [_split(p) for p in pf]
        ts = [_split(_dot3(t, _split(eye + p))) for t, p in zip(ts, pf)]
    sols = [_dot3(t, _split(r)) for t, r in zip(ts, rhss)]
    qks = [_dot_nt(q, k) * d for q, k, d in zip(qs, ks, decays)]
    sts = [st_ref[h] for h in heads]
    v_news = [s[:, 0:dh] - _dot(s[:, dh:2 * dh], st) for s, st in zip(sols, sts)]
    os_ = [_dot(q * jnp.exp(g), st) + _dot(qk, vn) for q, g, st, qk, vn in zip(qs, gs, sts, qks, v_news)]
    for h in heads:
        glast = gs[h][ROWS - 1:ROWS, :]
        kd = ks[h] * jnp.exp(glast - gs[h])
        st_ref[h] = sts[h] * jnp.exp(glast) + _dot(kd.T, v_news[h])
    outs = []
    for h in heads:
        o = os_[h]
        o = o * lax.rsqrt(jnp.mean(o * o, axis=-1, keepdims=True) + 1e-6)
        outs.append(o * nw_ref[...] * _silu(gate_ref[:, h * dh:(h + 1) * dh]))
    y_ref[...] = jnp.concatenate(outs, axis=1).astype(y_ref.dtype)


def _gdn_mixer(proj, small, nb, seq, conv_w, dt_bias, a_log, norm_w):
    nblk = seq // ROWS
    hp = jnp.zeros((SUBLANES, LANES), F32)
    hp = hp.at[0, SM_A:SM_A + GDN_HEADS].set(dt_bias).at[1, SM_A:SM_A + GDN_HEADS].set(a_log)
    hpt = hp.T
    rowmap = lambda b, c: (b * nblk + c, 0)
    const = lambda b, c: (0, 0)
    return pl.pallas_call(
        _gdn_kernel,
        out_shape=jax.ShapeDtypeStruct((nb * seq, GDN_WIDTH), BF16),
        grid=(nb, nblk),
        in_specs=[
            pl.BlockSpec((ROWS, 3 * GDN_WIDTH), lambda b, c: (b * nblk + c, OFF_QKV // (3 * GDN_WIDTH))),
            pl.BlockSpec((ROWS, GDN_WIDTH), lambda b, c: (b * nblk + c, OFF_GGATE // GDN_WIDTH)),
            pl.BlockSpec((ROWS, LANES), rowmap),
            pl.BlockSpec((GDN_CONV, 3 * GDN_WIDTH), const),
            pl.BlockSpec((SUBLANES, LANES), const),
            pl.BlockSpec((LANES, SUBLANES), const),
            pl.BlockSpec((1, GDN_HEAD_DIM), const),
        ],
        out_specs=pl.BlockSpec((ROWS, GDN_WIDTH), rowmap),
        scratch_shapes=[pltpu.VMEM((ROWS + SUBLANES, 3 * GDN_WIDTH), F32),
                        pltpu.VMEM((GDN_HEADS, GDN_HEAD_DIM, GDN_HEAD_DIM), F32)],
        compiler_params=_params(("parallel", "arbitrary")),
        name="gdn_mixer",
    )(proj, proj, small, conv_w, hp, hpt, norm_w[None, :])


def _hgrn_kernel(q_ref, f_ref, i_ref, gate_ref, lb_ref, nw_ref, y_ref, st_ref):
    c = pl.program_id(1)

    @pl.when(c == 0)
    def _():
        st_ref[...] = jnp.zeros(st_ref.shape, F32)

    row = _iota2((ROWS, ROWS), 0)
    col = _iota2((ROWS, ROWS), 1)
    tril = (row >= col).astype(F32)
    ones_b = jnp.ones((LANES, LANES), BF16)
    rmod = _iota2((ROWS, LANES), 0) & (SUBLANES - 1)
    rfull = _iota2((ROWS, LANES), 0)
    dk = HG_HEAD_DIM
    outs = []
    for h in range(HG_HEADS):
        sl = slice(h * dk, (h + 1) * dk)
        lb = lb_ref[:, sl]
        fr = f_ref[:, sl]
        log_sig = jnp.minimum(fr, 0.0) - jnp.log1p(jnp.exp(-jnp.abs(fr)))
        a = jnp.log(lb)
        y = jnp.log1p(-lb) + log_sig
        log_f = jnp.maximum(a, y) + jnp.log1p(jnp.exp(-jnp.abs(a - y)))
        k = (1.0 - lb) * _sigmoid(-fr)
        q = _silu(q_ref[:, sl])
        v = i_ref[:, sl]
        b = _dot_hi(tril, log_f)

        st = st_ref[h]
        o = _dot_nt(q * jnp.exp(b), st)
        blast = b[ROWS - 1:ROWS, :]
        st_ref[h] = st * jnp.exp(blast) + _dot_tn(v, k * jnp.exp(blast - b))

        pmat = jnp.zeros((ROWS, ROWS), F32)
        m = SUBLANES
        while m < ROWS:
            ref = jnp.concatenate(
                [jnp.broadcast_to(b[p0 + m - 1:p0 + m, :], (2 * m, dk)) for p0 in range(0, ROWS, 2 * m)], axis=0)
            right = (rfull & m) != 0
            qs = q * jnp.exp(jnp.where(right, b - ref, -jnp.inf))
            ks = k * jnp.exp(jnp.where(right, -jnp.inf, ref - b))
            parent = ~(2 * m - 1)
            pmat = pmat + jnp.where((row & parent) == (col & parent), _dot_nt(qs, ks), 0.0)
            m *= 2
        o = o + _dot(pmat, v)

        for d in range(SUBLANES):
            if d == 0:
                e = q * k
                vd = v
            else:
                kd = pltpu.roll(k, d, 0)
                bd = pltpu.roll(b, d, 0)
                vd = pltpu.roll(v, d, 0)
                e = q * kd * jnp.exp(jnp.where(rmod >= d, b - bd, -jnp.inf))
            score = jnp.dot(e.astype(BF16), ones_b, preferred_element_type=F32)
            o = o + score * vd

        o = o * lax.rsqrt(jnp.mean(o * o, axis=-1, keepdims=True) + 1e-6)
        outs.append(o * nw_ref[...] * _silu(gate_ref[:, sl]))
    y_ref[...] = jnp.concatenate(outs, axis=1).astype(y_ref.dtype)


def _hgrn_mixer(proj, nb, seq, lb, norm_w):
    nblk = seq // ROWS
    const = lambda b, c: (0, 0)

    def colspec(off):
        return pl.BlockSpec((ROWS, HG_WIDTH), lambda b, c: (b * nblk + c, off // HG_WIDTH))

    return pl.pallas_call(
        _hgrn_kernel,
        out_shape=jax.ShapeDtypeStruct((nb * seq, HG_WIDTH), BF16),
        grid=(nb, nblk),
        in_specs=[colspec(OFF_HQ), colspec(OFF_HF), colspec(OFF_HI), colspec(OFF_HGATE),
                  pl.BlockSpec((1, HG_WIDTH), const),
                  pl.BlockSpec((1, HG_HEAD_DIM), const)],
        out_specs=pl.BlockSpec((ROWS, HG_WIDTH), lambda b, c: (b * nblk + c, 0)),
        scratch_shapes=[pltpu.VMEM((HG_HEADS, HG_HEAD_DIM, HG_HEAD_DIM), F32)],
        compiler_params=_params(("parallel", "arbitrary")),
        name="hgrn2_mixer",
    )(proj, proj, proj, proj, lb[None, :], norm_w[None, :])


def _layer_norm(x, g, b):
    mu = jnp.mean(x, axis=-1, keepdims=True)
    xc = x - mu
    var = jnp.mean(xc * xc, axis=-1, keepdims=True)
    return xc * lax.rsqrt(var + 1e-5) * g + b


def _masked_max(x, mask):
    return jnp.max(jnp.where(mask, x, -jnp.inf), axis=-1, keepdims=True)


def _first_lane(cond, lane):
    return jnp.min(jnp.where(cond, lane, float(LANES)), axis=-1, keepdims=True)


def _outproj_kernel(ys_ref, yg_ref, yh_ref, h_ref, w_ref, g_ref, b_ref, wr_ref, br_ref,
                    h1_ref, h1b_ref, comb_ref):
    mix = jnp.dot(ys_ref[...], w_ref[0:SSD_WIDTH, :], preferred_element_type=F32)
    mix = mix + jnp.dot(yg_ref[...], w_ref[SSD_WIDTH:SSD_WIDTH + GDN_WIDTH, :], preferred_element_type=F32)
    mix = mix + jnp.dot(yh_ref[...], w_ref[SSD_WIDTH + GDN_WIDTH:D_MIX, :], preferred_element_type=F32)
    h1 = _layer_norm(DN_ALPHA * h_ref[...] + mix, g_ref[...], b_ref[...])
    h1_ref[...] = h1
    h1b_ref[...] = h1.astype(BF16)

    logits = _dot_hi(h1, wr_ref[...]) + br_ref[...]
    lane_i = _iota2(logits.shape, 1)
    lane = lane_i.astype(F32)
    gmask = lane_i < MOE_GROUPS
    gmax = _masked_max(logits, gmask)
    gexp = jnp.where(gmask, jnp.exp(logits - gmax), 0.0)
    gprob = gexp / jnp.sum(gexp, axis=-1, keepdims=True)
    g_p = _masked_max(gprob, gmask)
    g_idx = _first_lane(gmask & (gprob == g_p), lane)
    egroup = ((lane_i - MOE_GROUPS) >> 2).astype(F32)
    emask = (lane_i >= MOE_GROUPS) & (lane_i < MOE_GROUPS + N_EXPERTS) & (egroup == g_idx)
    emax = _masked_max(logits, emask)
    eexp = jnp.where(emask, jnp.exp(logits - emax), 0.0)
    eprob = eexp / jnp.sum(eexp, axis=-1, keepdims=True)
    p1 = _masked_max(eprob, emask)
    i1 = _first_lane(emask & (eprob == p1), lane)
    emask2 = emask & (lane != i1)
    p2 = _masked_max(eprob, emask2)
    i2 = _first_lane(emask2 & (eprob == p2), lane)
    denom = p1 + p2
    w1 = g_p * p1 / denom
    w2 = g_p * p2 / denom
    comb_ref[...] = (jnp.where(lane == i1 - MOE_GROUPS, w1, 0.0)
                     + jnp.where(lane == i2 - MOE_GROUPS, w2, 0.0))


def _outproj(ys, yg, yh, h, w_out, ln_g, ln_b, w_router, b_router, tm):
    t = h.shape[0]
    rowmap = lambda i: (i, 0)
    const = lambda i: (0, 0)
    return pl.pallas_call(
        _outproj_kernel,
        out_shape=(jax.ShapeDtypeStruct((t, D_MODEL), F32),
                   jax.ShapeDtypeStruct((t, D_MODEL), BF16),
                   jax.ShapeDtypeStruct((t, LANES), F32)),
        grid=(t // tm,),
        in_specs=[pl.BlockSpec((tm, SSD_WIDTH), rowmap),
                  pl.BlockSpec((tm, GDN_WIDTH), rowmap),
                  pl.BlockSpec((tm, HG_WIDTH), rowmap),
                  pl.BlockSpec((tm, D_MODEL), rowmap),
                  pl.BlockSpec((D_MIX, D_MODEL), const),
                  pl.BlockSpec((1, D_MODEL), const),
                  pl.BlockSpec((1, D_MODEL), const),
                  pl.BlockSpec((D_MODEL, LANES), const),
                  pl.BlockSpec((1, LANES), const)],
        out_specs=(pl.BlockSpec((tm, D_MODEL), rowmap),
                   pl.BlockSpec((tm, D_MODEL), rowmap),
                   pl.BlockSpec((tm, LANES), rowmap)),
        compiler_params=_params(("parallel",)),
        name="out_proj_ln_router",
    )(ys, yg, yh, h, w_out, ln_g[None, :], ln_b[None, :], w_router, b_router)


def _moe_kernel(hb_ref, h_ref, comb_ref, wgu_ref, wdn_ref, g_ref, b_ref, o_ref, ob_ref, acc_ref):
    e = pl.program_id(1)

    @pl.when(e == 0)
    def _():
        acc_ref[...] = jnp.zeros(acc_ref.shape, F32)

    gu = jnp.dot(hb_ref[...], wgu_ref[0], preferred_element_type=F32)
    comb = comb_ref[...]
    lane = _iota2(comb.shape, 1)
    cw = jnp.sum(jnp.where(lane == e, comb, 0.0), axis=-1, keepdims=True)
    hmid = _silu(gu[:, 0:D_EXPERT]) * gu[:, D_EXPERT:2 * D_EXPERT] * cw
    acc_ref[...] += jnp.dot(hmid.astype(BF16), wdn_ref[0], preferred_element_type=F32)

    @pl.when(e == N_EXPERTS - 1)
    def _():
        h2 = _layer_norm(DN_ALPHA * h_ref[...] + acc_ref[...], g_ref[...], b_ref[...])
        o_ref[...] = h2
        ob_ref[...] = h2.astype(BF16)


def _moe(hb, h, comb, w_gu, w_dn, ln_g, ln_b, tm):
    t = h.shape[0]
    rowmap = lambda i, e: (i, 0)
    const = lambda i, e: (0, 0)
    return pl.pallas_call(
        _moe_kernel,
        out_shape=(jax.ShapeDtypeStruct((t, D_MODEL), F32),
                   jax.ShapeDtypeStruct((t, D_MODEL), BF16)),
        grid=(t // tm, N_EXPERTS),
        in_specs=[pl.BlockSpec((tm, D_MODEL), rowmap),
                  pl.BlockSpec((tm, D_MODEL), rowmap),
                  pl.BlockSpec((tm, LANES), rowmap),
                  pl.BlockSpec((1, D_MODEL, 2 * D_EXPERT), lambda i, e: (e, 0, 0)),
                  pl.BlockSpec((1, D_EXPERT, D_MODEL), lambda i, e: (e, 0, 0)),
                  pl.BlockSpec((1, D_MODEL), const),
                  pl.BlockSpec((1, D_MODEL), const)],
        out_specs=(pl.BlockSpec((tm, D_MODEL), rowmap),
                   pl.BlockSpec((tm, D_MODEL), rowmap)),
        scratch_shapes=[pltpu.VMEM((tm, D_MODEL), F32)],
        compiler_params=_params(("parallel", "arbitrary")),
        name="moe_ln",
    )(hb, h, comb, w_gu, w_dn, ln_g[None, :], ln_b[None, :])


def _split_w_in(w):
    sizes = (SSD_WIDTH, SSD_XBC, SSD_HEADS, 3 * GDN_WIDTH, GDN_WIDTH, GDN_HEADS, GDN_HEADS,
             HG_WIDTH, HG_WIDTH, HG_WIDTH, HG_WIDTH)
    parts, acc = [], 0
    for s in sizes:
        parts.append(w[:, acc:acc + s])
        acc += s
    z, xbc, dt, qkv, ggate, gb, ga, hq, hf, hi, hgate = parts
    main = jnp.concatenate([xbc, qkv, z, ggate, hq, hf, hi, hgate], axis=1).astype(BF16)
    small = jnp.concatenate([dt, gb, ga], axis=1)
    small = jnp.pad(small, ((0, 0), (0, LANES - small.shape[1])))
    return main, small


def kernel(x, w_in, ssd_conv_w, ssd_conv_b, ssd_dt_bias, ssd_a_log, ssd_d, ssd_norm_w, gdn_conv_w, gdn_dt_bias, gdn_a_log, gdn_norm_w, hg_lb_logits, hg_norm_w, w_out, ln1_g, ln1_b, w_router_group, b_router_group, w_router_expert, b_router_expert, w_expert_gate_up, w_expert_down, ln2_g, ln2_b):
    nb, seq, d = x.shape
    t = nb * seq
    lb_cum = jnp.cumsum(jax.nn.softmax(hg_lb_logits.astype(F32), axis=0), axis=0)
    lb_all = lb_cum - lb_cum[0:1]
    h = x.reshape(t, d)
    hb = h.astype(BF16)
    for l in range(DEPTH):
        w_main, w_small = _split_w_in(w_in[l])
        proj = _matmul(hb, w_main, 512, MAIN_COLS // 2)
        small = _matmul(h, w_small, 1024, LANES, hi=True)
        y_ssd = _ssd_mixer(proj, small, nb, seq, ssd_conv_w[l], ssd_conv_b[l], ssd_dt_bias[l],
                           ssd_a_log[l], ssd_d[l], ssd_norm_w[l])
        y_gdn = _gdn_mixer(proj, small, nb, seq, gdn_conv_w[l], gdn_dt_bias[l], gdn_a_log[l], gdn_norm_w[l])
        y_hg = _hgrn_mixer(proj, nb, seq, lb_all[l], hg_norm_w[l])
        w_router = jnp.pad(jnp.concatenate([w_router_group[l], w_router_expert[l]], axis=1),
                           ((0, 0), (0, LANES - MOE_GROUPS - N_EXPERTS)))
        b_router = jnp.pad(jnp.concatenate([b_router_group[l], b_router_expert[l]]),
                           (0, LANES - MOE_GROUPS - N_EXPERTS))[None, :]
        h1, h1b, comb = _outproj(y_ssd, y_gdn, y_hg, h, w_out[l].astype(BF16), ln1_g[l], ln1_b[l],
                                 w_router, b_router, 512)
        h, hb = _moe(h1b, h1, comb, w_expert_gate_up[l].astype(BF16), w_expert_down[l].astype(BF16),
                     ln2_g[l], ln2_b[l], 512)
    return h.reshape(nb, seq, d)
```

```python
import functools

import jax
import jax.numpy as jnp
from jax import lax
from jax.experimental import pallas as pl
from jax.experimental.pallas import tpu as pltpu

F32 = jnp.float32
BF16 = jnp.bfloat16
HIGHEST = lax.Precision.HIGHEST

D_MODEL = 1024
DEPTH = 4
SSD_HEADS = 16
SSD_HEAD_DIM = 64
SSD_WIDTH = 1024
SSD_GROUPS = 2
SSD_STATE = 128
SSD_BC = 256
SSD_XBC = 1536
SSD_CONV = 4
GDN_HEADS = 4
GDN_HEAD_DIM = 128
GDN_WIDTH = 512
GDN_CONV = 4
GDN_CHUNK = 64
HG_HEADS = 4
HG_HEAD_DIM = 128
HG_WIDTH = 512
D_MIX = 2048
MOE_GROUPS = 4
EXPERTS_PER_GROUP = 4
N_EXPERTS = 16
D_EXPERT = 256
DN_ALPHA = (2 * DEPTH) ** 0.25

LANES = 128
SUBLANES = 8
ROWS = 128
VMEM_LIMIT = 48 * 1024 * 1024

OFF_XBC = 0
OFF_QKV = 1536
OFF_Z = 3072
OFF_GGATE = 4096
OFF_HQ = 4608
OFF_HF = 5120
OFF_HI = 5632
OFF_HGATE = 6144
OFF_SMALL = 6656
PROJ_COLS = 6912
SM_DT = 0
SM_B = 16
SM_A = 20


def _dot(a, b):
    return jnp.dot(a.astype(BF16), b.astype(BF16), preferred_element_type=F32)


def _dot_nt(a, b):
    return lax.dot_general(a.astype(BF16), b.astype(BF16), (((1,), (1,)), ((), ())),
                           preferred_element_type=F32)


def _dot_tn(a, b):
    return lax.dot_general(a.astype(BF16), b.astype(BF16), (((0,), (0,)), ((), ())),
                           preferred_element_type=F32)


def _dot_hi(a, b):
    return jnp.dot(a, b, precision=HIGHEST, preferred_element_type=F32)


def _split(a):
    hi = a.astype(BF16)
    return hi, (a - hi.astype(F32)).astype(BF16)


def _dot3(a, b):
    (ah, al), (bh, bl) = a, b
    return (jnp.dot(ah, bh, preferred_element_type=F32) + jnp.dot(ah, bl, preferred_element_type=F32)
            + jnp.dot(al, bh, preferred_element_type=F32))


def _dot2(a, b01):
    hi = a.astype(BF16)
    lo = (a - hi.astype(F32)).astype(BF16)
    b = b01.astype(BF16)
    return (jnp.dot(hi, b, preferred_element_type=F32) + jnp.dot(lo, b, preferred_element_type=F32))


def _sigmoid(x):
    return 1.0 / (1.0 + jnp.exp(-x))


def _silu(x):
    return x * _sigmoid(x)


def _softplus(x):
    return jnp.maximum(x, 0.0) + jnp.log1p(jnp.exp(-jnp.abs(x)))


def _iota2(shape, dim):
    return lax.broadcasted_iota(jnp.int32, shape, dim)


def _params(sem):
    return pltpu.CompilerParams(dimension_semantics=sem, vmem_limit_bytes=VMEM_LIMIT)


def _matmul_kernel(x_ref, w_ref, o_ref):
    o_ref[...] = jnp.dot(x_ref[...], w_ref[...], preferred_element_type=F32)


def _matmul(x, w, tm, tn):
    t, k = x.shape
    n = w.shape[1]
    return pl.pallas_call(
        _matmul_kernel,
        out_shape=jax.ShapeDtypeStruct((t, n), F32),
        grid=(n // tn, t // tm),
        in_specs=[pl.BlockSpec((tm, k), lambda j, i: (i, 0)),
                  pl.BlockSpec((k, tn), lambda j, i: (0, j))],
        out_specs=pl.BlockSpec((tm, tn), lambda j, i: (i, j)),
        compiler_params=_params(("parallel", "arbitrary")),
        name="in_proj",
    )(x, w)


def _causal_conv(x, xpad_ref, cw, first):
    rows = x.shape[0]

    @pl.when(first)
    def _():
        xpad_ref[0:SUBLANES, :] = jnp.zeros((SUBLANES, x.shape[1]), F32)

    xpad_ref[SUBLANES:SUBLANES + rows, :] = x
    k = cw.shape[0]
    acc = x * cw[k - 1:k, :]
    for i in range(k - 1):
        off = SUBLANES - (k - 1) + i
        acc = acc + xpad_ref[off:off + rows, :] * cw[i:i + 1, :]
    xpad_ref[0:SUBLANES, :] = xpad_ref[rows:rows + SUBLANES, :]
    return acc


def _ssd_kernel(z_ref, xbc_ref, sm_ref, cw_ref, cb_ref, hp_ref, hpt_ref, dsk_ref, nw_ref,
                y_ref, xpad_ref, st_ref):
    c = pl.program_id(1)

    @pl.when(c == 0)
    def _():
        st_ref[...] = jnp.zeros(st_ref.shape, F32)

    xa = _causal_conv(xbc_ref[...], xpad_ref, cw_ref[...], c == 0) + cb_ref[...]
    xa = _silu(xa)
    xs = xa[:, 0:SSD_WIDTH]

    row = _iota2((ROWS, ROWS), 0)
    col = _iota2((ROWS, ROWS), 1)
    causal = row >= col
    tril = causal.astype(F32)
    triu = (row <= col).astype(F32)

    sm = sm_ref[...]
    dt = _softplus(sm + hp_ref[0:1, :])
    da = dt * (-jnp.exp(hp_ref[1:2, :]))
    acum = _dot_hi(tril, da)
    smt = sm.T
    dtt = _softplus(smt[0:SSD_HEADS, :] + hpt_ref[0:SSD_HEADS, 0:1])
    dat = dtt * (-jnp.exp(hpt_ref[0:SSD_HEADS, 1:2]))
    acumt = _dot_hi(dat, triu)

    hrow = _iota2((LANES, SSD_WIDTH), 0)
    hcol = _iota2((LANES, SSD_WIDTH), 1)
    expand = ((hcol >> 6) == hrow).astype(F32)
    dtx = _dot2(dt, expand)
    eax = _dot2(jnp.exp(acum), expand)
    tex = _dot2(jnp.exp(acum[ROWS - 1:ROWS, :] - acum), expand)
    xdt = xs * dtx
    xdt_b = xdt.astype(BF16)
    xend_b = (xdt * tex).astype(BF16)
    lane_lo = _iota2((ROWS, LANES), 1) < SSD_HEAD_DIM

    hg = SSD_HEADS // SSD_GROUPS
    gw = SSD_WIDTH // SSD_GROUPS
    y_parts = []
    for g in range(SSD_GROUPS):
        bm = xa[:, SSD_WIDTH + g * SSD_STATE:SSD_WIDTH + (g + 1) * SSD_STATE]
        cm = xa[:, SSD_WIDTH + SSD_BC + g * SSD_STATE:SSD_WIDTH + SSD_BC + (g + 1) * SSD_STATE]
        cm_b = cm.astype(BF16)
        cb = _dot_nt(cm_b, bm)
        yd = []
        for pair in range(hg // 2):
            res = []
            for sub in range(2):
                h = g * hg + pair * 2 + sub
                seg = acum[:, h:h + 1] - acumt[h:h + 1, :]
                lmat = cb * jnp.exp(jnp.where(causal, seg, -jnp.inf))
                c0 = (g * hg + pair * 2) * SSD_HEAD_DIM
                res.append(jnp.dot(lmat.astype(BF16), xdt_b[:, c0:c0 + LANES],
                                   preferred_element_type=F32))
            yd.append(jnp.where(lane_lo, res[0], res[1]))
        yd = jnp.concatenate(yd, axis=1)
        st = st_ref[g]
        yoff = jnp.dot(cm_b, st.astype(BF16), preferred_element_type=F32) * eax[:, g * gw:(g + 1) * gw]
        st_ref[g] = (st * eax[ROWS - 1:ROWS, g * gw:(g + 1) * gw]
                     + jnp.dot(bm.T.astype(BF16), xend_b[:, g * gw:(g + 1) * gw],
                               preferred_element_type=F32))
        y = yd + yoff + xs[:, g * gw:(g + 1) * gw] * dsk_ref[:, g * gw:(g + 1) * gw]
        y = y * _silu(z_ref[:, g * gw:(g + 1) * gw])
        ms = jnp.mean(y * y, axis=-1, keepdims=True)
        y_parts.append(y * lax.rsqrt(ms + 1e-6) * nw_ref[:, g * gw:(g + 1) * gw])
    y_ref[...] = jnp.concatenate(y_parts, axis=1).astype(y_ref.dtype)


def _ssd_mixer(proj, nb, seq, conv_w, conv_b, dt_bias, a_log, d_skip, norm_w):
    nblk = seq // ROWS
    hp = jnp.zeros((SUBLANES, LANES), F32)
    hp = hp.at[0, SM_DT:SM_DT + SSD_HEADS].set(dt_bias).at[1, SM_DT:SM_DT + SSD_HEADS].set(a_log)
    hpt = hp.T
    dsk = jnp.repeat(d_skip, SSD_HEAD_DIM)[None, :]
    rowmap = lambda b, c: (b * nblk + c, 0)
    const = lambda b, c: (0, 0)
    return pl.pallas_call(
        _ssd_kernel,
        out_shape=jax.ShapeDtypeStruct((nb * seq, SSD_WIDTH), BF16),
        grid=(nb, nblk),
        in_specs=[
            pl.BlockSpec((ROWS, SSD_WIDTH), lambda b, c: (b * nblk + c, OFF_Z // SSD_WIDTH)),
            pl.BlockSpec((ROWS, SSD_XBC), lambda b, c: (b * nblk + c, OFF_XBC // SSD_XBC)),
            pl.BlockSpec((ROWS, LANES), lambda b, c: (b * nblk + c, OFF_SMALL // LANES)),
            pl.BlockSpec((SSD_CONV, SSD_XBC), const),
            pl.BlockSpec((1, SSD_XBC), const),
            pl.BlockSpec((SUBLANES, LANES), const),
            pl.BlockSpec((LANES, SUBLANES), const),
            pl.BlockSpec((1, SSD_WIDTH), const),
            pl.BlockSpec((1, SSD_WIDTH), const),
        ],
        out_specs=pl.BlockSpec((ROWS, SSD_WIDTH), rowmap),
        scratch_shapes=[pltpu.VMEM((ROWS + SUBLANES, SSD_XBC), F32),
                        pltpu.VMEM((SSD_GROUPS, SSD_STATE, SSD_WIDTH // SSD_GROUPS), F32)],
        compiler_params=_params(("parallel", "arbitrary")),
        name="ssd_mixer",
    )(proj, proj, proj, conv_w, conv_b[None, :], hp, hpt, dsk, norm_w[None, :])


def _gdn_kernel(qkv_ref, gate_ref, sm_ref, cw_ref, hp_ref, hpt_ref, nw_ref, y_ref, xpad_ref, st_ref):
    c = pl.program_id(1)

    @pl.when(c == 0)
    def _():
        st_ref[...] = jnp.zeros(st_ref.shape, F32)

    xa = _silu(_causal_conv(qkv_ref[...], xpad_ref, cw_ref[...], c == 0))

    row = _iota2((ROWS, ROWS), 0)
    col = _iota2((ROWS, ROWS), 1)
    incl = row >= col
    strict = row > col
    eye = (row == col).astype(F32)
    tril = incl.astype(F32)
    triu = (row <= col).astype(F32)

    sm = sm_ref[...]
    la = -jnp.exp(hp_ref[1:2, :]) * _softplus(sm + hp_ref[0:1, :])
    gcum = _dot_hi(tril, la)
    smt = sm.T
    lat = (-jnp.exp(hpt_ref[SM_B:SM_B + SUBLANES, 1:2])
           * _softplus(smt[SM_B:SM_B + SUBLANES, :] + hpt_ref[SM_B:SM_B + SUBLANES, 0:1]))
    gcumt = _dot_hi(lat, triu)
    beta_all = _sigmoid(sm)

    dh = GDN_HEAD_DIM
    heads = range(GDN_HEADS)
    qs, ks, xs, rhss, decays, gs = [], [], [], [], [], []
    for h in heads:
        q = xa[:, h * dh:(h + 1) * dh]
        k = xa[:, GDN_WIDTH + h * dh:GDN_WIDTH + (h + 1) * dh]
        v = xa[:, 2 * GDN_WIDTH + h * dh:2 * GDN_WIDTH + (h + 1) * dh]
        q = q * lax.rsqrt(jnp.sum(q * q, axis=-1, keepdims=True) + 1e-6) * (dh ** -0.5)
        k = k * lax.rsqrt(jnp.sum(k * k, axis=-1, keepdims=True) + 1e-6)
        g = gcum[:, SM_A + h:SM_A + h + 1]
        gt = gcumt[SM_A - SM_B + h:SM_A - SM_B + h + 1, :]
        beta = beta_all[:, SM_B + h:SM_B + h + 1]
        decay = jnp.exp(jnp.where(incl, g - gt, -jnp.inf))
        kb = k * beta
        xs.append(-jnp.where(strict, _dot_nt(kb, k) * decay, 0.0))
        rhss.append(jnp.concatenate([v * beta, kb * jnp.exp(g)], axis=1))
        qs.append(q)
        ks.append(k)
        decays.append(decay)
        gs.append(g)
    ps = [_split(x) for x in xs]
    ts = [_split(eye + x) for x in xs]
    levels = ROWS.bit_length() - 2
    for j in range(levels):
        pf = [_dot3(p, p) for p in ps]
        ps = [_split(p) for p in pf]
        ts = [_split(_dot3(t, _split(eye + p))) for t, p in zip(ts, pf)]
    sols = [_dot3(t, _split(r)) for t, r in zip(ts, rhss)]
    qks = [_dot_nt(q, k) * d for q, k, d in zip(qs, ks, decays)]
    sts = [st_ref[h] for h in heads]
    v_news = [s[:, 0:dh] - _dot(s[:, dh:2 * dh], st) for s, st in zip(sols, sts)]
    os_ = [_dot(q * jnp.exp(g), st) + _dot(qk, vn) for q, g, st, qk, vn in zip(qs, gs, sts, qks, v_news)]
    for h in heads:
        glast = gs[h][ROWS - 1:ROWS, :]
        kd = ks[h] * jnp.exp(glast - gs[h])
        st_ref[h] = sts[h] * jnp.exp(glast) + _dot(kd.T, v_news[h])
    outs = []
    for h in heads:
        o = os_[h]
        o = o * lax.rsqrt(jnp.mean(o * o, axis=-1, keepdims=True) + 1e-6)
        outs.append(o * nw_ref[...] * _silu(gate_ref[:, h * dh:(h + 1) * dh]))
    y_ref[...] = jnp.concatenate(outs, axis=1).astype(y_ref.dtype)


def _gdn_mixer(proj, nb, seq, conv_w, dt_bias, a_log, norm_w):
    nblk = seq // ROWS
    hp = jnp.zeros((SUBLANES, LANES), F32)
    hp = hp.at[0, SM_A:SM_A + GDN_HEADS].set(dt_bias).at[1, SM_A:SM_A + GDN_HEADS].set(a_log)
    hpt = hp.T
    rowmap = lambda b, c: (b * nblk + c, 0)
    const = lambda b, c: (0, 0)
    return pl.pallas_call(
        _gdn_kernel,
        out_shape=jax.ShapeDtypeStruct((nb * seq, GDN_WIDTH), BF16),
        grid=(nb, nblk),
        in_specs=[
            pl.BlockSpec((ROWS, 3 * GDN_WIDTH), lambda b, c: (b * nblk + c, OFF_QKV // (3 * GDN_WIDTH))),
            pl.BlockSpec((ROWS, GDN_WIDTH), lambda b, c: (b * nblk + c, OFF_GGATE // GDN_WIDTH)),
            pl.BlockSpec((ROWS, LANES), lambda b, c: (b * nblk + c, OFF_SMALL // LANES)),
            pl.BlockSpec((GDN_CONV, 3 * GDN_WIDTH), const),
            pl.BlockSpec((SUBLANES, LANES), const),
            pl.BlockSpec((LANES, SUBLANES), const),
            pl.BlockSpec((1, GDN_HEAD_DIM), const),
        ],
        out_specs=pl.BlockSpec((ROWS, GDN_WIDTH), rowmap),
        scratch_shapes=[pltpu.VMEM((ROWS + SUBLANES, 3 * GDN_WIDTH), F32),
                        pltpu.VMEM((GDN_HEADS, GDN_HEAD_DIM, GDN_HEAD_DIM), F32)],
        compiler_params=_params(("parallel", "arbitrary")),
        name="gdn_mixer",
    )(proj, proj, proj, conv_w, hp, hpt, norm_w[None, :])


def _hgrn_kernel(q_ref, f_ref, i_ref, gate_ref, lb_ref, nw_ref, y_ref, st_ref):
    c = pl.program_id(1)

    @pl.when(c == 0)
    def _():
        st_ref[...] = jnp.zeros(st_ref.shape, F32)

    row = _iota2((ROWS, ROWS), 0)
    col = _iota2((ROWS, ROWS), 1)
    tril = (row >= col).astype(F32)
    ones_b = jnp.ones((LANES, LANES), BF16)
    rmod = _iota2((ROWS, LANES), 0) & (SUBLANES - 1)
    rfull = _iota2((ROWS, LANES), 0)
    dk = HG_HEAD_DIM
    outs = []
    for h in range(HG_HEADS):
        sl = slice(h * dk, (h + 1) * dk)
        lb = lb_ref[:, sl]
        fr = f_ref[:, sl]
        log_sig = jnp.minimum(fr, 0.0) - jnp.log1p(jnp.exp(-jnp.abs(fr)))
        a = jnp.log(lb)
        y = jnp.log1p(-lb) + log_sig
        log_f = jnp.maximum(a, y) + jnp.log1p(jnp.exp(-jnp.abs(a - y)))
        k = (1.0 - lb) * _sigmoid(-fr)
        q = _silu(q_ref[:, sl])
        v = i_ref[:, sl]
        b = _dot_hi(tril, log_f)

        st = st_ref[h]
        o = _dot_nt(q * jnp.exp(b), st)
        blast = b[ROWS - 1:ROWS, :]
        st_ref[h] = st * jnp.exp(blast) + _dot_tn(v, k * jnp.exp(blast - b))

        pmat = jnp.zeros((ROWS, ROWS), F32)
        m = SUBLANES
        while m < ROWS:
            ref = jnp.concatenate(
                [jnp.broadcast_to(b[p0 + m - 1:p0 + m, :], (2 * m, dk)) for p0 in range(0, ROWS, 2 * m)], axis=0)
            right = (rfull & m) != 0
            qs = q * jnp.exp(jnp.where(right, b - ref, -jnp.inf))
            ks = k * jnp.exp(jnp.where(right, -jnp.inf, ref - b))
            parent = ~(2 * m - 1)
            pmat = pmat + jnp.where((row & parent) == (col & parent), _dot_nt(qs, ks), 0.0)
            m *= 2
        o = o + _dot(pmat, v)

        for d in range(SUBLANES):
            if d == 0:
                e = q * k
                vd = v
            else:
                kd = pltpu.roll(k, d, 0)
                bd = pltpu.roll(b, d, 0)
                vd = pltpu.roll(v, d, 0)
                e = q * kd * jnp.exp(jnp.where(rmod >= d, b - bd, -jnp.inf))
            score = jnp.dot(e.astype(BF16), ones_b, preferred_element_type=F32)
            o = o + score * vd

        o = o * lax.rsqrt(jnp.mean(o * o, axis=-1, keepdims=True) + 1e-6)
        outs.append(o * nw_ref[...] * _silu(gate_ref[:, sl]))
    y_ref[...] = jnp.concatenate(outs, axis=1).astype(y_ref.dtype)


def _hgrn_mixer(proj, nb, seq, lb, norm_w):
    nblk = seq // ROWS
    const = lambda b, c: (0, 0)

    def colspec(off):
        return pl.BlockSpec((ROWS, HG_WIDTH), lambda b, c: (b * nblk + c, off // HG_WIDTH))

    return pl.pallas_call(
        _hgrn_kernel,
        out_shape=jax.ShapeDtypeStruct((nb * seq, HG_WIDTH), BF16),
        grid=(nb, nblk),
        in_specs=[colspec(OFF_HQ), colspec(OFF_HF), colspec(OFF_HI), colspec(OFF_HGATE),
                  pl.BlockSpec((1, HG_WIDTH), const),
                  pl.BlockSpec((1, HG_HEAD_DIM), const)],
        out_specs=pl.BlockSpec((ROWS, HG_WIDTH), lambda b, c: (b * nblk + c, 0)),
        scratch_shapes=[pltpu.VMEM((HG_HEADS, HG_HEAD_DIM, HG_HEAD_DIM), F32)],
        compiler_params=_params(("parallel", "arbitrary")),
        name="hgrn2_mixer",
    )(proj, proj, proj, proj, lb[None, :], norm_w[None, :])


def _layer_norm(x, g, b):
    mu = jnp.mean(x, axis=-1, keepdims=True)
    xc = x - mu
    var = jnp.mean(xc * xc, axis=-1, keepdims=True)
    return xc * lax.rsqrt(var + 1e-5) * g + b


def _masked_max(x, mask):
    return jnp.max(jnp.where(mask, x, -jnp.inf), axis=-1, keepdims=True)


def _first_lane(cond, lane):
    return jnp.min(jnp.where(cond, lane, float(LANES)), axis=-1, keepdims=True)


def _outproj_kernel(ys_ref, yg_ref, yh_ref, h_ref, w_ref, g_ref, b_ref, wr_ref, br_ref,
                    h1_ref, h1b_ref, comb_ref):
    mix = jnp.dot(ys_ref[...], w_ref[0:SSD_WIDTH, :], preferred_element_type=F32)
    mix = mix + jnp.dot(yg_ref[...], w_ref[SSD_WIDTH:SSD_WIDTH + GDN_WIDTH, :], preferred_element_type=F32)
    mix = mix + jnp.dot(yh_ref[...], w_ref[SSD_WIDTH + GDN_WIDTH:D_MIX, :], preferred_element_type=F32)
    h1 = _layer_norm(DN_ALPHA * h_ref[...] + mix, g_ref[...], b_ref[...])
    h1_ref[...] = h1
    h1b_ref[...] = h1.astype(BF16)

    h_hi, h_lo = _split(h1)
    wr = wr_ref[...]
    l2 = jnp.dot(h_hi, wr, preferred_element_type=F32) + jnp.dot(h_lo, wr, preferred_element_type=F32)
    logits = l2[:, 0:LANES] + l2[:, LANES:2 * LANES] + br_ref[...]
    lane_i = _iota2(logits.shape, 1)
    lane = lane_i.astype(F32)
    gmask = lane_i < MOE_GROUPS
    gmax = _masked_max(logits, gmask)
    gexp = jnp.where(gmask, jnp.exp(logits - gmax), 0.0)
    gprob = gexp / jnp.sum(gexp, axis=-1, keepdims=True)
    g_p = _masked_max(gprob, gmask)
    g_idx = _first_lane(gmask & (gprob == g_p), lane)
    egroup = ((lane_i - MOE_GROUPS) >> 2).astype(F32)
    emask = (lane_i >= MOE_GROUPS) & (lane_i < MOE_GROUPS + N_EXPERTS) & (egroup == g_idx)
    emax = _masked_max(logits, emask)
    eexp = jnp.where(emask, jnp.exp(logits - emax), 0.0)
    eprob = eexp / jnp.sum(eexp, axis=-1, keepdims=True)
    p1 = _masked_max(eprob, emask)
    i1 = _first_lane(emask & (eprob == p1), lane)
    emask2 = emask & (lane != i1)
    p2 = _masked_max(eprob, emask2)
    i2 = _first_lane(emask2 & (eprob == p2), lane)
    denom = p1 + p2
    w1 = g_p * p1 / denom
    w2 = g_p * p2 / denom
    comb_ref[...] = (jnp.where(lane == i1 - MOE_GROUPS, w1, 0.0)
                     + jnp.where(lane == i2 - MOE_GROUPS, w2, 0.0))


def _outproj(ys, yg, yh, h, w_out, ln_g, ln_b, w_router, b_router, tm):
    t = h.shape[0]
    rowmap = lambda i: (i, 0)
    const = lambda i: (0, 0)
    return pl.pallas_call(
        _outproj_kernel,
        out_shape=(jax.ShapeDtypeStruct((t, D_MODEL), F32),
                   jax.ShapeDtypeStruct((t, D_MODEL), BF16),
                   jax.ShapeDtypeStruct((t, LANES), F32)),
        grid=(t // tm,),
        in_specs=[pl.BlockSpec((tm, SSD_WIDTH), rowmap),
                  pl.BlockSpec((tm, GDN_WIDTH), rowmap),
                  pl.BlockSpec((tm, HG_WIDTH), rowmap),
                  pl.BlockSpec((tm, D_MODEL), rowmap),
                  pl.BlockSpec((D_MIX, D_MODEL), const),
                  pl.BlockSpec((1, D_MODEL), const),
                  pl.BlockSpec((1, D_MODEL), const),
                  pl.BlockSpec((D_MODEL, 2 * LANES), const),
                  pl.BlockSpec((1, LANES), const)],
        out_specs=(pl.BlockSpec((tm, D_MODEL), rowmap),
                   pl.BlockSpec((tm, D_MODEL), rowmap),
                   pl.BlockSpec((tm, LANES), rowmap)),
        compiler_params=_params(("parallel",)),
        name="out_proj_ln_router",
    )(ys, yg, yh, h, w_out, ln_g[None, :], ln_b[None, :], w_router, b_router)


def _moe_kernel(hb_ref, h_ref, comb_ref, wgu_ref, wdn_ref, g_ref, b_ref, o_ref, ob_ref, acc_ref):
    e = pl.program_id(1)

    @pl.when(e == 0)
    def _():
        acc_ref[...] = jnp.zeros(acc_ref.shape, F32)

    gu = jnp.dot(hb_ref[...], wgu_ref[0], preferred_element_type=F32)
    comb = comb_ref[...]
    lane = _iota2(comb.shape, 1)
    cw = jnp.sum(jnp.where(lane == e, comb, 0.0), axis=-1, keepdims=True)
    hmid = _silu(gu[:, 0:D_EXPERT]) * gu[:, D_EXPERT:2 * D_EXPERT] * cw
    acc_ref[...] += jnp.dot(hmid.astype(BF16), wdn_ref[0], preferred_element_type=F32)

    @pl.when(e == N_EXPERTS - 1)
    def _():
        h2 = _layer_norm(DN_ALPHA * h_ref[...] + acc_ref[...], g_ref[...], b_ref[...])
        o_ref[...] = h2
        ob_ref[...] = h2.astype(BF16)


def _moe(hb, h, comb, w_gu, w_dn, ln_g, ln_b, tm):
    t = h.shape[0]
    rowmap = lambda i, e: (i, 0)
    const = lambda i, e: (0, 0)
    return pl.pallas_call(
        _moe_kernel,
        out_shape=(jax.ShapeDtypeStruct((t, D_MODEL), F32),
                   jax.ShapeDtypeStruct((t, D_MODEL), BF16)),
        grid=(t // tm, N_EXPERTS),
        in_specs=[pl.BlockSpec((tm, D_MODEL), rowmap),
                  pl.BlockSpec((tm, D_MODEL), rowmap),
                  pl.BlockSpec((tm, LANES), rowmap),
                  pl.BlockSpec((1, D_MODEL, 2 * D_EXPERT), lambda i, e: (e, 0, 0)),
                  pl.BlockSpec((1, D_EXPERT, D_MODEL), lambda i, e: (e, 0, 0)),
                  pl.BlockSpec((1, D_MODEL), const),
                  pl.BlockSpec((1, D_MODEL), const)],
        out_specs=(pl.BlockSpec((tm, D_MODEL), rowmap),
                   pl.BlockSpec((tm, D_MODEL), rowmap)),
        scratch_shapes=[pltpu.VMEM((tm, D_MODEL), F32)],
        compiler_params=_params(("parallel", "arbitrary")),
        name="moe_ln",
    )(hb, h, comb, w_gu, w_dn, ln_g[None, :], ln_b[None, :])


def _split_w_in(w):
    sizes = (SSD_WIDTH, SSD_XBC, SSD_HEADS, 3 * GDN_WIDTH, GDN_WIDTH, GDN_HEADS, GDN_HEADS,
             HG_WIDTH, HG_WIDTH, HG_WIDTH, HG_WIDTH)
    parts, acc = [], 0
    for s in sizes:
        parts.append(w[:, acc:acc + s])
        acc += s
    z, xbc, dt, qkv, ggate, gb, ga, hq, hf, hi, hgate = parts
    cols = jnp.concatenate([xbc, qkv, z, ggate, hq, hf, hi, hgate, dt, gb, ga], axis=1).astype(BF16)
    return jnp.pad(cols, ((0, 0), (0, PROJ_COLS - cols.shape[1])))


def kernel(x, w_in, ssd_conv_w, ssd_conv_b, ssd_dt_bias, ssd_a_log, ssd_d, ssd_norm_w, gdn_conv_w, gdn_dt_bias, gdn_a_log, gdn_norm_w, hg_lb_logits, hg_norm_w, w_out, ln1_g, ln1_b, w_router_group, b_router_group, w_router_expert, b_router_expert, w_expert_gate_up, w_expert_down, ln2_g, ln2_b):
    nb, seq, d = x.shape
    t = nb * seq
    lb_cum = jnp.cumsum(jax.nn.softmax(hg_lb_logits.astype(F32), axis=0), axis=0)
    lb_all = lb_cum - lb_cum[0:1]
    h = x.reshape(t, d)
    hb = h.astype(BF16)
    for l in range(DEPTH):
        proj = _matmul(hb, _split_w_in(w_in[l]), 512, PROJ_COLS // 3)
        y_ssd = _ssd_mixer(proj, nb, seq, ssd_conv_w[l], ssd_conv_b[l], ssd_dt_bias[l],
                           ssd_a_log[l], ssd_d[l], ssd_norm_w[l])
        y_gdn = _gdn_mixer(proj, nb, seq, gdn_conv_w[l], gdn_dt_bias[l], gdn_a_log[l], gdn_norm_w[l])
        y_hg = _hgrn_mixer(proj, nb, seq, lb_all[l], hg_norm_w[l])
        w_router = jnp.pad(jnp.concatenate([w_router_group[l], w_router_expert[l]], axis=1),
                           ((0, 0), (0, LANES - MOE_GROUPS - N_EXPERTS)))
        w_router = jnp.concatenate(_split(w_router), axis=1)
        b_router = jnp.pad(jnp.concatenate([b_router_group[l], b_router_expert[l]]),
                           (0, LANES - MOE_GROUPS - N_EXPERTS))[None, :]
        h1, h1b, comb = _outproj(y_ssd, y_gdn, y_hg, h, w_out[l].astype(BF16), ln1_g[l], ln1_b[l],
                                 w_router, b_router, 512)
        h, hb = _moe(h1b, h1, comb, w_expert_gate_up[l].astype(BF16), w_expert_down[l].astype(BF16),
                     ln2_g[l], ln2_b[l], 512)
    return h.reshape(nb, seq, d)
```

```python
import functools

import jax
import jax.numpy as jnp
from jax import lax
from jax.experimental import pallas as pl
from jax.experimental.pallas import tpu as pltpu

F32 = jnp.float32
BF16 = jnp.bfloat16
HIGHEST = lax.Precision.HIGHEST

D_MODEL = 1024
DEPTH = 4
SSD_HEADS = 16
SSD_HEAD_DIM = 64
SSD_WIDTH = 1024
SSD_GROUPS = 2
SSD_STATE = 128
SSD_BC = 256
SSD_XBC = 1536
SSD_CONV = 4
GDN_HEADS = 4
GDN_HEAD_DIM = 128
GDN_WIDTH = 512
GDN_CONV = 4
GDN_CHUNK = 64
HG_HEADS = 4
HG_HEAD_DIM = 128
HG_WIDTH = 512
D_MIX = 2048
MOE_GROUPS = 4
EXPERTS_PER_GROUP = 4
N_EXPERTS = 16
D_EXPERT = 256
DN_ALPHA = (2 * DEPTH) ** 0.25

LANES = 128
SUBLANES = 8
ROWS = 128
VMEM_LIMIT = 48 * 1024 * 1024
MOE_VMEM_LIMIT = 56 * 1024 * 1024

OFF_XBC = 0
OFF_QKV = 1536
OFF_Z = 3072
OFF_GGATE = 4096
OFF_HQ = 4608
OFF_HF = 5120
OFF_HI = 5632
OFF_HGATE = 6144
OFF_SMALL = 6656
PROJ_COLS = 6912
SM_DT = 0
SM_B = 16
SM_A = 20


def _dot(a, b):
    return jnp.dot(a.astype(BF16), b.astype(BF16), preferred_element_type=F32)


def _dot_nt(a, b):
    return lax.dot_general(a.astype(BF16), b.astype(BF16), (((1,), (1,)), ((), ())),
                           preferred_element_type=F32)


def _dot_tn(a, b):
    return lax.dot_general(a.astype(BF16), b.astype(BF16), (((0,), (0,)), ((), ())),
                           preferred_element_type=F32)


def _dot_hi(a, b):
    return jnp.dot(a, b, precision=HIGHEST, preferred_element_type=F32)


def _split(a):
    hi = a.astype(BF16)
    return hi, (a - hi.astype(F32)).astype(BF16)


def _dot3(a, b):
    (ah, al), (bh, bl) = a, b
    return (jnp.dot(ah, bh, preferred_element_type=F32) + jnp.dot(ah, bl, preferred_element_type=F32)
            + jnp.dot(al, bh, preferred_element_type=F32))


def _dot2(a, b01):
    hi = a.astype(BF16)
    lo = (a - hi.astype(F32)).astype(BF16)
    b = b01.astype(BF16)
    return (jnp.dot(hi, b, preferred_element_type=F32) + jnp.dot(lo, b, preferred_element_type=F32))


def _sigmoid(x):
    return 1.0 / (1.0 + jnp.exp(-x))


def _silu(x):
    return x * _sigmoid(x)


def _softplus(x):
    return jnp.maximum(x, 0.0) + jnp.log1p(jnp.exp(-jnp.abs(x)))


def _iota2(shape, dim):
    return lax.broadcasted_iota(jnp.int32, shape, dim)


def _params(sem):
    return pltpu.CompilerParams(dimension_semantics=sem, vmem_limit_bytes=VMEM_LIMIT)


def _matmul_kernel(x_ref, w_ref, o_ref):
    o_ref[...] = jnp.dot(x_ref[...], w_ref[...], preferred_element_type=F32)


def _matmul(x, w, tm, tn):
    t, k = x.shape
    n = w.shape[1]
    return pl.pallas_call(
        _matmul_kernel,
        out_shape=jax.ShapeDtypeStruct((t, n), F32),
        grid=(n // tn, t // tm),
        in_specs=[pl.BlockSpec((tm, k), lambda j, i: (i, 0)),
                  pl.BlockSpec((k, tn), lambda j, i: (0, j))],
        out_specs=pl.BlockSpec((tm, tn), lambda j, i: (i, j)),
        compiler_params=_params(("parallel", "arbitrary")),
        name="in_proj",
    )(x, w)


def _causal_conv(x, xpad_ref, cw, first):
    rows = x.shape[0]

    @pl.when(first)
    def _():
        xpad_ref[0:SUBLANES, :] = jnp.zeros((SUBLANES, x.shape[1]), F32)

    xpad_ref[SUBLANES:SUBLANES + rows, :] = x
    k = cw.shape[0]
    acc = x * cw[k - 1:k, :]
    for i in range(k - 1):
        off = SUBLANES - (k - 1) + i
        acc = acc + xpad_ref[off:off + rows, :] * cw[i:i + 1, :]
    xpad_ref[0:SUBLANES, :] = xpad_ref[rows:rows + SUBLANES, :]
    return acc


def _ssd_kernel(z_ref, xbc_ref, sm_ref, cw_ref, cb_ref, hp_ref, hpt_ref, dsk_ref, nw_ref,
                y_ref, xpad_ref, st_ref):
    c = pl.program_id(1)

    @pl.when(c == 0)
    def _():
        st_ref[...] = jnp.zeros(st_ref.shape, F32)

    xa = _causal_conv(xbc_ref[...], xpad_ref, cw_ref[...], c == 0) + cb_ref[...]
    xa = _silu(xa)
    xs = xa[:, 0:SSD_WIDTH]

    row = _iota2((ROWS, ROWS), 0)
    col = _iota2((ROWS, ROWS), 1)
    causal = row >= col
    tril = causal.astype(F32)
    triu = (row <= col).astype(F32)

    sm = sm_ref[...]
    dt = _softplus(sm + hp_ref[0:1, :])
    da = dt * (-jnp.exp(hp_ref[1:2, :]))
    acum = _dot_hi(tril, da)
    smt = sm.T
    dtt = _softplus(smt[0:SSD_HEADS, :] + hpt_ref[0:SSD_HEADS, 0:1])
    dat = dtt * (-jnp.exp(hpt_ref[0:SSD_HEADS, 1:2]))
    acumt = _dot_hi(dat, triu)

    hrow = _iota2((LANES, SSD_WIDTH), 0)
    hcol = _iota2((LANES, SSD_WIDTH), 1)
    expand = ((hcol >> 6) == hrow).astype(F32)
    dtx = _dot2(dt, expand)
    eax = _dot2(jnp.exp(acum), expand)
    tex = _dot2(jnp.exp(acum[ROWS - 1:ROWS, :] - acum), expand)
    xdt = xs * dtx
    xdt_b = xdt.astype(BF16)
    xend_b = (xdt * tex).astype(BF16)
    lane_lo = _iota2((ROWS, LANES), 1) < SSD_HEAD_DIM

    hg = SSD_HEADS // SSD_GROUPS
    gw = SSD_WIDTH // SSD_GROUPS
    y_parts = []
    for g in range(SSD_GROUPS):
        bm = xa[:, SSD_WIDTH + g * SSD_STATE:SSD_WIDTH + (g + 1) * SSD_STATE]
        cm = xa[:, SSD_WIDTH + SSD_BC + g * SSD_STATE:SSD_WIDTH + SSD_BC + (g + 1) * SSD_STATE]
        cm_b = cm.astype(BF16)
        cb = _dot_nt(cm_b, bm)
        yd = []
        for pair in range(hg // 2):
            res = []
            for sub in range(2):
                h = g * hg + pair * 2 + sub
                seg = acum[:, h:h + 1] - acumt[h:h + 1, :]
                lmat = cb * jnp.exp(jnp.where(causal, seg, -jnp.inf))
                c0 = (g * hg + pair * 2) * SSD_HEAD_DIM
                res.append(jnp.dot(lmat.astype(BF16), xdt_b[:, c0:c0 + LANES],
                                   preferred_element_type=F32))
            yd.append(jnp.where(lane_lo, res[0], res[1]))
        yd = jnp.concatenate(yd, axis=1)
        st = st_ref[g]
        yoff = jnp.dot(cm_b, st.astype(BF16), preferred_element_type=F32) * eax[:, g * gw:(g + 1) * gw]
        st_ref[g] = (st * eax[ROWS - 1:ROWS, g * gw:(g + 1) * gw]
                     + jnp.dot(bm.T.astype(BF16), xend_b[:, g * gw:(g + 1) * gw],
                               preferred_element_type=F32))
        y = yd + yoff + xs[:, g * gw:(g + 1) * gw] * dsk_ref[:, g * gw:(g + 1) * gw]
        y = y * _silu(z_ref[:, g * gw:(g + 1) * gw])
        ms = jnp.mean(y * y, axis=-1, keepdims=True)
        y_parts.append(y * lax.rsqrt(ms + 1e-6) * nw_ref[:, g * gw:(g + 1) * gw])
    y_ref[...] = jnp.concatenate(y_parts, axis=1).astype(y_ref.dtype)


def _ssd_mixer(proj, nb, seq, conv_w, conv_b, dt_bias, a_log, d_skip, norm_w):
    nblk = seq // ROWS
    hp = jnp.zeros((SUBLANES, LANES), F32)
    hp = hp.at[0, SM_DT:SM_DT + SSD_HEADS].set(dt_bias).at[1, SM_DT:SM_DT + SSD_HEADS].set(a_log)
    hpt = hp.T
    dsk = jnp.repeat(d_skip, SSD_HEAD_DIM)[None, :]
    rowmap = lambda b, c: (b * nblk + c, 0)
    const = lambda b, c: (0, 0)
    return pl.pallas_call(
        _ssd_kernel,
        out_shape=jax.ShapeDtypeStruct((nb * seq, SSD_WIDTH), BF16),
        grid=(nb, nblk),
        in_specs=[
            pl.BlockSpec((ROWS, SSD_WIDTH), lambda b, c: (b * nblk + c, OFF_Z // SSD_WIDTH)),
            pl.BlockSpec((ROWS, SSD_XBC), lambda b, c: (b * nblk + c, OFF_XBC // SSD_XBC)),
            pl.BlockSpec((ROWS, LANES), lambda b, c: (b * nblk + c, OFF_SMALL // LANES)),
            pl.BlockSpec((SSD_CONV, SSD_XBC), const),
            pl.BlockSpec((1, SSD_XBC), const),
            pl.BlockSpec((SUBLANES, LANES), const),
            pl.BlockSpec((LANES, SUBLANES), const),
            pl.BlockSpec((1, SSD_WIDTH), const),
            pl.BlockSpec((1, SSD_WIDTH), const),
        ],
        out_specs=pl.BlockSpec((ROWS, SSD_WIDTH), rowmap),
        scratch_shapes=[pltpu.VMEM((ROWS + SUBLANES, SSD_XBC), F32),
                        pltpu.VMEM((SSD_GROUPS, SSD_STATE, SSD_WIDTH // SSD_GROUPS), F32)],
        compiler_params=_params(("parallel", "arbitrary")),
        name="ssd_mixer",
    )(proj, proj, proj, conv_w, conv_b[None, :], hp, hpt, dsk, norm_w[None, :])


def _gdn_kernel(qkv_ref, gate_ref, sm_ref, cw_ref, hp_ref, hpt_ref, nw_ref, y_ref, xpad_ref, st_ref):
    c = pl.program_id(1)

    @pl.when(c == 0)
    def _():
        st_ref[...] = jnp.zeros(st_ref.shape, F32)

    xa = _silu(_causal_conv(qkv_ref[...], xpad_ref, cw_ref[...], c == 0))

    row = _iota2((ROWS, ROWS), 0)
    col = _iota2((ROWS, ROWS), 1)
    incl = row >= col
    strict = row > col
    eye = (row == col).astype(F32)
    tril = incl.astype(F32)
    triu = (row <= col).astype(F32)

    sm = sm_ref[...]
    la = -jnp.exp(hp_ref[1:2, :]) * _softplus(sm + hp_ref[0:1, :])
    gcum = _dot_hi(tril, la)
    smt = sm.T
    lat = (-jnp.exp(hpt_ref[SM_B:SM_B + SUBLANES, 1:2])
           * _softplus(smt[SM_B:SM_B + SUBLANES, :] + hpt_ref[SM_B:SM_B + SUBLANES, 0:1]))
    gcumt = _dot_hi(lat, triu)
    beta_all = _sigmoid(sm)

    dh = GDN_HEAD_DIM
    heads = range(GDN_HEADS)
    qs, ks, xs, rhss, decays, gs = [], [], [], [], [], []
    for h in heads:
        q = xa[:, h * dh:(h + 1) * dh]
        k = xa[:, GDN_WIDTH + h * dh:GDN_WIDTH + (h + 1) * dh]
        v = xa[:, 2 * GDN_WIDTH + h * dh:2 * GDN_WIDTH + (h + 1) * dh]
        q = q * lax.rsqrt(jnp.sum(q * q, axis=-1, keepdims=True) + 1e-6) * (dh ** -0.5)
        k = k * lax.rsqrt(jnp.sum(k * k, axis=-1, keepdims=True) + 1e-6)
        g = gcum[:, SM_A + h:SM_A + h + 1]
        gt = gcumt[SM_A - SM_B + h:SM_A - SM_B + h + 1, :]
        beta = beta_all[:, SM_B + h:SM_B + h + 1]
        decay = jnp.exp(jnp.where(incl, g - gt, -jnp.inf))
        kb = k * beta
        xs.append(-jnp.where(strict, _dot_nt(kb, k) * decay, 0.0))
        rhss.append(jnp.concatenate([v * beta, kb * jnp.exp(g)], axis=1))
        qs.append(q)
        ks.append(k)
        decays.append(decay)
        gs.append(g)
    ps = [_split(x) for x in xs]
    ts = [_split(eye + x) for x in xs]
    levels = ROWS.bit_length() - 2
    for j in range(levels):
        pf = [_dot3(p, p) for p in ps]
        ps = [_split(p) for p in pf]
        ts = [_split(_dot3(t, _split(eye + p))) for t, p in zip(ts, pf)]
    sols = [_dot3(t, _split(r)) for t, r in zip(ts, rhss)]
    qks = [_dot_nt(q, k) * d for q, k, d in zip(qs, ks, decays)]
    sts = [st_ref[h] for h in heads]
    v_news = [s[:, 0:dh] - _dot(s[:, dh:2 * dh], st) for s, st in zip(sols, sts)]
    os_ = [_dot(q * jnp.exp(g), st) + _dot(qk, vn) for q, g, st, qk, vn in zip(qs, gs, sts, qks, v_news)]
    for h in heads:
        glast = gs[h][ROWS - 1:ROWS, :]
        kd = ks[h] * jnp.exp(glast - gs[h])
        st_ref[h] = sts[h] * jnp.exp(glast) + _dot(kd.T, v_news[h])
    outs = []
    for h in heads:
        o = os_[h]
        o = o * lax.rsqrt(jnp.mean(o * o, axis=-1, keepdims=True) + 1e-6)
        outs.append(o * nw_ref[...] * _silu(gate_ref[:, h * dh:(h + 1) * dh]))
    y_ref[...] = jnp.concatenate(outs, axis=1).astype(y_ref.dtype)


def _gdn_mixer(proj, nb, seq, conv_w, dt_bias, a_log, norm_w):
    nblk = seq // ROWS
    hp = jnp.zeros((SUBLANES, LANES), F32)
    hp = hp.at[0, SM_A:SM_A + GDN_HEADS].set(dt_bias).at[1, SM_A:SM_A + GDN_HEADS].set(a_log)
    hpt = hp.T
    rowmap = lambda b, c: (b * nblk + c, 0)
    const = lambda b, c: (0, 0)
    return pl.pallas_call(
        _gdn_kernel,
        out_shape=jax.ShapeDtypeStruct((nb * seq, GDN_WIDTH), BF16),
        grid=(nb, nblk),
        in_specs=[
            pl.BlockSpec((ROWS, 3 * GDN_WIDTH), lambda b, c: (b * nblk + c, OFF_QKV // (3 * GDN_WIDTH))),
            pl.BlockSpec((ROWS, GDN_WIDTH), lambda b, c: (b * nblk + c, OFF_GGATE // GDN_WIDTH)),
            pl.BlockSpec((ROWS, LANES), lambda b, c: (b * nblk + c, OFF_SMALL // LANES)),
            pl.BlockSpec((GDN_CONV, 3 * GDN_WIDTH), const),
            pl.BlockSpec((SUBLANES, LANES), const),
            pl.BlockSpec((LANES, SUBLANES), const),
            pl.BlockSpec((1, GDN_HEAD_DIM), const),
        ],
        out_specs=pl.BlockSpec((ROWS, GDN_WIDTH), rowmap),
        scratch_shapes=[pltpu.VMEM((ROWS + SUBLANES, 3 * GDN_WIDTH), F32),
                        pltpu.VMEM((GDN_HEADS, GDN_HEAD_DIM, GDN_HEAD_DIM), F32)],
        compiler_params=_params(("parallel", "arbitrary")),
        name="gdn_mixer",
    )(proj, proj, proj, conv_w, hp, hpt, norm_w[None, :])


def _hgrn_kernel(q_ref, f_ref, i_ref, gate_ref, lb_ref, nw_ref, y_ref, st_ref):
    c = pl.program_id(1)

    @pl.when(c == 0)
    def _():
        st_ref[...] = jnp.zeros(st_ref.shape, F32)

    row = _iota2((ROWS, ROWS), 0)
    col = _iota2((ROWS, ROWS), 1)
    tril = (row >= col).astype(F32)
    ones_b = jnp.ones((LANES, LANES), BF16)
    rmod = _iota2((ROWS, LANES), 0) & (SUBLANES - 1)
    rfull = _iota2((ROWS, LANES), 0)
    dk = HG_HEAD_DIM
    outs = []
    for h in range(HG_HEADS):
        sl = slice(h * dk, (h + 1) * dk)
        lb = lb_ref[:, sl]
        fr = f_ref[:, sl]
        log_sig = jnp.minimum(fr, 0.0) - jnp.log1p(jnp.exp(-jnp.abs(fr)))
        a = jnp.log(lb)
        y = jnp.log1p(-lb) + log_sig
        log_f = jnp.maximum(a, y) + jnp.log1p(jnp.exp(-jnp.abs(a - y)))
        k = (1.0 - lb) * _sigmoid(-fr)
        q = _silu(q_ref[:, sl])
        v = i_ref[:, sl]
        b = _dot_hi(tril, log_f)

        st = st_ref[h]
        o = _dot_nt(q * jnp.exp(b), st)
        blast = b[ROWS - 1:ROWS, :]
        st_ref[h] = st * jnp.exp(blast) + _dot_tn(v, k * jnp.exp(blast - b))

        pmat = jnp.zeros((ROWS, ROWS), F32)
        m = SUBLANES
        while m < ROWS:
            ref = jnp.concatenate(
                [jnp.broadcast_to(b[p0 + m - 1:p0 + m, :], (2 * m, dk)) for p0 in range(0, ROWS, 2 * m)], axis=0)
            right = (rfull & m) != 0
            qs = q * jnp.exp(jnp.where(right, b - ref, -jnp.inf))
            ks = k * jnp.exp(jnp.where(right, -jnp.inf, ref - b))
            parent = ~(2 * m - 1)
            pmat = pmat + jnp.where((row & parent) == (col & parent), _dot_nt(qs, ks), 0.0)
            m *= 2
        o = o + _dot(pmat, v)

        for d in range(SUBLANES):
            if d == 0:
                e = q * k
                vd = v
            else:
                kd = pltpu.roll(k, d, 0)
                bd = pltpu.roll(b, d, 0)
                vd = pltpu.roll(v, d, 0)
                e = q * kd * jnp.exp(jnp.where(rmod >= d, b - bd, -jnp.inf))
            score = jnp.dot(e.astype(BF16), ones_b, preferred_element_type=F32)
            o = o + score * vd

        o = o * lax.rsqrt(jnp.mean(o * o, axis=-1, keepdims=True) + 1e-6)
        outs.append(o * nw_ref[...] * _silu(gate_ref[:, sl]))
    y_ref[...] = jnp.concatenate(outs, axis=1).astype(y_ref.dtype)


def _hgrn_mixer(proj, nb, seq, lb, norm_w):
    nblk = seq // ROWS
    const = lambda b, c: (0, 0)

    def colspec(off):
        return pl.BlockSpec((ROWS, HG_WIDTH), lambda b, c: (b * nblk + c, off // HG_WIDTH))

    return pl.pallas_call(
        _hgrn_kernel,
        out_shape=jax.ShapeDtypeStruct((nb * seq, HG_WIDTH), BF16),
        grid=(nb, nblk),
        in_specs=[colspec(OFF_HQ), colspec(OFF_HF), colspec(OFF_HI), colspec(OFF_HGATE),
                  pl.BlockSpec((1, HG_WIDTH), const),
                  pl.BlockSpec((1, HG_HEAD_DIM), const)],
        out_specs=pl.BlockSpec((ROWS, HG_WIDTH), lambda b, c: (b * nblk + c, 0)),
        scratch_shapes=[pltpu.VMEM((HG_HEADS, HG_HEAD_DIM, HG_HEAD_DIM), F32)],
        compiler_params=_params(("parallel", "arbitrary")),
        name="hgrn2_mixer",
    )(proj, proj, proj, proj, lb[None, :], norm_w[None, :])


def _layer_norm(x, g, b):
    mu = jnp.mean(x, axis=-1, keepdims=True)
    xc = x - mu
    var = jnp.mean(xc * xc, axis=-1, keepdims=True)
    return xc * lax.rsqrt(var + 1e-5) * g + b


def _masked_max(x, mask):
    return jnp.max(jnp.where(mask, x, -jnp.inf), axis=-1, keepdims=True)


def _first_lane(cond, lane):
    return jnp.min(jnp.where(cond, lane, float(LANES)), axis=-1, keepdims=True)


def _outproj_kernel(ys_ref, yg_ref, yh_ref, h_ref, w_ref, g_ref, b_ref, wr_ref, br_ref,
                    h1_ref, h1b_ref, comb_ref):
    mix = jnp.dot(ys_ref[...], w_ref[0:SSD_WIDTH, :], preferred_element_type=F32)
    mix = mix + jnp.dot(yg_ref[...], w_ref[SSD_WIDTH:SSD_WIDTH + GDN_WIDTH, :], preferred_element_type=F32)
    mix = mix + jnp.dot(yh_ref[...], w_ref[SSD_WIDTH + GDN_WIDTH:D_MIX, :], preferred_element_type=F32)
    h1 = _layer_norm(DN_ALPHA * h_ref[...] + mix, g_ref[...], b_ref[...])
    h1_ref[...] = h1
    h1b_ref[...] = h1.astype(BF16)

    h_hi, h_lo = _split(h1)
    wr = wr_ref[...]
    l2 = jnp.dot(h_hi, wr, preferred_element_type=F32) + jnp.dot(h_lo, wr, preferred_element_type=F32)
    logits = l2[:, 0:LANES] + l2[:, LANES:2 * LANES] + br_ref[...]
    lane_i = _iota2(logits.shape, 1)
    lane = lane_i.astype(F32)
    gmask = lane_i < MOE_GROUPS
    gmax = _masked_max(logits, gmask)
    gexp = jnp.where(gmask, jnp.exp(logits - gmax), 0.0)
    gprob = gexp / jnp.sum(gexp, axis=-1, keepdims=True)
    g_p = _masked_max(gprob, gmask)
    g_idx = _first_lane(gmask & (gprob == g_p), lane)
    egroup = ((lane_i - MOE_GROUPS) >> 2).astype(F32)
    emask = (lane_i >= MOE_GROUPS) & (lane_i < MOE_GROUPS + N_EXPERTS) & (egroup == g_idx)
    emax = _masked_max(logits, emask)
    eexp = jnp.where(emask, jnp.exp(logits - emax), 0.0)
    eprob = eexp / jnp.sum(eexp, axis=-1, keepdims=True)
    p1 = _masked_max(eprob, emask)
    i1 = _first_lane(emask & (eprob == p1), lane)
    emask2 = emask & (lane != i1)
    p2 = _masked_max(eprob, emask2)
    i2 = _first_lane(emask2 & (eprob == p2), lane)
    denom = p1 + p2
    w1 = g_p * p1 / denom
    w2 = g_p * p2 / denom
    comb_ref[...] = (jnp.where(lane == i1 - MOE_GROUPS, w1, 0.0)
                     + jnp.where(lane == i2 - MOE_GROUPS, w2, 0.0)
                     + jnp.where(lane_i == N_EXPERTS, g_idx, 0.0))


def _outproj(ys, yg, yh, h, w_out, ln_g, ln_b, w_router, b_router, tm):
    t = h.shape[0]
    rowmap = lambda i: (i, 0)
    const = lambda i: (0, 0)
    return pl.pallas_call(
        _outproj_kernel,
        out_shape=(jax.ShapeDtypeStruct((t, D_MODEL), F32),
                   jax.ShapeDtypeStruct((t, D_MODEL), BF16),
                   jax.ShapeDtypeStruct((t, LANES), F32)),
        grid=(t // tm,),
        in_specs=[pl.BlockSpec((tm, SSD_WIDTH), rowmap),
                  pl.BlockSpec((tm, GDN_WIDTH), rowmap),
                  pl.BlockSpec((tm, HG_WIDTH), rowmap),
                  pl.BlockSpec((tm, D_MODEL), rowmap),
                  pl.BlockSpec((D_MIX, D_MODEL), const),
                  pl.BlockSpec((1, D_MODEL), const),
                  pl.BlockSpec((1, D_MODEL), const),
                  pl.BlockSpec((D_MODEL, 2 * LANES), const),
                  pl.BlockSpec((1, LANES), const)],
        out_specs=(pl.BlockSpec((tm, D_MODEL), rowmap),
                   pl.BlockSpec((tm, D_MODEL), rowmap),
                   pl.BlockSpec((tm, LANES), rowmap)),
        compiler_params=_params(("parallel",)),
        name="out_proj_ln_router",
    )(ys, yg, yh, h, w_out, ln_g[None, :], ln_b[None, :], w_router, b_router)


MOE_CHUNK = 128


def _prefix_lanes(v, idx):
    axis = 1 if v.shape[0] == 1 else 0
    out = jnp.zeros_like(v)
    for g in range(MOE_GROUPS - 1):
        vg = v[:, g:g + 1] if axis == 1 else v[g:g + 1, :]
        out = out + jnp.where(idx > g, vg, 0.0)
    return out


def _moe_kernel(hb_ref, h_ref, comb_ref, wgu_ref, wdn_ref, g_ref, b_ref, o_ref, ob_ref,
                xs_ref, cw_ref, ys_ref):
    tm = hb_ref.shape[0]
    comb = comb_ref[...]
    lane = _iota2((tm, LANES), 1).astype(F32)
    gid = comb[:, N_EXPERTS:N_EXPERTS + 1]
    gsel = (lane == gid).astype(F32)
    tr = _iota2((tm, tm), 0)
    tc = _iota2((tm, tm), 1)

    rank_c = _dot((tc < tr).astype(F32), gsel)
    cnt_r = jnp.sum(gsel, axis=0, keepdims=True)
    start_r = _prefix_lanes(cnt_r, _iota2((1, LANES), 1))
    dest_c = jnp.sum(gsel * (start_r + rank_c), axis=1, keepdims=True)
    eye8 = (_iota2((SUBLANES, LANES), 0) == _iota2((SUBLANES, LANES), 1)).astype(F32)
    gsel_t = _dot_nt(eye8, gsel)
    rank_r = _dot(gsel_t, (tr < tc).astype(F32))
    cnt_c = jnp.sum(gsel_t, axis=1, keepdims=True)
    start_c = _prefix_lanes(cnt_c, _iota2((SUBLANES, 1), 0))
    dest_r = jnp.sum(gsel_t * (start_c + rank_r), axis=0, keepdims=True)

    perm = (dest_r == tr.astype(F32)).astype(BF16)
    xs_ref[...] = jnp.dot(perm, hb_ref[...], preferred_element_type=F32).astype(BF16)
    c_hi, c_lo = _split(comb)
    cw2 = jnp.dot(perm, jnp.concatenate([c_hi, c_lo], axis=1), preferred_element_type=F32)
    cw_ref[...] = cw2[:, 0:LANES] + cw2[:, LANES:2 * LANES]
    ys_ref[...] = jnp.zeros(ys_ref.shape, F32)

    lane1 = _iota2((1, LANES), 1)
    starts, ends = [], []
    for g in range(MOE_GROUPS):
        starts.append(jnp.sum(jnp.where(lane1 == g, start_r, 0.0)))
        ends.append(starts[g] + jnp.sum(jnp.where(lane1 == g, cnt_r, 0.0)))
    for c in range(tm // MOE_CHUNK):
        lo = c * MOE_CHUNK
        hi = lo + MOE_CHUNK
        for g in range(MOE_GROUPS):
            @pl.when((starts[g] < hi) & (ends[g] > lo))
            def _(lo=lo, hi=hi, g=g):
                x = xs_ref[lo:hi, :]
                cw = cw_ref[lo:hi, :]
                hm = []
                for e in range(EXPERTS_PER_GROUP):
                    ex = g * EXPERTS_PER_GROUP + e
                    gu = jnp.dot(x, wgu_ref[ex], preferred_element_type=F32)
                    hm.append((_silu(gu[:, 0:D_EXPERT]) * gu[:, D_EXPERT:2 * D_EXPERT]
                               * cw[:, ex:ex + 1]).astype(BF16))
                wd = wdn_ref[g * EXPERTS_PER_GROUP:(g + 1) * EXPERTS_PER_GROUP].reshape(
                    EXPERTS_PER_GROUP * D_EXPERT, D_MODEL)
                ys_ref[lo:hi, :] += jnp.dot(jnp.concatenate(hm, axis=1), wd, preferred_element_type=F32)

    unperm = (dest_c == tc.astype(F32)).astype(BF16)
    y = jnp.dot(unperm, ys_ref[...].astype(BF16), preferred_element_type=F32)
    h2 = _layer_norm(DN_ALPHA * h_ref[...] + y, g_ref[...], b_ref[...])
    o_ref[...] = h2
    ob_ref[...] = h2.astype(BF16)


def _moe(hb, h, comb, w_gu, w_dn, ln_g, ln_b, tm):
    t = h.shape[0]
    rowmap = lambda i: (i, 0)
    const = lambda i: (0, 0)
    const3 = lambda i: (0, 0, 0)
    return pl.pallas_call(
        _moe_kernel,
        out_shape=(jax.ShapeDtypeStruct((t, D_MODEL), F32),
                   jax.ShapeDtypeStruct((t, D_MODEL), BF16)),
        grid=(t // tm,),
        in_specs=[pl.BlockSpec((tm, D_MODEL), rowmap),
                  pl.BlockSpec((tm, D_MODEL), rowmap),
                  pl.BlockSpec((tm, LANES), rowmap),
                  pl.BlockSpec((N_EXPERTS, D_MODEL, 2 * D_EXPERT), const3, pipeline_mode=pl.Buffered(1)),
                  pl.BlockSpec((N_EXPERTS, D_EXPERT, D_MODEL), const3, pipeline_mode=pl.Buffered(1)),
                  pl.BlockSpec((1, D_MODEL), const),
                  pl.BlockSpec((1, D_MODEL), const)],
        out_specs=(pl.BlockSpec((tm, D_MODEL), rowmap),
                   pl.BlockSpec((tm, D_MODEL), rowmap)),
        scratch_shapes=[pltpu.VMEM((tm, D_MODEL), BF16),
                        pltpu.VMEM((tm, LANES), F32),
                        pltpu.VMEM((tm, D_MODEL), F32)],
        compiler_params=pltpu.CompilerParams(dimension_semantics=("parallel",),
                                             vmem_limit_bytes=MOE_VMEM_LIMIT),
        name="moe_ln",
    )(hb, h, comb, w_gu, w_dn, ln_g[None, :], ln_b[None, :])


def _split_w_in(w):
    sizes = (SSD_WIDTH, SSD_XBC, SSD_HEADS, 3 * GDN_WIDTH, GDN_WIDTH, GDN_HEADS, GDN_HEADS,
             HG_WIDTH, HG_WIDTH, HG_WIDTH, HG_WIDTH)
    parts, acc = [], 0
    for s in sizes:
        parts.append(w[:, acc:acc + s])
        acc += s
    z, xbc, dt, qkv, ggate, gb, ga, hq, hf, hi, hgate = parts
    cols = jnp.concatenate([xbc, qkv, z, ggate, hq, hf, hi, hgate, dt, gb, ga], axis=1).astype(BF16)
    return jnp.pad(cols, ((0, 0), (0, PROJ_COLS - cols.shape[1])))


def kernel(x, w_in, ssd_conv_w, ssd_conv_b, ssd_dt_bias, ssd_a_log, ssd_d, ssd_norm_w, gdn_conv_w, gdn_dt_bias, gdn_a_log, gdn_norm_w, hg_lb_logits, hg_norm_w, w_out, ln1_g, ln1_b, w_router_group, b_router_group, w_router_expert, b_router_expert, w_expert_gate_up, w_expert_down, ln2_g, ln2_b):
    nb, seq, d = x.shape
    t = nb * seq
    lb_cum = jnp.cumsum(jax.nn.softmax(hg_lb_logits.astype(F32), axis=0), axis=0)
    lb_all = lb_cum - lb_cum[0:1]
    h = x.reshape(t, d)
    hb = h.astype(BF16)
    for l in range(DEPTH):
        proj = _matmul(hb, _split_w_in(w_in[l]), 512, PROJ_COLS // 3)
        y_ssd = _ssd_mixer(proj, nb, seq, ssd_conv_w[l], ssd_conv_b[l], ssd_dt_bias[l],
                           ssd_a_log[l], ssd_d[l], ssd_norm_w[l])
        y_gdn = _gdn_mixer(proj, nb, seq, gdn_conv_w[l], gdn_dt_bias[l], gdn_a_log[l], gdn_norm_w[l])
        y_hg = _hgrn_mixer(proj, nb, seq, lb_all[l], hg_norm_w[l])
        w_router = jnp.pad(jnp.concatenate([w_router_group[l], w_router_expert[l]], axis=1),
                           ((0, 0), (0, LANES - MOE_GROUPS - N_EXPERTS)))
        w_router = jnp.concatenate(_split(w_router), axis=1)
        b_router = jnp.pad(jnp.concatenate([b_router_group[l], b_router_expert[l]]),
                           (0, LANES - MOE_GROUPS - N_EXPERTS))[None, :]
        h1, h1b, comb = _outproj(y_ssd, y_gdn, y_hg, h, w_out[l].astype(BF16), ln1_g[l], ln1_b[l],
                                 w_router, b_router, 512)
        h, hb = _moe(h1b, h1, comb, w_expert_gate_up[l].astype(BF16), w_expert_down[l].astype(BF16),
                     ln2_g[l], ln2_b[l], 512)
    return h.reshape(nb, seq, d)
```

```python
import functools

import jax
import jax.numpy as jnp
from jax import lax
from jax.experimental import pallas as pl
from jax.experimental.pallas import tpu as pltpu

F32 = jnp.float32
BF16 = jnp.bfloat16
HIGHEST = lax.Precision.HIGHEST

D_MODEL = 1024
DEPTH = 4
SSD_HEADS = 16
SSD_HEAD_DIM = 64
SSD_WIDTH = 1024
SSD_GROUPS = 2
SSD_STATE = 128
SSD_BC = 256
SSD_XBC = 1536
SSD_CONV = 4
GDN_HEADS = 4
GDN_HEAD_DIM = 128
GDN_WIDTH = 512
GDN_CONV = 4
GDN_CHUNK = 64
HG_HEADS = 4
HG_HEAD_DIM = 128
HG_WIDTH = 512
D_MIX = 2048
MOE_GROUPS = 4
EXPERTS_PER_GROUP = 4
N_EXPERTS = 16
D_EXPERT = 256
DN_ALPHA = (2 * DEPTH) ** 0.25

LANES = 128
SUBLANES = 8
ROWS = 128
VMEM_LIMIT = 48 * 1024 * 1024
MOE_VMEM_LIMIT = 56 * 1024 * 1024

OFF_XBC = 0
OFF_QKV = 1536
OFF_Z = 3072
OFF_GGATE = 4096
OFF_HQ = 4608
OFF_HF = 5120
OFF_HI = 5632
OFF_HGATE = 6144
OFF_SMALL = 6656
PROJ_COLS = 6912
SM_DT = 0
SM_B = 16
SM_A = 20


def _dot(a, b):
    return jnp.dot(a.astype(BF16), b.astype(BF16), preferred_element_type=F32)


def _dot_nt(a, b):
    return lax.dot_general(a.astype(BF16), b.astype(BF16), (((1,), (1,)), ((), ())),
                           preferred_element_type=F32)


def _dot_tn(a, b):
    return lax.dot_general(a.astype(BF16), b.astype(BF16), (((0,), (0,)), ((), ())),
                           preferred_element_type=F32)


def _dot_hi(a, b):
    return jnp.dot(a, b, precision=HIGHEST, preferred_element_type=F32)


def _split(a):
    hi = a.astype(BF16)
    return hi, (a - hi.astype(F32)).astype(BF16)


def _dot3(a, b):
    (ah, al), (bh, bl) = a, b
    return (jnp.dot(ah, bh, preferred_element_type=F32) + jnp.dot(ah, bl, preferred_element_type=F32)
            + jnp.dot(al, bh, preferred_element_type=F32))


def _dot2(a, b01):
    hi = a.astype(BF16)
    lo = (a - hi.astype(F32)).astype(BF16)
    b = b01.astype(BF16)
    return (jnp.dot(hi, b, preferred_element_type=F32) + jnp.dot(lo, b, preferred_element_type=F32))


def _sigmoid(x):
    return 1.0 / (1.0 + jnp.exp(-x))


def _silu(x):
    return x * _sigmoid(x)


def _softplus(x):
    return jnp.maximum(x, 0.0) + jnp.log(1.0 + jnp.exp(-jnp.abs(x)))


def _iota2(shape, dim):
    return lax.broadcasted_iota(jnp.int32, shape, dim)


def _params(sem):
    return pltpu.CompilerParams(dimension_semantics=sem, vmem_limit_bytes=VMEM_LIMIT)


def _matmul_kernel(x_ref, w_ref, o_ref):
    o_ref[...] = jnp.dot(x_ref[...], w_ref[...], preferred_element_type=F32)


def _matmul(x, w, tm, tn):
    t, k = x.shape
    n = w.shape[1]
    return pl.pallas_call(
        _matmul_kernel,
        out_shape=jax.ShapeDtypeStruct((t, n), F32),
        grid=(n // tn, t // tm),
        in_specs=[pl.BlockSpec((tm, k), lambda j, i: (i, 0)),
                  pl.BlockSpec((k, tn), lambda j, i: (0, j))],
        out_specs=pl.BlockSpec((tm, tn), lambda j, i: (i, j)),
        compiler_params=_params(("parallel", "arbitrary")),
        name="in_proj",
    )(x, w)


CONV_SLAB = 256


def _conv_silu(x_ref, xpad_ref, xa_ref, cw_ref, cb_ref, first):
    rows, cols = x_ref.shape

    @pl.when(first)
    def _():
        xpad_ref[0:SUBLANES, :] = jnp.zeros((SUBLANES, cols), F32)

    k = cw_ref.shape[0]
    for c0 in range(0, cols, CONV_SLAB):
        cs = slice(c0, c0 + CONV_SLAB)
        x = x_ref[:, cs]
        xpad_ref[SUBLANES:SUBLANES + rows, cs] = x
        acc = x * cw_ref[k - 1:k, cs]
        if cb_ref is not None:
            acc = acc + cb_ref[:, cs]
        for i in range(k - 1):
            off = SUBLANES - (k - 1) + i
            acc = acc + xpad_ref[off:off + rows, cs] * cw_ref[i:i + 1, cs]
        xa_ref[:, cs] = _silu(acc)
        xpad_ref[0:SUBLANES, cs] = xpad_ref[rows:rows + SUBLANES, cs]


def _ssd_kernel(z_ref, xbc_ref, sm_ref, cw_ref, cb_ref, hp_ref, hpt_ref, dsk_ref, nw_ref,
                y_ref, xpad_ref, xa_ref, st_ref):
    c = pl.program_id(1)

    @pl.when(c == 0)
    def _():
        st_ref[...] = jnp.zeros(st_ref.shape, F32)

    _conv_silu(xbc_ref, xpad_ref, xa_ref, cw_ref, cb_ref, c == 0)

    row = _iota2((ROWS, ROWS), 0)
    col = _iota2((ROWS, ROWS), 1)
    causal = row >= col
    tril = causal.astype(F32)
    triu = (row <= col).astype(F32)

    sm = sm_ref[...]
    dt = _softplus(sm + hp_ref[0:1, :])
    da = dt * (-jnp.exp(hp_ref[1:2, :]))
    acum = _dot_hi(tril, da)
    smt = sm.T
    dtt = _softplus(smt[0:SSD_HEADS, :] + hpt_ref[0:SSD_HEADS, 0:1])
    dat = dtt * (-jnp.exp(hpt_ref[0:SSD_HEADS, 1:2]))
    acumt = _dot_hi(dat, triu)

    heads3 = jnp.concatenate([dt, jnp.exp(acum), jnp.exp(acum[ROWS - 1:ROWS, :] - acum)], axis=0)
    h_hi, h_lo = _split(heads3)
    heads6 = jnp.concatenate([h_hi, h_lo], axis=0)
    lane_lo = _iota2((ROWS, LANES), 1) < SSD_HEAD_DIM

    hg = SSD_HEADS // SSD_GROUPS
    gw = SSD_WIDTH // SSD_GROUPS
    hrow = _iota2((LANES, gw), 0)
    hcol = _iota2((LANES, gw), 1)
    for g in range(SSD_GROUPS):
        gs = slice(g * gw, (g + 1) * gw)
        expand = (((hcol >> 6) + g * hg) == hrow).astype(BF16)
        ex = jnp.dot(heads6, expand, preferred_element_type=F32)
        ex = ex[0:3 * ROWS, :] + ex[3 * ROWS:6 * ROWS, :]
        dtx = ex[0:ROWS, :]
        eax = ex[ROWS:2 * ROWS, :]
        tex = ex[2 * ROWS:3 * ROWS, :]
        xs = xa_ref[:, gs]
        xdt = xs * dtx
        xdt_b = xdt.astype(BF16)
        xend_b = (xdt * tex).astype(BF16)
        bm = xa_ref[:, SSD_WIDTH + g * SSD_STATE:SSD_WIDTH + (g + 1) * SSD_STATE]
        cm = xa_ref[:, SSD_WIDTH + SSD_BC + g * SSD_STATE:SSD_WIDTH + SSD_BC + (g + 1) * SSD_STATE]
        cm_b = cm.astype(BF16)
        cb = _dot_nt(cm_b, bm)
        yd = []
        for pair in range(hg // 2):
            res = []
            for sub in range(2):
                h = g * hg + pair * 2 + sub
                seg = acum[:, h:h + 1] - acumt[h:h + 1, :]
                lmat = cb * jnp.exp(jnp.where(causal, seg, -jnp.inf))
                c0 = pair * 2 * SSD_HEAD_DIM
                res.append(jnp.dot(lmat.astype(BF16), xdt_b[:, c0:c0 + LANES],
                                   preferred_element_type=F32))
            yd.append(jnp.where(lane_lo, res[0], res[1]))
        yd = jnp.concatenate(yd, axis=1)
        st = st_ref[g]
        yoff = jnp.dot(cm_b, st.astype(BF16), preferred_element_type=F32) * eax
        st_ref[g] = (st * eax[ROWS - 1:ROWS, :]
                     + jnp.dot(bm.T.astype(BF16), xend_b, preferred_element_type=F32))
        y = yd + yoff + xs * dsk_ref[:, gs]
        y = y * _silu(z_ref[:, gs])
        ms = jnp.mean(y * y, axis=-1, keepdims=True)
        y_ref[:, gs] = (y * lax.rsqrt(ms + 1e-6) * nw_ref[:, gs]).astype(y_ref.dtype)


def _ssd_mixer(proj, nb, seq, conv_w, conv_b, dt_bias, a_log, d_skip, norm_w):
    nblk = seq // ROWS
    hp = jnp.zeros((SUBLANES, LANES), F32)
    hp = hp.at[0, SM_DT:SM_DT + SSD_HEADS].set(dt_bias).at[1, SM_DT:SM_DT + SSD_HEADS].set(a_log)
    hpt = hp.T
    dsk = jnp.repeat(d_skip, SSD_HEAD_DIM)[None, :]
    rowmap = lambda b, c: (b * nblk + c, 0)
    const = lambda b, c: (0, 0)
    return pl.pallas_call(
        _ssd_kernel,
        out_shape=jax.ShapeDtypeStruct((nb * seq, SSD_WIDTH), BF16),
        grid=(nb, nblk),
        in_specs=[
            pl.BlockSpec((ROWS, SSD_WIDTH), lambda b, c: (b * nblk + c, OFF_Z // SSD_WIDTH)),
            pl.BlockSpec((ROWS, SSD_XBC), lambda b, c: (b * nblk + c, OFF_XBC // SSD_XBC)),
            pl.BlockSpec((ROWS, LANES), lambda b, c: (b * nblk + c, OFF_SMALL // LANES)),
            pl.BlockSpec((SSD_CONV, SSD_XBC), const),
            pl.BlockSpec((1, SSD_XBC), const),
            pl.BlockSpec((SUBLANES, LANES), const),
            pl.BlockSpec((LANES, SUBLANES), const),
            pl.BlockSpec((1, SSD_WIDTH), const),
            pl.BlockSpec((1, SSD_WIDTH), const),
        ],
        out_specs=pl.BlockSpec((ROWS, SSD_WIDTH), rowmap),
        scratch_shapes=[pltpu.VMEM((ROWS + SUBLANES, SSD_XBC), F32),
                        pltpu.VMEM((ROWS, SSD_XBC), F32),
                        pltpu.VMEM((SSD_GROUPS, SSD_STATE, SSD_WIDTH // SSD_GROUPS), F32)],
        compiler_params=_params(("parallel", "arbitrary")),
        name="ssd_mixer",
    )(proj, proj, proj, conv_w, conv_b[None, :], hp, hpt, dsk, norm_w[None, :])


def _gdn_kernel(qkv_ref, gate_ref, sm_ref, cw_ref, hp_ref, hpt_ref, nw_ref, y_ref, xpad_ref, xa_ref, st_ref):
    c = pl.program_id(1)

    @pl.when(c == 0)
    def _():
        st_ref[...] = jnp.zeros(st_ref.shape, F32)

    _conv_silu(qkv_ref, xpad_ref, xa_ref, cw_ref, None, c == 0)

    row = _iota2((ROWS, ROWS), 0)
    col = _iota2((ROWS, ROWS), 1)
    incl = row >= col
    strict = row > col
    eye = (row == col).astype(F32)
    tril = incl.astype(F32)
    triu = (row <= col).astype(F32)

    sm = sm_ref[...]
    la = -jnp.exp(hp_ref[1:2, :]) * _softplus(sm + hp_ref[0:1, :])
    gcum = _dot_hi(tril, la)
    smt = sm.T
    lat = (-jnp.exp(hpt_ref[SM_B:SM_B + SUBLANES, 1:2])
           * _softplus(smt[SM_B:SM_B + SUBLANES, :] + hpt_ref[SM_B:SM_B + SUBLANES, 0:1]))
    gcumt = _dot_hi(lat, triu)
    beta_all = _sigmoid(sm)

    dh = GDN_HEAD_DIM
    heads = range(GDN_HEADS)
    qs, ks, xs, rhss, decays, gs = [], [], [], [], [], []
    for h in heads:
        q = xa_ref[:, h * dh:(h + 1) * dh]
        k = xa_ref[:, GDN_WIDTH + h * dh:GDN_WIDTH + (h + 1) * dh]
        v = xa_ref[:, 2 * GDN_WIDTH + h * dh:2 * GDN_WIDTH + (h + 1) * dh]
        q = q * lax.rsqrt(jnp.sum(q * q, axis=-1, keepdims=True) + 1e-6) * (dh ** -0.5)
        k = k * lax.rsqrt(jnp.sum(k * k, axis=-1, keepdims=True) + 1e-6)
        g = gcum[:, SM_A + h:SM_A + h + 1]
        gt = gcumt[SM_A - SM_B + h:SM_A - SM_B + h + 1, :]
        beta = beta_all[:, SM_B + h:SM_B + h + 1]
        decay = jnp.exp(jnp.where(incl, g - gt, -jnp.inf))
        kb = k * beta
        xs.append(-jnp.where(strict, _dot_nt(kb, k) * decay, 0.0))
        rhss.append(jnp.concatenate([v * beta, kb * jnp.exp(g)], axis=1))
        qs.append(q)
        ks.append(k)
        decays.append(decay)
        gs.append(g)
    ps = xs
    ns = xs
    levels = ROWS.bit_length() - 2
    for j in range(levels):
        if j < GDN_HI_LEVELS:
            pp = [_split(p) for p in ps]
            ps = [_dot3(p, p) for p in pp]
            pp = [_split(p) for p in ps]
            ns = [n + p + _dot3(_split(n), p2) for n, p, p2 in zip(ns, ps, pp)]
        else:
            ps = [_dot(p, p) for p in ps]
            ns = [n + p + _dot(n, p) for n, p in zip(ns, ps)]
    sols = [r + _dot3(_split(n), _split(r)) for n, r in zip(ns, rhss)]
    qks = [_dot_nt(q, k) * d for q, k, d in zip(qs, ks, decays)]
    sts = [st_ref[h] for h in heads]
    v_news = [s[:, 0:dh] - _dot(s[:, dh:2 * dh], st) for s, st in zip(sols, sts)]
    os_ = [_dot(q * jnp.exp(g), st) + _dot(qk, vn) for q, g, st, qk, vn in zip(qs, gs, sts, qks, v_news)]
    for h in heads:
        glast = gs[h][ROWS - 1:ROWS, :]
        kd = ks[h] * jnp.exp(glast - gs[h])
        st_ref[h] = sts[h] * jnp.exp(glast) + _dot(kd.T, v_news[h])
    outs = []
    for h in heads:
        o = os_[h]
        o = o * lax.rsqrt(jnp.mean(o * o, axis=-1, keepdims=True) + 1e-6)
        outs.append(o * nw_ref[...] * _silu(gate_ref[:, h * dh:(h + 1) * dh]))
    y_ref[...] = jnp.concatenate(outs, axis=1).astype(y_ref.dtype)


def _gdn_mixer(proj, nb, seq, conv_w, dt_bias, a_log, norm_w):
    nblk = seq // ROWS
    hp = jnp.zeros((SUBLANES, LANES), F32)
    hp = hp.at[0, SM_A:SM_A + GDN_HEADS].set(dt_bias).at[1, SM_A:SM_A + GDN_HEADS].set(a_log)
    hpt = hp.T
    rowmap = lambda b, c: (b * nblk + c, 0)
    const = lambda b, c: (0, 0)
    return pl.pallas_call(
        _gdn_kernel,
        out_shape=jax.ShapeDtypeStruct((nb * seq, GDN_WIDTH), BF16),
        grid=(nb, nblk),
        in_specs=[
            pl.BlockSpec((ROWS, 3 * GDN_WIDTH), lambda b, c: (b * nblk + c, OFF_QKV // (3 * GDN_WIDTH))),
            pl.BlockSpec((ROWS, GDN_WIDTH), lambda b, c: (b * nblk + c, OFF_GGATE // GDN_WIDTH)),
            pl.BlockSpec((ROWS, LANES), lambda b, c: (b * nblk + c, OFF_SMALL // LANES)),
            pl.BlockSpec((GDN_CONV, 3 * GDN_WIDTH), const),
            pl.BlockSpec((SUBLANES, LANES), const),
            pl.BlockSpec((LANES, SUBLANES), const),
            pl.BlockSpec((1, GDN_HEAD_DIM), const),
        ],
        out_specs=pl.BlockSpec((ROWS, GDN_WIDTH), rowmap),
        scratch_shapes=[pltpu.VMEM((ROWS + SUBLANES, 3 * GDN_WIDTH), F32),
                        pltpu.VMEM((ROWS, 3 * GDN_WIDTH), F32),
                        pltpu.VMEM((GDN_HEADS, GDN_HEAD_DIM, GDN_HEAD_DIM), F32)],
        compiler_params=_params(("parallel", "arbitrary")),
        name="gdn_mixer",
    )(proj, proj, proj, conv_w, hp, hpt, norm_w[None, :])


def _hgrn_kernel(q_ref, f_ref, i_ref, gate_ref, lb_ref, nw_ref, y_ref, st_ref):
    c = pl.program_id(1)

    @pl.when(c == 0)
    def _():
        st_ref[...] = jnp.zeros(st_ref.shape, F32)

    row = _iota2((ROWS, ROWS), 0)
    col = _iota2((ROWS, ROWS), 1)
    tril = (row >= col).astype(F32)
    ones_b = jnp.ones((LANES, LANES), BF16)
    rmod = _iota2((ROWS, LANES), 0) & (SUBLANES - 1)
    rfull = _iota2((ROWS, LANES), 0)
    dk = HG_HEAD_DIM
    outs = []
    for h in range(HG_HEADS):
        sl = slice(h * dk, (h + 1) * dk)
        lb = lb_ref[:, sl]
        fr = f_ref[:, sl]
        log_sig = jnp.minimum(fr, 0.0) - jnp.log(1.0 + jnp.exp(-jnp.abs(fr)))
        a = jnp.log(lb)
        y = jnp.log1p(-lb) + log_sig
        log_f = jnp.maximum(a, y) + jnp.log(1.0 + jnp.exp(-jnp.abs(a - y)))
        k = (1.0 - lb) * _sigmoid(-fr)
        q = _silu(q_ref[:, sl])
        v = i_ref[:, sl]
        b = _dot_hi(tril, log_f)

        st = st_ref[h]
        o = _dot_nt(q * jnp.exp(b), st)
        blast = b[ROWS - 1:ROWS, :]
        st_ref[h] = st * jnp.exp(blast) + _dot_tn(v, k * jnp.exp(blast - b))

        pmat = jnp.zeros((ROWS, ROWS), F32)
        m = SUBLANES
        while m < ROWS:
            ref = jnp.concatenate(
                [jnp.broadcast_to(b[p0 + m - 1:p0 + m, :], (2 * m, dk)) for p0 in range(0, ROWS, 2 * m)], axis=0)
            right = (rfull & m) != 0
            qs = q * jnp.exp(jnp.where(right, b - ref, -jnp.inf))
            ks = k * jnp.exp(jnp.where(right, -jnp.inf, ref - b))
            parent = ~(2 * m - 1)
            pmat = pmat + jnp.where((row & parent) == (col & parent), _dot_nt(qs, ks), 0.0)
            m *= 2
        o = o + _dot(pmat, v)

        for d in range(SUBLANES):
            if d == 0:
                e = q * k
                vd = v
            else:
                kd = pltpu.roll(k, d, 0)
                bd = pltpu.roll(b, d, 0)
                vd = pltpu.roll(v, d, 0)
                e = q * kd * jnp.exp(jnp.where(rmod >= d, b - bd, -jnp.inf))
            score = jnp.dot(e.astype(BF16), ones_b, preferred_element_type=F32)
            o = o + score * vd

        o = o * lax.rsqrt(jnp.mean(o * o, axis=-1, keepdims=True) + 1e-6)
        outs.append(o * nw_ref[...] * _silu(gate_ref[:, sl]))
    y_ref[...] = jnp.concatenate(outs, axis=1).astype(y_ref.dtype)


def _hgrn_mixer(proj, nb, seq, lb, norm_w):
    nblk = seq // ROWS
    const = lambda b, c: (0, 0)

    def colspec(off):
        return pl.BlockSpec((ROWS, HG_WIDTH), lambda b, c: (b * nblk + c, off // HG_WIDTH))

    return pl.pallas_call(
        _hgrn_kernel,
        out_shape=jax.ShapeDtypeStruct((nb * seq, HG_WIDTH), BF16),
        grid=(nb, nblk),
        in_specs=[colspec(OFF_HQ), colspec(OFF_HF), colspec(OFF_HI), colspec(OFF_HGATE),
                  pl.BlockSpec((1, HG_WIDTH), const),
                  pl.BlockSpec((1, HG_HEAD_DIM), const)],
        out_specs=pl.BlockSpec((ROWS, HG_WIDTH), lambda b, c: (b * nblk + c, 0)),
        scratch_shapes=[pltpu.VMEM((HG_HEADS, HG_HEAD_DIM, HG_HEAD_DIM), F32)],
        compiler_params=_params(("parallel", "arbitrary")),
        name="hgrn2_mixer",
    )(proj, proj, proj, proj, lb[None, :], norm_w[None, :])


def _layer_norm(x, g, b):
    mu = jnp.mean(x, axis=-1, keepdims=True)
    xc = x - mu
    var = jnp.mean(xc * xc, axis=-1, keepdims=True)
    return xc * lax.rsqrt(var + 1e-5) * g + b


def _masked_max(x, mask):
    return jnp.max(jnp.where(mask, x, -jnp.inf), axis=-1, keepdims=True)


def _first_lane(cond, lane):
    return jnp.min(jnp.where(cond, lane, float(LANES)), axis=-1, keepdims=True)


def _outproj_kernel(ys_ref, yg_ref, yh_ref, h_ref, w_ref, g_ref, b_ref, wr_ref, br_ref,
                    h1_ref, h1b_ref, comb_ref):
    mix = jnp.dot(ys_ref[...], w_ref[0:SSD_WIDTH, :], preferred_element_type=F32)
    mix = mix + jnp.dot(yg_ref[...], w_ref[SSD_WIDTH:SSD_WIDTH + GDN_WIDTH, :], preferred_element_type=F32)
    mix = mix + jnp.dot(yh_ref[...], w_ref[SSD_WIDTH + GDN_WIDTH:D_MIX, :], preferred_element_type=F32)
    h1 = _layer_norm(DN_ALPHA * h_ref[...] + mix, g_ref[...], b_ref[...])
    h1_ref[...] = h1
    h1b_ref[...] = h1.astype(BF16)

    h_hi, h_lo = _split(h1)
    wr = wr_ref[...]
    l2 = jnp.dot(h_hi, wr, preferred_element_type=F32) + jnp.dot(h_lo, wr, preferred_element_type=F32)
    logits = l2[:, 0:LANES] + l2[:, LANES:2 * LANES] + br_ref[...]
    lane_i = _iota2(logits.shape, 1)
    lane = lane_i.astype(F32)
    gmask = lane_i < MOE_GROUPS
    gmax = _masked_max(logits, gmask)
    gexp = jnp.where(gmask, jnp.exp(logits - gmax), 0.0)
    gprob = gexp / jnp.sum(gexp, axis=-1, keepdims=True)
    g_p = _masked_max(gprob, gmask)
    g_idx = _first_lane(gmask & (gprob == g_p), lane)
    egroup = ((lane_i - MOE_GROUPS) >> 2).astype(F32)
    emask = (lane_i >= MOE_GROUPS) & (lane_i < MOE_GROUPS + N_EXPERTS) & (egroup == g_idx)
    emax = _masked_max(logits, emask)
    eexp = jnp.where(emask, jnp.exp(logits - emax), 0.0)
    eprob = eexp / jnp.sum(eexp, axis=-1, keepdims=True)
    p1 = _masked_max(eprob, emask)
    i1 = _first_lane(emask & (eprob == p1), lane)
    emask2 = emask & (lane != i1)
    p2 = _masked_max(eprob, emask2)
    i2 = _first_lane(emask2 & (eprob == p2), lane)
    denom = p1 + p2
    w1 = g_p * p1 / denom
    w2 = g_p * p2 / denom
    comb_ref[...] = (jnp.where(lane == i1 - MOE_GROUPS, w1, 0.0)
                     + jnp.where(lane == i2 - MOE_GROUPS, w2, 0.0)
                     + jnp.where(lane_i == N_EXPERTS, g_idx, 0.0))


def _outproj(ys, yg, yh, h, w_out, ln_g, ln_b, w_router, b_router, tm):
    t = h.shape[0]
    rowmap = lambda i: (i, 0)
    const = lambda i: (0, 0)
    return pl.pallas_call(
        _outproj_kernel,
        out_shape=(jax.ShapeDtypeStruct((t, D_MODEL), F32),
                   jax.ShapeDtypeStruct((t, D_MODEL), BF16),
                   jax.ShapeDtypeStruct((t, LANES), F32)),
        grid=(t // tm,),
        in_specs=[pl.BlockSpec((tm, SSD_WIDTH), rowmap),
                  pl.BlockSpec((tm, GDN_WIDTH), rowmap),
                  pl.BlockSpec((tm, HG_WIDTH), rowmap),
                  pl.BlockSpec((tm, D_MODEL), rowmap),
                  pl.BlockSpec((D_MIX, D_MODEL), const),
                  pl.BlockSpec((1, D_MODEL), const),
                  pl.BlockSpec((1, D_MODEL), const),
                  pl.BlockSpec((D_MODEL, 2 * LANES), const),
                  pl.BlockSpec((1, LANES), const)],
        out_specs=(pl.BlockSpec((tm, D_MODEL), rowmap),
                   pl.BlockSpec((tm, D_MODEL), rowmap),
                   pl.BlockSpec((tm, LANES), rowmap)),
        compiler_params=_params(("parallel",)),
        name="out_proj_ln_router",
    )(ys, yg, yh, h, w_out, ln_g[None, :], ln_b[None, :], w_router, b_router)


GDN_HI_LEVELS = 2
MOE_CHUNK = 128


def _prefix_lanes(v, idx):
    axis = 1 if v.shape[0] == 1 else 0
    out = jnp.zeros_like(v)
    for g in range(MOE_GROUPS - 1):
        vg = v[:, g:g + 1] if axis == 1 else v[g:g + 1, :]
        out = out + jnp.where(idx > g, vg, 0.0)
    return out


def _moe_kernel(hb_ref, h_ref, comb_ref, wgu_ref, wdn_ref, g_ref, b_ref, o_ref, ob_ref,
                xs_ref, cw_ref, ys_ref):
    tm = hb_ref.shape[0]
    comb = comb_ref[...]
    lane = _iota2((tm, LANES), 1).astype(F32)
    gid = comb[:, N_EXPERTS:N_EXPERTS + 1]
    gsel = (lane == gid).astype(F32)
    tr = _iota2((tm, tm), 0)
    tc = _iota2((tm, tm), 1)

    rank_c = _dot((tc < tr).astype(F32), gsel)
    cnt_r = jnp.sum(gsel, axis=0, keepdims=True)
    start_r = _prefix_lanes(cnt_r, _iota2((1, LANES), 1))
    dest_c = jnp.sum(gsel * (start_r + rank_c), axis=1, keepdims=True)
    eye8 = (_iota2((SUBLANES, LANES), 0) == _iota2((SUBLANES, LANES), 1)).astype(F32)
    gsel_t = _dot_nt(eye8, gsel)
    rank_r = _dot(gsel_t, (tr < tc).astype(F32))
    cnt_c = jnp.sum(gsel_t, axis=1, keepdims=True)
    start_c = _prefix_lanes(cnt_c, _iota2((SUBLANES, 1), 0))
    dest_r = jnp.sum(gsel_t * (start_c + rank_r), axis=0, keepdims=True)

    perm = (dest_r == tr.astype(F32)).astype(BF16)
    xs_ref[...] = jnp.dot(perm, hb_ref[...], preferred_element_type=F32).astype(BF16)
    c_hi, c_lo = _split(comb)
    cw2 = jnp.dot(perm, jnp.concatenate([c_hi, c_lo], axis=1), preferred_element_type=F32)
    cw_ref[...] = cw2[:, 0:LANES] + cw2[:, LANES:2 * LANES]
    ys_ref[...] = jnp.zeros(ys_ref.shape, F32)

    lane1 = _iota2((1, LANES), 1)
    starts, ends = [], []
    for g in range(MOE_GROUPS):
        starts.append(jnp.sum(jnp.where(lane1 == g, start_r, 0.0)))
        ends.append(starts[g] + jnp.sum(jnp.where(lane1 == g, cnt_r, 0.0)))
    for c in range(tm // MOE_CHUNK):
        lo = c * MOE_CHUNK
        hi = lo + MOE_CHUNK
        for g in range(MOE_GROUPS):
            @pl.when((starts[g] < hi) & (ends[g] > lo))
            def _(lo=lo, hi=hi, g=g):
                x = xs_ref[lo:hi, :]
                cw = cw_ref[lo:hi, :]
                hm = []
                for e in range(EXPERTS_PER_GROUP):
                    ex = g * EXPERTS_PER_GROUP + e
                    gu = jnp.dot(x, wgu_ref[ex], preferred_element_type=F32)
                    hm.append((_silu(gu[:, 0:D_EXPERT]) * gu[:, D_EXPERT:2 * D_EXPERT]
                               * cw[:, ex:ex + 1]).astype(BF16))
                wd = wdn_ref[g * EXPERTS_PER_GROUP:(g + 1) * EXPERTS_PER_GROUP].reshape(
                    EXPERTS_PER_GROUP * D_EXPERT, D_MODEL)
                ys_ref[lo:hi, :] += jnp.dot(jnp.concatenate(hm, axis=1), wd, preferred_element_type=F32)

    unperm = (dest_c == tc.astype(F32)).astype(BF16)
    y = jnp.dot(unperm, ys_ref[...].astype(BF16), preferred_element_type=F32)
    h2 = _layer_norm(DN_ALPHA * h_ref[...] + y, g_ref[...], b_ref[...])
    o_ref[...] = h2
    ob_ref[...] = h2.astype(BF16)


def _moe(hb, h, comb, w_gu, w_dn, ln_g, ln_b, tm):
    t = h.shape[0]
    rowmap = lambda i: (i, 0)
    const = lambda i: (0, 0)
    const3 = lambda i: (0, 0, 0)
    return pl.pallas_call(
        _moe_kernel,
        out_shape=(jax.ShapeDtypeStruct((t, D_MODEL), F32),
                   jax.ShapeDtypeStruct((t, D_MODEL), BF16)),
        grid=(t // tm,),
        in_specs=[pl.BlockSpec((tm, D_MODEL), rowmap),
                  pl.BlockSpec((tm, D_MODEL), rowmap),
                  pl.BlockSpec((tm, LANES), rowmap),
                  pl.BlockSpec((N_EXPERTS, D_MODEL, 2 * D_EXPERT), const3, pipeline_mode=pl.Buffered(1)),
                  pl.BlockSpec((N_EXPERTS, D_EXPERT, D_MODEL), const3, pipeline_mode=pl.Buffered(1)),
                  pl.BlockSpec((1, D_MODEL), const),
                  pl.BlockSpec((1, D_MODEL), const)],
        out_specs=(pl.BlockSpec((tm, D_MODEL), rowmap),
                   pl.BlockSpec((tm, D_MODEL), rowmap)),
        scratch_shapes=[pltpu.VMEM((tm, D_MODEL), BF16),
                        pltpu.VMEM((tm, LANES), F32),
                        pltpu.VMEM((tm, D_MODEL), F32)],
        compiler_params=pltpu.CompilerParams(dimension_semantics=("parallel",),
                                             vmem_limit_bytes=MOE_VMEM_LIMIT),
        name="moe_ln",
    )(hb, h, comb, w_gu, w_dn, ln_g[None, :], ln_b[None, :])


def _split_w_in(w):
    sizes = (SSD_WIDTH, SSD_XBC, SSD_HEADS, 3 * GDN_WIDTH, GDN_WIDTH, GDN_HEADS, GDN_HEADS,
             HG_WIDTH, HG_WIDTH, HG_WIDTH, HG_WIDTH)
    parts, acc = [], 0
    for s in sizes:
        parts.append(w[:, acc:acc + s])
        acc += s
    z, xbc, dt, qkv, ggate, gb, ga, hq, hf, hi, hgate = parts
    cols = jnp.concatenate([xbc, qkv, z, ggate, hq, hf, hi, hgate, dt, gb, ga], axis=1).astype(BF16)
    return jnp.pad(cols, ((0, 0), (0, PROJ_COLS - cols.shape[1])))


def kernel(x, w_in, ssd_conv_w, ssd_conv_b, ssd_dt_bias, ssd_a_log, ssd_d, ssd_norm_w, gdn_conv_w, gdn_dt_bias, gdn_a_log, gdn_norm_w, hg_lb_logits, hg_norm_w, w_out, ln1_g, ln1_b, w_router_group, b_router_group, w_router_expert, b_router_expert, w_expert_gate_up, w_expert_down, ln2_g, ln2_b):
    nb, seq, d = x.shape
    t = nb * seq
    lb_cum = jnp.cumsum(jax.nn.softmax(hg_lb_logits.astype(F32), axis=0), axis=0)
    lb_all = lb_cum - lb_cum[0:1]
    h = x.reshape(t, d)
    hb = h.astype(BF16)
    for l in range(DEPTH):
        proj = _matmul(hb, _split_w_in(w_in[l]), 512, PROJ_COLS // 3)
        y_ssd = _ssd_mixer(proj, nb, seq, ssd_conv_w[l], ssd_conv_b[l], ssd_dt_bias[l],
                           ssd_a_log[l], ssd_d[l], ssd_norm_w[l])
        y_gdn = _gdn_mixer(proj, nb, seq, gdn_conv_w[l], gdn_dt_bias[l], gdn_a_log[l], gdn_norm_w[l])
        y_hg = _hgrn_mixer(proj, nb, seq, lb_all[l], hg_norm_w[l])
        w_router = jnp.pad(jnp.concatenate([w_router_group[l], w_router_expert[l]], axis=1),
                           ((0, 0), (0, LANES - MOE_GROUPS - N_EXPERTS)))
        w_router = jnp.concatenate(_split(w_router), axis=1)
        b_router = jnp.pad(jnp.concatenate([b_router_group[l], b_router_expert[l]]),
                           (0, LANES - MOE_GROUPS - N_EXPERTS))[None, :]
        h1, h1b, comb = _outproj(y_ssd, y_gdn, y_hg, h, w_out[l].astype(BF16), ln1_g[l], ln1_b[l],
                                 w_router, b_router, 512)
        h, hb = _moe(h1b, h1, comb, w_expert_gate_up[l].astype(BF16), w_expert_down[l].astype(BF16),
                     ln2_g[l], ln2_b[l], 512)
    return h.reshape(nb, seq, d)
```

```python
import jax
import jax.numpy as jnp
from jax import lax
from jax.experimental import pallas as pl
from jax.experimental.pallas import tpu as pltpu

F32 = jnp.float32
BF16 = jnp.bfloat16
HIGHEST = lax.Precision.HIGHEST

D_MODEL = 1024
DEPTH = 4
SSD_HEADS = 16
SSD_HEAD_DIM = 64
SSD_WIDTH = 1024
SSD_GROUPS = 2
SSD_STATE = 128
SSD_BC = 256
SSD_XBC = 1536
SSD_CONV = 4
GDN_HEADS = 4
GDN_HEAD_DIM = 128
GDN_WIDTH = 512
GDN_CONV = 4
HG_HEADS = 4
HG_HEAD_DIM = 128
HG_WIDTH = 512
D_MIX = 2048
MOE_GROUPS = 4
EXPERTS_PER_GROUP = 4
N_EXPERTS = 16
D_EXPERT = 256
DN_ALPHA = (2 * DEPTH) ** 0.25

LANES = 128
SUBLANES = 8
ROWS = 128
VMEM_LIMIT = 48 * 1024 * 1024
MOE_VMEM_LIMIT = 56 * 1024 * 1024
CONV_SLAB = 256
GDN_HI_LEVELS = 2
MOE_CHUNK = 128

OFF_XBC = 0
OFF_QKV = 1536
OFF_Z = 3072
OFF_GGATE = 4096
OFF_HQ = 4608
OFF_HF = 5120
OFF_HI = 5632
OFF_HGATE = 6144
OFF_SMALL = 6656
PROJ_COLS = 6912
SM_DT = 0
SM_B = 16
SM_A = 20
MIX_SSD = 0
MIX_GDN = SSD_WIDTH
MIX_HG = SSD_WIDTH + GDN_WIDTH


def _dot(a, b):
    return jnp.dot(a.astype(BF16), b.astype(BF16), preferred_element_type=F32)


def _dot_nt(a, b):
    return lax.dot_general(a.astype(BF16), b.astype(BF16), (((1,), (1,)), ((), ())),
                           preferred_element_type=F32)


def _dot_tn(a, b):
    return lax.dot_general(a.astype(BF16), b.astype(BF16), (((0,), (0,)), ((), ())),
                           preferred_element_type=F32)


def _dot_hi(a, b):
    return jnp.dot(a, b, precision=HIGHEST, preferred_element_type=F32)


def _split(a):
    hi = a.astype(BF16)
    return hi, (a - hi.astype(F32)).astype(BF16)


def _dot3(a, b):
    (ah, al), (bh, bl) = a, b
    return (jnp.dot(ah, bh, preferred_element_type=F32) + jnp.dot(ah, bl, preferred_element_type=F32)
            + jnp.dot(al, bh, preferred_element_type=F32))


def _sigmoid(x):
    return 1.0 / (1.0 + jnp.exp(-x))


def _silu(x):
    return x * _sigmoid(x)


def _softplus(x):
    return jnp.maximum(x, 0.0) + jnp.log(1.0 + jnp.exp(-jnp.abs(x)))


def _iota2(shape, dim):
    return lax.broadcasted_iota(jnp.int32, shape, dim)


def _params(sem):
    return pltpu.CompilerParams(dimension_semantics=sem, vmem_limit_bytes=VMEM_LIMIT)


def _matmul_kernel(x_ref, w_ref, o_ref):
    o_ref[...] = jnp.dot(x_ref[...], w_ref[...], preferred_element_type=F32)


def _matmul(x, w, tm, tn):
    t, k = x.shape
    n = w.shape[1]
    return pl.pallas_call(
        _matmul_kernel,
        out_shape=jax.ShapeDtypeStruct((t, n), F32),
        grid=(n // tn, t // tm),
        in_specs=[pl.BlockSpec((tm, k), lambda j, i: (i, 0)),
                  pl.BlockSpec((k, tn), lambda j, i: (0, j))],
        out_specs=pl.BlockSpec((tm, tn), lambda j, i: (i, j)),
        compiler_params=_params(("parallel", "arbitrary")),
        name="in_proj",
    )(x, w)


def _conv_silu_stages(x_ref, xpad_ref, xa_ref, cw_ref, cb_ref):
    rows, cols = x_ref.shape
    k = cw_ref.shape[0]
    for c0 in range(0, cols, CONV_SLAB):
        cs = slice(c0, c0 + CONV_SLAB)
        x = x_ref[:, cs]
        xpad_ref[SUBLANES:SUBLANES + rows, cs] = x
        acc = x * cw_ref[k - 1:k, cs]
        if cb_ref is not None:
            acc = acc + cb_ref[:, cs]
        for i in range(k - 1):
            off = SUBLANES - (k - 1) + i
            acc = acc + xpad_ref[off:off + rows, cs] * cw_ref[i:i + 1, cs]
        xa_ref[:, cs] = _silu(acc)
        xpad_ref[0:SUBLANES, cs] = xpad_ref[rows:rows + SUBLANES, cs]
        yield


def _ssd_stages(z_ref, xbc_ref, sm_ref, cw_ref, cb_ref, hp_ref, hpt_ref, dsk_ref, nw_ref,
                y_ref, xpad_ref, xa_ref, st_ref):
    yield from _conv_silu_stages(xbc_ref, xpad_ref, xa_ref, cw_ref, cb_ref)

    row = _iota2((ROWS, ROWS), 0)
    col = _iota2((ROWS, ROWS), 1)
    causal = row >= col
    tril = causal.astype(F32)
    triu = (row <= col).astype(F32)

    sm = sm_ref[...]
    dt = _softplus(sm + hp_ref[0:1, :])
    da = dt * (-jnp.exp(hp_ref[1:2, :]))
    acum = _dot_hi(tril, da)
    smt = sm.T
    dtt = _softplus(smt[0:SSD_HEADS, :] + hpt_ref[0:SSD_HEADS, 0:1])
    dat = dtt * (-jnp.exp(hpt_ref[0:SSD_HEADS, 1:2]))
    acumt = _dot_hi(dat, triu)

    heads3 = jnp.concatenate([dt, jnp.exp(acum), jnp.exp(acum[ROWS - 1:ROWS, :] - acum)], axis=0)
    h_hi, h_lo = _split(heads3)
    heads6 = jnp.concatenate([h_hi, h_lo], axis=0)
    lane_lo = _iota2((ROWS, LANES), 1) < SSD_HEAD_DIM
    yield

    hg = SSD_HEADS // SSD_GROUPS
    gw = SSD_WIDTH // SSD_GROUPS
    hrow = _iota2((LANES, gw), 0)
    hcol = _iota2((LANES, gw), 1)
    for g in range(SSD_GROUPS):
        gs = slice(g * gw, (g + 1) * gw)
        expand = (((hcol >> 6) + g * hg) == hrow).astype(BF16)
        ex = jnp.dot(heads6, expand, preferred_element_type=F32)
        ex = ex[0:3 * ROWS, :] + ex[3 * ROWS:6 * ROWS, :]
        dtx = ex[0:ROWS, :]
        eax = ex[ROWS:2 * ROWS, :]
        tex = ex[2 * ROWS:3 * ROWS, :]
        yield
        xs = xa_ref[:, gs]
        xdt = xs * dtx
        xdt_b = xdt.astype(BF16)
        xend_b = (xdt * tex).astype(BF16)
        bm = xa_ref[:, SSD_WIDTH + g * SSD_STATE:SSD_WIDTH + (g + 1) * SSD_STATE]
        cm = xa_ref[:, SSD_WIDTH + SSD_BC + g * SSD_STATE:SSD_WIDTH + SSD_BC + (g + 1) * SSD_STATE]
        cm_b = cm.astype(BF16)
        cb = _dot_nt(cm_b, bm)
        yield
        yd = []
        for pair in range(hg // 2):
            res = []
            for sub in range(2):
                h = g * hg + pair * 2 + sub
                seg = acum[:, h:h + 1] - acumt[h:h + 1, :]
                lmat = cb * jnp.exp(jnp.where(causal, seg, -jnp.inf))
                c0 = pair * 2 * SSD_HEAD_DIM
                res.append(jnp.dot(lmat.astype(BF16), xdt_b[:, c0:c0 + LANES],
                                   preferred_element_type=F32))
            yd.append(jnp.where(lane_lo, res[0], res[1]))
            yield
        yd = jnp.concatenate(yd, axis=1)
        st = st_ref[g]
        yoff = jnp.dot(cm_b, st.astype(BF16), preferred_element_type=F32) * eax
        st_ref[g] = (st * eax[ROWS - 1:ROWS, :]
                     + jnp.dot(bm.T.astype(BF16), xend_b, preferred_element_type=F32))
        yield
        y = yd + yoff + xs * dsk_ref[:, gs]
        y = y * _silu(z_ref[:, gs])
        ms = jnp.mean(y * y, axis=-1, keepdims=True)
        y_ref[:, MIX_SSD + g * gw:MIX_SSD + (g + 1) * gw] = (
            y * lax.rsqrt(ms + 1e-6) * nw_ref[:, gs]).astype(y_ref.dtype)
        yield


def _gdn_stages(qkv_ref, gate_ref, sm_ref, cw_ref, hp_ref, hpt_ref, nw_ref, y_ref, xpad_ref, xa_ref, st_ref):
    yield from _conv_silu_stages(qkv_ref, xpad_ref, xa_ref, cw_ref, None)

    row = _iota2((ROWS, ROWS), 0)
    col = _iota2((ROWS, ROWS), 1)
    incl = row >= col
    strict = row > col
    tril = incl.astype(F32)
    triu = (row <= col).astype(F32)

    sm = sm_ref[...]
    la = -jnp.exp(hp_ref[1:2, :]) * _softplus(sm + hp_ref[0:1, :])
    gcum = _dot_hi(tril, la)
    smt = sm.T
    lat = (-jnp.exp(hpt_ref[SM_B:SM_B + SUBLANES, 1:2])
           * _softplus(smt[SM_B:SM_B + SUBLANES, :] + hpt_ref[SM_B:SM_B + SUBLANES, 0:1]))
    gcumt = _dot_hi(lat, triu)
    beta_all = _sigmoid(sm)
    yield

    dh = GDN_HEAD_DIM
    heads = range(GDN_HEADS)
    qs, ks, xs, rhss, decays, gs = [], [], [], [], [], []
    for h in heads:
        q = xa_ref[:, h * dh:(h + 1) * dh]
        k = xa_ref[:, GDN_WIDTH + h * dh:GDN_WIDTH + (h + 1) * dh]
        v = xa_ref[:, 2 * GDN_WIDTH + h * dh:2 * GDN_WIDTH + (h + 1) * dh]
        q = q * lax.rsqrt(jnp.sum(q * q, axis=-1, keepdims=True) + 1e-6) * (dh ** -0.5)
        k = k * lax.rsqrt(jnp.sum(k * k, axis=-1, keepdims=True) + 1e-6)
        g = gcum[:, SM_A + h:SM_A + h + 1]
        gt = gcumt[SM_A - SM_B + h:SM_A - SM_B + h + 1, :]
        beta = beta_all[:, SM_B + h:SM_B + h + 1]
        decay = jnp.exp(jnp.where(incl, g - gt, -jnp.inf))
        kb = k * beta
        xs.append(-jnp.where(strict, _dot_nt(kb, k) * decay, 0.0))
        rhss.append(jnp.concatenate([v * beta, kb * jnp.exp(g)], axis=1))
        qs.append(q)
        ks.append(k)
        decays.append(decay)
        gs.append(g)
        yield
    ps = xs
    ns = xs
    levels = ROWS.bit_length() - 2
    for j in range(levels):
        if j < GDN_HI_LEVELS:
            pp = [_split(p) for p in ps]
            ps = [_dot3(p, p) for p in pp]
            yield
            pp = [_split(p) for p in ps]
            ns = [n + p + _dot3(_split(n), p2) for n, p, p2 in zip(ns, ps, pp)]
        else:
            ps = [_dot(p, p) for p in ps]
            yield
            ns = [n + p + _dot(n, p) for n, p in zip(ns, ps)]
        yield
    sols = [r + _dot3(_split(n), _split(r)) for n, r in zip(ns, rhss)]
    yield
    qks = [_dot_nt(q, k) * d for q, k, d in zip(qs, ks, decays)]
    sts = [st_ref[h] for h in heads]
    v_news = [s[:, 0:dh] - _dot(s[:, dh:2 * dh], st) for s, st in zip(sols, sts)]
    yield
    os_ = [_dot(q * jnp.exp(g), st) + _dot(qk, vn) for q, g, st, qk, vn in zip(qs, gs, sts, qks, v_news)]
    yield
    for h in heads:
        glast = gs[h][ROWS - 1:ROWS, :]
        kd = ks[h] * jnp.exp(glast - gs[h])
        st_ref[h] = sts[h] * jnp.exp(glast) + _dot(kd.T, v_news[h])
    yield
    for h in heads:
        o = os_[h]
        o = o * lax.rsqrt(jnp.mean(o * o, axis=-1, keepdims=True) + 1e-6)
        y_ref[:, MIX_GDN + h * dh:MIX_GDN + (h + 1) * dh] = (
            o * nw_ref[...] * _silu(gate_ref[:, h * dh:(h + 1) * dh])).astype(y_ref.dtype)
        yield


def _hgrn_stages(q_ref, f_ref, i_ref, gate_ref, lb_ref, nw_ref, y_ref, st_ref):
    row = _iota2((ROWS, ROWS), 0)
    col = _iota2((ROWS, ROWS), 1)
    tril = (row >= col).astype(F32)
    ones_b = jnp.ones((LANES, LANES), BF16)
    rmod = _iota2((ROWS, LANES), 0) & (SUBLANES - 1)
    rfull = _iota2((ROWS, LANES), 0)
    dk = HG_HEAD_DIM
    for h in range(HG_HEADS):
        sl = slice(h * dk, (h + 1) * dk)
        lb = lb_ref[:, sl]
        fr = f_ref[:, sl]
        log_sig = jnp.minimum(fr, 0.0) - jnp.log(1.0 + jnp.exp(-jnp.abs(fr)))
        a = jnp.log(lb)
        y = jnp.log1p(-lb) + log_sig
        log_f = jnp.maximum(a, y) + jnp.log(1.0 + jnp.exp(-jnp.abs(a - y)))
        k = (1.0 - lb) * _sigmoid(-fr)
        q = _silu(q_ref[:, sl])
        v = i_ref[:, sl]
        b = _dot_hi(tril, log_f)
        yield

        st = st_ref[h]
        o = _dot_nt(q * jnp.exp(b), st)
        blast = b[ROWS - 1:ROWS, :]
        st_ref[h] = st * jnp.exp(blast) + _dot_tn(v, k * jnp.exp(blast - b))
        yield

        pmat = jnp.zeros((ROWS, ROWS), F32)
        b3 = b.reshape(ROWS // SUBLANES, SUBLANES, dk)
        m = 1
        while m < ROWS:
            if m < SUBLANES:
                ref = None
                for p0 in range(0, SUBLANES, 2 * m):
                    cand = jnp.broadcast_to(b3[:, p0 + m - 1:p0 + m, :], b3.shape).reshape(ROWS, dk)
                    ref = cand if ref is None else jnp.where(rmod >= p0, cand, ref)
            else:
                ref = jnp.concatenate(
                    [jnp.broadcast_to(b[p0 + m - 1:p0 + m, :], (2 * m, dk)) for p0 in range(0, ROWS, 2 * m)],
                    axis=0)
            right = (rfull & m) != 0
            qs = q * jnp.exp(jnp.where(right, b - ref, -jnp.inf))
            ks = k * jnp.exp(jnp.where(right, -jnp.inf, ref - b))
            parent = ~(2 * m - 1)
            pmat = pmat + jnp.where((row & parent) == (col & parent), _dot_nt(qs, ks), 0.0)
            m *= 2
            yield
        diag = jnp.dot((q * k).astype(BF16), ones_b, preferred_element_type=F32)
        o = o + _dot(pmat, v) + diag * v
        yield

        o = o * lax.rsqrt(jnp.mean(o * o, axis=-1, keepdims=True) + 1e-6)
        y_ref[:, MIX_HG + h * dk:MIX_HG + (h + 1) * dk] = (
            o * nw_ref[...] * _silu(gate_ref[:, sl])).astype(y_ref.dtype)
        yield


def _mixer_kernel(z_ref, xbc_ref, sm_ref, qkv_ref, ggate_ref, hq_ref, hf_ref, hi_ref, hgate_ref,
                  scw_ref, scb_ref, shp_ref, shpt_ref, dsk_ref, snw_ref,
                  gcw_ref, ghp_ref, ghpt_ref, gnw_ref, lb_ref, hnw_ref,
                  y_ref, sxpad_ref, sxa_ref, sst_ref, gxpad_ref, gxa_ref, gst_ref, hst_ref):
    @pl.when(pl.program_id(1) == 0)
    def _():
        sst_ref[...] = jnp.zeros(sst_ref.shape, F32)
        gst_ref[...] = jnp.zeros(gst_ref.shape, F32)
        hst_ref[...] = jnp.zeros(hst_ref.shape, F32)
        sxpad_ref[0:SUBLANES, :] = jnp.zeros((SUBLANES, sxpad_ref.shape[1]), F32)
        gxpad_ref[0:SUBLANES, :] = jnp.zeros((SUBLANES, gxpad_ref.shape[1]), F32)

    streams = [
        _gdn_stages(qkv_ref, ggate_ref, sm_ref, gcw_ref, ghp_ref, ghpt_ref, gnw_ref, y_ref,
                    gxpad_ref, gxa_ref, gst_ref),
        _ssd_stages(z_ref, xbc_ref, sm_ref, scw_ref, scb_ref, shp_ref, shpt_ref, dsk_ref, snw_ref,
                    y_ref, sxpad_ref, sxa_ref, sst_ref),
        _hgrn_stages(hq_ref, hf_ref, hi_ref, hgate_ref, lb_ref, hnw_ref, y_ref, hst_ref),
    ]
    advance = [1, 1, 2]
    live = [True] * len(streams)
    while any(live):
        for s, stream in enumerate(streams):
            for _ in range(advance[s]):
                if live[s]:
                    try:
                        next(stream)
                    except StopIteration:
                        live[s] = False


def _mixers(proj, nb, seq, ssd_conv_w, ssd_conv_b, ssd_dt_bias, ssd_a_log, ssd_d, ssd_norm_w,
            gdn_conv_w, gdn_dt_bias, gdn_a_log, gdn_norm_w, lb, hg_norm_w):
    nblk = seq // ROWS
    shp = jnp.zeros((SUBLANES, LANES), F32)
    shp = shp.at[0, SM_DT:SM_DT + SSD_HEADS].set(ssd_dt_bias).at[1, SM_DT:SM_DT + SSD_HEADS].set(ssd_a_log)
    ghp = jnp.zeros((SUBLANES, LANES), F32)
    ghp = ghp.at[0, SM_A:SM_A + GDN_HEADS].set(gdn_dt_bias).at[1, SM_A:SM_A + GDN_HEADS].set(gdn_a_log)
    dsk = jnp.repeat(ssd_d, SSD_HEAD_DIM)[None, :]
    const = lambda b, c: (0, 0)

    def cols(width, off):
        return pl.BlockSpec((ROWS, width), lambda b, c: (b * nblk + c, off // width))

    def whole(shape):
        return pl.BlockSpec(shape, const)

    return pl.pallas_call(
        _mixer_kernel,
        out_shape=jax.ShapeDtypeStruct((nb * seq, D_MIX), BF16),
        grid=(nb, nblk),
        in_specs=[
            cols(SSD_WIDTH, OFF_Z), cols(SSD_XBC, OFF_XBC), cols(LANES, OFF_SMALL),
            cols(3 * GDN_WIDTH, OFF_QKV), cols(GDN_WIDTH, OFF_GGATE),
            cols(HG_WIDTH, OFF_HQ), cols(HG_WIDTH, OFF_HF), cols(HG_WIDTH, OFF_HI), cols(HG_WIDTH, OFF_HGATE),
            whole((SSD_CONV, SSD_XBC)), whole((1, SSD_XBC)), whole((SUBLANES, LANES)), whole((LANES, SUBLANES)),
            whole((1, SSD_WIDTH)), whole((1, SSD_WIDTH)),
            whole((GDN_CONV, 3 * GDN_WIDTH)), whole((SUBLANES, LANES)), whole((LANES, SUBLANES)),
            whole((1, GDN_HEAD_DIM)), whole((1, HG_WIDTH)), whole((1, HG_HEAD_DIM)),
        ],
        out_specs=pl.BlockSpec((ROWS, D_MIX), lambda b, c: (b * nblk + c, 0)),
        scratch_shapes=[pltpu.VMEM((ROWS + SUBLANES, SSD_XBC), F32),
                        pltpu.VMEM((ROWS, SSD_XBC), F32),
                        pltpu.VMEM((SSD_GROUPS, SSD_STATE, SSD_WIDTH // SSD_GROUPS), F32),
                        pltpu.VMEM((ROWS + SUBLANES, 3 * GDN_WIDTH), F32),
                        pltpu.VMEM((ROWS, 3 * GDN_WIDTH), F32),
                        pltpu.VMEM((GDN_HEADS, GDN_HEAD_DIM, GDN_HEAD_DIM), F32),
                        pltpu.VMEM((HG_HEADS, HG_HEAD_DIM, HG_HEAD_DIM), F32)],
        compiler_params=_params(("parallel", "arbitrary")),
        name="mixers",
    )(proj, proj, proj, proj, proj, proj, proj, proj, proj,
      ssd_conv_w, ssd_conv_b[None, :], shp, shp.T, dsk, ssd_norm_w[None, :],
      gdn_conv_w, ghp, ghp.T, gdn_norm_w[None, :], lb[None, :], hg_norm_w[None, :])


def _layer_norm(x, g, b):
    mu = jnp.mean(x, axis=-1, keepdims=True)
    xc = x - mu
    var = jnp.mean(xc * xc, axis=-1, keepdims=True)
    return xc * lax.rsqrt(var + 1e-5) * g + b


def _masked_max(x, mask):
    return jnp.max(jnp.where(mask, x, -jnp.inf), axis=-1, keepdims=True)


def _first_lane(cond, lane):
    return jnp.min(jnp.where(cond, lane, float(LANES)), axis=-1, keepdims=True)


def _outproj_kernel(y_ref, h_ref, w_ref, g_ref, b_ref, wr_ref, br_ref, h1_ref, h1b_ref, comb_ref):
    mix = jnp.dot(y_ref[...], w_ref[...], preferred_element_type=F32)
    h1 = _layer_norm(DN_ALPHA * h_ref[...] + mix, g_ref[...], b_ref[...])
    h1_ref[...] = h1
    h1b_ref[...] = h1.astype(BF16)

    h_hi, h_lo = _split(h1)
    wr = wr_ref[...]
    l2 = jnp.dot(h_hi, wr, preferred_element_type=F32) + jnp.dot(h_lo, wr, preferred_element_type=F32)
    logits = l2[:, 0:LANES] + l2[:, LANES:2 * LANES] + br_ref[...]
    lane_i = _iota2(logits.shape, 1)
    lane = lane_i.astype(F32)
    gmask = lane_i < MOE_GROUPS
    gmax = _masked_max(logits, gmask)
    gexp = jnp.where(gmask, jnp.exp(logits - gmax), 0.0)
    gprob = gexp / jnp.sum(gexp, axis=-1, keepdims=True)
    g_p = _masked_max(gprob, gmask)
    g_idx = _first_lane(gmask & (gprob == g_p), lane)
    egroup = ((lane_i - MOE_GROUPS) >> 2).astype(F32)
    emask = (lane_i >= MOE_GROUPS) & (lane_i < MOE_GROUPS + N_EXPERTS) & (egroup == g_idx)
    emax = _masked_max(logits, emask)
    eexp = jnp.where(emask, jnp.exp(logits - emax), 0.0)
    eprob = eexp / jnp.sum(eexp, axis=-1, keepdims=True)
    p1 = _masked_max(eprob, emask)
    i1 = _first_lane(emask & (eprob == p1), lane)
    emask2 = emask & (lane != i1)
    p2 = _masked_max(eprob, emask2)
    i2 = _first_lane(emask2 & (eprob == p2), lane)
    denom = p1 + p2
    w1 = g_p * p1 / denom
    w2 = g_p * p2 / denom
    comb_ref[...] = (jnp.where(lane == i1 - MOE_GROUPS, w1, 0.0)
                     + jnp.where(lane == i2 - MOE_GROUPS, w2, 0.0)
                     + jnp.where(lane_i == N_EXPERTS, g_idx, 0.0))


def _outproj(y, h, w_out, ln_g, ln_b, w_router, b_router, tm):
    t = h.shape[0]
    rowmap = lambda i: (i, 0)
    const = lambda i: (0, 0)
    return pl.pallas_call(
        _outproj_kernel,
        out_shape=(jax.ShapeDtypeStruct((t, D_MODEL), F32),
                   jax.ShapeDtypeStruct((t, D_MODEL), BF16),
                   jax.ShapeDtypeStruct((t, LANES), F32)),
        grid=(t // tm,),
        in_specs=[pl.BlockSpec((tm, D_MIX), rowmap),
                  pl.BlockSpec((tm, D_MODEL), rowmap),
                  pl.BlockSpec((D_MIX, D_MODEL), const),
                  pl.BlockSpec((1, D_MODEL), const),
                  pl.BlockSpec((1, D_MODEL), const),
                  pl.BlockSpec((D_MODEL, 2 * LANES), const),
                  pl.BlockSpec((1, LANES), const)],
        out_specs=(pl.BlockSpec((tm, D_MODEL), rowmap),
                   pl.BlockSpec((tm, D_MODEL), rowmap),
                   pl.BlockSpec((tm, LANES), rowmap)),
        compiler_params=_params(("parallel",)),
        name="out_proj_ln_router",
    )(y, h, w_out, ln_g[None, :], ln_b[None, :], w_router, b_router)


def _prefix_lanes(v, idx):
    axis = 1 if v.shape[0] == 1 else 0
    out = jnp.zeros_like(v)
    for g in range(MOE_GROUPS - 1):
        vg = v[:, g:g + 1] if axis == 1 else v[g:g + 1, :]
        out = out + jnp.where(idx > g, vg, 0.0)
    return out


def _moe_kernel(hb_ref, h_ref, comb_ref, wgu_ref, wdn_ref, g_ref, b_ref, o_ref, ob_ref,
                xs_ref, cw_ref, ys_ref):
    tm = hb_ref.shape[0]
    comb = comb_ref[...]
    lane = _iota2((tm, LANES), 1).astype(F32)
    gid = comb[:, N_EXPERTS:N_EXPERTS + 1]
    gsel = (lane == gid).astype(F32)
    tr = _iota2((tm, tm), 0)
    tc = _iota2((tm, tm), 1)

    rank_c = _dot((tc < tr).astype(F32), gsel)
    cnt_r = jnp.sum(gsel, axis=0, keepdims=True)
    start_r = _prefix_lanes(cnt_r, _iota2((1, LANES), 1))
    dest_c = jnp.sum(gsel * (start_r + rank_c), axis=1, keepdims=True)
    eye8 = (_iota2((SUBLANES, LANES), 0) == _iota2((SUBLANES, LANES), 1)).astype(F32)
    gsel_t = _dot_nt(eye8, gsel)
    rank_r = _dot(gsel_t, (tr < tc).astype(F32))
    cnt_c = jnp.sum(gsel_t, axis=1, keepdims=True)
    start_c = _prefix_lanes(cnt_c, _iota2((SUBLANES, 1), 0))
    dest_r = jnp.sum(gsel_t * (start_c + rank_r), axis=0, keepdims=True)

    perm = (dest_r == tr.astype(F32)).astype(BF16)
    xs_ref[...] = jnp.dot(perm, hb_ref[...], preferred_element_type=F32).astype(BF16)
    c_hi, c_lo = _split(comb)
    cw2 = jnp.dot(perm, jnp.concatenate([c_hi, c_lo], axis=1), preferred_element_type=F32)
    cw_ref[...] = cw2[:, 0:LANES] + cw2[:, LANES:2 * LANES]
    ys_ref[...] = jnp.zeros(ys_ref.shape, F32)

    lane1 = _iota2((1, LANES), 1)
    starts, ends = [], []
    for g in range(MOE_GROUPS):
        starts.append(jnp.sum(jnp.where(lane1 == g, start_r, 0.0)))
        ends.append(starts[g] + jnp.sum(jnp.where(lane1 == g, cnt_r, 0.0)))
    for c in range(tm // MOE_CHUNK):
        lo = c * MOE_CHUNK
        hi = lo + MOE_CHUNK
        for g in range(MOE_GROUPS):
            @pl.when((starts[g] < hi) & (ends[g] > lo))
            def _(lo=lo, hi=hi, g=g):
                x = xs_ref[lo:hi, :]
                cw = cw_ref[lo:hi, :]
                hm = []
                for e in range(EXPERTS_PER_GROUP):
                    ex = g * EXPERTS_PER_GROUP + e
                    gu = jnp.dot(x, wgu_ref[ex], preferred_element_type=F32)
                    hm.append((_silu(gu[:, 0:D_EXPERT]) * gu[:, D_EXPERT:2 * D_EXPERT]
                               * cw[:, ex:ex + 1]).astype(BF16))
                wd = wdn_ref[g * EXPERTS_PER_GROUP:(g + 1) * EXPERTS_PER_GROUP].reshape(
                    EXPERTS_PER_GROUP * D_EXPERT, D_MODEL)
                ys_ref[lo:hi, :] += jnp.dot(jnp.concatenate(hm, axis=1), wd, preferred_element_type=F32)

    unperm = (dest_c == tc.astype(F32)).astype(BF16)
    y = jnp.dot(unperm, ys_ref[...].astype(BF16), preferred_element_type=F32)
    h2 = _layer_norm(DN_ALPHA * h_ref[...] + y, g_ref[...], b_ref[...])
    o_ref[...] = h2
    ob_ref[...] = h2.astype(BF16)


def _moe(hb, h, comb, w_gu, w_dn, ln_g, ln_b, tm):
    t = h.shape[0]
    rowmap = lambda i: (i, 0)
    const = lambda i: (0, 0)
    const3 = lambda i: (0, 0, 0)
    return pl.pallas_call(
        _moe_kernel,
        out_shape=(jax.ShapeDtypeStruct((t, D_MODEL), F32),
                   jax.ShapeDtypeStruct((t, D_MODEL), BF16)),
        grid=(t // tm,),
        in_specs=[pl.BlockSpec((tm, D_MODEL), rowmap),
                  pl.BlockSpec((tm, D_MODEL), rowmap),
                  pl.BlockSpec((tm, LANES), rowmap),
                  pl.BlockSpec((N_EXPERTS, D_MODEL, 2 * D_EXPERT), const3, pipeline_mode=pl.Buffered(1)),
                  pl.BlockSpec((N_EXPERTS, D_EXPERT, D_MODEL), const3, pipeline_mode=pl.Buffered(1)),
                  pl.BlockSpec((1, D_MODEL), const),
                  pl.BlockSpec((1, D_MODEL), const)],
        out_specs=(pl.BlockSpec((tm, D_MODEL), rowmap),
                   pl.BlockSpec((tm, D_MODEL), rowmap)),
        scratch_shapes=[pltpu.VMEM((tm, D_MODEL), BF16),
                        pltpu.VMEM((tm, LANES), F32),
                        pltpu.VMEM((tm, D_MODEL), F32)],
        compiler_params=pltpu.CompilerParams(dimension_semantics=("parallel",),
                                             vmem_limit_bytes=MOE_VMEM_LIMIT),
        name="moe_ln",
    )(hb, h, comb, w_gu, w_dn, ln_g[None, :], ln_b[None, :])


def _split_w_in(w):
    sizes = (SSD_WIDTH, SSD_XBC, SSD_HEADS, 3 * GDN_WIDTH, GDN_WIDTH, GDN_HEADS, GDN_HEADS,
             HG_WIDTH, HG_WIDTH, HG_WIDTH, HG_WIDTH)
    parts, acc = [], 0
    for s in sizes:
        parts.append(w[:, acc:acc + s])
        acc += s
    z, xbc, dt, qkv, ggate, gb, ga, hq, hf, hi, hgate = parts
    cols = jnp.concatenate([xbc, qkv, z, ggate, hq, hf, hi, hgate, dt, gb, ga], axis=1).astype(BF16)
    return jnp.pad(cols, ((0, 0), (0, PROJ_COLS - cols.shape[1])))


def kernel(x, w_in, ssd_conv_w, ssd_conv_b, ssd_dt_bias, ssd_a_log, ssd_d, ssd_norm_w, gdn_conv_w, gdn_dt_bias, gdn_a_log, gdn_norm_w, hg_lb_logits, hg_norm_w, w_out, ln1_g, ln1_b, w_router_group, b_router_group, w_router_expert, b_router_expert, w_expert_gate_up, w_expert_down, ln2_g, ln2_b):
    nb, seq, d = x.shape
    t = nb * seq
    lb_cum = jnp.cumsum(jax.nn.softmax(hg_lb_logits.astype(F32), axis=0), axis=0)
    lb_all = lb_cum - lb_cum[0:1]
    h = x.reshape(t, d)
    hb = h.astype(BF16)
    for l in range(DEPTH):
        proj = _matmul(hb, _split_w_in(w_in[l]), 512, PROJ_COLS // 3)
        y = _mixers(proj, nb, seq, ssd_conv_w[l], ssd_conv_b[l], ssd_dt_bias[l], ssd_a_log[l], ssd_d[l],
                    ssd_norm_w[l], gdn_conv_w[l], gdn_dt_bias[l], gdn_a_log[l], gdn_norm_w[l],
                    lb_all[l], hg_norm_w[l])
        w_router = jnp.pad(jnp.concatenate([w_router_group[l], w_router_expert[l]], axis=1),
                           ((0, 0), (0, LANES - MOE_GROUPS - N_EXPERTS)))
        w_router = jnp.concatenate(_split(w_router), axis=1)
        b_router = jnp.pad(jnp.concatenate([b_router_group[l], b_router_expert[l]]),
                           (0, LANES - MOE_GROUPS - N_EXPERTS))[None, :]
        h1, h1b, comb = _outproj(y, h, w_out[l].astype(BF16), ln1_g[l], ln1_b[l], w_router, b_router, 512)
        h, hb = _moe(h1b, h1, comb, w_expert_gate_up[l].astype(BF16), w_expert_down[l].astype(BF16),
                     ln2_g[l], ln2_b[l], 512)
    return h.reshape(nb, seq, d)
```

```python
import jax
import jax.numpy as jnp
from jax import lax
from jax.experimental import pallas as pl
from jax.experimental.pallas import tpu as pltpu

F32 = jnp.float32
BF16 = jnp.bfloat16

D_MODEL = 1024
DEPTH = 4
SSD_HEADS = 16
SSD_HEAD_DIM = 64
SSD_WIDTH = 1024
SSD_GROUPS = 2
SSD_STATE = 128
SSD_BC = 256
SSD_XBC = 1536
SSD_CONV = 4
GDN_HEADS = 4
GDN_HEAD_DIM = 128
GDN_WIDTH = 512
GDN_CONV = 4
HG_HEADS = 4
HG_HEAD_DIM = 128
HG_WIDTH = 512
D_MIX = 2048
MOE_GROUPS = 4
EXPERTS_PER_GROUP = 4
N_EXPERTS = 16
D_EXPERT = 256
DN_ALPHA = (2 * DEPTH) ** 0.25

LANES = 128
SUBLANES = 8
ROWS = 128
VMEM_LIMIT = 48 * 1024 * 1024
MOE_VMEM_LIMIT = 56 * 1024 * 1024
CONV_SLAB = 256
GDN_HI_LEVELS = 2
MOE_CHUNK = 128
PROJ_SLAB = 768

OFF_XBC = 0
OFF_QKV = 1536
OFF_Z = 3072
OFF_GGATE = 4096
OFF_HQ = 4608
OFF_HF = 5120
OFF_HI = 5632
OFF_HGATE = 6144
OFF_SMALL = 6656
PROJ_COLS = 6912
SM_DT = 0
SM_B = 16
SM_A = 20
MIX_SSD = 0
MIX_GDN = SSD_WIDTH
MIX_HG = SSD_WIDTH + GDN_WIDTH


def _dot(a, b):
    return jnp.dot(a.astype(BF16), b.astype(BF16), preferred_element_type=F32)


def _dot_nt(a, b):
    return lax.dot_general(a.astype(BF16), b.astype(BF16), (((1,), (1,)), ((), ())),
                           preferred_element_type=F32)


def _dot_tn(a, b):
    return lax.dot_general(a.astype(BF16), b.astype(BF16), (((0,), (0,)), ((), ())),
                           preferred_element_type=F32)


def _split3(a):
    p1 = a.astype(BF16)
    r1 = a - p1.astype(F32)
    p2 = r1.astype(BF16)
    return p1, p2, (r1 - p2.astype(F32)).astype(BF16)


def _dot_01(m01, x):
    m = m01.astype(BF16)
    p1, p2, p3 = _split3(x)
    return (jnp.dot(m, p1, preferred_element_type=F32) + jnp.dot(m, p2, preferred_element_type=F32)
            + jnp.dot(m, p3, preferred_element_type=F32))


def _dot_x01(x, m01):
    m = m01.astype(BF16)
    p1, p2, p3 = _split3(x)
    return (jnp.dot(p1, m, preferred_element_type=F32) + jnp.dot(p2, m, preferred_element_type=F32)
            + jnp.dot(p3, m, preferred_element_type=F32))


def _split(a):
    hi = a.astype(BF16)
    return hi, (a - hi.astype(F32)).astype(BF16)


def _dot3(a, b):
    (ah, al), (bh, bl) = a, b
    return (jnp.dot(ah, bh, preferred_element_type=F32) + jnp.dot(ah, bl, preferred_element_type=F32)
            + jnp.dot(al, bh, preferred_element_type=F32))


def _sigmoid(x):
    return 1.0 / (1.0 + jnp.exp(-x))


def _silu(x):
    return x * _sigmoid(x)


def _softplus(x):
    return jnp.maximum(x, 0.0) + jnp.log(1.0 + jnp.exp(-jnp.abs(x)))


def _iota2(shape, dim):
    return lax.broadcasted_iota(jnp.int32, shape, dim)


def _params(sem):
    return pltpu.CompilerParams(dimension_semantics=sem, vmem_limit_bytes=VMEM_LIMIT)


def _matmul_kernel(x_ref, w_ref, o_ref):
    o_ref[...] = jnp.dot(x_ref[...], w_ref[...], preferred_element_type=F32)


def _matmul(x, w, tm, tn):
    t, k = x.shape
    n = w.shape[1]
    return pl.pallas_call(
        _matmul_kernel,
        out_shape=jax.ShapeDtypeStruct((t, n), F32),
        grid=(n // tn, t // tm),
        in_specs=[pl.BlockSpec((tm, k), lambda j, i: (i, 0)),
                  pl.BlockSpec((k, tn), lambda j, i: (0, j))],
        out_specs=pl.BlockSpec((tm, tn), lambda j, i: (i, j)),
        compiler_params=_params(("parallel", "arbitrary")),
        name="in_proj",
    )(x, w)


def _conv_silu_stages(x_ref, xpad_ref, xa_ref, cw_ref, cb_ref):
    rows, cols = x_ref.shape
    k = cw_ref.shape[0]
    for c0 in range(0, cols, CONV_SLAB):
        cs = slice(c0, c0 + CONV_SLAB)
        x = x_ref[:, cs]
        xpad_ref[SUBLANES:SUBLANES + rows, cs] = x
        acc = x * cw_ref[k - 1:k, cs]
        if cb_ref is not None:
            acc = acc + cb_ref[:, cs]
        for i in range(k - 1):
            off = SUBLANES - (k - 1) + i
            acc = acc + xpad_ref[off:off + rows, cs] * cw_ref[i:i + 1, cs]
        xa_ref[:, cs] = _silu(acc)
        xpad_ref[0:SUBLANES, cs] = xpad_ref[rows:rows + SUBLANES, cs]
        yield


def _ssd_stages(z_ref, xbc_ref, sm_ref, cw_ref, cb_ref, hp_ref, hpt_ref, dsk_ref, nw_ref,
                y_ref, xpad_ref, xa_ref, st_ref):
    yield from _conv_silu_stages(xbc_ref, xpad_ref, xa_ref, cw_ref, cb_ref)

    row = _iota2((ROWS, ROWS), 0)
    col = _iota2((ROWS, ROWS), 1)
    causal = row >= col
    tril = causal.astype(F32)
    triu = (row <= col).astype(F32)

    sm = sm_ref[...]
    dt = _softplus(sm + hp_ref[0:1, :])
    da = dt * (-jnp.exp(hp_ref[1:2, :]))
    acum = _dot_01(tril, da)
    smt = sm.T
    dtt = _softplus(smt[0:SSD_HEADS, :] + hpt_ref[0:SSD_HEADS, 0:1])
    dat = dtt * (-jnp.exp(hpt_ref[0:SSD_HEADS, 1:2]))
    acumt = _dot_x01(dat, triu)

    heads3 = jnp.concatenate([dt, jnp.exp(acum), jnp.exp(acum[ROWS - 1:ROWS, :] - acum)], axis=0)
    h_hi, h_lo = _split(heads3)
    heads6 = jnp.concatenate([h_hi, h_lo], axis=0)
    lane_lo = _iota2((ROWS, LANES), 1) < SSD_HEAD_DIM
    yield

    hg = SSD_HEADS // SSD_GROUPS
    gw = SSD_WIDTH // SSD_GROUPS
    hrow = _iota2((LANES, gw), 0)
    hcol = _iota2((LANES, gw), 1)
    for g in range(SSD_GROUPS):
        gs = slice(g * gw, (g + 1) * gw)
        expand = (((hcol >> 6) + g * hg) == hrow).astype(BF16)
        ex = jnp.dot(heads6, expand, preferred_element_type=F32)
        ex = ex[0:3 * ROWS, :] + ex[3 * ROWS:6 * ROWS, :]
        dtx = ex[0:ROWS, :]
        eax = ex[ROWS:2 * ROWS, :]
        tex = ex[2 * ROWS:3 * ROWS, :]
        yield
        xs = xa_ref[:, gs]
        xdt = xs * dtx
        xdt_b = xdt.astype(BF16)
        xend_b = (xdt * tex).astype(BF16)
        bm = xa_ref[:, SSD_WIDTH + g * SSD_STATE:SSD_WIDTH + (g + 1) * SSD_STATE]
        cm = xa_ref[:, SSD_WIDTH + SSD_BC + g * SSD_STATE:SSD_WIDTH + SSD_BC + (g + 1) * SSD_STATE]
        cm_b = cm.astype(BF16)
        cb = _dot_nt(cm_b, bm)
        yield
        yd = []
        for pair in range(hg // 2):
            res = []
            for sub in range(2):
                h = g * hg + pair * 2 + sub
                seg = acum[:, h:h + 1] - acumt[h:h + 1, :]
                lmat = cb * jnp.exp(jnp.where(causal, seg, -jnp.inf))
                c0 = pair * 2 * SSD_HEAD_DIM
                res.append(jnp.dot(lmat.astype(BF16), xdt_b[:, c0:c0 + LANES],
                                   preferred_element_type=F32))
            yd.append(jnp.where(lane_lo, res[0], res[1]))
            yield
        yd = jnp.concatenate(yd, axis=1)
        st = st_ref[g]
        yoff = jnp.dot(cm_b, st.astype(BF16), preferred_element_type=F32) * eax
        st_ref[g] = (st * eax[ROWS - 1:ROWS, :]
                     + jnp.dot(bm.T.astype(BF16), xend_b, preferred_element_type=F32))
        yield
        y = yd + yoff + xs * dsk_ref[:, gs]
        y = y * _silu(z_ref[:, gs])
        ms = jnp.mean(y * y, axis=-1, keepdims=True)
        y_ref[:, MIX_SSD + g * gw:MIX_SSD + (g + 1) * gw] = (
            y * lax.rsqrt(ms + 1e-6) * nw_ref[:, gs]).astype(y_ref.dtype)
        yield


def _gdn_stages(qkv_ref, gate_ref, sm_ref, cw_ref, hp_ref, hpt_ref, nw_ref, y_ref, xpad_ref, xa_ref, st_ref):
    yield from _conv_silu_stages(qkv_ref, xpad_ref, xa_ref, cw_ref, None)

    row = _iota2((ROWS, ROWS), 0)
    col = _iota2((ROWS, ROWS), 1)
    incl = row >= col
    strict = row > col
    tril = incl.astype(F32)
    triu = (row <= col).astype(F32)

    sm = sm_ref[...]
    la = -jnp.exp(hp_ref[1:2, :]) * _softplus(sm + hp_ref[0:1, :])
    gcum = _dot_01(tril, la)
    smt = sm.T
    lat = (-jnp.exp(hpt_ref[SM_B:SM_B + SUBLANES, 1:2])
           * _softplus(smt[SM_B:SM_B + SUBLANES, :] + hpt_ref[SM_B:SM_B + SUBLANES, 0:1]))
    gcumt = _dot_x01(lat, triu)
    beta_all = _sigmoid(sm)
    yield

    dh = GDN_HEAD_DIM
    heads = range(GDN_HEADS)
    qs, ks, xs, rhss, decays, gs = [], [], [], [], [], []
    for h in heads:
        q = xa_ref[:, h * dh:(h + 1) * dh]
        k = xa_ref[:, GDN_WIDTH + h * dh:GDN_WIDTH + (h + 1) * dh]
        v = xa_ref[:, 2 * GDN_WIDTH + h * dh:2 * GDN_WIDTH + (h + 1) * dh]
        q = q * lax.rsqrt(jnp.sum(q * q, axis=-1, keepdims=True) + 1e-6) * (dh ** -0.5)
        k = k * lax.rsqrt(jnp.sum(k * k, axis=-1, keepdims=True) + 1e-6)
        g = gcum[:, SM_A + h:SM_A + h + 1]
        gt = gcumt[SM_A - SM_B + h:SM_A - SM_B + h + 1, :]
        beta = beta_all[:, SM_B + h:SM_B + h + 1]
        decay = jnp.exp(jnp.where(incl, g - gt, -jnp.inf))
        kb = k * beta
        xs.append(-jnp.where(strict, _dot_nt(kb, k) * decay, 0.0))
        rhss.append(jnp.concatenate([v * beta, kb * jnp.exp(g)], axis=1))
        qs.append(q)
        ks.append(k)
        decays.append(decay)
        gs.append(g)
        yield
    ps = xs
    ns = xs
    levels = ROWS.bit_length() - 2
    for j in range(levels):
        if j < GDN_HI_LEVELS:
            pp = [_split(p) for p in ps]
            ps = [_dot3(p, p) for p in pp]
            yield
            pp = [_split(p) for p in ps]
            ns = [n + p + _dot3(_split(n), p2) for n, p, p2 in zip(ns, ps, pp)]
        else:
            ps = [_dot(p, p) for p in ps]
            yield
            ns = [n + p + _dot(n, p) for n, p in zip(ns, ps)]
        yield
    sols = [r + _dot3(_split(n), _split(r)) for n, r in zip(ns, rhss)]
    yield
    qks = [_dot_nt(q, k) * d for q, k, d in zip(qs, ks, decays)]
    sts = [st_ref[h] for h in heads]
    v_news = [s[:, 0:dh] - _dot(s[:, dh:2 * dh], st) for s, st in zip(sols, sts)]
    yield
    os_ = [_dot(q * jnp.exp(g), st) + _dot(qk, vn) for q, g, st, qk, vn in zip(qs, gs, sts, qks, v_news)]
    yield
    for h in heads:
        glast = gs[h][ROWS - 1:ROWS, :]
        kd = ks[h] * jnp.exp(glast - gs[h])
        st_ref[h] = sts[h] * jnp.exp(glast) + _dot(kd.T, v_news[h])
    yield
    for h in heads:
        o = os_[h]
        o = o * lax.rsqrt(jnp.mean(o * o, axis=-1, keepdims=True) + 1e-6)
        y_ref[:, MIX_GDN + h * dh:MIX_GDN + (h + 1) * dh] = (
            o * nw_ref[...] * _silu(gate_ref[:, h * dh:(h + 1) * dh])).astype(y_ref.dtype)
        yield


def _hgrn_stages(q_ref, f_ref, i_ref, gate_ref, lb_ref, nw_ref, y_ref, st_ref):
    row = _iota2((ROWS, ROWS), 0)
    col = _iota2((ROWS, ROWS), 1)
    tril = (row >= col).astype(F32)
    rmod = _iota2((ROWS, LANES), 0) & (SUBLANES - 1)
    rfull = _iota2((ROWS, LANES), 0)
    dk = HG_HEAD_DIM
    for h in range(HG_HEADS):
        sl = slice(h * dk, (h + 1) * dk)
        lb = lb_ref[:, sl]
        fr = f_ref[:, sl]
        log_sig = jnp.minimum(fr, 0.0) - jnp.log(1.0 + jnp.exp(-jnp.abs(fr)))
        a = jnp.log(lb)
        y = jnp.log1p(-lb) + log_sig
        log_f = jnp.maximum(a, y) + jnp.log(1.0 + jnp.exp(-jnp.abs(a - y)))
        k = (1.0 - lb) * _sigmoid(-fr)
        q = _silu(q_ref[:, sl])
        v = i_ref[:, sl]
        b = _dot_01(tril, log_f)
        yield

        st = st_ref[h]
        o = _dot_nt(q * jnp.exp(b), st)
        blast = b[ROWS - 1:ROWS, :]
        st_ref[h] = st * jnp.exp(blast) + _dot_tn(v, k * jnp.exp(blast - b))
        yield

        pmat = jnp.zeros((ROWS, ROWS), F32)
        b3 = b.reshape(ROWS // SUBLANES, SUBLANES, dk)
        m = 1
        while m < ROWS:
            if m < SUBLANES:
                ref = None
                for p0 in range(0, SUBLANES, 2 * m):
                    cand = jnp.broadcast_to(b3[:, p0 + m - 1:p0 + m, :], b3.shape).reshape(ROWS, dk)
                    ref = cand if ref is None else jnp.where(rmod >= p0, cand, ref)
            else:
                ref = jnp.concatenate(
                    [jnp.broadcast_to(b[p0 + m - 1:p0 + m, :], (2 * m, dk)) for p0 in range(0, ROWS, 2 * m)],
                    axis=0)
            right = (rfull & m) != 0
            qs = q * jnp.exp(jnp.where(right, b - ref, -jnp.inf))
            ks = k * jnp.exp(jnp.where(right, -jnp.inf, ref - b))
            parent = ~(2 * m - 1)
            pmat = pmat + jnp.where((row & parent) == (col & parent), _dot_nt(qs, ks), 0.0)
            m *= 2
            yield
        o = o + _dot(pmat, v) + jnp.sum(q * k, axis=-1, keepdims=True) * v
        yield

        o = o * lax.rsqrt(jnp.mean(o * o, axis=-1, keepdims=True) + 1e-6)
        y_ref[:, MIX_HG + h * dk:MIX_HG + (h + 1) * dk] = (
            o * nw_ref[...] * _silu(gate_ref[:, sl])).astype(y_ref.dtype)
        yield


def _in_proj_stages(hb_ref, w_ref, out_ref):
    hb = hb_ref[...]
    for c0 in range(0, PROJ_COLS, PROJ_SLAB):
        out_ref[:, c0:c0 + PROJ_SLAB] = jnp.dot(hb, w_ref[:, c0:c0 + PROJ_SLAB], preferred_element_type=F32)
        yield


def _mixer_kernel(hb0_ref, hbn_ref, w_ref,
                  scw_ref, scb_ref, shp_ref, shpt_ref, dsk_ref, snw_ref,
                  gcw_ref, ghp_ref, ghpt_ref, gnw_ref, lb_ref, hnw_ref,
                  y_ref, proj_ref, sxpad_ref, sxa_ref, sst_ref, gxpad_ref, gxa_ref, gst_ref, hst_ref):
    c = pl.program_id(1)

    @pl.when(c == 0)
    def _():
        sst_ref[...] = jnp.zeros(sst_ref.shape, F32)
        gst_ref[...] = jnp.zeros(gst_ref.shape, F32)
        hst_ref[...] = jnp.zeros(hst_ref.shape, F32)
        sxpad_ref[0:SUBLANES, :] = jnp.zeros((SUBLANES, sxpad_ref.shape[1]), F32)
        gxpad_ref[0:SUBLANES, :] = jnp.zeros((SUBLANES, gxpad_ref.shape[1]), F32)
        for _ in _in_proj_stages(hb0_ref, w_ref, proj_ref.at[0]):
            pass

    cur = proj_ref.at[c % 2]
    nxt = proj_ref.at[(c + 1) % 2]

    def seg(off, width):
        return cur.at[:, off:off + width]

    sm_ref = seg(OFF_SMALL, LANES)
    streams = [
        _gdn_stages(seg(OFF_QKV, 3 * GDN_WIDTH), seg(OFF_GGATE, GDN_WIDTH), sm_ref, gcw_ref, ghp_ref, ghpt_ref,
                    gnw_ref, y_ref, gxpad_ref, gxa_ref, gst_ref),
        _ssd_stages(seg(OFF_Z, SSD_WIDTH), seg(OFF_XBC, SSD_XBC), sm_ref, scw_ref, scb_ref, shp_ref, shpt_ref,
                    dsk_ref, snw_ref, y_ref, sxpad_ref, sxa_ref, sst_ref),
        _hgrn_stages(seg(OFF_HQ, HG_WIDTH), seg(OFF_HF, HG_WIDTH), seg(OFF_HI, HG_WIDTH),
                     seg(OFF_HGATE, HG_WIDTH), lb_ref, hnw_ref, y_ref, hst_ref),
        _in_proj_stages(hbn_ref, w_ref, nxt),
    ]
    advance = [1, 1, 2, 1]
    live = [True] * len(streams)
    while any(live):
        for s, stream in enumerate(streams):
            for _ in range(advance[s]):
                if live[s]:
                    try:
                        next(stream)
                    except StopIteration:
                        live[s] = False


def _mixers(hb, w_proj, nb, seq, ssd_conv_w, ssd_conv_b, ssd_dt_bias, ssd_a_log, ssd_d, ssd_norm_w,
            gdn_conv_w, gdn_dt_bias, gdn_a_log, gdn_norm_w, lb, hg_norm_w):
    nblk = seq // ROWS
    shp = jnp.zeros((SUBLANES, LANES), F32)
    shp = shp.at[0, SM_DT:SM_DT + SSD_HEADS].set(ssd_dt_bias).at[1, SM_DT:SM_DT + SSD_HEADS].set(ssd_a_log)
    ghp = jnp.zeros((SUBLANES, LANES), F32)
    ghp = ghp.at[0, SM_A:SM_A + GDN_HEADS].set(gdn_dt_bias).at[1, SM_A:SM_A + GDN_HEADS].set(gdn_a_log)
    dsk = jnp.repeat(ssd_d, SSD_HEAD_DIM)[None, :]
    const = lambda b, c: (0, 0)

    def whole(shape):
        return pl.BlockSpec(shape, const)

    return pl.pallas_call(
        _mixer_kernel,
        out_shape=jax.ShapeDtypeStruct((nb * seq, D_MIX), BF16),
        grid=(nb, nblk),
        in_specs=[
            pl.BlockSpec((ROWS, D_MODEL), lambda b, c: (b * nblk + c, 0)),
            pl.BlockSpec((ROWS, D_MODEL), lambda b, c: (b * nblk + jnp.minimum(c + 1, nblk - 1), 0)),
            pl.BlockSpec((D_MODEL, PROJ_COLS), const, pipeline_mode=pl.Buffered(1)),
            whole((SSD_CONV, SSD_XBC)), whole((1, SSD_XBC)), whole((SUBLANES, LANES)), whole((LANES, SUBLANES)),
            whole((1, SSD_WIDTH)), whole((1, SSD_WIDTH)),
            whole((GDN_CONV, 3 * GDN_WIDTH)), whole((SUBLANES, LANES)), whole((LANES, SUBLANES)),
            whole((1, GDN_HEAD_DIM)), whole((1, HG_WIDTH)), whole((1, HG_HEAD_DIM)),
        ],
        out_specs=pl.BlockSpec((ROWS, D_MIX), lambda b, c: (b * nblk + c, 0)),
        scratch_shapes=[pltpu.VMEM((2, ROWS, PROJ_COLS), F32),
                        pltpu.VMEM((ROWS + SUBLANES, SSD_XBC), F32),
                        pltpu.VMEM((ROWS, SSD_XBC), F32),
                        pltpu.VMEM((SSD_GROUPS, SSD_STATE, SSD_WIDTH // SSD_GROUPS), F32),
                        pltpu.VMEM((ROWS + SUBLANES, 3 * GDN_WIDTH), F32),
                        pltpu.VMEM((ROWS, 3 * GDN_WIDTH), F32),
                        pltpu.VMEM((GDN_HEADS, GDN_HEAD_DIM, GDN_HEAD_DIM), F32),
                        pltpu.VMEM((HG_HEADS, HG_HEAD_DIM, HG_HEAD_DIM), F32)],
        compiler_params=_params(("parallel", "arbitrary")),
        name="in_proj_mixers",
    )(hb, hb, w_proj,
      ssd_conv_w, ssd_conv_b[None, :], shp, shp.T, dsk, ssd_norm_w[None, :],
      gdn_conv_w, ghp, ghp.T, gdn_norm_w[None, :], lb[None, :], hg_norm_w[None, :])


def _layer_norm(x, g, b):
    mu = jnp.mean(x, axis=-1, keepdims=True)
    xc = x - mu
    var = jnp.mean(xc * xc, axis=-1, keepdims=True)
    return xc * lax.rsqrt(var + 1e-5) * g + b


def _masked_max(x, mask):
    return jnp.max(jnp.where(mask, x, -jnp.inf), axis=-1, keepdims=True)


def _first_lane(cond, lane):
    return jnp.min(jnp.where(cond, lane, float(LANES)), axis=-1, keepdims=True)


def _outproj_kernel(y_ref, h_ref, w_ref, g_ref, b_ref, wr_ref, br_ref, h1_ref, h1b_ref, comb_ref):
    mix = jnp.dot(y_ref[...], w_ref[...], preferred_element_type=F32)
    h1 = _layer_norm(DN_ALPHA * h_ref[...] + mix, g_ref[...], b_ref[...])
    h1_ref[...] = h1
    h1b_ref[...] = h1.astype(BF16)

    h_hi, h_lo = _split(h1)
    wr = wr_ref[...]
    l2 = jnp.dot(h_hi, wr, preferred_element_type=F32) + jnp.dot(h_lo, wr, preferred_element_type=F32)
    logits = l2[:, 0:LANES] + l2[:, LANES:2 * LANES] + br_ref[...]
    lane_i = _iota2(logits.shape, 1)
    lane = lane_i.astype(F32)
    gmask = lane_i < MOE_GROUPS
    gmax = _masked_max(logits, gmask)
    gexp = jnp.where(gmask, jnp.exp(logits - gmax), 0.0)
    gprob = gexp / jnp.sum(gexp, axis=-1, keepdims=True)
    g_p = _masked_max(gprob, gmask)
    g_idx = _first_lane(gmask & (gprob == g_p), lane)
    egroup = ((lane_i - MOE_GROUPS) >> 2).astype(F32)
    emask = (lane_i >= MOE_GROUPS) & (lane_i < MOE_GROUPS + N_EXPERTS) & (egroup == g_idx)
    emax = _masked_max(logits, emask)
    eexp = jnp.where(emask, jnp.exp(logits - emax), 0.0)
    eprob = eexp / jnp.sum(eexp, axis=-1, keepdims=True)
    p1 = _masked_max(eprob, emask)
    i1 = _first_lane(emask & (eprob == p1), lane)
    emask2 = emask & (lane != i1)
    p2 = _masked_max(eprob, emask2)
    i2 = _first_lane(emask2 & (eprob == p2), lane)
    denom = p1 + p2
    w1 = g_p * p1 / denom
    w2 = g_p * p2 / denom
    comb_ref[...] = (jnp.where(lane == i1 - MOE_GROUPS, w1, 0.0)
                     + jnp.where(lane == i2 - MOE_GROUPS, w2, 0.0)
                     + jnp.where(lane_i == N_EXPERTS, g_idx, 0.0))


def _outproj(y, h, w_out, ln_g, ln_b, w_router, b_router, tm):
    t = h.shape[0]
    rowmap = lambda i: (i, 0)
    const = lambda i: (0, 0)
    return pl.pallas_call(
        _outproj_kernel,
        out_shape=(jax.ShapeDtypeStruct((t, D_MODEL), F32),
                   jax.ShapeDtypeStruct((t, D_MODEL), BF16),
                   jax.ShapeDtypeStruct((t, LANES), F32)),
        grid=(t // tm,),
        in_specs=[pl.BlockSpec((tm, D_MIX), rowmap),
                  pl.BlockSpec((tm, D_MODEL), rowmap),
                  pl.BlockSpec((D_MIX, D_MODEL), const),
                  pl.BlockSpec((1, D_MODEL), const),
                  pl.BlockSpec((1, D_MODEL), const),
                  pl.BlockSpec((D_MODEL, 2 * LANES), const),
                  pl.BlockSpec((1, LANES), const)],
        out_specs=(pl.BlockSpec((tm, D_MODEL), rowmap),
                   pl.BlockSpec((tm, D_MODEL), rowmap),
                   pl.BlockSpec((tm, LANES), rowmap)),
        compiler_params=_params(("parallel",)),
        name="out_proj_ln_router",
    )(y, h, w_out, ln_g[None, :], ln_b[None, :], w_router, b_router)


def _prefix_lanes(v, idx):
    axis = 1 if v.shape[0] == 1 else 0
    out = jnp.zeros_like(v)
    for g in range(MOE_GROUPS - 1):
        vg = v[:, g:g + 1] if axis == 1 else v[g:g + 1, :]
        out = out + jnp.where(idx > g, vg, 0.0)
    return out


def _moe_kernel(hb_ref, h_ref, comb_ref, wgu_ref, wdn_ref, g_ref, b_ref, o_ref, ob_ref,
                xs_ref, cw_ref, ys_ref):
    tm = hb_ref.shape[0]
    comb = comb_ref[...]
    lane = _iota2((tm, LANES), 1).astype(F32)
    gid = comb[:, N_EXPERTS:N_EXPERTS + 1]
    gsel = (lane == gid).astype(F32)
    tr = _iota2((tm, tm), 0)
    tc = _iota2((tm, tm), 1)

    rank_c = _dot((tc < tr).astype(F32), gsel)
    cnt_r = jnp.sum(gsel, axis=0, keepdims=True)
    start_r = _prefix_lanes(cnt_r, _iota2((1, LANES), 1))
    dest_c = jnp.sum(gsel * (start_r + rank_c), axis=1, keepdims=True)
    eye8 = (_iota2((SUBLANES, LANES), 0) == _iota2((SUBLANES, LANES), 1)).astype(F32)
    gsel_t = _dot_nt(eye8, gsel)
    rank_r = _dot(gsel_t, (tr < tc).astype(F32))
    cnt_c = jnp.sum(gsel_t, axis=1, keepdims=True)
    start_c = _prefix_lanes(cnt_c, _iota2((SUBLANES, 1), 0))
    dest_r = jnp.sum(gsel_t * (start_c + rank_r), axis=0, keepdims=True)

    perm = (dest_r == tr.astype(F32)).astype(BF16)
    xs_ref[...] = jnp.dot(perm, hb_ref[...], preferred_element_type=F32).astype(BF16)
    c_hi, c_lo = _split(comb)
    cw2 = jnp.dot(perm, jnp.concatenate([c_hi, c_lo], axis=1), preferred_element_type=F32)
    cw_ref[...] = cw2[:, 0:LANES] + cw2[:, LANES:2 * LANES]
    ys_ref[...] = jnp.zeros(ys_ref.shape, F32)

    lane1 = _iota2((1, LANES), 1)
    starts, ends = [], []
    for g in range(MOE_GROUPS):
        starts.append(jnp.sum(jnp.where(lane1 == g, start_r, 0.0)))
        ends.append(starts[g] + jnp.sum(jnp.where(lane1 == g, cnt_r, 0.0)))
    for c in range(tm // MOE_CHUNK):
        lo = c * MOE_CHUNK
        hi = lo + MOE_CHUNK
        for g in range(MOE_GROUPS):
            @pl.when((starts[g] < hi) & (ends[g] > lo))
            def _(lo=lo, hi=hi, g=g):
                x = xs_ref[lo:hi, :]
                cw = cw_ref[lo:hi, :]
                hm = []
                for e in range(EXPERTS_PER_GROUP):
                    ex = g * EXPERTS_PER_GROUP + e
                    gu = jnp.dot(x, wgu_ref[ex], preferred_element_type=F32)
                    hm.append((_silu(gu[:, 0:D_EXPERT]) * gu[:, D_EXPERT:2 * D_EXPERT]
                               * cw[:, ex:ex + 1]).astype(BF16))
                wd = wdn_ref[g * EXPERTS_PER_GROUP:(g + 1) * EXPERTS_PER_GROUP].reshape(
                    EXPERTS_PER_GROUP * D_EXPERT, D_MODEL)
                ys_ref[lo:hi, :] += jnp.dot(jnp.concatenate(hm, axis=1), wd, preferred_element_type=F32)

    unperm = (dest_c == tc.astype(F32)).astype(BF16)
    y = jnp.dot(unperm, ys_ref[...].astype(BF16), preferred_element_type=F32)
    h2 = _layer_norm(DN_ALPHA * h_ref[...] + y, g_ref[...], b_ref[...])
    o_ref[...] = h2
    ob_ref[...] = h2.astype(BF16)


def _moe(hb, h, comb, w_gu, w_dn, ln_g, ln_b, tm):
    t = h.shape[0]
    rowmap = lambda i: (i, 0)
    const = lambda i: (0, 0)
    const3 = lambda i: (0, 0, 0)
    return pl.pallas_call(
        _moe_kernel,
        out_shape=(jax.ShapeDtypeStruct((t, D_MODEL), F32),
                   jax.ShapeDtypeStruct((t, D_MODEL), BF16)),
        grid=(t // tm,),
        in_specs=[pl.BlockSpec((tm, D_MODEL), rowmap),
                  pl.BlockSpec((tm, D_MODEL), rowmap),
                  pl.BlockSpec((tm, LANES), rowmap),
                  pl.BlockSpec((N_EXPERTS, D_MODEL, 2 * D_EXPERT), const3, pipeline_mode=pl.Buffered(1)),
                  pl.BlockSpec((N_EXPERTS, D_EXPERT, D_MODEL), const3, pipeline_mode=pl.Buffered(1)),
                  pl.BlockSpec((1, D_MODEL), const),
                  pl.BlockSpec((1, D_MODEL), const)],
        out_specs=(pl.BlockSpec((tm, D_MODEL), rowmap),
                   pl.BlockSpec((tm, D_MODEL), rowmap)),
        scratch_shapes=[pltpu.VMEM((tm, D_MODEL), BF16),
                        pltpu.VMEM((tm, LANES), F32),
                        pltpu.VMEM((tm, D_MODEL), F32)],
        compiler_params=pltpu.CompilerParams(dimension_semantics=("parallel",),
                                             vmem_limit_bytes=MOE_VMEM_LIMIT),
        name="moe_ln",
    )(hb, h, comb, w_gu, w_dn, ln_g[None, :], ln_b[None, :])


def _split_w_in(w):
    sizes = (SSD_WIDTH, SSD_XBC, SSD_HEADS, 3 * GDN_WIDTH, GDN_WIDTH, GDN_HEADS, GDN_HEADS,
             HG_WIDTH, HG_WIDTH, HG_WIDTH, HG_WIDTH)
    parts, acc = [], 0
    for s in sizes:
        parts.append(w[:, acc:acc + s])
        acc += s
    z, xbc, dt, qkv, ggate, gb, ga, hq, hf, hi, hgate = parts
    cols = jnp.concatenate([xbc, qkv, z, ggate, hq, hf, hi, hgate, dt, gb, ga], axis=1).astype(BF16)
    return jnp.pad(cols, ((0, 0), (0, PROJ_COLS - cols.shape[1])))


def kernel(x, w_in, ssd_conv_w, ssd_conv_b, ssd_dt_bias, ssd_a_log, ssd_d, ssd_norm_w, gdn_conv_w, gdn_dt_bias, gdn_a_log, gdn_norm_w, hg_lb_logits, hg_norm_w, w_out, ln1_g, ln1_b, w_router_group, b_router_group, w_router_expert, b_router_expert, w_expert_gate_up, w_expert_down, ln2_g, ln2_b):
    nb, seq, d = x.shape
    t = nb * seq
    lb_cum = jnp.cumsum(jax.nn.softmax(hg_lb_logits.astype(F32), axis=0), axis=0)
    lb_all = lb_cum - lb_cum[0:1]
    h = x.reshape(t, d)
    hb = h.astype(BF16)
    for l in range(DEPTH):
        y = _mixers(hb, _split_w_in(w_in[l]), nb, seq, ssd_conv_w[l], ssd_conv_b[l], ssd_dt_bias[l], ssd_a_log[l], ssd_d[l],
                    ssd_norm_w[l], gdn_conv_w[l], gdn_dt_bias[l], gdn_a_log[l], gdn_norm_w[l],
                    lb_all[l], hg_norm_w[l])
        w_router = jnp.pad(jnp.concatenate([w_router_group[l], w_router_expert[l]], axis=1),
                           ((0, 0), (0, LANES - MOE_GROUPS - N_EXPERTS)))
        w_router = jnp.concatenate(_split(w_router), axis=1)
        b_router = jnp.pad(jnp.concatenate([b_router_group[l], b_router_expert[l]]),
                           (0, LANES - MOE_GROUPS - N_EXPERTS))[None, :]
        h1, h1b, comb = _outproj(y, h, w_out[l].astype(BF16), ln1_g[l], ln1_b[l], w_router, b_router, 512)
        h, hb = _moe(h1b, h1, comb, w_expert_gate_up[l].astype(BF16), w_expert_down[l].astype(BF16),
                     ln2_g[l], ln2_b[l], 512)
    return h.reshape(nb, seq, d)
```

```python
import jax
import jax.numpy as jnp
from jax import lax
from jax.experimental import pallas as pl
from jax.experimental.pallas import tpu as pltpu

F32 = jnp.float32
BF16 = jnp.bfloat16

D_MODEL = 1024
DEPTH = 4
SSD_HEADS = 16
SSD_HEAD_DIM = 64
SSD_WIDTH = 1024
SSD_GROUPS = 2
SSD_STATE = 128
SSD_BC = 256
SSD_XBC = 1536
SSD_CONV = 4
GDN_HEADS = 4
GDN_HEAD_DIM = 128
GDN_WIDTH = 512
GDN_CONV = 4
HG_HEADS = 4
HG_HEAD_DIM = 128
HG_WIDTH = 512
D_MIX = 2048
MOE_GROUPS = 4
EXPERTS_PER_GROUP = 4
N_EXPERTS = 16
D_EXPERT = 256
DN_ALPHA = (2 * DEPTH) ** 0.25

LANES = 128
SUBLANES = 8
ROWS = 128
VMEM_LIMIT = 48 * 1024 * 1024
MOE_VMEM_LIMIT = 56 * 1024 * 1024
CONV_SLAB = 256
GDN_HI_LEVELS = 2
MOE_CHUNK = 128
PROJ_SLAB = 768

OFF_XBC = 0
OFF_QKV = 1536
OFF_Z = 3072
OFF_GGATE = 4096
OFF_HQ = 4608
OFF_HF = 5120
OFF_HI = 5632
OFF_HGATE = 6144
OFF_SMALL = 6656
PROJ_COLS = 6912
SM_DT = 0
SM_B = 16
SM_A = 20
MIX_SSD = 0
MIX_GDN = SSD_WIDTH
MIX_HG = SSD_WIDTH + GDN_WIDTH


def _dot(a, b):
    return jnp.dot(a.astype(BF16), b.astype(BF16), preferred_element_type=F32)


def _dot_nt(a, b):
    return lax.dot_general(a.astype(BF16), b.astype(BF16), (((1,), (1,)), ((), ())),
                           preferred_element_type=F32)


def _dot_tn(a, b):
    return lax.dot_general(a.astype(BF16), b.astype(BF16), (((0,), (0,)), ((), ())),
                           preferred_element_type=F32)


def _split3(a):
    p1 = a.astype(BF16)
    r1 = a - p1.astype(F32)
    p2 = r1.astype(BF16)
    return p1, p2, (r1 - p2.astype(F32)).astype(BF16)


def _dot_01(m01, x):
    m = m01.astype(BF16)
    p1, p2, p3 = _split3(x)
    return (jnp.dot(m, p1, preferred_element_type=F32) + jnp.dot(m, p2, preferred_element_type=F32)
            + jnp.dot(m, p3, preferred_element_type=F32))


def _dot_x01(x, m01):
    m = m01.astype(BF16)
    p1, p2, p3 = _split3(x)
    return (jnp.dot(p1, m, preferred_element_type=F32) + jnp.dot(p2, m, preferred_element_type=F32)
            + jnp.dot(p3, m, preferred_element_type=F32))


def _split(a):
    hi = a.astype(BF16)
    return hi, (a - hi.astype(F32)).astype(BF16)


def _dot3(a, b):
    (ah, al), (bh, bl) = a, b
    return (jnp.dot(ah, bh, preferred_element_type=F32) + jnp.dot(ah, bl, preferred_element_type=F32)
            + jnp.dot(al, bh, preferred_element_type=F32))


def _sigmoid(x):
    return 1.0 / (1.0 + jnp.exp(-x))


def _silu(x):
    return x * _sigmoid(x)


def _softplus(x):
    return jnp.maximum(x, 0.0) + jnp.log(1.0 + jnp.exp(-jnp.abs(x)))


def _iota2(shape, dim):
    return lax.broadcasted_iota(jnp.int32, shape, dim)


def _params(sem):
    return pltpu.CompilerParams(dimension_semantics=sem, vmem_limit_bytes=VMEM_LIMIT)


def _matmul_kernel(x_ref, w_ref, o_ref):
    o_ref[...] = jnp.dot(x_ref[...], w_ref[...], preferred_element_type=F32)


def _matmul(x, w, tm, tn):
    t, k = x.shape
    n = w.shape[1]
    return pl.pallas_call(
        _matmul_kernel,
        out_shape=jax.ShapeDtypeStruct((t, n), F32),
        grid=(n // tn, t // tm),
        in_specs=[pl.BlockSpec((tm, k), lambda j, i: (i, 0)),
                  pl.BlockSpec((k, tn), lambda j, i: (0, j))],
        out_specs=pl.BlockSpec((tm, tn), lambda j, i: (i, j)),
        compiler_params=_params(("parallel", "arbitrary")),
        name="in_proj",
    )(x, w)


def _conv_silu_stages(x_ref, xpad_ref, xa_ref, cw_ref, cb_ref):
    rows, cols = x_ref.shape
    k = cw_ref.shape[0]
    for c0 in range(0, cols, CONV_SLAB):
        cs = slice(c0, c0 + CONV_SLAB)
        x = x_ref[:, cs]
        xpad_ref[SUBLANES:SUBLANES + rows, cs] = x
        acc = x * cw_ref[k - 1:k, cs]
        if cb_ref is not None:
            acc = acc + cb_ref[:, cs]
        for i in range(k - 1):
            off = SUBLANES - (k - 1) + i
            acc = acc + xpad_ref[off:off + rows, cs] * cw_ref[i:i + 1, cs]
        xa_ref[:, cs] = _silu(acc)
        xpad_ref[0:SUBLANES, cs] = xpad_ref[rows:rows + SUBLANES, cs]
        yield


def _ssd_stages(z_ref, xbc_ref, sm_ref, cw_ref, cb_ref, hp_ref, hpt_ref, dsk_ref, nw_ref,
                y_ref, xpad_ref, xa_ref, st_ref):
    yield from _conv_silu_stages(xbc_ref, xpad_ref, xa_ref, cw_ref, cb_ref)

    row = _iota2((ROWS, ROWS), 0)
    col = _iota2((ROWS, ROWS), 1)
    causal = row >= col
    tril = causal.astype(F32)
    triu = (row <= col).astype(F32)

    sm = sm_ref[...]
    dt = _softplus(sm + hp_ref[0:1, :])
    da = dt * (-jnp.exp(hp_ref[1:2, :]))
    acum = _dot_01(tril, da)
    smt = sm.T
    dtt = _softplus(smt[0:SSD_HEADS, :] + hpt_ref[0:SSD_HEADS, 0:1])
    dat = dtt * (-jnp.exp(hpt_ref[0:SSD_HEADS, 1:2]))
    acumt = _dot_x01(dat, triu)

    heads3 = jnp.concatenate([dt, jnp.exp(acum), jnp.exp(acum[ROWS - 1:ROWS, :] - acum)], axis=0)
    h_hi, h_lo = _split(heads3)
    heads6 = jnp.concatenate([h_hi, h_lo], axis=0)
    lane_lo = _iota2((ROWS, LANES), 1) < SSD_HEAD_DIM
    yield

    hg = SSD_HEADS // SSD_GROUPS
    gw = SSD_WIDTH // SSD_GROUPS
    hrow = _iota2((LANES, gw), 0)
    hcol = _iota2((LANES, gw), 1)
    for g in range(SSD_GROUPS):
        gs = slice(g * gw, (g + 1) * gw)
        expand = (((hcol >> 6) + g * hg) == hrow).astype(BF16)
        ex = jnp.dot(heads6, expand, preferred_element_type=F32)
        ex = ex[0:3 * ROWS, :] + ex[3 * ROWS:6 * ROWS, :]
        dtx = ex[0:ROWS, :]
        eax = ex[ROWS:2 * ROWS, :]
        tex = ex[2 * ROWS:3 * ROWS, :]
        yield
        xs = xa_ref[:, gs]
        xdt = xs * dtx
        xdt_b = xdt.astype(BF16)
        xend_b = (xdt * tex).astype(BF16)
        bm = xa_ref[:, SSD_WIDTH + g * SSD_STATE:SSD_WIDTH + (g + 1) * SSD_STATE]
        cm = xa_ref[:, SSD_WIDTH + SSD_BC + g * SSD_STATE:SSD_WIDTH + SSD_BC + (g + 1) * SSD_STATE]
        cm_b = cm.astype(BF16)
        cb = _dot_nt(cm_b, bm)
        yield
        yd = []
        for pair in range(hg // 2):
            res = []
            for sub in range(2):
                h = g * hg + pair * 2 + sub
                seg = acum[:, h:h + 1] - acumt[h:h + 1, :]
                lmat = cb * jnp.exp(jnp.where(causal, seg, -jnp.inf))
                c0 = pair * 2 * SSD_HEAD_DIM
                res.append(jnp.dot(lmat.astype(BF16), xdt_b[:, c0:c0 + LANES],
                                   preferred_element_type=F32))
            yd.append(jnp.where(lane_lo, res[0], res[1]))
            yield
        yd = jnp.concatenate(yd, axis=1)
        st = st_ref[g]
        yoff = jnp.dot(cm_b, st.astype(BF16), preferred_element_type=F32) * eax
        st_ref[g] = (st * eax[ROWS - 1:ROWS, :]
                     + jnp.dot(bm.T.astype(BF16), xend_b, preferred_element_type=F32))
        yield
        y = yd + yoff + xs * dsk_ref[:, gs]
        y = y * _silu(z_ref[:, gs])
        ms = jnp.mean(y * y, axis=-1, keepdims=True)
        y_ref[:, MIX_SSD + g * gw:MIX_SSD + (g + 1) * gw] = (
            y * lax.rsqrt(ms + 1e-6) * nw_ref[:, gs]).astype(y_ref.dtype)
        yield


def _gdn_stages(qkv_ref, gate_ref, sm_ref, cw_ref, hp_ref, hpt_ref, nw_ref, y_ref, xpad_ref, xa_ref, st_ref):
    yield from _conv_silu_stages(qkv_ref, xpad_ref, xa_ref, cw_ref, None)

    row = _iota2((ROWS, ROWS), 0)
    col = _iota2((ROWS, ROWS), 1)
    incl = row >= col
    strict = row > col
    tril = incl.astype(F32)
    triu = (row <= col).astype(F32)

    sm = sm_ref[...]
    la = -jnp.exp(hp_ref[1:2, :]) * _softplus(sm + hp_ref[0:1, :])
    gcum = _dot_01(tril, la)
    smt = sm.T
    lat = (-jnp.exp(hpt_ref[SM_B:SM_B + SUBLANES, 1:2])
           * _softplus(smt[SM_B:SM_B + SUBLANES, :] + hpt_ref[SM_B:SM_B + SUBLANES, 0:1]))
    gcumt = _dot_x01(lat, triu)
    beta_all = _sigmoid(sm)
    yield

    dh = GDN_HEAD_DIM
    heads = range(GDN_HEADS)
    qs, ks, xs, rhss, decays, gs = [], [], [], [], [], []
    for h in heads:
        q = xa_ref[:, h * dh:(h + 1) * dh]
        k = xa_ref[:, GDN_WIDTH + h * dh:GDN_WIDTH + (h + 1) * dh]
        v = xa_ref[:, 2 * GDN_WIDTH + h * dh:2 * GDN_WIDTH + (h + 1) * dh]
        q = q * lax.rsqrt(jnp.sum(q * q, axis=-1, keepdims=True) + 1e-6) * (dh ** -0.5)
        k = k * lax.rsqrt(jnp.sum(k * k, axis=-1, keepdims=True) + 1e-6)
        g = gcum[:, SM_A + h:SM_A + h + 1]
        gt = gcumt[SM_A - SM_B + h:SM_A - SM_B + h + 1, :]
        beta = beta_all[:, SM_B + h:SM_B + h + 1]
        decay = jnp.exp(jnp.where(incl, g - gt, -jnp.inf))
        kb = k * beta
        xs.append(-jnp.where(strict, _dot_nt(kb, k) * decay, 0.0))
        rhss.append(jnp.concatenate([v * beta, kb * jnp.exp(g)], axis=1))
        qs.append(q)
        ks.append(k)
        decays.append(decay)
        gs.append(g)
        yield
    ps = xs
    ns = xs
    levels = ROWS.bit_length() - 2
    for j in range(levels):
        if j < GDN_HI_LEVELS:
            pp = [_split(p) for p in ps]
            ps = [_dot3(p, p) for p in pp]
            yield
            pp = [_split(p) for p in ps]
            ns = [n + p + _dot3(_split(n), p2) for n, p, p2 in zip(ns, ps, pp)]
        else:
            ps = [_dot(p, p) for p in ps]
            yield
            ns = [n + p + _dot(n, p) for n, p in zip(ns, ps)]
        yield
    sols = [r + _dot3(_split(n), _split(r)) for n, r in zip(ns, rhss)]
    yield
    qks = [_dot_nt(q, k) * d for q, k, d in zip(qs, ks, decays)]
    sts = [st_ref[h] for h in heads]
    v_news = [s[:, 0:dh] - _dot(s[:, dh:2 * dh], st) for s, st in zip(sols, sts)]
    yield
    os_ = [_dot(q * jnp.exp(g), st) + _dot(qk, vn) for q, g, st, qk, vn in zip(qs, gs, sts, qks, v_news)]
    yield
    for h in heads:
        glast = gs[h][ROWS - 1:ROWS, :]
        kd = ks[h] * jnp.exp(glast - gs[h])
        st_ref[h] = sts[h] * jnp.exp(glast) + _dot(kd.T, v_news[h])
    yield
    for h in heads:
        o = os_[h]
        o = o * lax.rsqrt(jnp.mean(o * o, axis=-1, keepdims=True) + 1e-6)
        y_ref[:, MIX_GDN + h * dh:MIX_GDN + (h + 1) * dh] = (
            o * nw_ref[...] * _silu(gate_ref[:, h * dh:(h + 1) * dh])).astype(y_ref.dtype)
        yield


def _hgrn_stages(q_ref, f_ref, i_ref, gate_ref, lb_ref, nw_ref, y_ref, st_ref):
    row = _iota2((ROWS, ROWS), 0)
    col = _iota2((ROWS, ROWS), 1)
    tril = (row >= col).astype(F32)
    rmod = _iota2((ROWS, LANES), 0) & (SUBLANES - 1)
    rfull = _iota2((ROWS, LANES), 0)
    dk = HG_HEAD_DIM
    for h in range(HG_HEADS):
        sl = slice(h * dk, (h + 1) * dk)
        lb = lb_ref[:, sl]
        fr = f_ref[:, sl]
        log_sig = jnp.minimum(fr, 0.0) - jnp.log(1.0 + jnp.exp(-jnp.abs(fr)))
        a = jnp.log(lb)
        y = jnp.log1p(-lb) + log_sig
        log_f = jnp.maximum(a, y) + jnp.log(1.0 + jnp.exp(-jnp.abs(a - y)))
        k = (1.0 - lb) * _sigmoid(-fr)
        q = _silu(q_ref[:, sl])
        v = i_ref[:, sl]
        b = _dot_01(tril, log_f)
        yield

        st = st_ref[h]
        o = _dot_nt(q * jnp.exp(b), st)
        blast = b[ROWS - 1:ROWS, :]
        st_ref[h] = st * jnp.exp(blast) + _dot_tn(v, k * jnp.exp(blast - b))
        yield

        pmat = jnp.zeros((ROWS, ROWS), F32)
        b3 = b.reshape(ROWS // SUBLANES, SUBLANES, dk)
        m = 1
        while m < ROWS:
            if m < SUBLANES:
                ref = None
                for p0 in range(0, SUBLANES, 2 * m):
                    cand = jnp.broadcast_to(b3[:, p0 + m - 1:p0 + m, :], b3.shape).reshape(ROWS, dk)
                    ref = cand if ref is None else jnp.where(rmod >= p0, cand, ref)
            else:
                ref = jnp.concatenate(
                    [jnp.broadcast_to(b[p0 + m - 1:p0 + m, :], (2 * m, dk)) for p0 in range(0, ROWS, 2 * m)],
                    axis=0)
            right = (rfull & m) != 0
            qs = q * jnp.exp(jnp.where(right, b - ref, -jnp.inf))
            ks = k * jnp.exp(jnp.where(right, -jnp.inf, ref - b))
            parent = ~(2 * m - 1)
            pmat = pmat + jnp.where((row & parent) == (col & parent), _dot_nt(qs, ks), 0.0)
            m *= 2
            yield
        o = o + _dot(pmat, v) + jnp.sum(q * k, axis=-1, keepdims=True) * v
        yield

        o = o * lax.rsqrt(jnp.mean(o * o, axis=-1, keepdims=True) + 1e-6)
        y_ref[:, MIX_HG + h * dk:MIX_HG + (h + 1) * dk] = (
            o * nw_ref[...] * _silu(gate_ref[:, sl])).astype(y_ref.dtype)
        yield


def _in_proj_stages(hb_ref, w_ref, out_ref):
    hb = hb_ref[...]
    for c0 in range(0, PROJ_COLS, PROJ_SLAB):
        out_ref[:, c0:c0 + PROJ_SLAB] = jnp.dot(hb, w_ref[:, c0:c0 + PROJ_SLAB], preferred_element_type=F32)
        yield


def _mixer_kernel(hb0_ref, hbn_ref, w_ref,
                  scw_ref, scb_ref, shp_ref, shpt_ref, dsk_ref, snw_ref,
                  gcw_ref, ghp_ref, ghpt_ref, gnw_ref, lb_ref, hnw_ref,
                  y_ref, proj_ref, sxpad_ref, sxa_ref, sst_ref, gxpad_ref, gxa_ref, gst_ref, hst_ref):
    c = pl.program_id(1)

    @pl.when(c == 0)
    def _():
        sst_ref[...] = jnp.zeros(sst_ref.shape, F32)
        gst_ref[...] = jnp.zeros(gst_ref.shape, F32)
        hst_ref[...] = jnp.zeros(hst_ref.shape, F32)
        sxpad_ref[0:SUBLANES, :] = jnp.zeros((SUBLANES, sxpad_ref.shape[1]), F32)
        gxpad_ref[0:SUBLANES, :] = jnp.zeros((SUBLANES, gxpad_ref.shape[1]), F32)
        for _ in _in_proj_stages(hb0_ref, w_ref, proj_ref.at[0]):
            pass

    cur = proj_ref.at[c % 2]
    nxt = proj_ref.at[(c + 1) % 2]

    def seg(off, width):
        return cur.at[:, off:off + width]

    sm_ref = seg(OFF_SMALL, LANES)
    streams = [
        _gdn_stages(seg(OFF_QKV, 3 * GDN_WIDTH), seg(OFF_GGATE, GDN_WIDTH), sm_ref, gcw_ref, ghp_ref, ghpt_ref,
                    gnw_ref, y_ref, gxpad_ref, gxa_ref, gst_ref),
        _ssd_stages(seg(OFF_Z, SSD_WIDTH), seg(OFF_XBC, SSD_XBC), sm_ref, scw_ref, scb_ref, shp_ref, shpt_ref,
                    dsk_ref, snw_ref, y_ref, sxpad_ref, sxa_ref, sst_ref),
        _hgrn_stages(seg(OFF_HQ, HG_WIDTH), seg(OFF_HF, HG_WIDTH), seg(OFF_HI, HG_WIDTH),
                     seg(OFF_HGATE, HG_WIDTH), lb_ref, hnw_ref, y_ref, hst_ref),
        _in_proj_stages(hbn_ref, w_ref, nxt),
    ]
    advance = [1, 1, 2, 1]
    live = [True] * len(streams)
    while any(live):
        for s, stream in enumerate(streams):
            for _ in range(advance[s]):
                if live[s]:
                    try:
                        next(stream)
                    except StopIteration:
                        live[s] = False


def _mixers(hb, w_proj, layer, nb, seq, ssd_conv_w, ssd_conv_b, ssd_dt_bias, ssd_a_log, ssd_d, ssd_norm_w,
            gdn_conv_w, gdn_dt_bias, gdn_a_log, gdn_norm_w, lb, hg_norm_w):
    nblk = seq // ROWS
    shp = jnp.zeros((SUBLANES, LANES), F32)
    shp = shp.at[0, SM_DT:SM_DT + SSD_HEADS].set(ssd_dt_bias).at[1, SM_DT:SM_DT + SSD_HEADS].set(ssd_a_log)
    ghp = jnp.zeros((SUBLANES, LANES), F32)
    ghp = ghp.at[0, SM_A:SM_A + GDN_HEADS].set(gdn_dt_bias).at[1, SM_A:SM_A + GDN_HEADS].set(gdn_a_log)
    dsk = jnp.repeat(ssd_d, SSD_HEAD_DIM)[None, :]
    const = lambda b, c: (0, 0)

    def whole(shape):
        return pl.BlockSpec(shape, const)

    return pl.pallas_call(
        _mixer_kernel,
        out_shape=jax.ShapeDtypeStruct((nb * seq, D_MIX), BF16),
        grid=(nb, nblk),
        in_specs=[
            pl.BlockSpec((ROWS, D_MODEL), lambda b, c: (b * nblk + c, 0)),
            pl.BlockSpec((ROWS, D_MODEL), lambda b, c: (b * nblk + jnp.minimum(c + 1, nblk - 1), 0)),
            pl.BlockSpec((None, D_MODEL, PROJ_COLS), lambda b, c: (layer, 0, 0), pipeline_mode=pl.Buffered(1)),
            whole((SSD_CONV, SSD_XBC)), whole((1, SSD_XBC)), whole((SUBLANES, LANES)), whole((LANES, SUBLANES)),
            whole((1, SSD_WIDTH)), whole((1, SSD_WIDTH)),
            whole((GDN_CONV, 3 * GDN_WIDTH)), whole((SUBLANES, LANES)), whole((LANES, SUBLANES)),
            whole((1, GDN_HEAD_DIM)), whole((1, HG_WIDTH)), whole((1, HG_HEAD_DIM)),
        ],
        out_specs=pl.BlockSpec((ROWS, D_MIX), lambda b, c: (b * nblk + c, 0)),
        scratch_shapes=[pltpu.VMEM((2, ROWS, PROJ_COLS), F32),
                        pltpu.VMEM((ROWS + SUBLANES, SSD_XBC), F32),
                        pltpu.VMEM((ROWS, SSD_XBC), F32),
                        pltpu.VMEM((SSD_GROUPS, SSD_STATE, SSD_WIDTH // SSD_GROUPS), F32),
                        pltpu.VMEM((ROWS + SUBLANES, 3 * GDN_WIDTH), F32),
                        pltpu.VMEM((ROWS, 3 * GDN_WIDTH), F32),
                        pltpu.VMEM((GDN_HEADS, GDN_HEAD_DIM, GDN_HEAD_DIM), F32),
                        pltpu.VMEM((HG_HEADS, HG_HEAD_DIM, HG_HEAD_DIM), F32)],
        compiler_params=_params(("parallel", "arbitrary")),
        name="in_proj_mixers",
    )(hb, hb, w_proj,
      ssd_conv_w, ssd_conv_b[None, :], shp, shp.T, dsk, ssd_norm_w[None, :],
      gdn_conv_w, ghp, ghp.T, gdn_norm_w[None, :], lb[None, :], hg_norm_w[None, :])


def _layer_norm(x, g, b):
    mu = jnp.mean(x, axis=-1, keepdims=True)
    xc = x - mu
    var = jnp.mean(xc * xc, axis=-1, keepdims=True)
    return xc * lax.rsqrt(var + 1e-5) * g + b


ROUTER_ROWS = 32


def _outproj_kernel(y_ref, h_ref, w_ref, g_ref, b_ref, wr_ref, br_ref, h1_ref, h1b_ref, comb_ref):
    mix = jnp.dot(y_ref[...], w_ref[...], preferred_element_type=F32)
    h1 = _layer_norm(DN_ALPHA * h_ref[...] + mix, g_ref[...], b_ref[...])
    h1_ref[...] = h1
    h1b_ref[...] = h1.astype(BF16)

    h_hi, h_lo = _split(h1)
    wr = wr_ref[...]
    lt = _dot_nt(wr, h_hi) + _dot_nt(wr, h_lo)
    lt = lt[0:ROUTER_ROWS, :] + lt[ROUTER_ROWS:2 * ROUTER_ROWS, :] + br_ref[...]
    glog = [lt[g:g + 1, :] for g in range(MOE_GROUPS)]
    elog = [lt[MOE_GROUPS + e:MOE_GROUPS + e + 1, :] for e in range(N_EXPERTS)]

    def softmax(rows):
        m = rows[0]
        for r in rows[1:]:
            m = jnp.maximum(m, r)
        ex = [jnp.exp(r - m) for r in rows]
        tot = ex[0]
        for r in ex[1:]:
            tot = tot + r
        return [r / tot for r in ex]

    def top1(rows):
        best = rows[0]
        for r in rows[1:]:
            best = jnp.maximum(best, r)
        idx = jnp.full_like(best, float(len(rows) - 1))
        for j in range(len(rows) - 2, -1, -1):
            idx = jnp.where(rows[j] == best, float(j), idx)
        return best, idx

    g_p, g_idx = top1(softmax(glog))
    chosen = []
    for k in range(EXPERTS_PER_GROUP):
        acc = jnp.zeros_like(g_p)
        for g in range(MOE_GROUPS):
            acc = jnp.where(g_idx == float(g), elog[g * EXPERTS_PER_GROUP + k], acc)
        chosen.append(acc)
    eprob = softmax(chosen)
    p1, i1 = top1(eprob)
    p2, i2 = top1([jnp.where(i1 == float(k), -1.0, eprob[k]) for k in range(EXPERTS_PER_GROUP)])
    denom = p1 + p2
    w1 = g_p * p1 / denom
    w2 = g_p * p2 / denom
    wk = [jnp.where(i1 == float(k), w1, 0.0) + jnp.where(i2 == float(k), w2, 0.0)
          for k in range(EXPERTS_PER_GROUP)]
    rid = _iota2((ROUTER_ROWS, lt.shape[1]), 0)
    comb_t = jnp.where(rid == N_EXPERTS, g_idx, 0.0)
    for g in range(MOE_GROUPS):
        for k in range(EXPERTS_PER_GROUP):
            comb_t = jnp.where(rid == g * EXPERTS_PER_GROUP + k,
                               jnp.where(g_idx == float(g), wk[k], 0.0), comb_t)
    comb_t = jnp.concatenate([comb_t, jnp.zeros((LANES - ROUTER_ROWS, lt.shape[1]), F32)], axis=0)
    comb_ref[...] = comb_t.T


def _router_params(w_group, b_group, w_expert, b_expert):
    pad = ROUTER_ROWS - MOE_GROUPS - N_EXPERTS
    w_t = jnp.pad(jnp.concatenate([w_group, w_expert], axis=1).T, ((0, pad), (0, 0)))
    b_col = jnp.pad(jnp.concatenate([b_group, b_expert]), (0, pad))[:, None]
    return jnp.concatenate(_split(w_t), axis=0), b_col


def _outproj(y, h, w_out, layer, ln_g, ln_b, w_router, b_router, tm):
    t = h.shape[0]
    rowmap = lambda i: (i, 0)
    const = lambda i: (0, 0)
    return pl.pallas_call(
        _outproj_kernel,
        out_shape=(jax.ShapeDtypeStruct((t, D_MODEL), F32),
                   jax.ShapeDtypeStruct((t, D_MODEL), BF16),
                   jax.ShapeDtypeStruct((t, LANES), F32)),
        grid=(t // tm,),
        in_specs=[pl.BlockSpec((tm, D_MIX), rowmap),
                  pl.BlockSpec((tm, D_MODEL), rowmap),
                  pl.BlockSpec((None, D_MIX, D_MODEL), lambda i: (layer, 0, 0)),
                  pl.BlockSpec((1, D_MODEL), const),
                  pl.BlockSpec((1, D_MODEL), const),
                  pl.BlockSpec((2 * ROUTER_ROWS, D_MODEL), const),
                  pl.BlockSpec((ROUTER_ROWS, 1), const)],
        out_specs=(pl.BlockSpec((tm, D_MODEL), rowmap),
                   pl.BlockSpec((tm, D_MODEL), rowmap),
                   pl.BlockSpec((tm, LANES), rowmap)),
        compiler_params=_params(("parallel",)),
        name="out_proj_ln_router",
    )(y, h, w_out, ln_g[None, :], ln_b[None, :], w_router, b_router)


def _prefix_lanes(v, idx):
    axis = 1 if v.shape[0] == 1 else 0
    out = jnp.zeros_like(v)
    for g in range(MOE_GROUPS - 1):
        vg = v[:, g:g + 1] if axis == 1 else v[g:g + 1, :]
        out = out + jnp.where(idx > g, vg, 0.0)
    return out


def _moe_kernel(hb_ref, h_ref, comb_ref, wgu_ref, wdn_ref, g_ref, b_ref, o_ref, ob_ref,
                xs_ref, cw_ref, ys_ref):
    tm = hb_ref.shape[0]
    comb = comb_ref[...]
    lane = _iota2((tm, LANES), 1).astype(F32)
    gid = comb[:, N_EXPERTS:N_EXPERTS + 1]
    gsel = (lane == gid).astype(F32)
    tr = _iota2((tm, tm), 0)
    tc = _iota2((tm, tm), 1)

    rank_c = _dot((tc < tr).astype(F32), gsel)
    cnt_r = jnp.sum(gsel, axis=0, keepdims=True)
    start_r = _prefix_lanes(cnt_r, _iota2((1, LANES), 1))
    dest_c = jnp.sum(gsel * (start_r + rank_c), axis=1, keepdims=True)
    eye8 = (_iota2((SUBLANES, LANES), 0) == _iota2((SUBLANES, LANES), 1)).astype(F32)
    gsel_t = _dot_nt(eye8, gsel)
    rank_r = _dot(gsel_t, (tr < tc).astype(F32))
    cnt_c = jnp.sum(gsel_t, axis=1, keepdims=True)
    start_c = _prefix_lanes(cnt_c, _iota2((SUBLANES, 1), 0))
    dest_r = jnp.sum(gsel_t * (start_c + rank_r), axis=0, keepdims=True)

    perm = (dest_r == tr.astype(F32)).astype(BF16)
    xs_ref[...] = jnp.dot(perm, hb_ref[...], preferred_element_type=F32).astype(BF16)
    c_hi, c_lo = _split(comb)
    cw2 = jnp.dot(perm, jnp.concatenate([c_hi, c_lo], axis=1), preferred_element_type=F32)
    cw_ref[...] = cw2[:, 0:LANES] + cw2[:, LANES:2 * LANES]
    ys_ref[...] = jnp.zeros(ys_ref.shape, F32)

    lane1 = _iota2((1, LANES), 1)
    starts, ends = [], []
    for g in range(MOE_GROUPS):
        starts.append(jnp.sum(jnp.where(lane1 == g, start_r, 0.0)))
        ends.append(starts[g] + jnp.sum(jnp.where(lane1 == g, cnt_r, 0.0)))
    for c in range(tm // MOE_CHUNK):
        lo = c * MOE_CHUNK
        hi = lo + MOE_CHUNK
        for g in range(MOE_GROUPS):
            @pl.when((starts[g] < hi) & (ends[g] > lo))
            def _(lo=lo, hi=hi, g=g):
                x = xs_ref[lo:hi, :]
                cw = cw_ref[lo:hi, :]
                hm = []
                for e in range(EXPERTS_PER_GROUP):
                    ex = g * EXPERTS_PER_GROUP + e
                    gu = jnp.dot(x, wgu_ref[ex], preferred_element_type=F32)
                    hm.append((_silu(gu[:, 0:D_EXPERT]) * gu[:, D_EXPERT:2 * D_EXPERT]
                               * cw[:, ex:ex + 1]).astype(BF16))
                wd = wdn_ref[g * EXPERTS_PER_GROUP:(g + 1) * EXPERTS_PER_GROUP].reshape(
                    EXPERTS_PER_GROUP * D_EXPERT, D_MODEL)
                ys_ref[lo:hi, :] += jnp.dot(jnp.concatenate(hm, axis=1), wd, preferred_element_type=F32)

    unperm = (dest_c == tc.astype(F32)).astype(BF16)
    y = jnp.dot(unperm, ys_ref[...].astype(BF16), preferred_element_type=F32)
    h2 = _layer_norm(DN_ALPHA * h_ref[...] + y, g_ref[...], b_ref[...])
    o_ref[...] = h2
    ob_ref[...] = h2.astype(BF16)


def _moe(hb, h, comb, w_gu, w_dn, layer, ln_g, ln_b, tm):
    t = h.shape[0]
    rowmap = lambda i: (i, 0)
    const = lambda i: (0, 0)
    const3 = lambda i: (layer, 0, 0, 0)
    return pl.pallas_call(
        _moe_kernel,
        out_shape=(jax.ShapeDtypeStruct((t, D_MODEL), F32),
                   jax.ShapeDtypeStruct((t, D_MODEL), BF16)),
        grid=(t // tm,),
        in_specs=[pl.BlockSpec((tm, D_MODEL), rowmap),
                  pl.BlockSpec((tm, D_MODEL), rowmap),
                  pl.BlockSpec((tm, LANES), rowmap),
                  pl.BlockSpec((None, N_EXPERTS, D_MODEL, 2 * D_EXPERT), const3, pipeline_mode=pl.Buffered(1)),
                  pl.BlockSpec((None, N_EXPERTS, D_EXPERT, D_MODEL), const3, pipeline_mode=pl.Buffered(1)),
                  pl.BlockSpec((1, D_MODEL), const),
                  pl.BlockSpec((1, D_MODEL), const)],
        out_specs=(pl.BlockSpec((tm, D_MODEL), rowmap),
                   pl.BlockSpec((tm, D_MODEL), rowmap)),
        scratch_shapes=[pltpu.VMEM((tm, D_MODEL), BF16),
                        pltpu.VMEM((tm, LANES), F32),
                        pltpu.VMEM((tm, D_MODEL), F32)],
        compiler_params=pltpu.CompilerParams(dimension_semantics=("parallel",),
                                             vmem_limit_bytes=MOE_VMEM_LIMIT),
        name="moe_ln",
    )(hb, h, comb, w_gu, w_dn, ln_g[None, :], ln_b[None, :])


W_IN_SEGMENTS = ((SSD_WIDTH, OFF_Z), (SSD_XBC, OFF_XBC), (SSD_HEADS, OFF_SMALL + SM_DT),
                 (3 * GDN_WIDTH, OFF_QKV), (GDN_WIDTH, OFF_GGATE), (GDN_HEADS, OFF_SMALL + SM_B),
                 (GDN_HEADS, OFF_SMALL + SM_A), (HG_WIDTH, OFF_HQ), (HG_WIDTH, OFF_HF),
                 (HG_WIDTH, OFF_HI), (HG_WIDTH, OFF_HGATE))
IN_COLS = sum(width for width, _ in W_IN_SEGMENTS)
W_IN_ROWS = 128


def _w_in_kernel(w_ref, o_ref):
    small = []
    src = 0
    for width, dst in W_IN_SEGMENTS:
        if width >= LANES:
            o_ref[0, :, dst:dst + width] = w_ref[0, :, src:src + width].astype(BF16)
        else:
            small.append(w_ref[0, :, src:src + width])
        src += width
    used = sum(s.shape[1] for s in small)
    small.append(jnp.zeros((W_IN_ROWS, PROJ_COLS - OFF_SMALL - used), F32))
    o_ref[0, :, OFF_SMALL:PROJ_COLS] = jnp.concatenate(small, axis=1).astype(BF16)


def _split_w_in(w_in):
    depth, rows, _ = w_in.shape
    return pl.pallas_call(
        _w_in_kernel,
        out_shape=jax.ShapeDtypeStruct((depth, rows, PROJ_COLS), BF16),
        grid=(depth, rows // W_IN_ROWS),
        in_specs=[pl.BlockSpec((1, W_IN_ROWS, IN_COLS), lambda l, i: (l, i, 0))],
        out_specs=pl.BlockSpec((1, W_IN_ROWS, PROJ_COLS), lambda l, i: (l, i, 0)),
        compiler_params=_params(("parallel", "parallel")),
        name="w_in_relayout",
    )(w_in)


def kernel(x, w_in, ssd_conv_w, ssd_conv_b, ssd_dt_bias, ssd_a_log, ssd_d, ssd_norm_w, gdn_conv_w, gdn_dt_bias, gdn_a_log, gdn_norm_w, hg_lb_logits, hg_norm_w, w_out, ln1_g, ln1_b, w_router_group, b_router_group, w_router_expert, b_router_expert, w_expert_gate_up, w_expert_down, ln2_g, ln2_b):
    nb, seq, d = x.shape
    t = nb * seq
    lb_cum = jnp.cumsum(jax.nn.softmax(hg_lb_logits.astype(F32), axis=0), axis=0)
    lb_all = lb_cum - lb_cum[0:1]
    h = x.reshape(t, d)
    hb = h.astype(BF16)
    w_proj = _split_w_in(w_in)
    w_out_b = w_out.astype(BF16)
    w_gu_b = w_expert_gate_up.astype(BF16)
    w_dn_b = w_expert_down.astype(BF16)
    for l in range(DEPTH):
        y = _mixers(hb, w_proj, l, nb, seq, ssd_conv_w[l], ssd_conv_b[l], ssd_dt_bias[l], ssd_a_log[l], ssd_d[l],
                    ssd_norm_w[l], gdn_conv_w[l], gdn_dt_bias[l], gdn_a_log[l], gdn_norm_w[l],
                    lb_all[l], hg_norm_w[l])
        w_router, b_router = _router_params(w_router_group[l], b_router_group[l],
                                            w_router_expert[l], b_router_expert[l])
        h1, h1b, comb = _outproj(y, h, w_out_b, l, ln1_g[l], ln1_b[l], w_router, b_router, 512)
        h, hb = _moe(h1b, h1, comb, w_gu_b, w_dn_b, l, ln2_g[l], ln2_b[l], 512)
    return h.reshape(nb, seq, d)
```

```python
import jax
import jax.numpy as jnp
from jax import lax
from jax.experimental import pallas as pl
from jax.experimental.pallas import tpu as pltpu

F32 = jnp.float32
BF16 = jnp.bfloat16

D_MODEL = 1024
DEPTH = 4
SSD_HEADS = 16
SSD_HEAD_DIM = 64
SSD_WIDTH = 1024
SSD_GROUPS = 2
SSD_STATE = 128
SSD_BC = 256
SSD_XBC = 1536
SSD_CONV = 4
GDN_HEADS = 4
GDN_HEAD_DIM = 128
GDN_WIDTH = 512
GDN_CONV = 4
HG_HEADS = 4
HG_HEAD_DIM = 128
HG_WIDTH = 512
D_MIX = 2048
MOE_GROUPS = 4
EXPERTS_PER_GROUP = 4
N_EXPERTS = 16
D_EXPERT = 256
DN_ALPHA = (2 * DEPTH) ** 0.25

LANES = 128
SUBLANES = 8
ROWS = 128
VMEM_LIMIT = 48 * 1024 * 1024
MOE_VMEM_LIMIT = 56 * 1024 * 1024
CONV_SLAB = 256
GDN_HI_LEVELS = 2
MOE_CHUNK = 128
PROJ_SLAB = 768

OFF_XBC = 0
OFF_QKV = 1536
OFF_Z = 3072
OFF_GGATE = 4096
OFF_HQ = 4608
OFF_HF = 5120
OFF_HI = 5632
OFF_HGATE = 6144
OFF_SMALL = 6656
PROJ_COLS = 6912
SM_DT = 0
SM_B = 16
SM_A = 20
MIX_SSD = 0
MIX_GDN = SSD_WIDTH
MIX_HG = SSD_WIDTH + GDN_WIDTH


def _dot(a, b):
    return jnp.dot(a.astype(BF16), b.astype(BF16), preferred_element_type=F32)


def _dot_nt(a, b):
    return lax.dot_general(a.astype(BF16), b.astype(BF16), (((1,), (1,)), ((), ())),
                           preferred_element_type=F32)


def _dot_tn(a, b):
    return lax.dot_general(a.astype(BF16), b.astype(BF16), (((0,), (0,)), ((), ())),
                           preferred_element_type=F32)


def _split3(a):
    p1 = a.astype(BF16)
    r1 = a - p1.astype(F32)
    p2 = r1.astype(BF16)
    return p1, p2, (r1 - p2.astype(F32)).astype(BF16)


def _dot_01(m01, x):
    m = m01.astype(BF16)
    p1, p2, p3 = _split3(x)
    return (jnp.dot(m, p1, preferred_element_type=F32) + jnp.dot(m, p2, preferred_element_type=F32)
            + jnp.dot(m, p3, preferred_element_type=F32))


def _dot_x01(x, m01):
    m = m01.astype(BF16)
    p1, p2, p3 = _split3(x)
    return (jnp.dot(p1, m, preferred_element_type=F32) + jnp.dot(p2, m, preferred_element_type=F32)
            + jnp.dot(p3, m, preferred_element_type=F32))


def _split(a):
    hi = a.astype(BF16)
    return hi, (a - hi.astype(F32)).astype(BF16)


def _dot3(a, b):
    (ah, al), (bh, bl) = a, b
    return (jnp.dot(ah, bh, preferred_element_type=F32) + jnp.dot(ah, bl, preferred_element_type=F32)
            + jnp.dot(al, bh, preferred_element_type=F32))


def _sigmoid(x):
    return 1.0 / (1.0 + jnp.exp(-x))


def _silu(x):
    return x * _sigmoid(x)


def _softplus(x):
    return jnp.maximum(x, 0.0) + jnp.log(1.0 + jnp.exp(-jnp.abs(x)))


def _iota2(shape, dim):
    return lax.broadcasted_iota(jnp.int32, shape, dim)


def _params(sem):
    return pltpu.CompilerParams(dimension_semantics=sem, vmem_limit_bytes=VMEM_LIMIT)


def _matmul_kernel(x_ref, w_ref, o_ref):
    o_ref[...] = jnp.dot(x_ref[...], w_ref[...], preferred_element_type=F32)


def _matmul(x, w, tm, tn):
    t, k = x.shape
    n = w.shape[1]
    return pl.pallas_call(
        _matmul_kernel,
        out_shape=jax.ShapeDtypeStruct((t, n), F32),
        grid=(n // tn, t // tm),
        in_specs=[pl.BlockSpec((tm, k), lambda j, i: (i, 0)),
                  pl.BlockSpec((k, tn), lambda j, i: (0, j))],
        out_specs=pl.BlockSpec((tm, tn), lambda j, i: (i, j)),
        compiler_params=_params(("parallel", "arbitrary")),
        name="in_proj",
    )(x, w)


def _conv_silu_stages(x_ref, xpad_ref, xa_ref, cw_ref, cb_ref):
    rows, cols = x_ref.shape
    k = cw_ref.shape[0]
    for c0 in range(0, cols, CONV_SLAB):
        cs = slice(c0, c0 + CONV_SLAB)
        x = x_ref[:, cs]
        xpad_ref[SUBLANES:SUBLANES + rows, cs] = x
        acc = x * cw_ref[k - 1:k, cs]
        if cb_ref is not None:
            acc = acc + cb_ref[:, cs]
        for i in range(k - 1):
            off = SUBLANES - (k - 1) + i
            acc = acc + xpad_ref[off:off + rows, cs] * cw_ref[i:i + 1, cs]
        xa_ref[:, cs] = _silu(acc)
        xpad_ref[0:SUBLANES, cs] = xpad_ref[rows:rows + SUBLANES, cs]
        yield


def _ssd_stages(z_ref, xbc_ref, sm_ref, cw_ref, cb_ref, hp_ref, hpt_ref, dsk_ref, nw_ref,
                y_ref, xpad_ref, xa_ref, st_ref):
    yield from _conv_silu_stages(xbc_ref, xpad_ref, xa_ref, cw_ref, cb_ref)

    row = _iota2((ROWS, ROWS), 0)
    col = _iota2((ROWS, ROWS), 1)
    causal = row >= col
    tril = causal.astype(F32)
    triu = (row <= col).astype(F32)

    sm = sm_ref[...]
    dt = _softplus(sm + hp_ref[0:1, :])
    da = dt * (-jnp.exp(hp_ref[1:2, :]))
    acum = _dot_01(tril, da)
    smt = sm.T
    dtt = _softplus(smt[0:SSD_HEADS, :] + hpt_ref[0:SSD_HEADS, 0:1])
    dat = dtt * (-jnp.exp(hpt_ref[0:SSD_HEADS, 1:2]))
    acumt = _dot_x01(dat, triu)

    heads3 = jnp.concatenate([dt, jnp.exp(acum), jnp.exp(acum[ROWS - 1:ROWS, :] - acum)], axis=0)
    h_hi, h_lo = _split(heads3)
    heads6 = jnp.concatenate([h_hi, h_lo], axis=0)
    lane_lo = _iota2((ROWS, LANES), 1) < SSD_HEAD_DIM
    yield

    hg = SSD_HEADS // SSD_GROUPS
    gw = SSD_WIDTH // SSD_GROUPS
    hrow = _iota2((LANES, gw), 0)
    hcol = _iota2((LANES, gw), 1)
    for g in range(SSD_GROUPS):
        gs = slice(g * gw, (g + 1) * gw)
        expand = (((hcol >> 6) + g * hg) == hrow).astype(BF16)
        ex = jnp.dot(heads6, expand, preferred_element_type=F32)
        ex = ex[0:3 * ROWS, :] + ex[3 * ROWS:6 * ROWS, :]
        dtx = ex[0:ROWS, :]
        eax = ex[ROWS:2 * ROWS, :]
        tex = ex[2 * ROWS:3 * ROWS, :]
        yield
        xs = xa_ref[:, gs]
        xdt = xs * dtx
        xdt_b = xdt.astype(BF16)
        xend_b = (xdt * tex).astype(BF16)
        bm = xa_ref[:, SSD_WIDTH + g * SSD_STATE:SSD_WIDTH + (g + 1) * SSD_STATE]
        cm = xa_ref[:, SSD_WIDTH + SSD_BC + g * SSD_STATE:SSD_WIDTH + SSD_BC + (g + 1) * SSD_STATE]
        cm_b = cm.astype(BF16)
        cb = _dot_nt(cm_b, bm)
        yield
        yd = []
        for pair in range(hg // 2):
            res = []
            for sub in range(2):
                h = g * hg + pair * 2 + sub
                seg = acum[:, h:h + 1] - acumt[h:h + 1, :]
                lmat = cb * jnp.exp(jnp.where(causal, seg, -jnp.inf))
                c0 = pair * 2 * SSD_HEAD_DIM
                res.append(jnp.dot(lmat.astype(BF16), xdt_b[:, c0:c0 + LANES],
                                   preferred_element_type=F32))
            yd.append(jnp.where(lane_lo, res[0], res[1]))
            yield
        yd = jnp.concatenate(yd, axis=1)
        st = st_ref[g]
        yoff = jnp.dot(cm_b, st.astype(BF16), preferred_element_type=F32) * eax
        st_ref[g] = (st * eax[ROWS - 1:ROWS, :]
                     + jnp.dot(bm.T.astype(BF16), xend_b, preferred_element_type=F32))
        yield
        y = yd + yoff + xs * dsk_ref[:, gs]
        y = y * _silu(z_ref[:, gs])
        ms = jnp.mean(y * y, axis=-1, keepdims=True)
        y_ref[:, MIX_SSD + g * gw:MIX_SSD + (g + 1) * gw] = (
            y * lax.rsqrt(ms + 1e-6) * nw_ref[:, gs]).astype(y_ref.dtype)
        yield


def _gdn_stages(qkv_ref, gate_ref, sm_ref, cw_ref, hp_ref, hpt_ref, nw_ref, y_ref, xpad_ref, xa_ref, st_ref):
    yield from _conv_silu_stages(qkv_ref, xpad_ref, xa_ref, cw_ref, None)

    row = _iota2((ROWS, ROWS), 0)
    col = _iota2((ROWS, ROWS), 1)
    incl = row >= col
    strict = row > col
    tril = incl.astype(F32)
    triu = (row <= col).astype(F32)

    sm = sm_ref[...]
    la = -jnp.exp(hp_ref[1:2, :]) * _softplus(sm + hp_ref[0:1, :])
    gcum = _dot_01(tril, la)
    smt = sm.T
    lat = (-jnp.exp(hpt_ref[SM_B:SM_B + SUBLANES, 1:2])
           * _softplus(smt[SM_B:SM_B + SUBLANES, :] + hpt_ref[SM_B:SM_B + SUBLANES, 0:1]))
    gcumt = _dot_x01(lat, triu)
    beta_all = _sigmoid(sm)
    yield

    dh = GDN_HEAD_DIM
    heads = range(GDN_HEADS)
    qs, ks, xs, rhss, decays, gs = [], [], [], [], [], []
    for h in heads:
        q = xa_ref[:, h * dh:(h + 1) * dh]
        k = xa_ref[:, GDN_WIDTH + h * dh:GDN_WIDTH + (h + 1) * dh]
        v = xa_ref[:, 2 * GDN_WIDTH + h * dh:2 * GDN_WIDTH + (h + 1) * dh]
        q = q * lax.rsqrt(jnp.sum(q * q, axis=-1, keepdims=True) + 1e-6) * (dh ** -0.5)
        k = k * lax.rsqrt(jnp.sum(k * k, axis=-1, keepdims=True) + 1e-6)
        g = gcum[:, SM_A + h:SM_A + h + 1]
        gt = gcumt[SM_A - SM_B + h:SM_A - SM_B + h + 1, :]
        beta = beta_all[:, SM_B + h:SM_B + h + 1]
        decay = jnp.exp(jnp.where(incl, g - gt, -jnp.inf))
        kb = k * beta
        xs.append(-jnp.where(strict, _dot_nt(kb, k) * decay, 0.0))
        rhss.append(jnp.concatenate([v * beta, kb * jnp.exp(g)], axis=1))
        qs.append(q)
        ks.append(k)
        decays.append(decay)
        gs.append(g)
        yield
    ps = xs
    ns = xs
    levels = ROWS.bit_length() - 2
    for j in range(levels):
        if j < GDN_HI_LEVELS:
            pp = [_split(p) for p in ps]
            ps = [_dot3(p, p) for p in pp]
            yield
            pp = [_split(p) for p in ps]
            ns = [n + p + _dot3(_split(n), p2) for n, p, p2 in zip(ns, ps, pp)]
        else:
            ps = [_dot(p, p) for p in ps]
            yield
            ns = [n + p + _dot(n, p) for n, p in zip(ns, ps)]
        yield
    sols = [r + _dot3(_split(n), _split(r)) for n, r in zip(ns, rhss)]
    yield
    qks = [_dot_nt(q, k) * d for q, k, d in zip(qs, ks, decays)]
    sts = [st_ref[h] for h in heads]
    v_news = [s[:, 0:dh] - _dot(s[:, dh:2 * dh], st) for s, st in zip(sols, sts)]
    yield
    os_ = [_dot(q * jnp.exp(g), st) + _dot(qk, vn) for q, g, st, qk, vn in zip(qs, gs, sts, qks, v_news)]
    yield
    for h in heads:
        glast = gs[h][ROWS - 1:ROWS, :]
        kd = ks[h] * jnp.exp(glast - gs[h])
        st_ref[h] = sts[h] * jnp.exp(glast) + _dot(kd.T, v_news[h])
    yield
    for h in heads:
        o = os_[h]
        o = o * lax.rsqrt(jnp.mean(o * o, axis=-1, keepdims=True) + 1e-6)
        y_ref[:, MIX_GDN + h * dh:MIX_GDN + (h + 1) * dh] = (
            o * nw_ref[...] * _silu(gate_ref[:, h * dh:(h + 1) * dh])).astype(y_ref.dtype)
        yield


def _hgrn_stages(q_ref, f_ref, i_ref, gate_ref, lb_ref, nw_ref, y_ref, st_ref):
    row = _iota2((ROWS, ROWS), 0)
    col = _iota2((ROWS, ROWS), 1)
    tril = (row >= col).astype(F32)
    rmod = _iota2((ROWS, LANES), 0) & (SUBLANES - 1)
    rfull = _iota2((ROWS, LANES), 0)
    dk = HG_HEAD_DIM
    for h in range(HG_HEADS):
        sl = slice(h * dk, (h + 1) * dk)
        lb = lb_ref[:, sl]
        fr = f_ref[:, sl]
        log_sig = jnp.minimum(fr, 0.0) - jnp.log(1.0 + jnp.exp(-jnp.abs(fr)))
        a = jnp.log(lb)
        y = jnp.log1p(-lb) + log_sig
        log_f = jnp.maximum(a, y) + jnp.log(1.0 + jnp.exp(-jnp.abs(a - y)))
        k = (1.0 - lb) * _sigmoid(-fr)
        q = _silu(q_ref[:, sl])
        v = i_ref[:, sl]
        b = _dot_01(tril, log_f)
        yield

        st = st_ref[h]
        o = _dot_nt(q * jnp.exp(b), st)
        blast = b[ROWS - 1:ROWS, :]
        st_ref[h] = st * jnp.exp(blast) + _dot_tn(v, k * jnp.exp(blast - b))
        yield

        pmat = jnp.zeros((ROWS, ROWS), F32)
        b3 = b.reshape(ROWS // SUBLANES, SUBLANES, dk)
        m = 1
        while m < ROWS:
            if m < SUBLANES:
                ref = None
                for p0 in range(0, SUBLANES, 2 * m):
                    cand = jnp.broadcast_to(b3[:, p0 + m - 1:p0 + m, :], b3.shape).reshape(ROWS, dk)
                    ref = cand if ref is None else jnp.where(rmod >= p0, cand, ref)
            else:
                ref = jnp.concatenate(
                    [jnp.broadcast_to(b[p0 + m - 1:p0 + m, :], (2 * m, dk)) for p0 in range(0, ROWS, 2 * m)],
                    axis=0)
            right = (rfull & m) != 0
            qs = q * jnp.exp(jnp.where(right, b - ref, -jnp.inf))
            ks = k * jnp.exp(jnp.where(right, -jnp.inf, ref - b))
            parent = ~(2 * m - 1)
            pmat = pmat + jnp.where((row & parent) == (col & parent), _dot_nt(qs, ks), 0.0)
            m *= 2
            yield
        o = o + _dot(pmat, v) + jnp.sum(q * k, axis=-1, keepdims=True) * v
        yield

        o = o * lax.rsqrt(jnp.mean(o * o, axis=-1, keepdims=True) + 1e-6)
        y_ref[:, MIX_HG + h * dk:MIX_HG + (h + 1) * dk] = (
            o * nw_ref[...] * _silu(gate_ref[:, sl])).astype(y_ref.dtype)
        yield


def _in_proj_stages(hb_ref, w_ref, out_ref):
    hb = hb_ref[...]
    for c0 in range(0, PROJ_COLS, PROJ_SLAB):
        out_ref[:, c0:c0 + PROJ_SLAB] = jnp.dot(hb, w_ref[:, c0:c0 + PROJ_SLAB], preferred_element_type=F32)
        yield


def _mixer_kernel(hb0_ref, hbn_ref, w_ref,
                  scw_ref, scb_ref, shp_ref, shpt_ref, dsk_ref, snw_ref,
                  gcw_ref, ghp_ref, ghpt_ref, gnw_ref, lb_ref, hnw_ref,
                  y_ref, proj_ref, sxpad_ref, sxa_ref, sst_ref, gxpad_ref, gxa_ref, gst_ref, hst_ref):
    c = pl.program_id(1)

    @pl.when(c == 0)
    def _():
        sst_ref[...] = jnp.zeros(sst_ref.shape, F32)
        gst_ref[...] = jnp.zeros(gst_ref.shape, F32)
        hst_ref[...] = jnp.zeros(hst_ref.shape, F32)
        sxpad_ref[0:SUBLANES, :] = jnp.zeros((SUBLANES, sxpad_ref.shape[1]), F32)
        gxpad_ref[0:SUBLANES, :] = jnp.zeros((SUBLANES, gxpad_ref.shape[1]), F32)
        for _ in _in_proj_stages(hb0_ref, w_ref, proj_ref.at[0]):
            pass

    cur = proj_ref.at[c % 2]
    nxt = proj_ref.at[(c + 1) % 2]

    def seg(off, width):
        return cur.at[:, off:off + width]

    sm_ref = seg(OFF_SMALL, LANES)
    streams = [
        _gdn_stages(seg(OFF_QKV, 3 * GDN_WIDTH), seg(OFF_GGATE, GDN_WIDTH), sm_ref, gcw_ref, ghp_ref, ghpt_ref,
                    gnw_ref, y_ref, gxpad_ref, gxa_ref, gst_ref),
        _ssd_stages(seg(OFF_Z, SSD_WIDTH), seg(OFF_XBC, SSD_XBC), sm_ref, scw_ref, scb_ref, shp_ref, shpt_ref,
                    dsk_ref, snw_ref, y_ref, sxpad_ref, sxa_ref, sst_ref),
        _hgrn_stages(seg(OFF_HQ, HG_WIDTH), seg(OFF_HF, HG_WIDTH), seg(OFF_HI, HG_WIDTH),
                     seg(OFF_HGATE, HG_WIDTH), lb_ref, hnw_ref, y_ref, hst_ref),
        _in_proj_stages(hbn_ref, w_ref, nxt),
    ]
    advance = [1, 1, 2, 1]
    live = [True] * len(streams)
    while any(live):
        for s, stream in enumerate(streams):
            for _ in range(advance[s]):
                if live[s]:
                    try:
                        next(stream)
                    except StopIteration:
                        live[s] = False


def _mixers(hb, w_proj, layer, nb, seq, ssd_conv_w, ssd_conv_b, ssd_dt_bias, ssd_a_log, ssd_d, ssd_norm_w,
            gdn_conv_w, gdn_dt_bias, gdn_a_log, gdn_norm_w, lb, hg_norm_w):
    nblk = seq // ROWS
    shp = jnp.zeros((SUBLANES, LANES), F32)
    shp = shp.at[0, SM_DT:SM_DT + SSD_HEADS].set(ssd_dt_bias).at[1, SM_DT:SM_DT + SSD_HEADS].set(ssd_a_log)
    ghp = jnp.zeros((SUBLANES, LANES), F32)
    ghp = ghp.at[0, SM_A:SM_A + GDN_HEADS].set(gdn_dt_bias).at[1, SM_A:SM_A + GDN_HEADS].set(gdn_a_log)
    dsk = jnp.repeat(ssd_d, SSD_HEAD_DIM)[None, :]
    const = lambda b, c: (0, 0)

    def whole(shape):
        return pl.BlockSpec(shape, const)

    return pl.pallas_call(
        _mixer_kernel,
        out_shape=jax.ShapeDtypeStruct((nb * seq, D_MIX), BF16),
        grid=(nb, nblk),
        in_specs=[
            pl.BlockSpec((ROWS, D_MODEL), lambda b, c: (b * nblk + c, 0)),
            pl.BlockSpec((ROWS, D_MODEL), lambda b, c: (b * nblk + jnp.minimum(c + 1, nblk - 1), 0)),
            pl.BlockSpec((None, D_MODEL, PROJ_COLS), lambda b, c: (layer, 0, 0), pipeline_mode=pl.Buffered(1)),
            whole((SSD_CONV, SSD_XBC)), whole((1, SSD_XBC)), whole((SUBLANES, LANES)), whole((LANES, SUBLANES)),
            whole((1, SSD_WIDTH)), whole((1, SSD_WIDTH)),
            whole((GDN_CONV, 3 * GDN_WIDTH)), whole((SUBLANES, LANES)), whole((LANES, SUBLANES)),
            whole((1, GDN_HEAD_DIM)), whole((1, HG_WIDTH)), whole((1, HG_HEAD_DIM)),
        ],
        out_specs=pl.BlockSpec((ROWS, D_MIX), lambda b, c: (b * nblk + c, 0)),
        scratch_shapes=[pltpu.VMEM((2, ROWS, PROJ_COLS), F32),
                        pltpu.VMEM((ROWS + SUBLANES, SSD_XBC), F32),
                        pltpu.VMEM((ROWS, SSD_XBC), F32),
                        pltpu.VMEM((SSD_GROUPS, SSD_STATE, SSD_WIDTH // SSD_GROUPS), F32),
                        pltpu.VMEM((ROWS + SUBLANES, 3 * GDN_WIDTH), F32),
                        pltpu.VMEM((ROWS, 3 * GDN_WIDTH), F32),
                        pltpu.VMEM((GDN_HEADS, GDN_HEAD_DIM, GDN_HEAD_DIM), F32),
                        pltpu.VMEM((HG_HEADS, HG_HEAD_DIM, HG_HEAD_DIM), F32)],
        compiler_params=_params(("parallel", "arbitrary")),
        name="in_proj_mixers",
    )(hb, hb, w_proj,
      ssd_conv_w, ssd_conv_b[None, :], shp, shp.T, dsk, ssd_norm_w[None, :],
      gdn_conv_w, ghp, ghp.T, gdn_norm_w[None, :], lb[None, :], hg_norm_w[None, :])


def _layer_norm(x, g, b):
    mu = jnp.mean(x, axis=-1, keepdims=True)
    xc = x - mu
    var = jnp.mean(xc * xc, axis=-1, keepdims=True)
    return xc * lax.rsqrt(var + 1e-5) * g + b


ROUTER_ROWS = 32


def _outproj_kernel(y_ref, h_ref, w_ref, g_ref, b_ref, wr_ref, br_ref, h1_ref, h1b_ref, comb_ref):
    mix = jnp.dot(y_ref[...], w_ref[...], preferred_element_type=F32)
    h1 = _layer_norm(DN_ALPHA * h_ref[...] + mix, g_ref[...], b_ref[...])
    h1_ref[...] = h1
    h1b_ref[...] = h1.astype(BF16)

    h_hi, h_lo = _split(h1)
    wr = wr_ref[...]
    lt = _dot_nt(wr, h_hi) + _dot_nt(wr, h_lo)
    lt = lt[0:ROUTER_ROWS, :] + lt[ROUTER_ROWS:2 * ROUTER_ROWS, :] + br_ref[...]
    glog = [lt[g:g + 1, :] for g in range(MOE_GROUPS)]
    elog = [lt[MOE_GROUPS + e:MOE_GROUPS + e + 1, :] for e in range(N_EXPERTS)]

    def softmax(rows):
        m = rows[0]
        for r in rows[1:]:
            m = jnp.maximum(m, r)
        ex = [jnp.exp(r - m) for r in rows]
        tot = ex[0]
        for r in ex[1:]:
            tot = tot + r
        return [r / tot for r in ex]

    def top1(rows):
        best = rows[0]
        for r in rows[1:]:
            best = jnp.maximum(best, r)
        idx = jnp.full_like(best, float(len(rows) - 1))
        for j in range(len(rows) - 2, -1, -1):
            idx = jnp.where(rows[j] == best, float(j), idx)
        return best, idx

    g_p, g_idx = top1(softmax(glog))
    chosen = []
    for k in range(EXPERTS_PER_GROUP):
        acc = jnp.zeros_like(g_p)
        for g in range(MOE_GROUPS):
            acc = jnp.where(g_idx == float(g), elog[g * EXPERTS_PER_GROUP + k], acc)
        chosen.append(acc)
    eprob = softmax(chosen)
    p1, i1 = top1(eprob)
    p2, i2 = top1([jnp.where(i1 == float(k), -1.0, eprob[k]) for k in range(EXPERTS_PER_GROUP)])
    denom = p1 + p2
    w1 = g_p * p1 / denom
    w2 = g_p * p2 / denom
    wk = [jnp.where(i1 == float(k), w1, 0.0) + jnp.where(i2 == float(k), w2, 0.0)
          for k in range(EXPERTS_PER_GROUP)]
    rid = _iota2((ROUTER_ROWS, lt.shape[1]), 0)
    comb_t = jnp.where(rid == N_EXPERTS, g_idx, 0.0)
    for g in range(MOE_GROUPS):
        for k in range(EXPERTS_PER_GROUP):
            comb_t = jnp.where(rid == g * EXPERTS_PER_GROUP + k,
                               jnp.where(g_idx == float(g), wk[k], 0.0), comb_t)
    comb_t = jnp.concatenate([comb_t, jnp.zeros((LANES - ROUTER_ROWS, lt.shape[1]), F32)], axis=0)
    comb_ref[...] = comb_t.T


def _router_params(w_group, b_group, w_expert, b_expert):
    pad = ROUTER_ROWS - MOE_GROUPS - N_EXPERTS
    w_t = jnp.pad(jnp.concatenate([w_group, w_expert], axis=1).T, ((0, pad), (0, 0)))
    b_col = jnp.pad(jnp.concatenate([b_group, b_expert]), (0, pad))[:, None]
    return jnp.concatenate(_split(w_t), axis=0), b_col


def _outproj(y, h, w_out, layer, ln_g, ln_b, w_router, b_router, tm):
    t = h.shape[0]
    rowmap = lambda i: (i, 0)
    const = lambda i: (0, 0)
    return pl.pallas_call(
        _outproj_kernel,
        out_shape=(jax.ShapeDtypeStruct((t, D_MODEL), F32),
                   jax.ShapeDtypeStruct((t, D_MODEL), BF16),
                   jax.ShapeDtypeStruct((t, LANES), F32)),
        grid=(t // tm,),
        in_specs=[pl.BlockSpec((tm, D_MIX), rowmap),
                  pl.BlockSpec((tm, D_MODEL), rowmap),
                  pl.BlockSpec((None, D_MIX, D_MODEL), lambda i: (layer, 0, 0)),
                  pl.BlockSpec((1, D_MODEL), const),
                  pl.BlockSpec((1, D_MODEL), const),
                  pl.BlockSpec((2 * ROUTER_ROWS, D_MODEL), const),
                  pl.BlockSpec((ROUTER_ROWS, 1), const)],
        out_specs=(pl.BlockSpec((tm, D_MODEL), rowmap),
                   pl.BlockSpec((tm, D_MODEL), rowmap),
                   pl.BlockSpec((tm, LANES), rowmap)),
        compiler_params=_params(("parallel",)),
        name="out_proj_ln_router",
    )(y, h, w_out, ln_g[None, :], ln_b[None, :], w_router, b_router)


def _prefix_lanes(v, idx):
    axis = 1 if v.shape[0] == 1 else 0
    out = jnp.zeros_like(v)
    for g in range(MOE_GROUPS - 1):
        vg = v[:, g:g + 1] if axis == 1 else v[g:g + 1, :]
        out = out + jnp.where(idx > g, vg, 0.0)
    return out


def _moe_kernel(hb_ref, h_ref, comb_ref, wgu_ref, wdn_ref, g_ref, b_ref, o_ref, ob_ref,
                xs_ref, cw_ref, ys_ref):
    tm = hb_ref.shape[0]
    comb = comb_ref[...]
    lane = _iota2((tm, LANES), 1).astype(F32)
    gid = comb[:, N_EXPERTS:N_EXPERTS + 1]
    gsel = (lane == gid).astype(F32)
    tr = _iota2((tm, tm), 0)
    tc = _iota2((tm, tm), 1)

    rank_c = _dot((tc < tr).astype(F32), gsel)
    cnt_r = jnp.sum(gsel, axis=0, keepdims=True)
    start_r = _prefix_lanes(cnt_r, _iota2((1, LANES), 1))
    dest_c = jnp.sum(gsel * (start_r + rank_c), axis=1, keepdims=True)
    eye8 = (_iota2((SUBLANES, LANES), 0) == _iota2((SUBLANES, LANES), 1)).astype(F32)
    gsel_t = _dot_nt(eye8, gsel)
    rank_r = _dot(gsel_t, (tr < tc).astype(F32))
    cnt_c = jnp.sum(gsel_t, axis=1, keepdims=True)
    start_c = _prefix_lanes(cnt_c, _iota2((SUBLANES, 1), 0))
    dest_r = jnp.sum(gsel_t * (start_c + rank_r), axis=0, keepdims=True)

    perm = (dest_r == tr.astype(F32)).astype(BF16)
    xs_ref[...] = jnp.dot(perm, hb_ref[...], preferred_element_type=F32).astype(BF16)
    c_hi, c_lo = _split(comb)
    cw2 = jnp.dot(perm, jnp.concatenate([c_hi, c_lo], axis=1), preferred_element_type=F32)
    cw_ref[...] = cw2[:, 0:LANES] + cw2[:, LANES:2 * LANES]
    ys_ref[...] = jnp.zeros(ys_ref.shape, F32)

    lane1 = _iota2((1, LANES), 1)
    starts, ends = [], []
    for g in range(MOE_GROUPS):
        starts.append(jnp.sum(jnp.where(lane1 == g, start_r, 0.0)))
        ends.append(starts[g] + jnp.sum(jnp.where(lane1 == g, cnt_r, 0.0)))
    for c in range(tm // MOE_CHUNK):
        lo = c * MOE_CHUNK
        hi = lo + MOE_CHUNK
        for g in range(MOE_GROUPS):
            @pl.when((starts[g] < hi) & (ends[g] > lo))
            def _(lo=lo, hi=hi, g=g):
                x = xs_ref[lo:hi, :]
                cw = cw_ref[lo:hi, :]
                hm = []
                for e in range(EXPERTS_PER_GROUP):
                    ex = g * EXPERTS_PER_GROUP + e
                    gu = jnp.dot(x, wgu_ref[ex], preferred_element_type=F32)
                    hm.append((_silu(gu[:, 0:D_EXPERT]) * gu[:, D_EXPERT:2 * D_EXPERT]
                               * cw[:, ex:ex + 1]).astype(BF16))
                wd = wdn_ref[g * EXPERTS_PER_GROUP:(g + 1) * EXPERTS_PER_GROUP].reshape(
                    EXPERTS_PER_GROUP * D_EXPERT, D_MODEL)
                ys_ref[lo:hi, :] += jnp.dot(jnp.concatenate(hm, axis=1), wd, preferred_element_type=F32)

    unperm = (dest_c == tc.astype(F32)).astype(BF16)
    y = jnp.dot(unperm, ys_ref[...].astype(BF16), preferred_element_type=F32)
    h2 = _layer_norm(DN_ALPHA * h_ref[...] + y, g_ref[...], b_ref[...])
    o_ref[...] = h2
    ob_ref[...] = h2.astype(BF16)


def _moe(hb, h, comb, w_gu, w_dn, layer, ln_g, ln_b, tm):
    t = h.shape[0]
    rowmap = lambda i: (i, 0)
    const = lambda i: (0, 0)
    const3 = lambda i: (layer, 0, 0, 0)
    return pl.pallas_call(
        _moe_kernel,
        out_shape=(jax.ShapeDtypeStruct((t, D_MODEL), F32),
                   jax.ShapeDtypeStruct((t, D_MODEL), BF16)),
        grid=(t // tm,),
        in_specs=[pl.BlockSpec((tm, D_MODEL), rowmap),
                  pl.BlockSpec((tm, D_MODEL), rowmap),
                  pl.BlockSpec((tm, LANES), rowmap),
                  pl.BlockSpec((None, N_EXPERTS, D_MODEL, 2 * D_EXPERT), const3, pipeline_mode=pl.Buffered(1)),
                  pl.BlockSpec((None, N_EXPERTS, D_EXPERT, D_MODEL), const3, pipeline_mode=pl.Buffered(1)),
                  pl.BlockSpec((1, D_MODEL), const),
                  pl.BlockSpec((1, D_MODEL), const)],
        out_specs=(pl.BlockSpec((tm, D_MODEL), rowmap),
                   pl.BlockSpec((tm, D_MODEL), rowmap)),
        scratch_shapes=[pltpu.VMEM((tm, D_MODEL), BF16),
                        pltpu.VMEM((tm, LANES), F32),
                        pltpu.VMEM((tm, D_MODEL), F32)],
        compiler_params=pltpu.CompilerParams(dimension_semantics=("parallel",),
                                             vmem_limit_bytes=MOE_VMEM_LIMIT),
        name="moe_ln",
    )(hb, h, comb, w_gu, w_dn, ln_g[None, :], ln_b[None, :])


W_IN_SEGMENTS = ((SSD_WIDTH, OFF_Z), (SSD_XBC, OFF_XBC), (SSD_HEADS, OFF_SMALL + SM_DT),
                 (3 * GDN_WIDTH, OFF_QKV), (GDN_WIDTH, OFF_GGATE), (GDN_HEADS, OFF_SMALL + SM_B),
                 (GDN_HEADS, OFF_SMALL + SM_A), (HG_WIDTH, OFF_HQ), (HG_WIDTH, OFF_HF),
                 (HG_WIDTH, OFF_HI), (HG_WIDTH, OFF_HGATE))
W_T_BLOCK = 256
W_MAIN_BLOCKS = OFF_SMALL // W_T_BLOCK


def _w_in_kernel(tbl_ref, w_ref, small_ref, o_ref):
    j = pl.program_id(1)

    @pl.when(j < W_MAIN_BLOCKS)
    def _():
        o_ref[...] = w_ref[0].T.astype(BF16)

    @pl.when(j >= W_MAIN_BLOCKS)
    def _():
        o_ref[...] = small_ref[...].T.astype(BF16)


def _split_w_in(w_in):
    depth = w_in.shape[0]
    w_t = jnp.swapaxes(w_in, 1, 2)
    src_of, small_rows, src = {}, [], 0
    for width, dst in W_IN_SEGMENTS:
        if width >= W_T_BLOCK:
            for off in range(0, width, W_T_BLOCK):
                src_of[(dst + off) // W_T_BLOCK] = src + off
        else:
            small_rows.append(w_t[:, src:src + width, :])
        src += width
    table = jnp.array([src_of.get(j, 0) for j in range(PROJ_COLS // W_T_BLOCK)], jnp.int32)
    small = jnp.concatenate(small_rows, axis=1)
    small = jnp.pad(small, ((0, 0), (0, PROJ_COLS - OFF_SMALL - small.shape[1]), (0, 0)))
    return pl.pallas_call(
        _w_in_kernel,
        out_shape=jax.ShapeDtypeStruct((depth, D_MODEL, PROJ_COLS), BF16),
        grid_spec=pltpu.PrefetchScalarGridSpec(
            num_scalar_prefetch=1,
            grid=(depth, PROJ_COLS // W_T_BLOCK),
            in_specs=[pl.BlockSpec((pl.Element(1), pl.Element(W_T_BLOCK), pl.Element(D_MODEL)),
                                   lambda l, j, tbl: (l, pl.multiple_of(tbl[j], SUBLANES), 0)),
                      pl.BlockSpec((None, PROJ_COLS - OFF_SMALL, D_MODEL), lambda l, j, tbl: (l, 0, 0))],
            out_specs=pl.BlockSpec((None, D_MODEL, W_T_BLOCK), lambda l, j, tbl: (l, 0, j))),
        compiler_params=_params(("parallel", "arbitrary")),
        name="w_in_relayout",
    )(table, w_t, small)


def kernel(x, w_in, ssd_conv_w, ssd_conv_b, ssd_dt_bias, ssd_a_log, ssd_d, ssd_norm_w, gdn_conv_w, gdn_dt_bias, gdn_a_log, gdn_norm_w, hg_lb_logits, hg_norm_w, w_out, ln1_g, ln1_b, w_router_group, b_router_group, w_router_expert, b_router_expert, w_expert_gate_up, w_expert_down, ln2_g, ln2_b):
    nb, seq, d = x.shape
    t = nb * seq
    lb_cum = jnp.cumsum(jax.nn.softmax(hg_lb_logits.astype(F32), axis=0), axis=0)
    lb_all = lb_cum - lb_cum[0:1]
    h = x.reshape(t, d)
    hb = h.astype(BF16)
    w_proj = _split_w_in(w_in)
    w_out_b = w_out.astype(BF16)
    w_gu_b = w_expert_gate_up.astype(BF16)
    w_dn_b = w_expert_down.astype(BF16)
    for l in range(DEPTH):
        y = _mixers(hb, w_proj, l, nb, seq, ssd_conv_w[l], ssd_conv_b[l], ssd_dt_bias[l], ssd_a_log[l], ssd_d[l],
                    ssd_norm_w[l], gdn_conv_w[l], gdn_dt_bias[l], gdn_a_log[l], gdn_norm_w[l],
                    lb_all[l], hg_norm_w[l])
        w_router, b_router = _router_params(w_router_group[l], b_router_group[l],
                                            w_router_expert[l], b_router_expert[l])
        h1, h1b, comb = _outproj(y, h, w_out_b, l, ln1_g[l], ln1_b[l], w_router, b_router, 512)
        h, hb = _moe(h1b, h1, comb, w_gu_b, w_dn_b, l, ln2_g[l], ln2_b[l], 512)
    return h.reshape(nb, seq, d)
```

```python
import jax
import jax.numpy as jnp
from jax import lax
from jax.experimental import pallas as pl
from jax.experimental.pallas import tpu as pltpu

F32 = jnp.float32
BF16 = jnp.bfloat16

D_MODEL = 1024
DEPTH = 4
SSD_HEADS = 16
SSD_HEAD_DIM = 64
SSD_WIDTH = 1024
SSD_GROUPS = 2
SSD_STATE = 128
SSD_BC = 256
SSD_XBC = 1536
SSD_CONV = 4
GDN_HEADS = 4
GDN_HEAD_DIM = 128
GDN_WIDTH = 512
GDN_CONV = 4
HG_HEADS = 4
HG_HEAD_DIM = 128
HG_WIDTH = 512
D_MIX = 2048
MOE_GROUPS = 4
EXPERTS_PER_GROUP = 4
N_EXPERTS = 16
D_EXPERT = 256
DN_ALPHA = (2 * DEPTH) ** 0.25
LOG2E = 1.4426950408889634

LANES = 128
SUBLANES = 8
ROWS = 128
VMEM_LIMIT = 48 * 1024 * 1024
MOE_VMEM_LIMIT = 56 * 1024 * 1024
CONV_SLAB = 256
GDN_HI_LEVELS = 2
MOE_CHUNK = 128
PROJ_SLAB = 768
HG_SLOW_ROUNDS = 12

OFF_XBC = 0
OFF_QKV = 1536
OFF_Z = 3072
OFF_GGATE = 4096
OFF_HQ = 4608
OFF_HF = 5120
OFF_HI = 5632
OFF_HGATE = 6144
OFF_SMALL = 6656
PROJ_COLS = 6912
SM_DT = 0
SM_B = 16
SM_A = 20
MIX_SSD = 0
MIX_GDN = SSD_WIDTH
MIX_HG = SSD_WIDTH + GDN_WIDTH


def _dot(a, b):
    return jnp.dot(a.astype(BF16), b.astype(BF16), preferred_element_type=F32)


def _dot_nt(a, b):
    return lax.dot_general(a.astype(BF16), b.astype(BF16), (((1,), (1,)), ((), ())),
                           preferred_element_type=F32)


def _dot_tn(a, b):
    return lax.dot_general(a.astype(BF16), b.astype(BF16), (((0,), (0,)), ((), ())),
                           preferred_element_type=F32)


def _split3(a):
    p1 = a.astype(BF16)
    r1 = a - p1.astype(F32)
    p2 = r1.astype(BF16)
    return p1, p2, (r1 - p2.astype(F32)).astype(BF16)


def _dot_01(m01, x):
    m = m01.astype(BF16)
    p1, p2, p3 = _split3(x)
    return (jnp.dot(m, p1, preferred_element_type=F32) + jnp.dot(m, p2, preferred_element_type=F32)
            + jnp.dot(m, p3, preferred_element_type=F32))


def _dot_x01(x, m01):
    m = m01.astype(BF16)
    p1, p2, p3 = _split3(x)
    return (jnp.dot(p1, m, preferred_element_type=F32) + jnp.dot(p2, m, preferred_element_type=F32)
            + jnp.dot(p3, m, preferred_element_type=F32))


def _split(a):
    hi = a.astype(BF16)
    return hi, (a - hi.astype(F32)).astype(BF16)


def _dot3(a, b):
    (ah, al), (bh, bl) = a, b
    return (jnp.dot(ah, bh, preferred_element_type=F32) + jnp.dot(ah, bl, preferred_element_type=F32)
            + jnp.dot(al, bh, preferred_element_type=F32))


def _sigmoid(x):
    return 1.0 / (1.0 + jnp.exp(-x))


def _silu(x):
    return x * _sigmoid(x)


def _softplus(x):
    return jnp.maximum(x, 0.0) + jnp.log(1.0 + jnp.exp(-jnp.abs(x)))


def _iota2(shape, dim):
    return lax.broadcasted_iota(jnp.int32, shape, dim)


def _params(sem):
    return pltpu.CompilerParams(dimension_semantics=sem, vmem_limit_bytes=VMEM_LIMIT)


def _matmul_kernel(x_ref, w_ref, o_ref):
    o_ref[...] = jnp.dot(x_ref[...], w_ref[...], preferred_element_type=F32)


def _matmul(x, w, tm, tn):
    t, k = x.shape
    n = w.shape[1]
    return pl.pallas_call(
        _matmul_kernel,
        out_shape=jax.ShapeDtypeStruct((t, n), F32),
        grid=(n // tn, t // tm),
        in_specs=[pl.BlockSpec((tm, k), lambda j, i: (i, 0)),
                  pl.BlockSpec((k, tn), lambda j, i: (0, j))],
        out_specs=pl.BlockSpec((tm, tn), lambda j, i: (i, j)),
        compiler_params=_params(("parallel", "arbitrary")),
        name="in_proj",
    )(x, w)


def _conv_silu_stages(x_ref, xpad_ref, xa_ref, cw_ref, cb_ref):
    rows, cols = x_ref.shape
    k = cw_ref.shape[0]
    for c0 in range(0, cols, CONV_SLAB):
        cs = slice(c0, c0 + CONV_SLAB)
        x = x_ref[:, cs]
        xpad_ref[SUBLANES:SUBLANES + rows, cs] = x
        acc = x * cw_ref[k - 1:k, cs]
        if cb_ref is not None:
            acc = acc + cb_ref[:, cs]
        for i in range(k - 1):
            off = SUBLANES - (k - 1) + i
            acc = acc + xpad_ref[off:off + rows, cs] * cw_ref[i:i + 1, cs]
        xa_ref[:, cs] = _silu(acc)
        xpad_ref[0:SUBLANES, cs] = xpad_ref[rows:rows + SUBLANES, cs]
        yield


def _ssd_stages(z_ref, xbc_ref, sm_ref, cw_ref, cb_ref, hp_ref, hpt_ref, dsk_ref, nw_ref,
                y_ref, xpad_ref, xa_ref, st_ref):
    yield from _conv_silu_stages(xbc_ref, xpad_ref, xa_ref, cw_ref, cb_ref)

    row = _iota2((ROWS, ROWS), 0)
    col = _iota2((ROWS, ROWS), 1)
    causal = row >= col
    tril = causal.astype(F32)
    triu = (row <= col).astype(F32)

    sm = sm_ref[...]
    dt = _softplus(sm + hp_ref[0:1, :])
    da = dt * (-jnp.exp(hp_ref[1:2, :]))
    acum = _dot_01(tril, da)
    smt = sm.T
    dtt = _softplus(smt[0:SSD_HEADS, :] + hpt_ref[0:SSD_HEADS, 0:1])
    dat = dtt * (-jnp.exp(hpt_ref[0:SSD_HEADS, 1:2]))
    acumt = _dot_x01(dat, triu)

    heads3 = jnp.concatenate([dt, jnp.exp(acum), jnp.exp(acum[ROWS - 1:ROWS, :] - acum)], axis=0)
    h_hi, h_lo = _split(heads3)
    heads6 = jnp.concatenate([h_hi, h_lo], axis=0)
    lane_lo = _iota2((ROWS, LANES), 1) < SSD_HEAD_DIM
    yield

    hg = SSD_HEADS // SSD_GROUPS
    gw = SSD_WIDTH // SSD_GROUPS
    hrow = _iota2((LANES, gw), 0)
    hcol = _iota2((LANES, gw), 1)
    for g in range(SSD_GROUPS):
        gs = slice(g * gw, (g + 1) * gw)
        expand = (((hcol >> 6) + g * hg) == hrow).astype(BF16)
        ex = jnp.dot(heads6, expand, preferred_element_type=F32)
        ex = ex[0:3 * ROWS, :] + ex[3 * ROWS:6 * ROWS, :]
        dtx = ex[0:ROWS, :]
        eax = ex[ROWS:2 * ROWS, :]
        tex = ex[2 * ROWS:3 * ROWS, :]
        yield
        xs = xa_ref[:, gs]
        xdt = xs * dtx
        xdt_b = xdt.astype(BF16)
        xend_b = (xdt * tex).astype(BF16)
        bm = xa_ref[:, SSD_WIDTH + g * SSD_STATE:SSD_WIDTH + (g + 1) * SSD_STATE]
        cm = xa_ref[:, SSD_WIDTH + SSD_BC + g * SSD_STATE:SSD_WIDTH + SSD_BC + (g + 1) * SSD_STATE]
        cm_b = cm.astype(BF16)
        cb = _dot_nt(cm_b, bm)
        yield
        yd = []
        for pair in range(hg // 2):
            res = []
            for sub in range(2):
                h = g * hg + pair * 2 + sub
                seg = acum[:, h:h + 1] - acumt[h:h + 1, :]
                lmat = cb * jnp.exp(jnp.where(causal, seg, -jnp.inf))
                c0 = pair * 2 * SSD_HEAD_DIM
                res.append(jnp.dot(lmat.astype(BF16), xdt_b[:, c0:c0 + LANES],
                                   preferred_element_type=F32))
            yd.append(jnp.where(lane_lo, res[0], res[1]))
            yield
        yd = jnp.concatenate(yd, axis=1)
        st = st_ref[g]
        yoff = jnp.dot(cm_b, st.astype(BF16), preferred_element_type=F32) * eax
        st_ref[g] = (st * eax[ROWS - 1:ROWS, :]
                     + jnp.dot(bm.T.astype(BF16), xend_b, preferred_element_type=F32))
        yield
        y = yd + yoff + xs * dsk_ref[:, gs]
        y = y * _silu(z_ref[:, gs])
        ms = jnp.mean(y * y, axis=-1, keepdims=True)
        y_ref[:, MIX_SSD + g * gw:MIX_SSD + (g + 1) * gw] = (
            y * lax.rsqrt(ms + 1e-6) * nw_ref[:, gs]).astype(y_ref.dtype)
        yield


def _gdn_stages(qkv_ref, gate_ref, sm_ref, cw_ref, hp_ref, hpt_ref, nw_ref, y_ref, xpad_ref, xa_ref, st_ref):
    yield from _conv_silu_stages(qkv_ref, xpad_ref, xa_ref, cw_ref, None)

    row = _iota2((ROWS, ROWS), 0)
    col = _iota2((ROWS, ROWS), 1)
    incl = row >= col
    strict = row > col
    tril = incl.astype(F32)
    triu = (row <= col).astype(F32)

    sm = sm_ref[...]
    la = -jnp.exp(hp_ref[1:2, :]) * _softplus(sm + hp_ref[0:1, :])
    gcum = _dot_01(tril, la)
    smt = sm.T
    lat = (-jnp.exp(hpt_ref[SM_B:SM_B + SUBLANES, 1:2])
           * _softplus(smt[SM_B:SM_B + SUBLANES, :] + hpt_ref[SM_B:SM_B + SUBLANES, 0:1]))
    gcumt = _dot_x01(lat, triu)
    beta_all = _sigmoid(sm)
    yield

    dh = GDN_HEAD_DIM
    heads = range(GDN_HEADS)
    qs, ks, xs, rhss, decays, gs = [], [], [], [], [], []
    for h in heads:
        q = xa_ref[:, h * dh:(h + 1) * dh]
        k = xa_ref[:, GDN_WIDTH + h * dh:GDN_WIDTH + (h + 1) * dh]
        v = xa_ref[:, 2 * GDN_WIDTH + h * dh:2 * GDN_WIDTH + (h + 1) * dh]
        q = q * lax.rsqrt(jnp.sum(q * q, axis=-1, keepdims=True) + 1e-6) * (dh ** -0.5)
        k = k * lax.rsqrt(jnp.sum(k * k, axis=-1, keepdims=True) + 1e-6)
        g = gcum[:, SM_A + h:SM_A + h + 1]
        gt = gcumt[SM_A - SM_B + h:SM_A - SM_B + h + 1, :]
        beta = beta_all[:, SM_B + h:SM_B + h + 1]
        decay = jnp.exp(jnp.where(incl, g - gt, -jnp.inf))
        kb = k * beta
        xs.append(-jnp.where(strict, _dot_nt(kb, k) * decay, 0.0))
        rhss.append(jnp.concatenate([v * beta, kb * jnp.exp(g)], axis=1))
        qs.append(q)
        ks.append(k)
        decays.append(decay)
        gs.append(g)
        yield
    ps = xs
    ns = xs
    levels = ROWS.bit_length() - 2
    for j in range(levels):
        if j < GDN_HI_LEVELS:
            pp = [_split(p) for p in ps]
            ps = [_dot3(p, p) for p in pp]
            yield
            pp = [_split(p) for p in ps]
            ns = [n + p + _dot3(_split(n), p2) for n, p, p2 in zip(ns, ps, pp)]
        else:
            ps = [_dot(p, p) for p in ps]
            yield
            ns = [n + p + _dot(n, p) for n, p in zip(ns, ps)]
        yield
    sols = [r + _dot3(_split(n), _split(r)) for n, r in zip(ns, rhss)]
    yield
    qks = [_dot_nt(q, k) * d for q, k, d in zip(qs, ks, decays)]
    sts = [st_ref[h] for h in heads]
    v_news = [s[:, 0:dh] - _dot(s[:, dh:2 * dh], st) for s, st in zip(sols, sts)]
    yield
    os_ = [_dot(q * jnp.exp(g), st) + _dot(qk, vn) for q, g, st, qk, vn in zip(qs, gs, sts, qks, v_news)]
    yield
    for h in heads:
        glast = gs[h][ROWS - 1:ROWS, :]
        kd = ks[h] * jnp.exp(glast - gs[h])
        st_ref[h] = sts[h] * jnp.exp(glast) + _dot(kd.T, v_news[h])
    yield
    for h in heads:
        o = os_[h]
        o = o * lax.rsqrt(jnp.mean(o * o, axis=-1, keepdims=True) + 1e-6)
        y_ref[:, MIX_GDN + h * dh:MIX_GDN + (h + 1) * dh] = (
            o * nw_ref[...] * _silu(gate_ref[:, h * dh:(h + 1) * dh])).astype(y_ref.dtype)
        yield


def _hgrn_stages(q_ref, f_ref, i_ref, gate_ref, lb_ref, nw_ref, y_ref, st_ref):
    row = _iota2((ROWS, ROWS), 0)
    col = _iota2((ROWS, ROWS), 1)
    tril = (row >= col).astype(F32)
    rmod = _iota2((ROWS, LANES), 0) & (SUBLANES - 1)
    rfull = _iota2((ROWS, LANES), 0)
    dk = HG_HEAD_DIM
    level_masks = {}
    m = 1
    while m < ROWS:
        right = (rfull & m) != 0
        parent = ~(2 * m - 1)
        keep = (((row & m) ^ m) | (col & m) | ((row ^ col) & parent)) == 0
        level_masks[m] = (right, jnp.where(right, LOG2E, -LOG2E), keep)
        m *= 2
    for h in range(HG_HEADS):
        sl = slice(h * dk, (h + 1) * dk)
        lb = lb_ref[:, sl]
        fr = f_ref[:, sl]
        log_sig = jnp.minimum(fr, 0.0) - jnp.log(1.0 + jnp.exp(-jnp.abs(fr)))
        a = jnp.log(lb)
        y = jnp.log1p(-lb) + log_sig
        log_f = jnp.maximum(a, y) + jnp.log(1.0 + jnp.exp(-jnp.abs(a - y)))
        k = (1.0 - lb) * _sigmoid(-fr)
        q = _silu(q_ref[:, sl])
        v = i_ref[:, sl]
        b = _dot_01(tril, log_f)
        yield

        st = st_ref[h]
        o = _dot_nt(q * jnp.exp(b), st)
        blast = b[ROWS - 1:ROWS, :]
        st_ref[h] = st * jnp.exp(blast) + _dot_tn(v, k * jnp.exp(blast - b))
        yield

        pmat = jnp.zeros((ROWS, ROWS), F32)
        b3 = b.reshape(ROWS // SUBLANES, SUBLANES, dk)
        m = 1
        while m < ROWS:
            if m < SUBLANES:
                ref = None
                for p0 in range(0, SUBLANES, 2 * m):
                    cand = jnp.broadcast_to(b3[:, p0 + m - 1:p0 + m, :], b3.shape).reshape(ROWS, dk)
                    ref = cand if ref is None else jnp.where(rmod >= p0, cand, ref)
            else:
                ref = jnp.concatenate(
                    [jnp.broadcast_to(b[p0 + m - 1:p0 + m, :], (2 * m, dk)) for p0 in range(0, ROWS, 2 * m)],
                    axis=0)
            right, scale, keep = level_masks[m]
            z = jnp.where(right, q, k) * jnp.exp2((b - ref) * scale)
            zb = z.astype(BF16)
            pmat = pmat + jnp.where(keep, _dot_nt(zb, zb), 0.0)
            m *= 2
            yield
        o = o + _dot(pmat, v) + jnp.sum(q * k, axis=-1, keepdims=True) * v
        yield

        o = o * lax.rsqrt(jnp.mean(o * o, axis=-1, keepdims=True) + 1e-6)
        y_ref[:, MIX_HG + h * dk:MIX_HG + (h + 1) * dk] = (
            o * nw_ref[...] * _silu(gate_ref[:, sl])).astype(y_ref.dtype)
        yield


def _in_proj_stages(hb_ref, w_ref, out_ref):
    hb = hb_ref[...]
    for c0 in range(0, PROJ_COLS, PROJ_SLAB):
        out_ref[:, c0:c0 + PROJ_SLAB] = jnp.dot(hb, w_ref[:, c0:c0 + PROJ_SLAB], preferred_element_type=F32)
        yield


def _mixer_kernel(hb0_ref, hbn_ref, w_ref,
                  scw_ref, scb_ref, shp_ref, shpt_ref, dsk_ref, snw_ref,
                  gcw_ref, ghp_ref, ghpt_ref, gnw_ref, lb_ref, hnw_ref,
                  y_ref, proj_ref, sxpad_ref, sxa_ref, sst_ref, gxpad_ref, gxa_ref, gst_ref, hst_ref):
    c = pl.program_id(1)

    @pl.when(c == 0)
    def _():
        sst_ref[...] = jnp.zeros(sst_ref.shape, F32)
        gst_ref[...] = jnp.zeros(gst_ref.shape, F32)
        hst_ref[...] = jnp.zeros(hst_ref.shape, F32)
        sxpad_ref[0:SUBLANES, :] = jnp.zeros((SUBLANES, sxpad_ref.shape[1]), F32)
        gxpad_ref[0:SUBLANES, :] = jnp.zeros((SUBLANES, gxpad_ref.shape[1]), F32)
        for _ in _in_proj_stages(hb0_ref, w_ref, proj_ref.at[0]):
            pass

    cur = proj_ref.at[c % 2]
    nxt = proj_ref.at[(c + 1) % 2]

    def seg(off, width):
        return cur.at[:, off:off + width]

    sm_ref = seg(OFF_SMALL, LANES)
    streams = [
        _gdn_stages(seg(OFF_QKV, 3 * GDN_WIDTH), seg(OFF_GGATE, GDN_WIDTH), sm_ref, gcw_ref, ghp_ref, ghpt_ref,
                    gnw_ref, y_ref, gxpad_ref, gxa_ref, gst_ref),
        _ssd_stages(seg(OFF_Z, SSD_WIDTH), seg(OFF_XBC, SSD_XBC), sm_ref, scw_ref, scb_ref, shp_ref, shpt_ref,
                    dsk_ref, snw_ref, y_ref, sxpad_ref, sxa_ref, sst_ref),
        _hgrn_stages(seg(OFF_HQ, HG_WIDTH), seg(OFF_HF, HG_WIDTH), seg(OFF_HI, HG_WIDTH),
                     seg(OFF_HGATE, HG_WIDTH), lb_ref, hnw_ref, y_ref, hst_ref),
        _in_proj_stages(hbn_ref, w_ref, nxt),
    ]
    live = [True] * len(streams)
    rnd = 0
    while any(live):
        advance = [1, 1, 1 if rnd < HG_SLOW_ROUNDS else 2, 1]
        rnd += 1
        for s, stream in enumerate(streams):
            for _ in range(advance[s]):
                if live[s]:
                    try:
                        next(stream)
                    except StopIteration:
                        live[s] = False


def _mixers(hb, w_proj, layer, nb, seq, ssd_conv_w, ssd_conv_b, ssd_dt_bias, ssd_a_log, ssd_d, ssd_norm_w,
            gdn_conv_w, gdn_dt_bias, gdn_a_log, gdn_norm_w, lb, hg_norm_w):
    nblk = seq // ROWS
    shp = jnp.zeros((SUBLANES, LANES), F32)
    shp = shp.at[0, SM_DT:SM_DT + SSD_HEADS].set(ssd_dt_bias).at[1, SM_DT:SM_DT + SSD_HEADS].set(ssd_a_log)
    ghp = jnp.zeros((SUBLANES, LANES), F32)
    ghp = ghp.at[0, SM_A:SM_A + GDN_HEADS].set(gdn_dt_bias).at[1, SM_A:SM_A + GDN_HEADS].set(gdn_a_log)
    dsk = jnp.repeat(ssd_d, SSD_HEAD_DIM)[None, :]
    const = lambda b, c: (0, 0)

    def whole(shape):
        return pl.BlockSpec(shape, const)

    return pl.pallas_call(
        _mixer_kernel,
        out_shape=jax.ShapeDtypeStruct((nb * seq, D_MIX), BF16),
        grid=(nb, nblk),
        in_specs=[
            pl.BlockSpec((ROWS, D_MODEL), lambda b, c: (b * nblk + c, 0)),
            pl.BlockSpec((ROWS, D_MODEL), lambda b, c: (b * nblk + jnp.minimum(c + 1, nblk - 1), 0)),
            pl.BlockSpec((None, D_MODEL, PROJ_COLS), lambda b, c: (layer, 0, 0), pipeline_mode=pl.Buffered(1)),
            whole((SSD_CONV, SSD_XBC)), whole((1, SSD_XBC)), whole((SUBLANES, LANES)), whole((LANES, SUBLANES)),
            whole((1, SSD_WIDTH)), whole((1, SSD_WIDTH)),
            whole((GDN_CONV, 3 * GDN_WIDTH)), whole((SUBLANES, LANES)), whole((LANES, SUBLANES)),
            whole((1, GDN_HEAD_DIM)), whole((1, HG_WIDTH)), whole((1, HG_HEAD_DIM)),
        ],
        out_specs=pl.BlockSpec((ROWS, D_MIX), lambda b, c: (b * nblk + c, 0)),
        scratch_shapes=[pltpu.VMEM((2, ROWS, PROJ_COLS), F32),
                        pltpu.VMEM((ROWS + SUBLANES, SSD_XBC), F32),
                        pltpu.VMEM((ROWS, SSD_XBC), F32),
                        pltpu.VMEM((SSD_GROUPS, SSD_STATE, SSD_WIDTH // SSD_GROUPS), F32),
                        pltpu.VMEM((ROWS + SUBLANES, 3 * GDN_WIDTH), F32),
                        pltpu.VMEM((ROWS, 3 * GDN_WIDTH), F32),
                        pltpu.VMEM((GDN_HEADS, GDN_HEAD_DIM, GDN_HEAD_DIM), F32),
                        pltpu.VMEM((HG_HEADS, HG_HEAD_DIM, HG_HEAD_DIM), F32)],
        compiler_params=_params(("parallel", "arbitrary")),
        name="in_proj_mixers",
    )(hb, hb, w_proj,
      ssd_conv_w, ssd_conv_b[None, :], shp, shp.T, dsk, ssd_norm_w[None, :],
      gdn_conv_w, ghp, ghp.T, gdn_norm_w[None, :], lb[None, :], hg_norm_w[None, :])


def _layer_norm(x, g, b):
    mu = jnp.mean(x, axis=-1, keepdims=True)
    xc = x - mu
    var = jnp.mean(xc * xc, axis=-1, keepdims=True)
    return xc * lax.rsqrt(var + 1e-5) * g + b


ROUTER_ROWS = 32


def _outproj_kernel(y_ref, h_ref, w_ref, g_ref, b_ref, wr_ref, br_ref, h1_ref, h1b_ref, comb_ref):
    mix = jnp.dot(y_ref[...], w_ref[...], preferred_element_type=F32)
    h1 = _layer_norm(DN_ALPHA * h_ref[...] + mix, g_ref[...], b_ref[...])
    h1_ref[...] = h1
    h1b_ref[...] = h1.astype(BF16)

    h_hi, h_lo = _split(h1)
    wr = wr_ref[...]
    lt = _dot_nt(wr, h_hi) + _dot_nt(wr, h_lo)
    lt = lt[0:ROUTER_ROWS, :] + lt[ROUTER_ROWS:2 * ROUTER_ROWS, :] + br_ref[...]
    glog = [lt[g:g + 1, :] for g in range(MOE_GROUPS)]
    elog = [lt[MOE_GROUPS + e:MOE_GROUPS + e + 1, :] for e in range(N_EXPERTS)]

    def softmax(rows):
        m = rows[0]
        for r in rows[1:]:
            m = jnp.maximum(m, r)
        ex = [jnp.exp(r - m) for r in rows]
        tot = ex[0]
        for r in ex[1:]:
            tot = tot + r
        return [r / tot for r in ex]

    def top1(rows):
        best = rows[0]
        for r in rows[1:]:
            best = jnp.maximum(best, r)
        idx = jnp.full_like(best, float(len(rows) - 1))
        for j in range(len(rows) - 2, -1, -1):
            idx = jnp.where(rows[j] == best, float(j), idx)
        return best, idx

    g_p, g_idx = top1(softmax(glog))
    chosen = []
    for k in range(EXPERTS_PER_GROUP):
        acc = jnp.zeros_like(g_p)
        for g in range(MOE_GROUPS):
            acc = jnp.where(g_idx == float(g), elog[g * EXPERTS_PER_GROUP + k], acc)
        chosen.append(acc)
    eprob = softmax(chosen)
    p1, i1 = top1(eprob)
    p2, i2 = top1([jnp.where(i1 == float(k), -1.0, eprob[k]) for k in range(EXPERTS_PER_GROUP)])
    denom = p1 + p2
    w1 = g_p * p1 / denom
    w2 = g_p * p2 / denom
    wk = [jnp.where(i1 == float(k), w1, 0.0) + jnp.where(i2 == float(k), w2, 0.0)
          for k in range(EXPERTS_PER_GROUP)]
    rid = _iota2((ROUTER_ROWS, lt.shape[1]), 0)
    comb_t = jnp.where(rid == N_EXPERTS, g_idx, 0.0)
    for g in range(MOE_GROUPS):
        for k in range(EXPERTS_PER_GROUP):
            comb_t = jnp.where(rid == g * EXPERTS_PER_GROUP + k,
                               jnp.where(g_idx == float(g), wk[k], 0.0), comb_t)
    comb_t = jnp.concatenate([comb_t, jnp.zeros((LANES - ROUTER_ROWS, lt.shape[1]), F32)], axis=0)
    comb_ref[...] = comb_t.T


def _router_params(w_group, b_group, w_expert, b_expert):
    pad = ROUTER_ROWS - MOE_GROUPS - N_EXPERTS
    w_t = jnp.pad(jnp.concatenate([w_group, w_expert], axis=1).T, ((0, pad), (0, 0)))
    b_col = jnp.pad(jnp.concatenate([b_group, b_expert]), (0, pad))[:, None]
    return jnp.concatenate(_split(w_t), axis=0), b_col


def _outproj(y, h, w_out, layer, ln_g, ln_b, w_router, b_router, tm):
    t = h.shape[0]
    rowmap = lambda i: (i, 0)
    const = lambda i: (0, 0)
    return pl.pallas_call(
        _outproj_kernel,
        out_shape=(jax.ShapeDtypeStruct((t, D_MODEL), F32),
                   jax.ShapeDtypeStruct((t, D_MODEL), BF16),
                   jax.ShapeDtypeStruct((t, LANES), F32)),
        grid=(t // tm,),
        in_specs=[pl.BlockSpec((tm, D_MIX), rowmap),
                  pl.BlockSpec((tm, D_MODEL), rowmap),
                  pl.BlockSpec((None, D_MIX, D_MODEL), lambda i: (layer, 0, 0)),
                  pl.BlockSpec((1, D_MODEL), const),
                  pl.BlockSpec((1, D_MODEL), const),
                  pl.BlockSpec((2 * ROUTER_ROWS, D_MODEL), const),
                  pl.BlockSpec((ROUTER_ROWS, 1), const)],
        out_specs=(pl.BlockSpec((tm, D_MODEL), rowmap),
                   pl.BlockSpec((tm, D_MODEL), rowmap),
                   pl.BlockSpec((tm, LANES), rowmap)),
        compiler_params=_params(("parallel",)),
        name="out_proj_ln_router",
    )(y, h, w_out, ln_g[None, :], ln_b[None, :], w_router, b_router)


def _prefix_lanes(v, idx):
    axis = 1 if v.shape[0] == 1 else 0
    out = jnp.zeros_like(v)
    for g in range(MOE_GROUPS - 1):
        vg = v[:, g:g + 1] if axis == 1 else v[g:g + 1, :]
        out = out + jnp.where(idx > g, vg, 0.0)
    return out


def _moe_kernel(hb_ref, h_ref, comb_ref, wgu_ref, wdn_ref, g_ref, b_ref, o_ref, ob_ref,
                xs_ref, cw_ref, ys_ref):
    tm = hb_ref.shape[0]
    comb = comb_ref[...]
    lane = _iota2((tm, LANES), 1).astype(F32)
    gid = comb[:, N_EXPERTS:N_EXPERTS + 1]
    gsel = (lane == gid).astype(F32)
    tr = _iota2((tm, tm), 0)
    tc = _iota2((tm, tm), 1)

    rank_c = _dot((tc < tr).astype(F32), gsel)
    cnt_r = jnp.sum(gsel, axis=0, keepdims=True)
    start_r = _prefix_lanes(cnt_r, _iota2((1, LANES), 1))
    dest_c = jnp.sum(gsel * (start_r + rank_c), axis=1, keepdims=True)
    eye8 = (_iota2((SUBLANES, LANES), 0) == _iota2((SUBLANES, LANES), 1)).astype(F32)
    gsel_t = _dot_nt(eye8, gsel)
    rank_r = _dot(gsel_t, (tr < tc).astype(F32))
    cnt_c = jnp.sum(gsel_t, axis=1, keepdims=True)
    start_c = _prefix_lanes(cnt_c, _iota2((SUBLANES, 1), 0))
    dest_r = jnp.sum(gsel_t * (start_c + rank_r), axis=0, keepdims=True)

    perm = (dest_r == tr.astype(F32)).astype(BF16)
    xs_ref[...] = jnp.dot(perm, hb_ref[...], preferred_element_type=F32).astype(BF16)
    c_hi, c_lo = _split(comb)
    cw2 = jnp.dot(perm, jnp.concatenate([c_hi, c_lo], axis=1), preferred_element_type=F32)
    cw_ref[...] = cw2[:, 0:LANES] + cw2[:, LANES:2 * LANES]
    ys_ref[...] = jnp.zeros(ys_ref.shape, F32)

    lane1 = _iota2((1, LANES), 1)
    starts, ends = [], []
    for g in range(MOE_GROUPS):
        starts.append(jnp.sum(jnp.where(lane1 == g, start_r, 0.0)))
        ends.append(starts[g] + jnp.sum(jnp.where(lane1 == g, cnt_r, 0.0)))
    for c in range(tm // MOE_CHUNK):
        lo = c * MOE_CHUNK
        hi = lo + MOE_CHUNK
        for g in range(MOE_GROUPS):
            @pl.when((starts[g] < hi) & (ends[g] > lo))
            def _(lo=lo, hi=hi, g=g):
                x = xs_ref[lo:hi, :]
                cw = cw_ref[lo:hi, :]
                hm = []
                for e in range(EXPERTS_PER_GROUP):
                    ex = g * EXPERTS_PER_GROUP + e
                    gu = jnp.dot(x, wgu_ref[ex], preferred_element_type=F32)
                    hm.append((_silu(gu[:, 0:D_EXPERT]) * gu[:, D_EXPERT:2 * D_EXPERT]
                               * cw[:, ex:ex + 1]).astype(BF16))
                ys_ref[lo:hi, :] += jnp.dot(jnp.concatenate(hm, axis=1), wdn_ref[g],
                                            preferred_element_type=F32)

    unperm = (dest_c == tc.astype(F32)).astype(BF16)
    y = jnp.dot(unperm, ys_ref[...].astype(BF16), preferred_element_type=F32)
    h2 = _layer_norm(DN_ALPHA * h_ref[...] + y, g_ref[...], b_ref[...])
    o_ref[...] = h2
    ob_ref[...] = h2.astype(BF16)


def _moe(hb, h, comb, w_gu, w_dn, layer, ln_g, ln_b, tm):
    t = h.shape[0]
    rowmap = lambda i: (i, 0)
    const = lambda i: (0, 0)
    const3 = lambda i: (layer, 0, 0, 0)
    return pl.pallas_call(
        _moe_kernel,
        out_shape=(jax.ShapeDtypeStruct((t, D_MODEL), F32),
                   jax.ShapeDtypeStruct((t, D_MODEL), BF16)),
        grid=(t // tm,),
        in_specs=[pl.BlockSpec((tm, D_MODEL), rowmap),
                  pl.BlockSpec((tm, D_MODEL), rowmap),
                  pl.BlockSpec((tm, LANES), rowmap),
                  pl.BlockSpec((None, N_EXPERTS, D_MODEL, 2 * D_EXPERT), const3, pipeline_mode=pl.Buffered(1)),
                  pl.BlockSpec((None, MOE_GROUPS, EXPERTS_PER_GROUP * D_EXPERT, D_MODEL), const3,
                               pipeline_mode=pl.Buffered(1)),
                  pl.BlockSpec((1, D_MODEL), const),
                  pl.BlockSpec((1, D_MODEL), const)],
        out_specs=(pl.BlockSpec((tm, D_MODEL), rowmap),
                   pl.BlockSpec((tm, D_MODEL), rowmap)),
        scratch_shapes=[pltpu.VMEM((tm, D_MODEL), BF16),
                        pltpu.VMEM((tm, LANES), F32),
                        pltpu.VMEM((tm, D_MODEL), F32)],
        compiler_params=pltpu.CompilerParams(dimension_semantics=("parallel",),
                                             vmem_limit_bytes=MOE_VMEM_LIMIT),
        name="moe_ln",
    )(hb, h, comb, w_gu, w_dn, ln_g[None, :], ln_b[None, :])


W_IN_SEGMENTS = ((SSD_WIDTH, OFF_Z), (SSD_XBC, OFF_XBC), (SSD_HEADS, OFF_SMALL + SM_DT),
                 (3 * GDN_WIDTH, OFF_QKV), (GDN_WIDTH, OFF_GGATE), (GDN_HEADS, OFF_SMALL + SM_B),
                 (GDN_HEADS, OFF_SMALL + SM_A), (HG_WIDTH, OFF_HQ), (HG_WIDTH, OFF_HF),
                 (HG_WIDTH, OFF_HI), (HG_WIDTH, OFF_HGATE))
W_T_BLOCK = 256
W_MAIN_BLOCKS = OFF_SMALL // W_T_BLOCK


def _w_in_kernel(tbl_ref, w_ref, small_ref, o_ref):
    j = pl.program_id(1)

    @pl.when(j < W_MAIN_BLOCKS)
    def _():
        o_ref[...] = w_ref[0].T.astype(BF16)

    @pl.when(j >= W_MAIN_BLOCKS)
    def _():
        o_ref[...] = small_ref[...].T.astype(BF16)


def _split_w_in(w_in):
    depth = w_in.shape[0]
    w_t = jnp.swapaxes(w_in, 1, 2)
    src_of, small_rows, src = {}, [], 0
    for width, dst in W_IN_SEGMENTS:
        if width >= W_T_BLOCK:
            for off in range(0, width, W_T_BLOCK):
                src_of[(dst + off) // W_T_BLOCK] = src + off
        else:
            small_rows.append(w_t[:, src:src + width, :])
        src += width
    table = jnp.array([src_of.get(j, 0) for j in range(PROJ_COLS // W_T_BLOCK)], jnp.int32)
    small = jnp.concatenate(small_rows, axis=1)
    small = jnp.pad(small, ((0, 0), (0, PROJ_COLS - OFF_SMALL - small.shape[1]), (0, 0)))
    return pl.pallas_call(
        _w_in_kernel,
        out_shape=jax.ShapeDtypeStruct((depth, D_MODEL, PROJ_COLS), BF16),
        grid_spec=pltpu.PrefetchScalarGridSpec(
            num_scalar_prefetch=1,
            grid=(depth, PROJ_COLS // W_T_BLOCK),
            in_specs=[pl.BlockSpec((pl.Element(1), pl.Element(W_T_BLOCK), pl.Element(D_MODEL)),
                                   lambda l, j, tbl: (l, pl.multiple_of(tbl[j], SUBLANES), 0)),
                      pl.BlockSpec((None, PROJ_COLS - OFF_SMALL, D_MODEL), lambda l, j, tbl: (l, 0, 0))],
            out_specs=pl.BlockSpec((None, D_MODEL, W_T_BLOCK), lambda l, j, tbl: (l, 0, j))),
        compiler_params=_params(("parallel", "arbitrary")),
        name="w_in_relayout",
    )(table, w_t, small)


def kernel(x, w_in, ssd_conv_w, ssd_conv_b, ssd_dt_bias, ssd_a_log, ssd_d, ssd_norm_w, gdn_conv_w, gdn_dt_bias, gdn_a_log, gdn_norm_w, hg_lb_logits, hg_norm_w, w_out, ln1_g, ln1_b, w_router_group, b_router_group, w_router_expert, b_router_expert, w_expert_gate_up, w_expert_down, ln2_g, ln2_b):
    nb, seq, d = x.shape
    t = nb * seq
    lb_cum = jnp.cumsum(jax.nn.softmax(hg_lb_logits.astype(F32), axis=0), axis=0)
    lb_all = lb_cum - lb_cum[0:1]
    h = x.reshape(t, d)
    hb = h.astype(BF16)
    w_proj = _split_w_in(w_in)
    w_out_b = w_out.astype(BF16)
    w_gu_b = w_expert_gate_up.astype(BF16)
    w_dn_b = w_expert_down.astype(BF16).reshape(DEPTH, MOE_GROUPS, EXPERTS_PER_GROUP * D_EXPERT, D_MODEL)
    for l in range(DEPTH):
        y = _mixers(hb, w_proj, l, nb, seq, ssd_conv_w[l], ssd_conv_b[l], ssd_dt_bias[l], ssd_a_log[l], ssd_d[l],
                    ssd_norm_w[l], gdn_conv_w[l], gdn_dt_bias[l], gdn_a_log[l], gdn_norm_w[l],
                    lb_all[l], hg_norm_w[l])
        w_router, b_router = _router_params(w_router_group[l], b_router_group[l],
                                            w_router_expert[l], b_router_expert[l])
        h1, h1b, comb = _outproj(y, h, w_out_b, l, ln1_g[l], ln1_b[l], w_router, b_router, 512)
        h, hb = _moe(h1b, h1, comb, w_gu_b, w_dn_b, l, ln2_g[l], ln2_b[l], 512)
    return h.reshape(nb, seq, d)
```

```python
import jax
import jax.numpy as jnp
from jax import lax
from jax.experimental import pallas as pl
from jax.experimental.pallas import tpu as pltpu

F32 = jnp.float32
BF16 = jnp.bfloat16

D_MODEL = 1024
DEPTH = 4
SSD_HEADS = 16
SSD_HEAD_DIM = 64
SSD_WIDTH = 1024
SSD_GROUPS = 2
SSD_STATE = 128
SSD_BC = 256
SSD_XBC = 1536
SSD_CONV = 4
GDN_HEADS = 4
GDN_HEAD_DIM = 128
GDN_WIDTH = 512
GDN_CONV = 4
HG_HEADS = 4
HG_HEAD_DIM = 128
HG_WIDTH = 512
D_MIX = 2048
MOE_GROUPS = 4
EXPERTS_PER_GROUP = 4
N_EXPERTS = 16
D_EXPERT = 256
DN_ALPHA = (2 * DEPTH) ** 0.25
LOG2E = 1.4426950408889634

LANES = 128
SUBLANES = 8
ROWS = 128
VMEM_LIMIT = 48 * 1024 * 1024
MOE_VMEM_LIMIT = 56 * 1024 * 1024
CONV_SLAB = 256
GDN_HI_LEVELS = 2
MOE_CHUNK = 128
PROJ_SLAB = 768
HG_SLOW_ROUNDS = 12

OFF_XBC = 0
OFF_QKV = 1536
OFF_Z = 3072
OFF_GGATE = 4096
OFF_HQ = 4608
OFF_HF = 5120
OFF_HI = 5632
OFF_HGATE = 6144
OFF_SMALL = 6656
PROJ_COLS = 6912
SM_DT = 0
SM_B = 16
SM_A = 20
MIX_SSD = 0
MIX_GDN = SSD_WIDTH
MIX_HG = SSD_WIDTH + GDN_WIDTH


def _dot(a, b):
    return jnp.dot(a.astype(BF16), b.astype(BF16), preferred_element_type=F32)


def _dot_nt(a, b):
    return lax.dot_general(a.astype(BF16), b.astype(BF16), (((1,), (1,)), ((), ())),
                           preferred_element_type=F32)


def _dot_tn(a, b):
    return lax.dot_general(a.astype(BF16), b.astype(BF16), (((0,), (0,)), ((), ())),
                           preferred_element_type=F32)


def _split3(a):
    p1 = a.astype(BF16)
    r1 = a - p1.astype(F32)
    p2 = r1.astype(BF16)
    return p1, p2, (r1 - p2.astype(F32)).astype(BF16)


def _dot_01(m01, x):
    m = m01.astype(BF16)
    return jnp.dot(jnp.concatenate([m, m, m], axis=1), jnp.concatenate(_split3(x), axis=0),
                   preferred_element_type=F32)


def _dot_x01(x, m01):
    m = m01.astype(BF16)
    p1, p2, p3 = _split3(x)
    return (jnp.dot(p1, m, preferred_element_type=F32) + jnp.dot(p2, m, preferred_element_type=F32)
            + jnp.dot(p3, m, preferred_element_type=F32))


def _split(a):
    hi = a.astype(BF16)
    return hi, (a - hi.astype(F32)).astype(BF16)


def _dot3(a, b):
    (ah, al), (bh, bl) = a, b
    return jnp.dot(jnp.concatenate([ah, ah, al], axis=1), jnp.concatenate([bh, bl, bh], axis=0),
                   preferred_element_type=F32)


def _sigmoid(x):
    return 1.0 / (1.0 + jnp.exp(-x))


def _silu(x):
    return x * _sigmoid(x)


def _softplus(x):
    return jnp.maximum(x, 0.0) + jnp.log(1.0 + jnp.exp(-jnp.abs(x)))


def _iota2(shape, dim):
    return lax.broadcasted_iota(jnp.int32, shape, dim)


def _params(sem):
    return pltpu.CompilerParams(dimension_semantics=sem, vmem_limit_bytes=VMEM_LIMIT)


def _matmul_kernel(x_ref, w_ref, o_ref):
    o_ref[...] = jnp.dot(x_ref[...], w_ref[...], preferred_element_type=F32)


def _matmul(x, w, tm, tn):
    t, k = x.shape
    n = w.shape[1]
    return pl.pallas_call(
        _matmul_kernel,
        out_shape=jax.ShapeDtypeStruct((t, n), F32),
        grid=(n // tn, t // tm),
        in_specs=[pl.BlockSpec((tm, k), lambda j, i: (i, 0)),
                  pl.BlockSpec((k, tn), lambda j, i: (0, j))],
        out_specs=pl.BlockSpec((tm, tn), lambda j, i: (i, j)),
        compiler_params=_params(("parallel", "arbitrary")),
        name="in_proj",
    )(x, w)


def _conv_silu_stages(x_ref, xpad_ref, xa_ref, cw_ref, cb_ref):
    rows, cols = x_ref.shape
    k = cw_ref.shape[0]
    for c0 in range(0, cols, CONV_SLAB):
        cs = slice(c0, c0 + CONV_SLAB)
        x = x_ref[:, cs]
        xpad_ref[SUBLANES:SUBLANES + rows, cs] = x
        acc = x * cw_ref[k - 1:k, cs]
        if cb_ref is not None:
            acc = acc + cb_ref[:, cs]
        for i in range(k - 1):
            off = SUBLANES - (k - 1) + i
            acc = acc + xpad_ref[off:off + rows, cs] * cw_ref[i:i + 1, cs]
        xa_ref[:, cs] = _silu(acc)
        xpad_ref[0:SUBLANES, cs] = xpad_ref[rows:rows + SUBLANES, cs]
        yield


def _ssd_stages(z_ref, xbc_ref, sm_ref, cw_ref, cb_ref, hp_ref, hpt_ref, dsk_ref, nw_ref,
                y_ref, xpad_ref, xa_ref, st_ref):
    yield from _conv_silu_stages(xbc_ref, xpad_ref, xa_ref, cw_ref, cb_ref)

    row = _iota2((ROWS, ROWS), 0)
    col = _iota2((ROWS, ROWS), 1)
    causal = row >= col
    tril = causal.astype(F32)
    triu = (row <= col).astype(F32)

    sm = sm_ref[...]
    dt = _softplus(sm + hp_ref[0:1, :])
    da = dt * (-jnp.exp(hp_ref[1:2, :]))
    acum = _dot_01(tril, da)
    smt = sm.T
    dtt = _softplus(smt[0:SSD_HEADS, :] + hpt_ref[0:SSD_HEADS, 0:1])
    dat = dtt * (-jnp.exp(hpt_ref[0:SSD_HEADS, 1:2]))
    acumt = _dot_x01(dat, triu)

    heads3 = jnp.concatenate([dt, jnp.exp(acum), jnp.exp(acum[ROWS - 1:ROWS, :] - acum)], axis=0)
    h_hi, h_lo = _split(heads3)
    heads6 = jnp.concatenate([h_hi, h_lo], axis=0)
    lane_lo = _iota2((ROWS, LANES), 1) < SSD_HEAD_DIM
    yield

    hg = SSD_HEADS // SSD_GROUPS
    gw = SSD_WIDTH // SSD_GROUPS
    hrow = _iota2((LANES, gw), 0)
    hcol = _iota2((LANES, gw), 1)
    for g in range(SSD_GROUPS):
        gs = slice(g * gw, (g + 1) * gw)
        expand = (((hcol >> 6) + g * hg) == hrow).astype(BF16)
        ex = jnp.dot(heads6, expand, preferred_element_type=F32)
        ex = ex[0:3 * ROWS, :] + ex[3 * ROWS:6 * ROWS, :]
        dtx = ex[0:ROWS, :]
        eax = ex[ROWS:2 * ROWS, :]
        tex = ex[2 * ROWS:3 * ROWS, :]
        yield
        xs = xa_ref[:, gs]
        xdt = xs * dtx
        xdt_b = xdt.astype(BF16)
        xend_b = (xdt * tex).astype(BF16)
        bm = xa_ref[:, SSD_WIDTH + g * SSD_STATE:SSD_WIDTH + (g + 1) * SSD_STATE]
        cm = xa_ref[:, SSD_WIDTH + SSD_BC + g * SSD_STATE:SSD_WIDTH + SSD_BC + (g + 1) * SSD_STATE]
        cm_b = cm.astype(BF16)
        cb = _dot_nt(cm_b, bm)
        yield
        yd = []
        for pair in range(hg // 2):
            lmats = []
            for sub in range(2):
                h = g * hg + pair * 2 + sub
                seg = acum[:, h:h + 1] - acumt[h:h + 1, :]
                lmats.append((cb * jnp.exp(jnp.where(causal, seg, -jnp.inf))).astype(BF16))
            c0 = pair * 2 * SSD_HEAD_DIM
            x2 = xdt_b[:, c0:c0 + LANES]
            zero = jnp.zeros_like(x2)
            rhs = jnp.concatenate([jnp.where(lane_lo, x2, zero), jnp.where(lane_lo, zero, x2)], axis=0)
            yd.append(jnp.dot(jnp.concatenate(lmats, axis=1), rhs, preferred_element_type=F32))
            yield
        yd = jnp.concatenate(yd, axis=1)
        st = st_ref[g]
        yoff = jnp.dot(cm_b, st.astype(BF16), preferred_element_type=F32) * eax
        st_ref[g] = (st * eax[ROWS - 1:ROWS, :]
                     + jnp.dot(bm.T.astype(BF16), xend_b, preferred_element_type=F32))
        yield
        y = yd + yoff + xs * dsk_ref[:, gs]
        y = y * _silu(z_ref[:, gs])
        ms = jnp.mean(y * y, axis=-1, keepdims=True)
        y_ref[:, MIX_SSD + g * gw:MIX_SSD + (g + 1) * gw] = (
            y * lax.rsqrt(ms + 1e-6) * nw_ref[:, gs]).astype(y_ref.dtype)
        yield


def _gdn_stages(qkv_ref, gate_ref, sm_ref, cw_ref, hp_ref, hpt_ref, nw_ref, y_ref, xpad_ref, xa_ref, st_ref):
    yield from _conv_silu_stages(qkv_ref, xpad_ref, xa_ref, cw_ref, None)

    row = _iota2((ROWS, ROWS), 0)
    col = _iota2((ROWS, ROWS), 1)
    incl = row >= col
    strict = row > col
    tril = incl.astype(F32)
    triu = (row <= col).astype(F32)

    sm = sm_ref[...]
    la = -jnp.exp(hp_ref[1:2, :]) * _softplus(sm + hp_ref[0:1, :])
    gcum = _dot_01(tril, la)
    smt = sm.T
    lat = (-jnp.exp(hpt_ref[SM_B:SM_B + SUBLANES, 1:2])
           * _softplus(smt[SM_B:SM_B + SUBLANES, :] + hpt_ref[SM_B:SM_B + SUBLANES, 0:1]))
    gcumt = _dot_x01(lat, triu)
    beta_all = _sigmoid(sm)
    yield

    dh = GDN_HEAD_DIM
    heads = range(GDN_HEADS)
    qs, ks, xs, rhss, decays, gs = [], [], [], [], [], []
    for h in heads:
        q = xa_ref[:, h * dh:(h + 1) * dh]
        k = xa_ref[:, GDN_WIDTH + h * dh:GDN_WIDTH + (h + 1) * dh]
        v = xa_ref[:, 2 * GDN_WIDTH + h * dh:2 * GDN_WIDTH + (h + 1) * dh]
        q = q * lax.rsqrt(jnp.sum(q * q, axis=-1, keepdims=True) + 1e-6) * (dh ** -0.5)
        k = k * lax.rsqrt(jnp.sum(k * k, axis=-1, keepdims=True) + 1e-6)
        g = gcum[:, SM_A + h:SM_A + h + 1]
        gt = gcumt[SM_A - SM_B + h:SM_A - SM_B + h + 1, :]
        beta = beta_all[:, SM_B + h:SM_B + h + 1]
        decay = jnp.exp(jnp.where(incl, g - gt, -jnp.inf))
        kb = k * beta
        xs.append(-jnp.where(strict, _dot_nt(kb, k) * decay, 0.0))
        rhss.append(jnp.concatenate([v * beta, kb * jnp.exp(g)], axis=1))
        qs.append(q)
        ks.append(k)
        decays.append(decay)
        gs.append(g)
        yield
    ps = xs
    ns = xs
    levels = ROWS.bit_length() - 2
    for j in range(levels):
        if j < GDN_HI_LEVELS:
            pp = [_split(p) for p in ps]
            ps = [_dot3(p, p) for p in pp]
            yield
            pp = [_split(p) for p in ps]
            ns = [n + p + _dot3(_split(n), p2) for n, p, p2 in zip(ns, ps, pp)]
        else:
            ps = [_dot(p, p) for p in ps]
            yield
            ns = [n + p + _dot(n, p) for n, p in zip(ns, ps)]
        yield
    sols = [r + _dot(n, r) for n, r in zip(ns, rhss)]
    yield
    qks = [_dot_nt(q, k) * d for q, k, d in zip(qs, ks, decays)]
    sts = [st_ref[h] for h in heads]
    v_news = [s[:, 0:dh] - _dot(s[:, dh:2 * dh], st) for s, st in zip(sols, sts)]
    yield
    os_ = [_dot(jnp.concatenate([q * jnp.exp(g), qk], axis=1), jnp.concatenate([st, vn], axis=0))
           for q, g, st, qk, vn in zip(qs, gs, sts, qks, v_news)]
    yield
    for h in heads:
        glast = gs[h][ROWS - 1:ROWS, :]
        kd = ks[h] * jnp.exp(glast - gs[h])
        st_ref[h] = sts[h] * jnp.exp(glast) + _dot(kd.T, v_news[h])
    yield
    for h in heads:
        o = os_[h]
        o = o * lax.rsqrt(jnp.mean(o * o, axis=-1, keepdims=True) + 1e-6)
        y_ref[:, MIX_GDN + h * dh:MIX_GDN + (h + 1) * dh] = (
            o * nw_ref[...] * _silu(gate_ref[:, h * dh:(h + 1) * dh])).astype(y_ref.dtype)
        yield


def _hgrn_stages(q_ref, f_ref, i_ref, gate_ref, lb_ref, nw_ref, y_ref, st_ref):
    row = _iota2((ROWS, ROWS), 0)
    col = _iota2((ROWS, ROWS), 1)
    tril = (row >= col).astype(F32)
    rmod = _iota2((ROWS, LANES), 0) & (SUBLANES - 1)
    rfull = _iota2((ROWS, LANES), 0)
    dk = HG_HEAD_DIM
    level_masks = {}
    m = 1
    while m < ROWS:
        right = (rfull & m) != 0
        parent = ~(2 * m - 1)
        keep = (((row & m) ^ m) | (col & m) | ((row ^ col) & parent)) == 0
        level_masks[m] = (right, jnp.where(right, LOG2E, -LOG2E), keep)
        m *= 2
    for h in range(HG_HEADS):
        sl = slice(h * dk, (h + 1) * dk)
        lb = lb_ref[:, sl]
        fr = f_ref[:, sl]
        log_sig = jnp.minimum(fr, 0.0) - jnp.log(1.0 + jnp.exp(-jnp.abs(fr)))
        a = jnp.log(lb)
        y = jnp.log1p(-lb) + log_sig
        log_f = jnp.maximum(a, y) + jnp.log(1.0 + jnp.exp(-jnp.abs(a - y)))
        k = (1.0 - lb) * _sigmoid(-fr)
        q = _silu(q_ref[:, sl])
        v = i_ref[:, sl]
        b = _dot_01(tril, log_f)
        yield

        st = st_ref[h]
        o = _dot_nt(q * jnp.exp(b), st)
        blast = b[ROWS - 1:ROWS, :]
        st_ref[h] = st * jnp.exp(blast) + _dot_tn(v, k * jnp.exp(blast - b))
        yield

        pmat = jnp.zeros((ROWS, ROWS), F32)
        b3 = b.reshape(ROWS // SUBLANES, SUBLANES, dk)
        m = 1
        while m < ROWS:
            if m < SUBLANES:
                ref = None
                for p0 in range(0, SUBLANES, 2 * m):
                    cand = jnp.broadcast_to(b3[:, p0 + m - 1:p0 + m, :], b3.shape).reshape(ROWS, dk)
                    ref = cand if ref is None else jnp.where(rmod >= p0, cand, ref)
            else:
                ref = jnp.concatenate(
                    [jnp.broadcast_to(b[p0 + m - 1:p0 + m, :], (2 * m, dk)) for p0 in range(0, ROWS, 2 * m)],
                    axis=0)
            right, scale, keep = level_masks[m]
            z = jnp.where(right, q, k) * jnp.exp2((b - ref) * scale)
            zb = z.astype(BF16)
            pmat = pmat + jnp.where(keep, _dot_nt(zb, zb), 0.0)
            m *= 2
            yield
        o = o + _dot(pmat, v) + jnp.sum(q * k, axis=-1, keepdims=True) * v
        yield

        o = o * lax.rsqrt(jnp.mean(o * o, axis=-1, keepdims=True) + 1e-6)
        y_ref[:, MIX_HG + h * dk:MIX_HG + (h + 1) * dk] = (
            o * nw_ref[...] * _silu(gate_ref[:, sl])).astype(y_ref.dtype)
        yield


def _in_proj_stages(hb_ref, w_ref, out_ref):
    hb = hb_ref[...]
    for c0 in range(0, PROJ_COLS, PROJ_SLAB):
        out_ref[:, c0:c0 + PROJ_SLAB] = jnp.dot(hb, w_ref[:, c0:c0 + PROJ_SLAB], preferred_element_type=F32)
        yield


def _mixer_kernel(hb0_ref, hbn_ref, w_ref,
                  scw_ref, scb_ref, shp_ref, shpt_ref, dsk_ref, snw_ref,
                  gcw_ref, ghp_ref, ghpt_ref, gnw_ref, lb_ref, hnw_ref,
                  y_ref, proj_ref, sxpad_ref, sxa_ref, sst_ref, gxpad_ref, gxa_ref, gst_ref, hst_ref):
    c = pl.program_id(1)

    @pl.when(c == 0)
    def _():
        sst_ref[...] = jnp.zeros(sst_ref.shape, F32)
        gst_ref[...] = jnp.zeros(gst_ref.shape, F32)
        hst_ref[...] = jnp.zeros(hst_ref.shape, F32)
        sxpad_ref[0:SUBLANES, :] = jnp.zeros((SUBLANES, sxpad_ref.shape[1]), F32)
        gxpad_ref[0:SUBLANES, :] = jnp.zeros((SUBLANES, gxpad_ref.shape[1]), F32)
        for _ in _in_proj_stages(hb0_ref, w_ref, proj_ref.at[0]):
            pass

    cur = proj_ref.at[c % 2]
    nxt = proj_ref.at[(c + 1) % 2]

    def seg(off, width):
        return cur.at[:, off:off + width]

    sm_ref = seg(OFF_SMALL, LANES)
    streams = [
        _gdn_stages(seg(OFF_QKV, 3 * GDN_WIDTH), seg(OFF_GGATE, GDN_WIDTH), sm_ref, gcw_ref, ghp_ref, ghpt_ref,
                    gnw_ref, y_ref, gxpad_ref, gxa_ref, gst_ref),
        _ssd_stages(seg(OFF_Z, SSD_WIDTH), seg(OFF_XBC, SSD_XBC), sm_ref, scw_ref, scb_ref, shp_ref, shpt_ref,
                    dsk_ref, snw_ref, y_ref, sxpad_ref, sxa_ref, sst_ref),
        _hgrn_stages(seg(OFF_HQ, HG_WIDTH), seg(OFF_HF, HG_WIDTH), seg(OFF_HI, HG_WIDTH),
                     seg(OFF_HGATE, HG_WIDTH), lb_ref, hnw_ref, y_ref, hst_ref),
        _in_proj_stages(hbn_ref, w_ref, nxt),
    ]
    live = [True] * len(streams)
    rnd = 0
    while any(live):
        advance = [1, 1, 1 if rnd < HG_SLOW_ROUNDS else 2, 1]
        rnd += 1
        for s, stream in enumerate(streams):
            for _ in range(advance[s]):
                if live[s]:
                    try:
                        next(stream)
                    except StopIteration:
                        live[s] = False


def _mixers(hb, w_proj, layer, nb, seq, ssd_conv_w, ssd_conv_b, ssd_dt_bias, ssd_a_log, ssd_d, ssd_norm_w,
            gdn_conv_w, gdn_dt_bias, gdn_a_log, gdn_norm_w, lb, hg_norm_w):
    nblk = seq // ROWS
    shp = jnp.zeros((SUBLANES, LANES), F32)
    shp = shp.at[0, SM_DT:SM_DT + SSD_HEADS].set(ssd_dt_bias).at[1, SM_DT:SM_DT + SSD_HEADS].set(ssd_a_log)
    ghp = jnp.zeros((SUBLANES, LANES), F32)
    ghp = ghp.at[0, SM_A:SM_A + GDN_HEADS].set(gdn_dt_bias).at[1, SM_A:SM_A + GDN_HEADS].set(gdn_a_log)
    dsk = jnp.repeat(ssd_d, SSD_HEAD_DIM)[None, :]
    const = lambda b, c: (0, 0)

    def whole(shape):
        return pl.BlockSpec(shape, const)

    return pl.pallas_call(
        _mixer_kernel,
        out_shape=jax.ShapeDtypeStruct((nb * seq, D_MIX), BF16),
        grid=(nb, nblk),
        in_specs=[
            pl.BlockSpec((ROWS, D_MODEL), lambda b, c: (b * nblk + c, 0)),
            pl.BlockSpec((ROWS, D_MODEL), lambda b, c: (b * nblk + jnp.minimum(c + 1, nblk - 1), 0)),
            pl.BlockSpec((None, D_MODEL, PROJ_COLS), lambda b, c: (layer, 0, 0), pipeline_mode=pl.Buffered(1)),
            whole((SSD_CONV, SSD_XBC)), whole((1, SSD_XBC)), whole((SUBLANES, LANES)), whole((LANES, SUBLANES)),
            whole((1, SSD_WIDTH)), whole((1, SSD_WIDTH)),
            whole((GDN_CONV, 3 * GDN_WIDTH)), whole((SUBLANES, LANES)), whole((LANES, SUBLANES)),
            whole((1, GDN_HEAD_DIM)), whole((1, HG_WIDTH)), whole((1, HG_HEAD_DIM)),
        ],
        out_specs=pl.BlockSpec((ROWS, D_MIX), lambda b, c: (b * nblk + c, 0)),
        scratch_shapes=[pltpu.VMEM((2, ROWS, PROJ_COLS), F32),
                        pltpu.VMEM((ROWS + SUBLANES, SSD_XBC), F32),
                        pltpu.VMEM((ROWS, SSD_XBC), F32),
                        pltpu.VMEM((SSD_GROUPS, SSD_STATE, SSD_WIDTH // SSD_GROUPS), F32),
                        pltpu.VMEM((ROWS + SUBLANES, 3 * GDN_WIDTH), F32),
                        pltpu.VMEM((ROWS, 3 * GDN_WIDTH), F32),
                        pltpu.VMEM((GDN_HEADS, GDN_HEAD_DIM, GDN_HEAD_DIM), F32),
                        pltpu.VMEM((HG_HEADS, HG_HEAD_DIM, HG_HEAD_DIM), F32)],
        compiler_params=_params(("parallel", "arbitrary")),
        name="in_proj_mixers",
    )(hb, hb, w_proj,
      ssd_conv_w, ssd_conv_b[None, :], shp, shp.T, dsk, ssd_norm_w[None, :],
      gdn_conv_w, ghp, ghp.T, gdn_norm_w[None, :], lb[None, :], hg_norm_w[None, :])


def _layer_norm(x, g, b):
    mu = jnp.mean(x, axis=-1, keepdims=True)
    xc = x - mu
    var = jnp.mean(xc * xc, axis=-1, keepdims=True)
    return xc * lax.rsqrt(var + 1e-5) * g + b


ROUTER_ROWS = 32


def _outproj_kernel(y_ref, h_ref, w_ref, g_ref, b_ref, wr_ref, br_ref, h1_ref, h1b_ref, comb_ref):
    mix = jnp.dot(y_ref[...], w_ref[...], preferred_element_type=F32)
    h1 = _layer_norm(DN_ALPHA * h_ref[...] + mix, g_ref[...], b_ref[...])
    h1_ref[...] = h1
    h1b_ref[...] = h1.astype(BF16)

    h_hi, h_lo = _split(h1)
    wr = wr_ref[...]
    lt = _dot_nt(wr, h_hi) + _dot_nt(wr, h_lo)
    lt = lt[0:ROUTER_ROWS, :] + lt[ROUTER_ROWS:2 * ROUTER_ROWS, :] + br_ref[...]
    glog = [lt[g:g + 1, :] for g in range(MOE_GROUPS)]
    elog = [lt[MOE_GROUPS + e:MOE_GROUPS + e + 1, :] for e in range(N_EXPERTS)]

    def softmax(rows):
        m = rows[0]
        for r in rows[1:]:
            m = jnp.maximum(m, r)
        ex = [jnp.exp(r - m) for r in rows]
        tot = ex[0]
        for r in ex[1:]:
            tot = tot + r
        return [r / tot for r in ex]

    def top1(rows):
        best = rows[0]
        for r in rows[1:]:
            best = jnp.maximum(best, r)
        idx = jnp.full_like(best, float(len(rows) - 1))
        for j in range(len(rows) - 2, -1, -1):
            idx = jnp.where(rows[j] == best, float(j), idx)
        return best, idx

    g_p, g_idx = top1(softmax(glog))
    chosen = []
    for k in range(EXPERTS_PER_GROUP):
        acc = jnp.zeros_like(g_p)
        for g in range(MOE_GROUPS):
            acc = jnp.where(g_idx == float(g), elog[g * EXPERTS_PER_GROUP + k], acc)
        chosen.append(acc)
    eprob = softmax(chosen)
    p1, i1 = top1(eprob)
    p2, i2 = top1([jnp.where(i1 == float(k), -1.0, eprob[k]) for k in range(EXPERTS_PER_GROUP)])
    denom = p1 + p2
    w1 = g_p * p1 / denom
    w2 = g_p * p2 / denom
    wk = [jnp.where(i1 == float(k), w1, 0.0) + jnp.where(i2 == float(k), w2, 0.0)
          for k in range(EXPERTS_PER_GROUP)]
    rid = _iota2((ROUTER_ROWS, lt.shape[1]), 0)
    comb_t = jnp.where(rid == N_EXPERTS, g_idx, 0.0)
    for g in range(MOE_GROUPS):
        for k in range(EXPERTS_PER_GROUP):
            comb_t = jnp.where(rid == g * EXPERTS_PER_GROUP + k,
                               jnp.where(g_idx == float(g), wk[k], 0.0), comb_t)
    comb_t = jnp.concatenate([comb_t, jnp.zeros((LANES - ROUTER_ROWS, lt.shape[1]), F32)], axis=0)
    comb_ref[...] = comb_t.T


def _router_params(w_group, b_group, w_expert, b_expert):
    pad = ROUTER_ROWS - MOE_GROUPS - N_EXPERTS
    w_t = jnp.pad(jnp.concatenate([w_group, w_expert], axis=1).T, ((0, pad), (0, 0)))
    b_col = jnp.pad(jnp.concatenate([b_group, b_expert]), (0, pad))[:, None]
    return jnp.concatenate(_split(w_t), axis=0), b_col


def _outproj(y, h, w_out, layer, ln_g, ln_b, w_router, b_router, tm):
    t = h.shape[0]
    rowmap = lambda i: (i, 0)
    const = lambda i: (0, 0)
    return pl.pallas_call(
        _outproj_kernel,
        out_shape=(jax.ShapeDtypeStruct((t, D_MODEL), F32),
                   jax.ShapeDtypeStruct((t, D_MODEL), BF16),
                   jax.ShapeDtypeStruct((t, LANES), F32)),
        grid=(t // tm,),
        in_specs=[pl.BlockSpec((tm, D_MIX), rowmap),
                  pl.BlockSpec((tm, D_MODEL), rowmap),
                  pl.BlockSpec((None, D_MIX, D_MODEL), lambda i: (layer, 0, 0)),
                  pl.BlockSpec((1, D_MODEL), const),
                  pl.BlockSpec((1, D_MODEL), const),
                  pl.BlockSpec((2 * ROUTER_ROWS, D_MODEL), const),
                  pl.BlockSpec((ROUTER_ROWS, 1), const)],
        out_specs=(pl.BlockSpec((tm, D_MODEL), rowmap),
                   pl.BlockSpec((tm, D_MODEL), rowmap),
                   pl.BlockSpec((tm, LANES), rowmap)),
        compiler_params=_params(("parallel",)),
        name="out_proj_ln_router",
    )(y, h, w_out, ln_g[None, :], ln_b[None, :], w_router, b_router)


def _prefix_lanes(v, idx):
    axis = 1 if v.shape[0] == 1 else 0
    out = jnp.zeros_like(v)
    for g in range(MOE_GROUPS - 1):
        vg = v[:, g:g + 1] if axis == 1 else v[g:g + 1, :]
        out = out + jnp.where(idx > g, vg, 0.0)
    return out


def _moe_kernel(hb_ref, h_ref, comb_ref, wgu_ref, wdn_ref, g_ref, b_ref, o_ref, ob_ref,
                xs_ref, cw_ref, ys_ref):
    tm = hb_ref.shape[0]
    comb = comb_ref[...]
    lane = _iota2((tm, LANES), 1).astype(F32)
    gid = comb[:, N_EXPERTS:N_EXPERTS + 1]
    gsel = (lane == gid).astype(F32)
    tr = _iota2((tm, tm), 0)
    tc = _iota2((tm, tm), 1)

    rank_c = _dot((tc < tr).astype(F32), gsel)
    cnt_r = jnp.sum(gsel, axis=0, keepdims=True)
    start_r = _prefix_lanes(cnt_r, _iota2((1, LANES), 1))
    dest_c = jnp.sum(gsel * (start_r + rank_c), axis=1, keepdims=True)
    eye8 = (_iota2((SUBLANES, LANES), 0) == _iota2((SUBLANES, LANES), 1)).astype(F32)
    gsel_t = _dot_nt(eye8, gsel)
    rank_r = _dot(gsel_t, (tr < tc).astype(F32))
    cnt_c = jnp.sum(gsel_t, axis=1, keepdims=True)
    start_c = _prefix_lanes(cnt_c, _iota2((SUBLANES, 1), 0))
    dest_r = jnp.sum(gsel_t * (start_c + rank_r), axis=0, keepdims=True)

    perm = (dest_r == tr.astype(F32)).astype(BF16)
    xs_ref[...] = jnp.dot(perm, hb_ref[...], preferred_element_type=F32).astype(BF16)
    c_hi, c_lo = _split(comb)
    cw2 = jnp.dot(perm, jnp.concatenate([c_hi, c_lo], axis=1), preferred_element_type=F32)
    cw_ref[...] = cw2[:, 0:LANES] + cw2[:, LANES:2 * LANES]
    ys_ref[...] = jnp.zeros(ys_ref.shape, F32)

    lane1 = _iota2((1, LANES), 1)
    starts, ends = [], []
    for g in range(MOE_GROUPS):
        starts.append(jnp.sum(jnp.where(lane1 == g, start_r, 0.0)))
        ends.append(starts[g] + jnp.sum(jnp.where(lane1 == g, cnt_r, 0.0)))
    for c in range(tm // MOE_CHUNK):
        lo = c * MOE_CHUNK
        hi = lo + MOE_CHUNK
        for g in range(MOE_GROUPS):
            @pl.when((starts[g] < hi) & (ends[g] > lo))
            def _(lo=lo, hi=hi, g=g):
                x = xs_ref[lo:hi, :]
                cw = cw_ref[lo:hi, :]
                hm = []
                for e in range(EXPERTS_PER_GROUP):
                    ex = g * EXPERTS_PER_GROUP + e
                    gu = jnp.dot(x, wgu_ref[ex], preferred_element_type=F32)
                    hm.append((_silu(gu[:, 0:D_EXPERT]) * gu[:, D_EXPERT:2 * D_EXPERT]
                               * cw[:, ex:ex + 1]).astype(BF16))
                ys_ref[lo:hi, :] += jnp.dot(jnp.concatenate(hm, axis=1), wdn_ref[g],
                                            preferred_element_type=F32)

    unperm = (dest_c == tc.astype(F32)).astype(BF16)
    y = jnp.dot(unperm, ys_ref[...].astype(BF16), preferred_element_type=F32)
    h2 = _layer_norm(DN_ALPHA * h_ref[...] + y, g_ref[...], b_ref[...])
    o_ref[...] = h2
    ob_ref[...] = h2.astype(BF16)


def _moe(hb, h, comb, w_gu, w_dn, layer, ln_g, ln_b, tm):
    t = h.shape[0]
    rowmap = lambda i: (i, 0)
    const = lambda i: (0, 0)
    const3 = lambda i: (layer, 0, 0, 0)
    return pl.pallas_call(
        _moe_kernel,
        out_shape=(jax.ShapeDtypeStruct((t, D_MODEL), F32),
                   jax.ShapeDtypeStruct((t, D_MODEL), BF16)),
        grid=(t // tm,),
        in_specs=[pl.BlockSpec((tm, D_MODEL), rowmap),
                  pl.BlockSpec((tm, D_MODEL), rowmap),
                  pl.BlockSpec((tm, LANES), rowmap),
                  pl.BlockSpec((None, N_EXPERTS, D_MODEL, 2 * D_EXPERT), const3, pipeline_mode=pl.Buffered(1)),
                  pl.BlockSpec((None, MOE_GROUPS, EXPERTS_PER_GROUP * D_EXPERT, D_MODEL), const3,
                               pipeline_mode=pl.Buffered(1)),
                  pl.BlockSpec((1, D_MODEL), const),
                  pl.BlockSpec((1, D_MODEL), const)],
        out_specs=(pl.BlockSpec((tm, D_MODEL), rowmap),
                   pl.BlockSpec((tm, D_MODEL), rowmap)),
        scratch_shapes=[pltpu.VMEM((tm, D_MODEL), BF16),
                        pltpu.VMEM((tm, LANES), F32),
                        pltpu.VMEM((tm, D_MODEL), F32)],
        compiler_params=pltpu.CompilerParams(dimension_semantics=("parallel",),
                                             vmem_limit_bytes=MOE_VMEM_LIMIT),
        name="moe_ln",
    )(hb, h, comb, w_gu, w_dn, ln_g[None, :], ln_b[None, :])


W_IN_SEGMENTS = ((SSD_WIDTH, OFF_Z), (SSD_XBC, OFF_XBC), (SSD_HEADS, OFF_SMALL + SM_DT),
                 (3 * GDN_WIDTH, OFF_QKV), (GDN_WIDTH, OFF_GGATE), (GDN_HEADS, OFF_SMALL + SM_B),
                 (GDN_HEADS, OFF_SMALL + SM_A), (HG_WIDTH, OFF_HQ), (HG_WIDTH, OFF_HF),
                 (HG_WIDTH, OFF_HI), (HG_WIDTH, OFF_HGATE))
W_T_BLOCK = 256
W_MAIN_BLOCKS = OFF_SMALL // W_T_BLOCK


def _w_in_kernel(tbl_ref, w_ref, small_ref, o_ref):
    j = pl.program_id(1)

    @pl.when(j < W_MAIN_BLOCKS)
    def _():
        o_ref[...] = w_ref[0].T.astype(BF16)

    @pl.when(j >= W_MAIN_BLOCKS)
    def _():
        o_ref[...] = small_ref[...].T.astype(BF16)


def _split_w_in(w_in):
    depth = w_in.shape[0]
    w_t = jnp.swapaxes(w_in, 1, 2)
    src_of, small_rows, src = {}, [], 0
    for width, dst in W_IN_SEGMENTS:
        if width >= W_T_BLOCK:
            for off in range(0, width, W_T_BLOCK):
                src_of[(dst + off) // W_T_BLOCK] = src + off
        else:
            small_rows.append(w_t[:, src:src + width, :])
        src += width
    table = jnp.array([src_of.get(j, 0) for j in range(PROJ_COLS // W_T_BLOCK)], jnp.int32)
    small = jnp.concatenate(small_rows, axis=1)
    small = jnp.pad(small, ((0, 0), (0, PROJ_COLS - OFF_SMALL - small.shape[1]), (0, 0)))
    return pl.pallas_call(
        _w_in_kernel,
        out_shape=jax.ShapeDtypeStruct((depth, D_MODEL, PROJ_COLS), BF16),
        grid_spec=pltpu.PrefetchScalarGridSpec(
            num_scalar_prefetch=1,
            grid=(depth, PROJ_COLS // W_T_BLOCK),
            in_specs=[pl.BlockSpec((pl.Element(1), pl.Element(W_T_BLOCK), pl.Element(D_MODEL)),
                                   lambda l, j, tbl: (l, pl.multiple_of(tbl[j], SUBLANES), 0)),
                      pl.BlockSpec((None, PROJ_COLS - OFF_SMALL, D_MODEL), lambda l, j, tbl: (l, 0, 0))],
            out_specs=pl.BlockSpec((None, D_MODEL, W_T_BLOCK), lambda l, j, tbl: (l, 0, j))),
        compiler_params=_params(("parallel", "arbitrary")),
        name="w_in_relayout",
    )(table, w_t, small)


def kernel(x, w_in, ssd_conv_w, ssd_conv_b, ssd_dt_bias, ssd_a_log, ssd_d, ssd_norm_w, gdn_conv_w, gdn_dt_bias, gdn_a_log, gdn_norm_w, hg_lb_logits, hg_norm_w, w_out, ln1_g, ln1_b, w_router_group, b_router_group, w_router_expert, b_router_expert, w_expert_gate_up, w_expert_down, ln2_g, ln2_b):
    nb, seq, d = x.shape
    t = nb * seq
    lb_cum = jnp.cumsum(jax.nn.softmax(hg_lb_logits.astype(F32), axis=0), axis=0)
    lb_all = lb_cum - lb_cum[0:1]
    h = x.reshape(t, d)
    hb = h.astype(BF16)
    w_proj = _split_w_in(w_in)
    w_out_b = w_out.astype(BF16)
    w_gu_b = w_expert_gate_up.astype(BF16)
    w_dn_b = w_expert_down.astype(BF16).reshape(DEPTH, MOE_GROUPS, EXPERTS_PER_GROUP * D_EXPERT, D_MODEL)
    for l in range(DEPTH):
        y = _mixers(hb, w_proj, l, nb, seq, ssd_conv_w[l], ssd_conv_b[l], ssd_dt_bias[l], ssd_a_log[l], ssd_d[l],
                    ssd_norm_w[l], gdn_conv_w[l], gdn_dt_bias[l], gdn_a_log[l], gdn_norm_w[l],
                    lb_all[l], hg_norm_w[l])
        w_router, b_router = _router_params(w_router_group[l], b_router_group[l],
                                            w_router_expert[l], b_router_expert[l])
        h1, h1b, comb = _outproj(y, h, w_out_b, l, ln1_g[l], ln1_b[l], w_router, b_router, 512)
        h, hb = _moe(h1b, h1, comb, w_gu_b, w_dn_b, l, ln2_g[l], ln2_b[l], 512)
    return h.reshape(nb, seq, d)
```

```python
import jax
import jax.numpy as jnp
from jax import lax
from jax.experimental import pallas as pl
from jax.experimental.pallas import tpu as pltpu

F32 = jnp.float32
BF16 = jnp.bfloat16

D_MODEL = 1024
DEPTH = 4
SSD_HEADS = 16
SSD_HEAD_DIM = 64
SSD_WIDTH = 1024
SSD_GROUPS = 2
SSD_STATE = 128
SSD_BC = 256
SSD_XBC = 1536
SSD_CONV = 4
GDN_HEADS = 4
GDN_HEAD_DIM = 128
GDN_WIDTH = 512
GDN_CONV = 4
HG_HEADS = 4
HG_HEAD_DIM = 128
HG_WIDTH = 512
D_MIX = 2048
MOE_GROUPS = 4
EXPERTS_PER_GROUP = 4
N_EXPERTS = 16
D_EXPERT = 256
DN_ALPHA = (2 * DEPTH) ** 0.25
LOG2E = 1.4426950408889634

LANES = 128
SUBLANES = 8
ROWS = 128
VMEM_LIMIT = 48 * 1024 * 1024
MOE_VMEM_LIMIT = 56 * 1024 * 1024
CONV_SLAB = 256
GDN_HI_LEVELS = 2
MOE_CHUNK = 128
MOE_ALIGN = 16
PROJ_SLAB = 768
HG_SLOW_ROUNDS = 12

OFF_XBC = 0
OFF_QKV = 1536
OFF_Z = 3072
OFF_GGATE = 4096
OFF_HQ = 4608
OFF_HF = 5120
OFF_HI = 5632
OFF_HGATE = 6144
OFF_SMALL = 6656
PROJ_COLS = 6912
SM_DT = 0
SM_B = 16
SM_A = 20
MIX_SSD = 0
MIX_GDN = SSD_WIDTH
MIX_HG = SSD_WIDTH + GDN_WIDTH


def _dot(a, b):
    return jnp.dot(a.astype(BF16), b.astype(BF16), preferred_element_type=F32)


def _dot_nt(a, b):
    return lax.dot_general(a.astype(BF16), b.astype(BF16), (((1,), (1,)), ((), ())),
                           preferred_element_type=F32)


def _dot_tn(a, b):
    return lax.dot_general(a.astype(BF16), b.astype(BF16), (((0,), (0,)), ((), ())),
                           preferred_element_type=F32)


def _split3(a):
    p1 = a.astype(BF16)
    r1 = a - p1.astype(F32)
    p2 = r1.astype(BF16)
    return p1, p2, (r1 - p2.astype(F32)).astype(BF16)


def _dot_01(m01, x):
    m = m01.astype(BF16)
    p1, p2, p3 = _split3(x)
    return (jnp.dot(m, p1, preferred_element_type=F32) + jnp.dot(m, p2, preferred_element_type=F32)
            + jnp.dot(m, p3, preferred_element_type=F32))


def _dot_x01(x, m01):
    m = m01.astype(BF16)
    p1, p2, p3 = _split3(x)
    return (jnp.dot(p1, m, preferred_element_type=F32) + jnp.dot(p2, m, preferred_element_type=F32)
            + jnp.dot(p3, m, preferred_element_type=F32))


def _split(a):
    hi = a.astype(BF16)
    return hi, (a - hi.astype(F32)).astype(BF16)


def _dot3(a, b):
    (ah, al), (bh, bl) = a, b
    return (jnp.dot(ah, bh, preferred_element_type=F32) + jnp.dot(ah, bl, preferred_element_type=F32)
            + jnp.dot(al, bh, preferred_element_type=F32))


def _sigmoid(x):
    return 1.0 / (1.0 + jnp.exp(-x))


def _silu(x):
    return x * _sigmoid(x)


def _softplus(x):
    return jnp.maximum(x, 0.0) + jnp.log(1.0 + jnp.exp(-jnp.abs(x)))


def _iota2(shape, dim):
    return lax.broadcasted_iota(jnp.int32, shape, dim)


def _params(sem):
    return pltpu.CompilerParams(dimension_semantics=sem, vmem_limit_bytes=VMEM_LIMIT)


def _matmul_kernel(x_ref, w_ref, o_ref):
    o_ref[...] = jnp.dot(x_ref[...], w_ref[...], preferred_element_type=F32)


def _matmul(x, w, tm, tn):
    t, k = x.shape
    n = w.shape[1]
    return pl.pallas_call(
        _matmul_kernel,
        out_shape=jax.ShapeDtypeStruct((t, n), F32),
        grid=(n // tn, t // tm),
        in_specs=[pl.BlockSpec((tm, k), lambda j, i: (i, 0)),
                  pl.BlockSpec((k, tn), lambda j, i: (0, j))],
        out_specs=pl.BlockSpec((tm, tn), lambda j, i: (i, j)),
        compiler_params=_params(("parallel", "arbitrary")),
        name="in_proj",
    )(x, w)


def _conv_silu_stages(x_ref, xpad_ref, xa_ref, cw_ref, cb_ref):
    rows, cols = x_ref.shape
    k = cw_ref.shape[0]
    for c0 in range(0, cols, CONV_SLAB):
        cs = slice(c0, c0 + CONV_SLAB)
        x = x_ref[:, cs]
        xpad_ref[SUBLANES:SUBLANES + rows, cs] = x
        acc = x * cw_ref[k - 1:k, cs]
        if cb_ref is not None:
            acc = acc + cb_ref[:, cs]
        for i in range(k - 1):
            off = SUBLANES - (k - 1) + i
            acc = acc + xpad_ref[off:off + rows, cs] * cw_ref[i:i + 1, cs]
        xa_ref[:, cs] = _silu(acc)
        xpad_ref[0:SUBLANES, cs] = xpad_ref[rows:rows + SUBLANES, cs]
        yield


def _ssd_stages(z_ref, xbc_ref, sm_ref, cw_ref, cb_ref, hp_ref, hpt_ref, dsk_ref, nw_ref,
                y_ref, xpad_ref, xa_ref, st_ref):
    yield from _conv_silu_stages(xbc_ref, xpad_ref, xa_ref, cw_ref, cb_ref)

    row = _iota2((ROWS, ROWS), 0)
    col = _iota2((ROWS, ROWS), 1)
    causal = row >= col
    tril = causal.astype(F32)
    triu = (row <= col).astype(F32)

    sm = sm_ref[...]
    dt = _softplus(sm + hp_ref[0:1, :])
    da = dt * (-jnp.exp(hp_ref[1:2, :]))
    acum = _dot_01(tril, da)
    smt = sm.T
    dtt = _softplus(smt[0:SSD_HEADS, :] + hpt_ref[0:SSD_HEADS, 0:1])
    dat = dtt * (-jnp.exp(hpt_ref[0:SSD_HEADS, 1:2]))
    acumt = _dot_x01(dat, triu)

    heads3 = jnp.concatenate([dt, jnp.exp(acum), jnp.exp(acum[ROWS - 1:ROWS, :] - acum)], axis=0)
    h_hi, h_lo = _split(heads3)
    heads6 = jnp.concatenate([h_hi, h_lo], axis=0)
    lane_lo = _iota2((ROWS, LANES), 1) < SSD_HEAD_DIM
    yield

    hg = SSD_HEADS // SSD_GROUPS
    gw = SSD_WIDTH // SSD_GROUPS
    hrow = _iota2((LANES, gw), 0)
    hcol = _iota2((LANES, gw), 1)
    for g in range(SSD_GROUPS):
        gs = slice(g * gw, (g + 1) * gw)
        expand = (((hcol >> 6) + g * hg) == hrow).astype(BF16)
        ex = jnp.dot(heads6, expand, preferred_element_type=F32)
        ex = ex[0:3 * ROWS, :] + ex[3 * ROWS:6 * ROWS, :]
        dtx = ex[0:ROWS, :]
        eax = ex[ROWS:2 * ROWS, :]
        tex = ex[2 * ROWS:3 * ROWS, :]
        yield
        xs = xa_ref[:, gs]
        xdt = xs * dtx
        xdt_b = xdt.astype(BF16)
        xend_b = (xdt * tex).astype(BF16)
        bm = xa_ref[:, SSD_WIDTH + g * SSD_STATE:SSD_WIDTH + (g + 1) * SSD_STATE]
        cm = xa_ref[:, SSD_WIDTH + SSD_BC + g * SSD_STATE:SSD_WIDTH + SSD_BC + (g + 1) * SSD_STATE]
        cm_b = cm.astype(BF16)
        cb = _dot_nt(cm_b, bm)
        yield
        yd = []
        for pair in range(hg // 2):
            res = []
            for sub in range(2):
                h = g * hg + pair * 2 + sub
                seg = acum[:, h:h + 1] - acumt[h:h + 1, :]
                lmat = cb * jnp.exp(jnp.where(causal, seg, -jnp.inf))
                c0 = pair * 2 * SSD_HEAD_DIM
                res.append(jnp.dot(lmat.astype(BF16), xdt_b[:, c0:c0 + LANES],
                                   preferred_element_type=F32))
            yd.append(jnp.where(lane_lo, res[0], res[1]))
            yield
        yd = jnp.concatenate(yd, axis=1)
        st = st_ref[g]
        yoff = jnp.dot(cm_b, st.astype(BF16), preferred_element_type=F32) * eax
        st_ref[g] = (st * eax[ROWS - 1:ROWS, :]
                     + jnp.dot(bm.T.astype(BF16), xend_b, preferred_element_type=F32))
        yield
        y = yd + yoff + xs * dsk_ref[:, gs]
        y = y * _silu(z_ref[:, gs])
        ms = jnp.mean(y * y, axis=-1, keepdims=True)
        y_ref[:, MIX_SSD + g * gw:MIX_SSD + (g + 1) * gw] = (
            y * lax.rsqrt(ms + 1e-6) * nw_ref[:, gs]).astype(y_ref.dtype)
        yield


def _gdn_stages(qkv_ref, gate_ref, sm_ref, cw_ref, hp_ref, hpt_ref, nw_ref, y_ref, xpad_ref, xa_ref, st_ref):
    yield from _conv_silu_stages(qkv_ref, xpad_ref, xa_ref, cw_ref, None)

    row = _iota2((ROWS, ROWS), 0)
    col = _iota2((ROWS, ROWS), 1)
    incl = row >= col
    strict = row > col
    tril = incl.astype(F32)
    triu = (row <= col).astype(F32)

    sm = sm_ref[...]
    la = -jnp.exp(hp_ref[1:2, :]) * _softplus(sm + hp_ref[0:1, :])
    gcum = _dot_01(tril, la)
    smt = sm.T
    lat = (-jnp.exp(hpt_ref[SM_B:SM_B + SUBLANES, 1:2])
           * _softplus(smt[SM_B:SM_B + SUBLANES, :] + hpt_ref[SM_B:SM_B + SUBLANES, 0:1]))
    gcumt = _dot_x01(lat, triu)
    beta_all = _sigmoid(sm)
    yield

    dh = GDN_HEAD_DIM
    heads = range(GDN_HEADS)
    qs, ks, xs, rhss, decays, gs = [], [], [], [], [], []
    for h in heads:
        q = xa_ref[:, h * dh:(h + 1) * dh]
        k = xa_ref[:, GDN_WIDTH + h * dh:GDN_WIDTH + (h + 1) * dh]
        v = xa_ref[:, 2 * GDN_WIDTH + h * dh:2 * GDN_WIDTH + (h + 1) * dh]
        q = q * lax.rsqrt(jnp.sum(q * q, axis=-1, keepdims=True) + 1e-6) * (dh ** -0.5)
        k = k * lax.rsqrt(jnp.sum(k * k, axis=-1, keepdims=True) + 1e-6)
        g = gcum[:, SM_A + h:SM_A + h + 1]
        gt = gcumt[SM_A - SM_B + h:SM_A - SM_B + h + 1, :]
        beta = beta_all[:, SM_B + h:SM_B + h + 1]
        decay = jnp.exp(jnp.where(incl, g - gt, -jnp.inf))
        kb = k * beta
        xs.append(-jnp.where(strict, _dot_nt(kb, k) * decay, 0.0))
        rhss.append(jnp.concatenate([v * beta, kb * jnp.exp(g)], axis=1))
        qs.append(q)
        ks.append(k)
        decays.append(decay)
        gs.append(g)
        yield
    ps = xs
    ns = xs
    levels = ROWS.bit_length() - 2
    for j in range(levels):
        if j < GDN_HI_LEVELS:
            pp = [_split(p) for p in ps]
            ps = [_dot3(p, p) for p in pp]
            yield
            pp = [_split(p) for p in ps]
            ns = [n + p + _dot3(_split(n), p2) for n, p, p2 in zip(ns, ps, pp)]
        else:
            ps = [_dot(p, p) for p in ps]
            yield
            ns = [n + p + _dot(n, p) for n, p in zip(ns, ps)]
        yield
    sols = [r + _dot(n, r) for n, r in zip(ns, rhss)]
    yield
    qks = [_dot_nt(q, k) * d for q, k, d in zip(qs, ks, decays)]
    sts = [st_ref[h] for h in heads]
    v_news = [s[:, 0:dh] - _dot(s[:, dh:2 * dh], st) for s, st in zip(sols, sts)]
    yield
    os_ = [_dot(q * jnp.exp(g), st) + _dot(qk, vn) for q, g, st, qk, vn in zip(qs, gs, sts, qks, v_news)]
    yield
    for h in heads:
        glast = gs[h][ROWS - 1:ROWS, :]
        kd = ks[h] * jnp.exp(glast - gs[h])
        st_ref[h] = sts[h] * jnp.exp(glast) + _dot(kd.T, v_news[h])
    yield
    for h in heads:
        o = os_[h]
        o = o * lax.rsqrt(jnp.mean(o * o, axis=-1, keepdims=True) + 1e-6)
        y_ref[:, MIX_GDN + h * dh:MIX_GDN + (h + 1) * dh] = (
            o * nw_ref[...] * _silu(gate_ref[:, h * dh:(h + 1) * dh])).astype(y_ref.dtype)
        yield


def _hgrn_stages(q_ref, f_ref, i_ref, gate_ref, lb_ref, nw_ref, y_ref, st_ref):
    row = _iota2((ROWS, ROWS), 0)
    col = _iota2((ROWS, ROWS), 1)
    tril = (row >= col).astype(F32)
    rmod = _iota2((ROWS, LANES), 0) & (SUBLANES - 1)
    rfull = _iota2((ROWS, LANES), 0)
    dk = HG_HEAD_DIM
    level_masks = {}
    m = 1
    while m < ROWS:
        right = (rfull & m) != 0
        parent = ~(2 * m - 1)
        keep = (((row & m) ^ m) | (col & m) | ((row ^ col) & parent)) == 0
        level_masks[m] = (right, jnp.where(right, LOG2E, -LOG2E), keep)
        m *= 2
    for h in range(HG_HEADS):
        sl = slice(h * dk, (h + 1) * dk)
        lb = lb_ref[:, sl]
        fr = f_ref[:, sl]
        log_sig = jnp.minimum(fr, 0.0) - jnp.log(1.0 + jnp.exp(-jnp.abs(fr)))
        a = jnp.log(lb)
        y = jnp.log1p(-lb) + log_sig
        log_f = jnp.maximum(a, y) + jnp.log(1.0 + jnp.exp(-jnp.abs(a - y)))
        k = (1.0 - lb) * _sigmoid(-fr)
        q = _silu(q_ref[:, sl])
        v = i_ref[:, sl]
        b = _dot_01(tril, log_f)
        yield

        st = st_ref[h]
        o = _dot_nt(q * jnp.exp(b), st)
        blast = b[ROWS - 1:ROWS, :]
        st_ref[h] = st * jnp.exp(blast) + _dot_tn(v, k * jnp.exp(blast - b))
        yield

        pmat = jnp.zeros((ROWS, ROWS), F32)
        b3 = b.reshape(ROWS // SUBLANES, SUBLANES, dk)
        m = 1
        while m < ROWS:
            if m < SUBLANES:
                ref = None
                for p0 in range(0, SUBLANES, 2 * m):
                    cand = jnp.broadcast_to(b3[:, p0 + m - 1:p0 + m, :], b3.shape).reshape(ROWS, dk)
                    ref = cand if ref is None else jnp.where(rmod >= p0, cand, ref)
            else:
                ref = jnp.concatenate(
                    [jnp.broadcast_to(b[p0 + m - 1:p0 + m, :], (2 * m, dk)) for p0 in range(0, ROWS, 2 * m)],
                    axis=0)
            right, scale, keep = level_masks[m]
            z = jnp.where(right, q, k) * jnp.exp2((b - ref) * scale)
            zb = z.astype(BF16)
            pmat = pmat + jnp.where(keep, _dot_nt(zb, zb), 0.0)
            m *= 2
            yield
        o = o + _dot(pmat, v) + jnp.sum(q * k, axis=-1, keepdims=True) * v
        yield

        o = o * lax.rsqrt(jnp.mean(o * o, axis=-1, keepdims=True) + 1e-6)
        y_ref[:, MIX_HG + h * dk:MIX_HG + (h + 1) * dk] = (
            o * nw_ref[...] * _silu(gate_ref[:, sl])).astype(y_ref.dtype)
        yield


def _in_proj_stages(hb_ref, w_ref, out_ref):
    hb = hb_ref[...]
    for c0 in range(0, PROJ_COLS, PROJ_SLAB):
        out_ref[:, c0:c0 + PROJ_SLAB] = jnp.dot(hb, w_ref[:, c0:c0 + PROJ_SLAB], preferred_element_type=F32)
        yield


def _mixer_kernel(hb0_ref, hbn_ref, w_ref,
                  scw_ref, scb_ref, shp_ref, shpt_ref, dsk_ref, snw_ref,
                  gcw_ref, ghp_ref, ghpt_ref, gnw_ref, lb_ref, hnw_ref,
                  y_ref, proj_ref, sxpad_ref, sxa_ref, sst_ref, gxpad_ref, gxa_ref, gst_ref, hst_ref):
    c = pl.program_id(1)

    @pl.when(c == 0)
    def _():
        sst_ref[...] = jnp.zeros(sst_ref.shape, F32)
        gst_ref[...] = jnp.zeros(gst_ref.shape, F32)
        hst_ref[...] = jnp.zeros(hst_ref.shape, F32)
        sxpad_ref[0:SUBLANES, :] = jnp.zeros((SUBLANES, sxpad_ref.shape[1]), F32)
        gxpad_ref[0:SUBLANES, :] = jnp.zeros((SUBLANES, gxpad_ref.shape[1]), F32)
        for _ in _in_proj_stages(hb0_ref, w_ref, proj_ref.at[0]):
            pass

    cur = proj_ref.at[c % 2]
    nxt = proj_ref.at[(c + 1) % 2]

    def seg(off, width):
        return cur.at[:, off:off + width]

    sm_ref = seg(OFF_SMALL, LANES)
    streams = [
        _gdn_stages(seg(OFF_QKV, 3 * GDN_WIDTH), seg(OFF_GGATE, GDN_WIDTH), sm_ref, gcw_ref, ghp_ref, ghpt_ref,
                    gnw_ref, y_ref, gxpad_ref, gxa_ref, gst_ref),
        _ssd_stages(seg(OFF_Z, SSD_WIDTH), seg(OFF_XBC, SSD_XBC), sm_ref, scw_ref, scb_ref, shp_ref, shpt_ref,
                    dsk_ref, snw_ref, y_ref, sxpad_ref, sxa_ref, sst_ref),
        _hgrn_stages(seg(OFF_HQ, HG_WIDTH), seg(OFF_HF, HG_WIDTH), seg(OFF_HI, HG_WIDTH),
                     seg(OFF_HGATE, HG_WIDTH), lb_ref, hnw_ref, y_ref, hst_ref),
        _in_proj_stages(hbn_ref, w_ref, nxt),
    ]
    live = [True] * len(streams)
    rnd = 0
    while any(live):
        advance = [1, 1, 1 if rnd < HG_SLOW_ROUNDS else 2, 1]
        rnd += 1
        for s, stream in enumerate(streams):
            for _ in range(advance[s]):
                if live[s]:
                    try:
                        next(stream)
                    except StopIteration:
                        live[s] = False


def _mixers(hb, w_proj, layer, nb, seq, ssd_conv_w, ssd_conv_b, ssd_dt_bias, ssd_a_log, ssd_d, ssd_norm_w,
            gdn_conv_w, gdn_dt_bias, gdn_a_log, gdn_norm_w, lb, hg_norm_w):
    nblk = seq // ROWS
    shp = jnp.zeros((SUBLANES, LANES), F32)
    shp = shp.at[0, SM_DT:SM_DT + SSD_HEADS].set(ssd_dt_bias).at[1, SM_DT:SM_DT + SSD_HEADS].set(ssd_a_log)
    ghp = jnp.zeros((SUBLANES, LANES), F32)
    ghp = ghp.at[0, SM_A:SM_A + GDN_HEADS].set(gdn_dt_bias).at[1, SM_A:SM_A + GDN_HEADS].set(gdn_a_log)
    dsk = jnp.repeat(ssd_d, SSD_HEAD_DIM)[None, :]
    const = lambda b, c: (0, 0)

    def whole(shape):
        return pl.BlockSpec(shape, const)

    return pl.pallas_call(
        _mixer_kernel,
        out_shape=jax.ShapeDtypeStruct((nb * seq, D_MIX), BF16),
        grid=(nb, nblk),
        in_specs=[
            pl.BlockSpec((ROWS, D_MODEL), lambda b, c: (b * nblk + c, 0)),
            pl.BlockSpec((ROWS, D_MODEL), lambda b, c: (b * nblk + jnp.minimum(c + 1, nblk - 1), 0)),
            pl.BlockSpec((None, D_MODEL, PROJ_COLS), lambda b, c: (layer, 0, 0), pipeline_mode=pl.Buffered(1)),
            whole((SSD_CONV, SSD_XBC)), whole((1, SSD_XBC)), whole((SUBLANES, LANES)), whole((LANES, SUBLANES)),
            whole((1, SSD_WIDTH)), whole((1, SSD_WIDTH)),
            whole((GDN_CONV, 3 * GDN_WIDTH)), whole((SUBLANES, LANES)), whole((LANES, SUBLANES)),
            whole((1, GDN_HEAD_DIM)), whole((1, HG_WIDTH)), whole((1, HG_HEAD_DIM)),
        ],
        out_specs=pl.BlockSpec((ROWS, D_MIX), lambda b, c: (b * nblk + c, 0)),
        scratch_shapes=[pltpu.VMEM((2, ROWS, PROJ_COLS), F32),
                        pltpu.VMEM((ROWS + SUBLANES, SSD_XBC), F32),
                        pltpu.VMEM((ROWS, SSD_XBC), F32),
                        pltpu.VMEM((SSD_GROUPS, SSD_STATE, SSD_WIDTH // SSD_GROUPS), F32),
                        pltpu.VMEM((ROWS + SUBLANES, 3 * GDN_WIDTH), F32),
                        pltpu.VMEM((ROWS, 3 * GDN_WIDTH), F32),
                        pltpu.VMEM((GDN_HEADS, GDN_HEAD_DIM, GDN_HEAD_DIM), F32),
                        pltpu.VMEM((HG_HEADS, HG_HEAD_DIM, HG_HEAD_DIM), F32)],
        compiler_params=_params(("parallel", "arbitrary")),
        name="in_proj_mixers",
    )(hb, hb, w_proj,
      ssd_conv_w, ssd_conv_b[None, :], shp, shp.T, dsk, ssd_norm_w[None, :],
      gdn_conv_w, ghp, ghp.T, gdn_norm_w[None, :], lb[None, :], hg_norm_w[None, :])


def _layer_norm(x, g, b):
    mu = jnp.mean(x, axis=-1, keepdims=True)
    xc = x - mu
    var = jnp.mean(xc * xc, axis=-1, keepdims=True)
    return xc * lax.rsqrt(var + 1e-5) * g + b


ROUTER_ROWS = 32


def _outproj_kernel(y_ref, h_ref, w_ref, g_ref, b_ref, wr_ref, br_ref, h1_ref, h1b_ref, comb_ref):
    mix = jnp.dot(y_ref[...], w_ref[...], preferred_element_type=F32)
    h1 = _layer_norm(DN_ALPHA * h_ref[...] + mix, g_ref[...], b_ref[...])
    h1_ref[...] = h1
    h1b_ref[...] = h1.astype(BF16)

    h_hi, h_lo = _split(h1)
    wr = wr_ref[...]
    lt = _dot_nt(wr, h_hi) + _dot_nt(wr, h_lo)
    lt = lt[0:ROUTER_ROWS, :] + lt[ROUTER_ROWS:2 * ROUTER_ROWS, :] + br_ref[...]
    glog = [lt[g:g + 1, :] for g in range(MOE_GROUPS)]
    elog = [lt[MOE_GROUPS + e:MOE_GROUPS + e + 1, :] for e in range(N_EXPERTS)]

    def softmax(rows):
        m = rows[0]
        for r in rows[1:]:
            m = jnp.maximum(m, r)
        ex = [jnp.exp(r - m) for r in rows]
        tot = ex[0]
        for r in ex[1:]:
            tot = tot + r
        return [r / tot for r in ex]

    def top1(rows):
        best = rows[0]
        for r in rows[1:]:
            best = jnp.maximum(best, r)
        idx = jnp.full_like(best, float(len(rows) - 1))
        for j in range(len(rows) - 2, -1, -1):
            idx = jnp.where(rows[j] == best, float(j), idx)
        return best, idx

    g_p, g_idx = top1(softmax(glog))
    chosen = []
    for k in range(EXPERTS_PER_GROUP):
        acc = jnp.zeros_like(g_p)
        for g in range(MOE_GROUPS):
            acc = jnp.where(g_idx == float(g), elog[g * EXPERTS_PER_GROUP + k], acc)
        chosen.append(acc)
    eprob = softmax(chosen)
    p1, i1 = top1(eprob)
    p2, i2 = top1([jnp.where(i1 == float(k), -1.0, eprob[k]) for k in range(EXPERTS_PER_GROUP)])
    denom = p1 + p2
    w1 = g_p * p1 / denom
    w2 = g_p * p2 / denom
    wk = [jnp.where(i1 == float(k), w1, 0.0) + jnp.where(i2 == float(k), w2, 0.0)
          for k in range(EXPERTS_PER_GROUP)]
    rid = _iota2((ROUTER_ROWS, lt.shape[1]), 0)
    comb_t = jnp.where(rid == N_EXPERTS, g_idx, 0.0)
    for g in range(MOE_GROUPS):
        for k in range(EXPERTS_PER_GROUP):
            comb_t = jnp.where(rid == g * EXPERTS_PER_GROUP + k,
                               jnp.where(g_idx == float(g), wk[k], 0.0), comb_t)
    comb_t = jnp.concatenate([comb_t, jnp.zeros((LANES - ROUTER_ROWS, lt.shape[1]), F32)], axis=0)
    comb_ref[...] = comb_t.T


def _router_params(w_group, b_group, w_expert, b_expert):
    pad = ROUTER_ROWS - MOE_GROUPS - N_EXPERTS
    w_t = jnp.pad(jnp.concatenate([w_group, w_expert], axis=1).T, ((0, pad), (0, 0)))
    b_col = jnp.pad(jnp.concatenate([b_group, b_expert]), (0, pad))[:, None]
    return jnp.concatenate(_split(w_t), axis=0), b_col


def _outproj(y, h, w_out, layer, ln_g, ln_b, w_router, b_router, tm):
    t = h.shape[0]
    rowmap = lambda i: (i, 0)
    const = lambda i: (0, 0)
    return pl.pallas_call(
        _outproj_kernel,
        out_shape=(jax.ShapeDtypeStruct((t, D_MODEL), F32),
                   jax.ShapeDtypeStruct((t, D_MODEL), BF16),
                   jax.ShapeDtypeStruct((t, LANES), F32)),
        grid=(t // tm,),
        in_specs=[pl.BlockSpec((tm, D_MIX), rowmap),
                  pl.BlockSpec((tm, D_MODEL), rowmap),
                  pl.BlockSpec((None, D_MIX, D_MODEL), lambda i: (layer, 0, 0)),
                  pl.BlockSpec((1, D_MODEL), const),
                  pl.BlockSpec((1, D_MODEL), const),
                  pl.BlockSpec((2 * ROUTER_ROWS, D_MODEL), const),
                  pl.BlockSpec((ROUTER_ROWS, 1), const)],
        out_specs=(pl.BlockSpec((tm, D_MODEL), rowmap),
                   pl.BlockSpec((tm, D_MODEL), rowmap),
                   pl.BlockSpec((tm, LANES), rowmap)),
        compiler_params=_params(("parallel",)),
        name="out_proj_ln_router",
    )(y, h, w_out, ln_g[None, :], ln_b[None, :], w_router, b_router)


def _prefix_lanes(v, idx):
    axis = 1 if v.shape[0] == 1 else 0
    out = jnp.zeros_like(v)
    for g in range(MOE_GROUPS - 1):
        vg = v[:, g:g + 1] if axis == 1 else v[g:g + 1, :]
        out = out + jnp.where(idx > g, vg, 0.0)
    return out


def _moe_kernel(hb_ref, h_ref, comb_ref, wgu_ref, wdn_ref, g_ref, b_ref, o_ref, ob_ref,
                xs_ref, cw_ref, ys_ref):
    tm = hb_ref.shape[0]
    comb = comb_ref[...]
    lane = _iota2((tm, LANES), 1).astype(F32)
    gid = comb[:, N_EXPERTS:N_EXPERTS + 1]
    gsel = (lane == gid).astype(F32)
    tr = _iota2((tm, tm), 0)
    tc = _iota2((tm, tm), 1)

    ns = xs_ref.shape[0] - MOE_CHUNK

    def aligned(cnt):
        return jnp.floor((cnt + (MOE_ALIGN - 1)) * (1.0 / MOE_ALIGN)) * MOE_ALIGN

    rank_c = _dot((tc < tr).astype(F32), gsel)
    cnt_r = jnp.sum(gsel, axis=0, keepdims=True)
    start_r = _prefix_lanes(aligned(cnt_r), _iota2((1, LANES), 1))
    dest_c = jnp.sum(gsel * (start_r + rank_c), axis=1, keepdims=True)
    eye8 = (_iota2((SUBLANES, LANES), 0) == _iota2((SUBLANES, LANES), 1)).astype(F32)
    gsel_t = _dot_nt(eye8, gsel)
    rank_r = _dot(gsel_t, (tr < tc).astype(F32))
    cnt_c = jnp.sum(gsel_t, axis=1, keepdims=True)
    start_c = _prefix_lanes(aligned(cnt_c), _iota2((SUBLANES, 1), 0))
    dest_r = jnp.sum(gsel_t * (start_c + rank_r), axis=0, keepdims=True)

    perm = (dest_r == _iota2((ns, tm), 0).astype(F32)).astype(BF16)
    xs_ref[0:ns, :] = jnp.dot(perm, hb_ref[...], preferred_element_type=F32).astype(BF16)
    xs_ref[ns:ns + MOE_CHUNK, :] = jnp.zeros((MOE_CHUNK, D_MODEL), BF16)
    c_hi, c_lo = _split(comb)
    cw2 = jnp.dot(perm, jnp.concatenate([c_hi, c_lo], axis=1), preferred_element_type=F32)
    cw_ref[0:ns, :] = cw2[:, 0:LANES] + cw2[:, LANES:2 * LANES]
    cw_ref[ns:ns + MOE_CHUNK, :] = jnp.zeros((MOE_CHUNK, LANES), F32)
    ys_ref[...] = jnp.zeros(ys_ref.shape, F32)

    lane1 = _iota2((1, LANES), 1)
    for g in range(MOE_GROUPS):
        start = jnp.sum(jnp.where(lane1 == g, start_r, 0.0)).astype(jnp.int32)
        cnt = jnp.sum(jnp.where(lane1 == g, cnt_r, 0.0))
        for j in range(tm // MOE_CHUNK):
            @pl.when(cnt > j * MOE_CHUNK)
            def _(g=g, j=j, start=start):
                rows = pl.ds(pl.multiple_of(start + j * MOE_CHUNK, MOE_ALIGN), MOE_CHUNK)
                x = xs_ref[rows, :]
                cw = cw_ref[rows, :]
                hm = []
                for e in range(EXPERTS_PER_GROUP):
                    ex = g * EXPERTS_PER_GROUP + e
                    gu = jnp.dot(x, wgu_ref[ex], preferred_element_type=F32)
                    hm.append((_silu(gu[:, 0:D_EXPERT]) * gu[:, D_EXPERT:2 * D_EXPERT]
                               * cw[:, ex:ex + 1]).astype(BF16))
                ys_ref[rows, :] += jnp.dot(jnp.concatenate(hm, axis=1), wdn_ref[g],
                                           preferred_element_type=F32)

    unperm = (dest_c == _iota2((tm, ns), 1).astype(F32)).astype(BF16)
    y = jnp.dot(unperm, ys_ref[0:ns, :].astype(BF16), preferred_element_type=F32)
    h2 = _layer_norm(DN_ALPHA * h_ref[...] + y, g_ref[...], b_ref[...])
    o_ref[...] = h2
    ob_ref[...] = h2.astype(BF16)


def _moe(hb, h, comb, w_gu, w_dn, layer, ln_g, ln_b, tm):
    t = h.shape[0]
    rowmap = lambda i: (i, 0)
    const = lambda i: (0, 0)
    const3 = lambda i: (layer, 0, 0, 0)
    sorted_rows = -(-(tm + MOE_GROUPS * MOE_ALIGN) // LANES) * LANES + MOE_CHUNK
    return pl.pallas_call(
        _moe_kernel,
        out_shape=(jax.ShapeDtypeStruct((t, D_MODEL), F32),
                   jax.ShapeDtypeStruct((t, D_MODEL), BF16)),
        grid=(t // tm,),
        in_specs=[pl.BlockSpec((tm, D_MODEL), rowmap),
                  pl.BlockSpec((tm, D_MODEL), rowmap),
                  pl.BlockSpec((tm, LANES), rowmap),
                  pl.BlockSpec((None, N_EXPERTS, D_MODEL, 2 * D_EXPERT), const3, pipeline_mode=pl.Buffered(1)),
                  pl.BlockSpec((None, MOE_GROUPS, EXPERTS_PER_GROUP * D_EXPERT, D_MODEL), const3,
                               pipeline_mode=pl.Buffered(1)),
                  pl.BlockSpec((1, D_MODEL), const),
                  pl.BlockSpec((1, D_MODEL), const)],
        out_specs=(pl.BlockSpec((tm, D_MODEL), rowmap),
                   pl.BlockSpec((tm, D_MODEL), rowmap)),
        scratch_shapes=[pltpu.VMEM((sorted_rows, D_MODEL), BF16),
                        pltpu.VMEM((sorted_rows, LANES), F32),
                        pltpu.VMEM((sorted_rows, D_MODEL), F32)],
        compiler_params=pltpu.CompilerParams(dimension_semantics=("parallel",),
                                             vmem_limit_bytes=MOE_VMEM_LIMIT),
        name="moe_ln",
    )(hb, h, comb, w_gu, w_dn, ln_g[None, :], ln_b[None, :])


W_IN_SEGMENTS = ((SSD_WIDTH, OFF_Z), (SSD_XBC, OFF_XBC), (SSD_HEADS, OFF_SMALL + SM_DT),
                 (3 * GDN_WIDTH, OFF_QKV), (GDN_WIDTH, OFF_GGATE), (GDN_HEADS, OFF_SMALL + SM_B),
                 (GDN_HEADS, OFF_SMALL + SM_A), (HG_WIDTH, OFF_HQ), (HG_WIDTH, OFF_HF),
                 (HG_WIDTH, OFF_HI), (HG_WIDTH, OFF_HGATE))
W_T_BLOCK = 256
W_MAIN_BLOCKS = OFF_SMALL // W_T_BLOCK


def _w_in_kernel(tbl_ref, w_ref, small_ref, o_ref):
    j = pl.program_id(1)

    @pl.when(j < W_MAIN_BLOCKS)
    def _():
        o_ref[...] = w_ref[0].T.astype(BF16)

    @pl.when(j >= W_MAIN_BLOCKS)
    def _():
        o_ref[...] = small_ref[...].T.astype(BF16)


def _split_w_in(w_in):
    depth = w_in.shape[0]
    w_t = jnp.swapaxes(w_in, 1, 2)
    src_of, small_rows, src = {}, [], 0
    for width, dst in W_IN_SEGMENTS:
        if width >= W_T_BLOCK:
            for off in range(0, width, W_T_BLOCK):
                src_of[(dst + off) // W_T_BLOCK] = src + off
        else:
            small_rows.append(w_t[:, src:src + width, :])
        src += width
    table = jnp.array([src_of.get(j, 0) for j in range(PROJ_COLS // W_T_BLOCK)], jnp.int32)
    small = jnp.concatenate(small_rows, axis=1)
    small = jnp.pad(small, ((0, 0), (0, PROJ_COLS - OFF_SMALL - small.shape[1]), (0, 0)))
    return pl.pallas_call(
        _w_in_kernel,
        out_shape=jax.ShapeDtypeStruct((depth, D_MODEL, PROJ_COLS), BF16),
        grid_spec=pltpu.PrefetchScalarGridSpec(
            num_scalar_prefetch=1,
            grid=(depth, PROJ_COLS // W_T_BLOCK),
            in_specs=[pl.BlockSpec((pl.Element(1), pl.Element(W_T_BLOCK), pl.Element(D_MODEL)),
                                   lambda l, j, tbl: (l, pl.multiple_of(tbl[j], SUBLANES), 0)),
                      pl.BlockSpec((None, PROJ_COLS - OFF_SMALL, D_MODEL), lambda l, j, tbl: (l, 0, 0))],
            out_specs=pl.BlockSpec((None, D_MODEL, W_T_BLOCK), lambda l, j, tbl: (l, 0, j))),
        compiler_params=_params(("parallel", "arbitrary")),
        name="w_in_relayout",
    )(table, w_t, small)


def kernel(x, w_in, ssd_conv_w, ssd_conv_b, ssd_dt_bias, ssd_a_log, ssd_d, ssd_norm_w, gdn_conv_w, gdn_dt_bias, gdn_a_log, gdn_norm_w, hg_lb_logits, hg_norm_w, w_out, ln1_g, ln1_b, w_router_group, b_router_group, w_router_expert, b_router_expert, w_expert_gate_up, w_expert_down, ln2_g, ln2_b):
    nb, seq, d = x.shape
    t = nb * seq
    lb_cum = jnp.cumsum(jax.nn.softmax(hg_lb_logits.astype(F32), axis=0), axis=0)
    lb_all = lb_cum - lb_cum[0:1]
    h = x.reshape(t, d)
    hb = h.astype(BF16)
    w_proj = _split_w_in(w_in)
    w_out_b = w_out.astype(BF16)
    w_gu_b = w_expert_gate_up.astype(BF16)
    w_dn_b = w_expert_down.astype(BF16).reshape(DEPTH, MOE_GROUPS, EXPERTS_PER_GROUP * D_EXPERT, D_MODEL)
    for l in range(DEPTH):
        y = _mixers(hb, w_proj, l, nb, seq, ssd_conv_w[l], ssd_conv_b[l], ssd_dt_bias[l], ssd_a_log[l], ssd_d[l],
                    ssd_norm_w[l], gdn_conv_w[l], gdn_dt_bias[l], gdn_a_log[l], gdn_norm_w[l],
                    lb_all[l], hg_norm_w[l])
        w_router, b_router = _router_params(w_router_group[l], b_router_group[l],
                                            w_router_expert[l], b_router_expert[l])
        h1, h1b, comb = _outproj(y, h, w_out_b, l, ln1_g[l], ln1_b[l], w_router, b_router, 512)
        h, hb = _moe(h1b, h1, comb, w_gu_b, w_dn_b, l, ln2_g[l], ln2_b[l], 512)
    return h.reshape(nb, seq, d)
```

```python
import jax
import jax.numpy as jnp
from jax import lax
from jax.experimental import pallas as pl
from jax.experimental.pallas import tpu as pltpu

F32 = jnp.float32
BF16 = jnp.bfloat16

D_MODEL = 1024
DEPTH = 4
SSD_HEADS = 16
SSD_HEAD_DIM = 64
SSD_WIDTH = 1024
SSD_GROUPS = 2
SSD_STATE = 128
SSD_BC = 256
SSD_XBC = 1536
SSD_CONV = 4
GDN_HEADS = 4
GDN_HEAD_DIM = 128
GDN_WIDTH = 512
GDN_CONV = 4
HG_HEADS = 4
HG_HEAD_DIM = 128
HG_WIDTH = 512
D_MIX = 2048
MOE_GROUPS = 4
EXPERTS_PER_GROUP = 4
N_EXPERTS = 16
D_EXPERT = 256
DN_ALPHA = (2 * DEPTH) ** 0.25
LOG2E = 1.4426950408889634

LANES = 128
SUBLANES = 8
ROWS = 128
VMEM_LIMIT = 48 * 1024 * 1024
MOE_VMEM_LIMIT = 56 * 1024 * 1024
CONV_SLAB = 256
GDN_HI_LEVELS = 2
MOE_CHUNK = 128
MOE_ALIGN = 16
PROJ_SLAB = 768
HG_SLOW_ROUNDS = 12

OFF_XBC = 0
OFF_QKV = 1536
OFF_Z = 3072
OFF_GGATE = 4096
OFF_HQ = 4608
OFF_HF = 5120
OFF_HI = 5632
OFF_HGATE = 6144
OFF_SMALL = 6656
PROJ_COLS = 6912
SM_DT = 0
SM_B = 16
SM_A = 20
MIX_SSD = 0
MIX_GDN = SSD_WIDTH
MIX_HG = SSD_WIDTH + GDN_WIDTH


def _dot(a, b):
    return jnp.dot(a.astype(BF16), b.astype(BF16), preferred_element_type=F32)


def _dot_nt(a, b):
    return lax.dot_general(a.astype(BF16), b.astype(BF16), (((1,), (1,)), ((), ())),
                           preferred_element_type=F32)


def _dot_tn(a, b):
    return lax.dot_general(a.astype(BF16), b.astype(BF16), (((0,), (0,)), ((), ())),
                           preferred_element_type=F32)


def _split3(a):
    p1 = a.astype(BF16)
    r1 = a - p1.astype(F32)
    p2 = r1.astype(BF16)
    return p1, p2, (r1 - p2.astype(F32)).astype(BF16)


def _dot_01(m01, x):
    m = m01.astype(BF16)
    p1, p2, p3 = _split3(x)
    return (jnp.dot(m, p1, preferred_element_type=F32) + jnp.dot(m, p2, preferred_element_type=F32)
            + jnp.dot(m, p3, preferred_element_type=F32))


def _dot_x01(x, m01):
    m = m01.astype(BF16)
    p1, p2, p3 = _split3(x)
    return (jnp.dot(p1, m, preferred_element_type=F32) + jnp.dot(p2, m, preferred_element_type=F32)
            + jnp.dot(p3, m, preferred_element_type=F32))


def _split(a):
    hi = a.astype(BF16)
    return hi, (a - hi.astype(F32)).astype(BF16)


def _dot3(a, b):
    (ah, al), (bh, bl) = a, b
    return (jnp.dot(ah, bh, preferred_element_type=F32) + jnp.dot(ah, bl, preferred_element_type=F32)
            + jnp.dot(al, bh, preferred_element_type=F32))


def _sigmoid(x):
    return 1.0 / (1.0 + jnp.exp(-x))


def _silu(x):
    return x * _sigmoid(x)


def _softplus(x):
    return jnp.maximum(x, 0.0) + jnp.log(1.0 + jnp.exp(-jnp.abs(x)))


def _iota2(shape, dim):
    return lax.broadcasted_iota(jnp.int32, shape, dim)


def _params(sem):
    return pltpu.CompilerParams(dimension_semantics=sem, vmem_limit_bytes=VMEM_LIMIT)


def _matmul_kernel(x_ref, w_ref, o_ref):
    o_ref[...] = jnp.dot(x_ref[...], w_ref[...], preferred_element_type=F32)


def _matmul(x, w, tm, tn):
    t, k = x.shape
    n = w.shape[1]
    return pl.pallas_call(
        _matmul_kernel,
        out_shape=jax.ShapeDtypeStruct((t, n), F32),
        grid=(n // tn, t // tm),
        in_specs=[pl.BlockSpec((tm, k), lambda j, i: (i, 0)),
                  pl.BlockSpec((k, tn), lambda j, i: (0, j))],
        out_specs=pl.BlockSpec((tm, tn), lambda j, i: (i, j)),
        compiler_params=_params(("parallel", "arbitrary")),
        name="in_proj",
    )(x, w)


def _conv_silu_stages(x_ref, xpad_ref, xa_ref, cw_ref, cb_ref):
    rows, cols = x_ref.shape
    k = cw_ref.shape[0]
    for c0 in range(0, cols, CONV_SLAB):
        cs = slice(c0, c0 + CONV_SLAB)
        x = x_ref[:, cs]
        xpad_ref[SUBLANES:SUBLANES + rows, cs] = x
        acc = x * cw_ref[k - 1:k, cs]
        if cb_ref is not None:
            acc = acc + cb_ref[:, cs]
        for i in range(k - 1):
            off = SUBLANES - (k - 1) + i
            acc = acc + xpad_ref[off:off + rows, cs] * cw_ref[i:i + 1, cs]
        xa_ref[:, cs] = _silu(acc)
        xpad_ref[0:SUBLANES, cs] = xpad_ref[rows:rows + SUBLANES, cs]
        yield


def _ssd_stages(z_ref, xbc_ref, sm_ref, cw_ref, cb_ref, hp_ref, hpt_ref, dsk_ref, nw_ref,
                y_ref, xpad_ref, xa_ref, st_ref):
    yield from _conv_silu_stages(xbc_ref, xpad_ref, xa_ref, cw_ref, cb_ref)

    row = _iota2((ROWS, ROWS), 0)
    col = _iota2((ROWS, ROWS), 1)
    causal = row >= col
    tril = causal.astype(F32)
    triu = (row <= col).astype(F32)

    sm = sm_ref[...]
    dt = _softplus(sm + hp_ref[0:1, :])
    da = dt * (-jnp.exp(hp_ref[1:2, :]))
    acum = _dot_01(tril, da)
    smt = sm.T
    dtt = _softplus(smt[0:SSD_HEADS, :] + hpt_ref[0:SSD_HEADS, 0:1])
    dat = dtt * (-jnp.exp(hpt_ref[0:SSD_HEADS, 1:2]))
    acumt = _dot_x01(dat, triu)

    heads3 = jnp.concatenate([dt, jnp.exp(acum), jnp.exp(acum[ROWS - 1:ROWS, :] - acum)], axis=0)
    h_hi, h_lo = _split(heads3)
    heads6 = jnp.concatenate([h_hi, h_lo], axis=0)
    lane_lo = _iota2((ROWS, LANES), 1) < SSD_HEAD_DIM
    yield

    hg = SSD_HEADS // SSD_GROUPS
    gw = SSD_WIDTH // SSD_GROUPS
    hrow = _iota2((LANES, gw), 0)
    hcol = _iota2((LANES, gw), 1)
    for g in range(SSD_GROUPS):
        gs = slice(g * gw, (g + 1) * gw)
        expand = (((hcol >> 6) + g * hg) == hrow).astype(BF16)
        ex = jnp.dot(heads6, expand, preferred_element_type=F32)
        ex = ex[0:3 * ROWS, :] + ex[3 * ROWS:6 * ROWS, :]
        dtx = ex[0:ROWS, :]
        eax = ex[ROWS:2 * ROWS, :]
        tex = ex[2 * ROWS:3 * ROWS, :]
        yield
        xs = xa_ref[:, gs]
        xdt = xs * dtx
        xdt_b = xdt.astype(BF16)
        xend_b = (xdt * tex).astype(BF16)
        bm = xa_ref[:, SSD_WIDTH + g * SSD_STATE:SSD_WIDTH + (g + 1) * SSD_STATE]
        cm = xa_ref[:, SSD_WIDTH + SSD_BC + g * SSD_STATE:SSD_WIDTH + SSD_BC + (g + 1) * SSD_STATE]
        cm_b = cm.astype(BF16)
        cb = _dot_nt(cm_b, bm)
        yield
        yd = []
        for pair in range(hg // 2):
            res = []
            for sub in range(2):
                h = g * hg + pair * 2 + sub
                seg = acum[:, h:h + 1] - acumt[h:h + 1, :]
                lmat = cb * jnp.exp(jnp.where(causal, seg, -jnp.inf))
                c0 = pair * 2 * SSD_HEAD_DIM
                res.append(jnp.dot(lmat.astype(BF16), xdt_b[:, c0:c0 + LANES],
                                   preferred_element_type=F32))
            yd.append(jnp.where(lane_lo, res[0], res[1]))
            yield
        yd = jnp.concatenate(yd, axis=1)
        st = st_ref[g]
        yoff = jnp.dot(cm_b, st.astype(BF16), preferred_element_type=F32) * eax
        st_ref[g] = (st * eax[ROWS - 1:ROWS, :]
                     + jnp.dot(bm.T.astype(BF16), xend_b, preferred_element_type=F32))
        yield
        y = yd + yoff + xs * dsk_ref[:, gs]
        y = y * _silu(z_ref[:, gs])
        ms = jnp.mean(y * y, axis=-1, keepdims=True)
        y_ref[:, MIX_SSD + g * gw:MIX_SSD + (g + 1) * gw] = (
            y * lax.rsqrt(ms + 1e-6) * nw_ref[:, gs]).astype(y_ref.dtype)
        yield


def _gdn_stages(qkv_ref, gate_ref, sm_ref, cw_ref, hp_ref, hpt_ref, nw_ref, y_ref, xpad_ref, xa_ref, st_ref):
    yield from _conv_silu_stages(qkv_ref, xpad_ref, xa_ref, cw_ref, None)

    row = _iota2((ROWS, ROWS), 0)
    col = _iota2((ROWS, ROWS), 1)
    incl = row >= col
    strict = row > col
    tril = incl.astype(F32)
    triu = (row <= col).astype(F32)

    sm = sm_ref[...]
    la = -jnp.exp(hp_ref[1:2, :]) * _softplus(sm + hp_ref[0:1, :])
    gcum = _dot_01(tril, la)
    smt = sm.T
    lat = (-jnp.exp(hpt_ref[SM_B:SM_B + SUBLANES, 1:2])
           * _softplus(smt[SM_B:SM_B + SUBLANES, :] + hpt_ref[SM_B:SM_B + SUBLANES, 0:1]))
    gcumt = _dot_x01(lat, triu)
    beta_all = _sigmoid(sm)
    yield

    dh = GDN_HEAD_DIM
    heads = range(GDN_HEADS)
    qs, ks, xs, rhss, decays, gs = [], [], [], [], [], []
    for h in heads:
        q = xa_ref[:, h * dh:(h + 1) * dh]
        k = xa_ref[:, GDN_WIDTH + h * dh:GDN_WIDTH + (h + 1) * dh]
        v = xa_ref[:, 2 * GDN_WIDTH + h * dh:2 * GDN_WIDTH + (h + 1) * dh]
        q = q * lax.rsqrt(jnp.sum(q * q, axis=-1, keepdims=True) + 1e-6) * (dh ** -0.5)
        k = k * lax.rsqrt(jnp.sum(k * k, axis=-1, keepdims=True) + 1e-6)
        g = gcum[:, SM_A + h:SM_A + h + 1]
        gt = gcumt[SM_A - SM_B + h:SM_A - SM_B + h + 1, :]
        beta = beta_all[:, SM_B + h:SM_B + h + 1]
        decay = jnp.exp(jnp.where(incl, g - gt, -jnp.inf))
        kb = k * beta
        xs.append(-jnp.where(strict, _dot_nt(kb, k) * decay, 0.0))
        rhss.append(jnp.concatenate([v * beta, kb * jnp.exp(g)], axis=1))
        qs.append(q)
        ks.append(k)
        decays.append(decay)
        gs.append(g)
        yield
    ps = xs
    ns = xs
    levels = ROWS.bit_length() - 2
    for j in range(levels):
        if j < GDN_HI_LEVELS:
            pp = [_split(p) for p in ps]
            ps = [_dot3(p, p) for p in pp]
            yield
            pp = [_split(p) for p in ps]
            ns = [n + p + _dot3(_split(n), p2) for n, p, p2 in zip(ns, ps, pp)]
        else:
            ps = [_dot(p, p) for p in ps]
            yield
            ns = [n + p + _dot(n, p) for n, p in zip(ns, ps)]
        yield
    sols = [r + _dot(n, r) for n, r in zip(ns, rhss)]
    yield
    qks = [_dot_nt(q, k) * d for q, k, d in zip(qs, ks, decays)]
    sts = [st_ref[h] for h in heads]
    v_news = [s[:, 0:dh] - _dot(s[:, dh:2 * dh], st) for s, st in zip(sols, sts)]
    yield
    os_ = [_dot(q * jnp.exp(g), st) + _dot(qk, vn) for q, g, st, qk, vn in zip(qs, gs, sts, qks, v_news)]
    yield
    for h in heads:
        glast = gs[h][ROWS - 1:ROWS, :]
        kd = ks[h] * jnp.exp(glast - gs[h])
        st_ref[h] = sts[h] * jnp.exp(glast) + _dot(kd.T, v_news[h])
    yield
    for h in heads:
        o = os_[h]
        o = o * lax.rsqrt(jnp.mean(o * o, axis=-1, keepdims=True) + 1e-6)
        y_ref[:, MIX_GDN + h * dh:MIX_GDN + (h + 1) * dh] = (
            o * nw_ref[...] * _silu(gate_ref[:, h * dh:(h + 1) * dh])).astype(y_ref.dtype)
        yield


def _hgrn_stages(q_ref, f_ref, i_ref, gate_ref, lb_ref, nw_ref, y_ref, st_ref):
    row = _iota2((ROWS, ROWS), 0)
    col = _iota2((ROWS, ROWS), 1)
    tril = (row >= col).astype(F32)
    rmod = _iota2((ROWS, LANES), 0) & (SUBLANES - 1)
    rfull = _iota2((ROWS, LANES), 0)
    dk = HG_HEAD_DIM
    level_masks = {}
    m = 1
    while m < ROWS:
        right = (rfull & m) != 0
        parent = ~(2 * m - 1)
        keep = (((row & m) ^ m) | (col & m) | ((row ^ col) & parent)) == 0
        level_masks[m] = (right, jnp.where(right, LOG2E, -LOG2E), keep)
        m *= 2
    for h in range(HG_HEADS):
        sl = slice(h * dk, (h + 1) * dk)
        lb = lb_ref[:, sl]
        fr = f_ref[:, sl]
        log_sig = jnp.minimum(fr, 0.0) - jnp.log(1.0 + jnp.exp(-jnp.abs(fr)))
        a = jnp.log(lb)
        y = jnp.log1p(-lb) + log_sig
        log_f = jnp.maximum(a, y) + jnp.log(1.0 + jnp.exp(-jnp.abs(a - y)))
        k = (1.0 - lb) * _sigmoid(-fr)
        q = _silu(q_ref[:, sl])
        v = i_ref[:, sl]
        b = _dot_01(tril, log_f)
        yield

        st = st_ref[h]
        o = _dot_nt(q * jnp.exp(b), st)
        blast = b[ROWS - 1:ROWS, :]
        st_ref[h] = st * jnp.exp(blast) + _dot_tn(v, k * jnp.exp(blast - b))
        yield

        pmat = jnp.zeros((ROWS, ROWS), F32)
        b3 = b.reshape(ROWS // SUBLANES, SUBLANES, dk)
        m = 1
        while m < ROWS:
            if m < SUBLANES:
                ref = None
                for p0 in range(0, SUBLANES, 2 * m):
                    cand = jnp.broadcast_to(b3[:, p0 + m - 1:p0 + m, :], b3.shape).reshape(ROWS, dk)
                    ref = cand if ref is None else jnp.where(rmod >= p0, cand, ref)
            else:
                ref = jnp.concatenate(
                    [jnp.broadcast_to(b[p0 + m - 1:p0 + m, :], (2 * m, dk)) for p0 in range(0, ROWS, 2 * m)],
                    axis=0)
            right, scale, keep = level_masks[m]
            z = jnp.where(right, q, k) * jnp.exp2((b - ref) * scale)
            zb = z.astype(BF16)
            pmat = pmat + jnp.where(keep, _dot_nt(zb, zb), 0.0)
            m *= 2
            yield
        o = o + _dot(pmat, v) + jnp.sum(q * k, axis=-1, keepdims=True) * v
        yield

        o = o * lax.rsqrt(jnp.mean(o * o, axis=-1, keepdims=True) + 1e-6)
        y_ref[:, MIX_HG + h * dk:MIX_HG + (h + 1) * dk] = (
            o * nw_ref[...] * _silu(gate_ref[:, sl])).astype(y_ref.dtype)
        yield


def _in_proj_stages(hb_ref, w_ref, out_ref):
    hb = hb_ref[...]
    for c0 in range(0, PROJ_COLS, PROJ_SLAB):
        out_ref[:, c0:c0 + PROJ_SLAB] = jnp.dot(hb, w_ref[:, c0:c0 + PROJ_SLAB], preferred_element_type=F32)
        yield


def _mixer_kernel(hb0_ref, hbn_ref, w_ref,
                  scw_ref, scb_ref, shp_ref, shpt_ref, dsk_ref, snw_ref,
                  gcw_ref, ghp_ref, ghpt_ref, gnw_ref, lb_ref, hnw_ref,
                  y_ref, proj_ref, sxpad_ref, sxa_ref, sst_ref, gxpad_ref, gxa_ref, gst_ref, hst_ref):
    c = pl.program_id(1)

    @pl.when(c == 0)
    def _():
        sst_ref[...] = jnp.zeros(sst_ref.shape, F32)
        gst_ref[...] = jnp.zeros(gst_ref.shape, F32)
        hst_ref[...] = jnp.zeros(hst_ref.shape, F32)
        sxpad_ref[0:SUBLANES, :] = jnp.zeros((SUBLANES, sxpad_ref.shape[1]), F32)
        gxpad_ref[0:SUBLANES, :] = jnp.zeros((SUBLANES, gxpad_ref.shape[1]), F32)
        for _ in _in_proj_stages(hb0_ref, w_ref, proj_ref.at[0]):
            pass

    cur = proj_ref.at[c % 2]
    nxt = proj_ref.at[(c + 1) % 2]

    def seg(off, width):
        return cur.at[:, off:off + width]

    sm_ref = seg(OFF_SMALL, LANES)
    streams = [
        _gdn_stages(seg(OFF_QKV, 3 * GDN_WIDTH), seg(OFF_GGATE, GDN_WIDTH), sm_ref, gcw_ref, ghp_ref, ghpt_ref,
                    gnw_ref, y_ref, gxpad_ref, gxa_ref, gst_ref),
        _ssd_stages(seg(OFF_Z, SSD_WIDTH), seg(OFF_XBC, SSD_XBC), sm_ref, scw_ref, scb_ref, shp_ref, shpt_ref,
                    dsk_ref, snw_ref, y_ref, sxpad_ref, sxa_ref, sst_ref),
        _hgrn_stages(seg(OFF_HQ, HG_WIDTH), seg(OFF_HF, HG_WIDTH), seg(OFF_HI, HG_WIDTH),
                     seg(OFF_HGATE, HG_WIDTH), lb_ref, hnw_ref, y_ref, hst_ref),
        _in_proj_stages(hbn_ref, w_ref, nxt),
    ]
    live = [True] * len(streams)
    rnd = 0
    while any(live):
        advance = [1, 1, 1 if rnd < HG_SLOW_ROUNDS else 2, 1]
        rnd += 1
        for s, stream in enumerate(streams):
            for _ in range(advance[s]):
                if live[s]:
                    try:
                        next(stream)
                    except StopIteration:
                        live[s] = False


def _mixers(hb, w_proj, layer, nb, seq, ssd_conv_w, ssd_conv_b, ssd_dt_bias, ssd_a_log, ssd_d, ssd_norm_w,
            gdn_conv_w, gdn_dt_bias, gdn_a_log, gdn_norm_w, lb, hg_norm_w):
    nblk = seq // ROWS
    shp = jnp.zeros((SUBLANES, LANES), F32)
    shp = shp.at[0, SM_DT:SM_DT + SSD_HEADS].set(ssd_dt_bias).at[1, SM_DT:SM_DT + SSD_HEADS].set(ssd_a_log)
    ghp = jnp.zeros((SUBLANES, LANES), F32)
    ghp = ghp.at[0, SM_A:SM_A + GDN_HEADS].set(gdn_dt_bias).at[1, SM_A:SM_A + GDN_HEADS].set(gdn_a_log)
    dsk = jnp.repeat(ssd_d, SSD_HEAD_DIM)[None, :]
    const = lambda b, c: (0, 0)

    def whole(shape):
        return pl.BlockSpec(shape, const)

    return pl.pallas_call(
        _mixer_kernel,
        out_shape=jax.ShapeDtypeStruct((nb * seq, D_MIX), BF16),
        grid=(nb, nblk),
        in_specs=[
            pl.BlockSpec((ROWS, D_MODEL), lambda b, c: (b * nblk + c, 0)),
            pl.BlockSpec((ROWS, D_MODEL), lambda b, c: (b * nblk + jnp.minimum(c + 1, nblk - 1), 0)),
            pl.BlockSpec((None, D_MODEL, PROJ_COLS), lambda b, c: (layer, 0, 0), pipeline_mode=pl.Buffered(1)),
            whole((SSD_CONV, SSD_XBC)), whole((1, SSD_XBC)), whole((SUBLANES, LANES)), whole((LANES, SUBLANES)),
            whole((1, SSD_WIDTH)), whole((1, SSD_WIDTH)),
            whole((GDN_CONV, 3 * GDN_WIDTH)), whole((SUBLANES, LANES)), whole((LANES, SUBLANES)),
            whole((1, GDN_HEAD_DIM)), whole((1, HG_WIDTH)), whole((1, HG_HEAD_DIM)),
        ],
        out_specs=pl.BlockSpec((ROWS, D_MIX), lambda b, c: (b * nblk + c, 0)),
        scratch_shapes=[pltpu.VMEM((2, ROWS, PROJ_COLS), F32),
                        pltpu.VMEM((ROWS + SUBLANES, SSD_XBC), F32),
                        pltpu.VMEM((ROWS, SSD_XBC), F32),
                        pltpu.VMEM((SSD_GROUPS, SSD_STATE, SSD_WIDTH // SSD_GROUPS), F32),
                        pltpu.VMEM((ROWS + SUBLANES, 3 * GDN_WIDTH), F32),
                        pltpu.VMEM((ROWS, 3 * GDN_WIDTH), F32),
                        pltpu.VMEM((GDN_HEADS, GDN_HEAD_DIM, GDN_HEAD_DIM), F32),
                        pltpu.VMEM((HG_HEADS, HG_HEAD_DIM, HG_HEAD_DIM), F32)],
        compiler_params=_params(("parallel", "arbitrary")),
        name="in_proj_mixers",
    )(hb, hb, w_proj,
      ssd_conv_w, ssd_conv_b[None, :], shp, shp.T, dsk, ssd_norm_w[None, :],
      gdn_conv_w, ghp, ghp.T, gdn_norm_w[None, :], lb[None, :], hg_norm_w[None, :])


def _layer_norm(x, g, b):
    mu = jnp.mean(x, axis=-1, keepdims=True)
    xc = x - mu
    var = jnp.mean(xc * xc, axis=-1, keepdims=True)
    return xc * lax.rsqrt(var + 1e-5) * g + b


ROUTER_ROWS = 32


def _outproj_kernel(y_ref, h_ref, w_ref, g_ref, b_ref, wr_ref, br_ref, h1_ref, h1b_ref, comb_ref):
    mix = jnp.dot(y_ref[...], w_ref[...], preferred_element_type=F32)
    h1 = _layer_norm(DN_ALPHA * h_ref[...] + mix, g_ref[...], b_ref[...])
    h1_ref[...] = h1
    h1b_ref[...] = h1.astype(BF16)

    h_hi, h_lo = _split(h1)
    wr = wr_ref[...]
    lt = _dot_nt(wr, h_hi) + _dot_nt(wr, h_lo)
    lt = lt[0:ROUTER_ROWS, :] + lt[ROUTER_ROWS:2 * ROUTER_ROWS, :] + br_ref[...]
    glog = [lt[g:g + 1, :] for g in range(MOE_GROUPS)]
    elog = [lt[MOE_GROUPS + e:MOE_GROUPS + e + 1, :] for e in range(N_EXPERTS)]

    def softmax(rows):
        m = rows[0]
        for r in rows[1:]:
            m = jnp.maximum(m, r)
        ex = [jnp.exp(r - m) for r in rows]
        tot = ex[0]
        for r in ex[1:]:
            tot = tot + r
        return [r / tot for r in ex]

    def top1(rows):
        best = rows[0]
        for r in rows[1:]:
            best = jnp.maximum(best, r)
        idx = jnp.full_like(best, float(len(rows) - 1))
        for j in range(len(rows) - 2, -1, -1):
            idx = jnp.where(rows[j] == best, float(j), idx)
        return best, idx

    g_p, g_idx = top1(softmax(glog))
    chosen = []
    for k in range(EXPERTS_PER_GROUP):
        acc = jnp.zeros_like(g_p)
        for g in range(MOE_GROUPS):
            acc = jnp.where(g_idx == float(g), elog[g * EXPERTS_PER_GROUP + k], acc)
        chosen.append(acc)
    eprob = softmax(chosen)
    p1, i1 = top1(eprob)
    p2, i2 = top1([jnp.where(i1 == float(k), -1.0, eprob[k]) for k in range(EXPERTS_PER_GROUP)])
    denom = p1 + p2
    w1 = g_p * p1 / denom
    w2 = g_p * p2 / denom
    wk = [jnp.where(i1 == float(k), w1, 0.0) + jnp.where(i2 == float(k), w2, 0.0)
          for k in range(EXPERTS_PER_GROUP)]
    rid = _iota2((ROUTER_ROWS, lt.shape[1]), 0)
    comb_t = jnp.where(rid == N_EXPERTS, g_idx, 0.0)
    for g in range(MOE_GROUPS):
        for k in range(EXPERTS_PER_GROUP):
            comb_t = jnp.where(rid == g * EXPERTS_PER_GROUP + k,
                               jnp.where(g_idx == float(g), wk[k], 0.0), comb_t)
    comb_t = jnp.concatenate([comb_t, jnp.zeros((LANES - ROUTER_ROWS, lt.shape[1]), F32)], axis=0)
    comb_ref[...] = comb_t.T


def _router_params(w_group, b_group, w_expert, b_expert):
    pad = ROUTER_ROWS - MOE_GROUPS - N_EXPERTS
    w_t = jnp.pad(jnp.concatenate([w_group, w_expert], axis=1).T, ((0, pad), (0, 0)))
    b_col = jnp.pad(jnp.concatenate([b_group, b_expert]), (0, pad))[:, None]
    return jnp.concatenate(_split(w_t), axis=0), b_col


def _outproj(y, h, w_out, layer, ln_g, ln_b, w_router, b_router, tm):
    t = h.shape[0]
    rowmap = lambda i: (i, 0)
    const = lambda i: (0, 0)
    return pl.pallas_call(
        _outproj_kernel,
        out_shape=(jax.ShapeDtypeStruct((t, D_MODEL), F32),
                   jax.ShapeDtypeStruct((t, D_MODEL), BF16),
                   jax.ShapeDtypeStruct((t, LANES), F32)),
        grid=(t // tm,),
        in_specs=[pl.BlockSpec((tm, D_MIX), rowmap),
                  pl.BlockSpec((tm, D_MODEL), rowmap),
                  pl.BlockSpec((None, D_MIX, D_MODEL), lambda i: (layer, 0, 0)),
                  pl.BlockSpec((1, D_MODEL), const),
                  pl.BlockSpec((1, D_MODEL), const),
                  pl.BlockSpec((2 * ROUTER_ROWS, D_MODEL), const),
                  pl.BlockSpec((ROUTER_ROWS, 1), const)],
        out_specs=(pl.BlockSpec((tm, D_MODEL), rowmap),
                   pl.BlockSpec((tm, D_MODEL), rowmap),
                   pl.BlockSpec((tm, LANES), rowmap)),
        compiler_params=_params(("parallel",)),
        name="out_proj_ln_router",
    )(y, h, w_out, ln_g[None, :], ln_b[None, :], w_router, b_router)


def _prefix_lanes(v, idx):
    axis = 1 if v.shape[0] == 1 else 0
    out = jnp.zeros_like(v)
    for g in range(MOE_GROUPS - 1):
        vg = v[:, g:g + 1] if axis == 1 else v[g:g + 1, :]
        out = out + jnp.where(idx > g, vg, 0.0)
    return out


def _moe_kernel(hb_ref, h_ref, comb_ref, wgu_ref, wdn_ref, g_ref, b_ref, o_ref, ob_ref,
                xs_ref, cw_ref, ys_ref):
    tm = hb_ref.shape[0]
    comb = comb_ref[...]
    lane = _iota2((tm, LANES), 1).astype(F32)
    gid = comb[:, N_EXPERTS:N_EXPERTS + 1]
    gsel = (lane == gid).astype(F32)
    tr = _iota2((tm, tm), 0)
    tc = _iota2((tm, tm), 1)

    ns = xs_ref.shape[0] - MOE_CHUNK

    def aligned(cnt):
        return jnp.floor((cnt + (MOE_ALIGN - 1)) * (1.0 / MOE_ALIGN)) * MOE_ALIGN

    rank_c = _dot((tc < tr).astype(F32), gsel)
    cnt_r = jnp.sum(gsel, axis=0, keepdims=True)
    start_r = _prefix_lanes(aligned(cnt_r), _iota2((1, LANES), 1))
    dest_c = jnp.sum(gsel * (start_r + rank_c), axis=1, keepdims=True)
    eye8 = (_iota2((SUBLANES, LANES), 0) == _iota2((SUBLANES, LANES), 1)).astype(F32)
    gsel_t = _dot_nt(eye8, gsel)
    rank_r = _dot(gsel_t, (tr < tc).astype(F32))
    cnt_c = jnp.sum(gsel_t, axis=1, keepdims=True)
    start_c = _prefix_lanes(aligned(cnt_c), _iota2((SUBLANES, 1), 0))
    dest_r = jnp.sum(gsel_t * (start_c + rank_r), axis=0, keepdims=True)

    perm = (dest_r == _iota2((ns, tm), 0).astype(F32)).astype(BF16)
    xs_ref[0:ns, :] = jnp.dot(perm, hb_ref[...], preferred_element_type=F32).astype(BF16)
    xs_ref[ns:ns + MOE_CHUNK, :] = jnp.zeros((MOE_CHUNK, D_MODEL), BF16)
    c_hi, c_lo = _split(comb)
    cw2 = jnp.dot(perm, jnp.concatenate([c_hi, c_lo], axis=1), preferred_element_type=F32)
    cw_ref[0:ns, :] = cw2[:, 0:LANES] + cw2[:, LANES:2 * LANES]
    cw_ref[ns:ns + MOE_CHUNK, :] = jnp.zeros((MOE_CHUNK, LANES), F32)
    ys_ref[...] = jnp.zeros(ys_ref.shape, F32)

    lane1 = _iota2((1, LANES), 1)
    starts = [jnp.sum(jnp.where(lane1 == g, start_r, 0.0)).astype(jnp.int32) for g in range(MOE_GROUPS)]
    cnts = [jnp.sum(jnp.where(lane1 == g, cnt_r, 0.0)) for g in range(MOE_GROUPS)]
    for j in range(tm // MOE_CHUNK):
        for g in range(MOE_GROUPS):
            def block(g=g, j=j):
                rows = pl.ds(pl.multiple_of(starts[g] + j * MOE_CHUNK, MOE_ALIGN), MOE_CHUNK)
                x = xs_ref[rows, :]
                cw = cw_ref[rows, :]
                hm = []
                for e in range(EXPERTS_PER_GROUP):
                    ex = g * EXPERTS_PER_GROUP + e
                    gu = jnp.dot(x, wgu_ref[ex], preferred_element_type=F32)
                    hm.append((_silu(gu[:, 0:D_EXPERT]) * gu[:, D_EXPERT:2 * D_EXPERT]
                               * cw[:, ex:ex + 1]).astype(BF16))
                ys_ref[rows, :] += jnp.dot(jnp.concatenate(hm, axis=1), wdn_ref[g],
                                           preferred_element_type=F32)

            if j == 0:
                block()
            else:
                pl.when(cnts[g] > j * MOE_CHUNK)(block)

    unperm = (dest_c == _iota2((tm, ns), 1).astype(F32)).astype(BF16)
    y = jnp.dot(unperm, ys_ref[0:ns, :].astype(BF16), preferred_element_type=F32)
    h2 = _layer_norm(DN_ALPHA * h_ref[...] + y, g_ref[...], b_ref[...])
    o_ref[...] = h2
    ob_ref[...] = h2.astype(BF16)


def _moe(hb, h, comb, w_gu, w_dn, layer, ln_g, ln_b, tm):
    t = h.shape[0]
    rowmap = lambda i: (i, 0)
    const = lambda i: (0, 0)
    const3 = lambda i: (layer, 0, 0, 0)
    sorted_rows = -(-(tm + MOE_GROUPS * MOE_ALIGN) // LANES) * LANES + MOE_CHUNK
    return pl.pallas_call(
        _moe_kernel,
        out_shape=(jax.ShapeDtypeStruct((t, D_MODEL), F32),
                   jax.ShapeDtypeStruct((t, D_MODEL), BF16)),
        grid=(t // tm,),
        in_specs=[pl.BlockSpec((tm, D_MODEL), rowmap),
                  pl.BlockSpec((tm, D_MODEL), rowmap),
                  pl.BlockSpec((tm, LANES), rowmap),
                  pl.BlockSpec((None, N_EXPERTS, D_MODEL, 2 * D_EXPERT), const3, pipeline_mode=pl.Buffered(1)),
                  pl.BlockSpec((None, MOE_GROUPS, EXPERTS_PER_GROUP * D_EXPERT, D_MODEL), const3,
                               pipeline_mode=pl.Buffered(1)),
                  pl.BlockSpec((1, D_MODEL), const),
                  pl.BlockSpec((1, D_MODEL), const)],
        out_specs=(pl.BlockSpec((tm, D_MODEL), rowmap),
                   pl.BlockSpec((tm, D_MODEL), rowmap)),
        scratch_shapes=[pltpu.VMEM((sorted_rows, D_MODEL), BF16),
                        pltpu.VMEM((sorted_rows, LANES), F32),
                        pltpu.VMEM((sorted_rows, D_MODEL), F32)],
        compiler_params=pltpu.CompilerParams(dimension_semantics=("parallel",),
                                             vmem_limit_bytes=MOE_VMEM_LIMIT),
        name="moe_ln",
    )(hb, h, comb, w_gu, w_dn, ln_g[None, :], ln_b[None, :])


W_IN_SEGMENTS = ((SSD_WIDTH, OFF_Z), (SSD_XBC, OFF_XBC), (SSD_HEADS, OFF_SMALL + SM_DT),
                 (3 * GDN_WIDTH, OFF_QKV), (GDN_WIDTH, OFF_GGATE), (GDN_HEADS, OFF_SMALL + SM_B),
                 (GDN_HEADS, OFF_SMALL + SM_A), (HG_WIDTH, OFF_HQ), (HG_WIDTH, OFF_HF),
                 (HG_WIDTH, OFF_HI), (HG_WIDTH, OFF_HGATE))
W_T_BLOCK = 256
W_MAIN_BLOCKS = OFF_SMALL // W_T_BLOCK


def _w_in_kernel(tbl_ref, w_ref, small_ref, o_ref):
    j = pl.program_id(1)

    @pl.when(j < W_MAIN_BLOCKS)
    def _():
        o_ref[...] = w_ref[0].T.astype(BF16)

    @pl.when(j >= W_MAIN_BLOCKS)
    def _():
        o_ref[...] = small_ref[...].T.astype(BF16)


def _split_w_in(w_in):
    depth = w_in.shape[0]
    w_t = jnp.swapaxes(w_in, 1, 2)
    src_of, small_rows, src = {}, [], 0
    for width, dst in W_IN_SEGMENTS:
        if width >= W_T_BLOCK:
            for off in range(0, width, W_T_BLOCK):
                src_of[(dst + off) // W_T_BLOCK] = src + off
        else:
            small_rows.append(w_t[:, src:src + width, :])
        src += width
    table = jnp.array([src_of.get(j, 0) for j in range(PROJ_COLS // W_T_BLOCK)], jnp.int32)
    small = jnp.concatenate(small_rows, axis=1)
    small = jnp.pad(small, ((0, 0), (0, PROJ_COLS - OFF_SMALL - small.shape[1]), (0, 0)))
    return pl.pallas_call(
        _w_in_kernel,
        out_shape=jax.ShapeDtypeStruct((depth, D_MODEL, PROJ_COLS), BF16),
        grid_spec=pltpu.PrefetchScalarGridSpec(
            num_scalar_prefetch=1,
            grid=(depth, PROJ_COLS // W_T_BLOCK),
            in_specs=[pl.BlockSpec((pl.Element(1), pl.Element(W_T_BLOCK), pl.Element(D_MODEL)),
                                   lambda l, j, tbl: (l, pl.multiple_of(tbl[j], SUBLANES), 0)),
                      pl.BlockSpec((None, PROJ_COLS - OFF_SMALL, D_MODEL), lambda l, j, tbl: (l, 0, 0))],
            out_specs=pl.BlockSpec((None, D_MODEL, W_T_BLOCK), lambda l, j, tbl: (l, 0, j))),
        compiler_params=_params(("parallel", "arbitrary")),
        name="w_in_relayout",
    )(table, w_t, small)


def kernel(x, w_in, ssd_conv_w, ssd_conv_b, ssd_dt_bias, ssd_a_log, ssd_d, ssd_norm_w, gdn_conv_w, gdn_dt_bias, gdn_a_log, gdn_norm_w, hg_lb_logits, hg_norm_w, w_out, ln1_g, ln1_b, w_router_group, b_router_group, w_router_expert, b_router_expert, w_expert_gate_up, w_expert_down, ln2_g, ln2_b):
    nb, seq, d = x.shape
    t = nb * seq
    lb_cum = jnp.cumsum(jax.nn.softmax(hg_lb_logits.astype(F32), axis=0), axis=0)
    lb_all = lb_cum - lb_cum[0:1]
    h = x.reshape(t, d)
    hb = h.astype(BF16)
    w_proj = _split_w_in(w_in)
    w_out_b = w_out.astype(BF16)
    w_gu_b = w_expert_gate_up.astype(BF16)
    w_dn_b = w_expert_down.astype(BF16).reshape(DEPTH, MOE_GROUPS, EXPERTS_PER_GROUP * D_EXPERT, D_MODEL)
    for l in range(DEPTH):
        y = _mixers(hb, w_proj, l, nb, seq, ssd_conv_w[l], ssd_conv_b[l], ssd_dt_bias[l], ssd_a_log[l], ssd_d[l],
                    ssd_norm_w[l], gdn_conv_w[l], gdn_dt_bias[l], gdn_a_log[l], gdn_norm_w[l],
                    lb_all[l], hg_norm_w[l])
        w_router, b_router = _router_params(w_router_group[l], b_router_group[l],
                                            w_router_expert[l], b_router_expert[l])
        h1, h1b, comb = _outproj(y, h, w_out_b, l, ln1_g[l], ln1_b[l], w_router, b_router, 512)
        h, hb = _moe(h1b, h1, comb, w_gu_b, w_dn_b, l, ln2_g[l], ln2_b[l], 512)
    return h.reshape(nb, seq, d)
```

```python
import jax
import jax.numpy as jnp
from jax import lax
from jax.experimental import pallas as pl
from jax.experimental.pallas import tpu as pltpu

F32 = jnp.float32
BF16 = jnp.bfloat16

D_MODEL = 1024
DEPTH = 4
SSD_HEADS = 16
SSD_HEAD_DIM = 64
SSD_WIDTH = 1024
SSD_GROUPS = 2
SSD_STATE = 128
SSD_BC = 256
SSD_XBC = 1536
SSD_CONV = 4
GDN_HEADS = 4
GDN_HEAD_DIM = 128
GDN_WIDTH = 512
GDN_CONV = 4
HG_HEADS = 4
HG_HEAD_DIM = 128
HG_WIDTH = 512
D_MIX = 2048
MOE_GROUPS = 4
EXPERTS_PER_GROUP = 4
N_EXPERTS = 16
D_EXPERT = 256
DN_ALPHA = (2 * DEPTH) ** 0.25
LOG2E = 1.4426950408889634

LANES = 128
SUBLANES = 8
ROWS = 128
VMEM_LIMIT = 48 * 1024 * 1024
MOE_VMEM_LIMIT = 56 * 1024 * 1024
CONV_SLAB = 256
GDN_HI_LEVELS = 2
MOE_CHUNK = 160
MOE_ALIGN = 16
PROJ_SLAB = 768
HG_SLOW_ROUNDS = 12

OFF_XBC = 0
OFF_QKV = 1536
OFF_Z = 3072
OFF_GGATE = 4096
OFF_HQ = 4608
OFF_HF = 5120
OFF_HI = 5632
OFF_HGATE = 6144
OFF_SMALL = 6656
PROJ_COLS = 6912
SM_DT = 0
SM_B = 16
SM_A = 20
MIX_SSD = 0
MIX_GDN = SSD_WIDTH
MIX_HG = SSD_WIDTH + GDN_WIDTH


def _dot(a, b):
    return jnp.dot(a.astype(BF16), b.astype(BF16), preferred_element_type=F32)


def _dot_nt(a, b):
    return lax.dot_general(a.astype(BF16), b.astype(BF16), (((1,), (1,)), ((), ())),
                           preferred_element_type=F32)


def _dot_tn(a, b):
    return lax.dot_general(a.astype(BF16), b.astype(BF16), (((0,), (0,)), ((), ())),
                           preferred_element_type=F32)


def _split3(a):
    p1 = a.astype(BF16)
    r1 = a - p1.astype(F32)
    p2 = r1.astype(BF16)
    return p1, p2, (r1 - p2.astype(F32)).astype(BF16)


def _dot_01(m01, x):
    m = m01.astype(BF16)
    p1, p2, p3 = _split3(x)
    return (jnp.dot(m, p1, preferred_element_type=F32) + jnp.dot(m, p2, preferred_element_type=F32)
            + jnp.dot(m, p3, preferred_element_type=F32))


def _dot_x01(x, m01):
    m = m01.astype(BF16)
    p1, p2, p3 = _split3(x)
    return (jnp.dot(p1, m, preferred_element_type=F32) + jnp.dot(p2, m, preferred_element_type=F32)
            + jnp.dot(p3, m, preferred_element_type=F32))


def _split(a):
    hi = a.astype(BF16)
    return hi, (a - hi.astype(F32)).astype(BF16)


def _dot3(a, b):
    (ah, al), (bh, bl) = a, b
    return (jnp.dot(ah, bh, preferred_element_type=F32) + jnp.dot(ah, bl, preferred_element_type=F32)
            + jnp.dot(al, bh, preferred_element_type=F32))


def _sigmoid(x):
    return 1.0 / (1.0 + jnp.exp(-x))


def _silu(x):
    return x * _sigmoid(x)


def _softplus(x):
    return jnp.maximum(x, 0.0) + jnp.log(1.0 + jnp.exp(-jnp.abs(x)))


def _iota2(shape, dim):
    return lax.broadcasted_iota(jnp.int32, shape, dim)


def _params(sem):
    return pltpu.CompilerParams(dimension_semantics=sem, vmem_limit_bytes=VMEM_LIMIT)


def _matmul_kernel(x_ref, w_ref, o_ref):
    o_ref[...] = jnp.dot(x_ref[...], w_ref[...], preferred_element_type=F32)


def _matmul(x, w, tm, tn):
    t, k = x.shape
    n = w.shape[1]
    return pl.pallas_call(
        _matmul_kernel,
        out_shape=jax.ShapeDtypeStruct((t, n), F32),
        grid=(n // tn, t // tm),
        in_specs=[pl.BlockSpec((tm, k), lambda j, i: (i, 0)),
                  pl.BlockSpec((k, tn), lambda j, i: (0, j))],
        out_specs=pl.BlockSpec((tm, tn), lambda j, i: (i, j)),
        compiler_params=_params(("parallel", "arbitrary")),
        name="in_proj",
    )(x, w)


def _conv_silu_stages(x_ref, xpad_ref, xa_ref, cw_ref, cb_ref):
    rows, cols = x_ref.shape
    k = cw_ref.shape[0]
    for c0 in range(0, cols, CONV_SLAB):
        cs = slice(c0, c0 + CONV_SLAB)
        x = x_ref[:, cs]
        xpad_ref[SUBLANES:SUBLANES + rows, cs] = x
        acc = x * cw_ref[k - 1:k, cs]
        if cb_ref is not None:
            acc = acc + cb_ref[:, cs]
        for i in range(k - 1):
            off = SUBLANES - (k - 1) + i
            acc = acc + xpad_ref[off:off + rows, cs] * cw_ref[i:i + 1, cs]
        xa_ref[:, cs] = _silu(acc)
        xpad_ref[0:SUBLANES, cs] = xpad_ref[rows:rows + SUBLANES, cs]
        yield


def _ssd_stages(z_ref, xbc_ref, sm_ref, cw_ref, cb_ref, hp_ref, hpt_ref, dsk_ref, nw_ref,
                y_ref, xpad_ref, xa_ref, st_ref):
    yield from _conv_silu_stages(xbc_ref, xpad_ref, xa_ref, cw_ref, cb_ref)

    row = _iota2((ROWS, ROWS), 0)
    col = _iota2((ROWS, ROWS), 1)
    causal = row >= col
    tril = causal.astype(F32)
    triu = (row <= col).astype(F32)

    sm = sm_ref[...]
    dt = _softplus(sm + hp_ref[0:1, :])
    da = dt * (-jnp.exp(hp_ref[1:2, :]))
    acum = _dot_01(tril, da)
    smt = sm.T
    dtt = _softplus(smt[0:SSD_HEADS, :] + hpt_ref[0:SSD_HEADS, 0:1])
    dat = dtt * (-jnp.exp(hpt_ref[0:SSD_HEADS, 1:2]))
    acumt = _dot_x01(dat, triu)

    heads3 = jnp.concatenate([dt, jnp.exp(acum), jnp.exp(acum[ROWS - 1:ROWS, :] - acum)], axis=0)
    h_hi, h_lo = _split(heads3)
    heads6 = jnp.concatenate([h_hi, h_lo], axis=0)
    lane_lo = _iota2((ROWS, LANES), 1) < SSD_HEAD_DIM
    yield

    hg = SSD_HEADS // SSD_GROUPS
    gw = SSD_WIDTH // SSD_GROUPS
    hrow = _iota2((LANES, gw), 0)
    hcol = _iota2((LANES, gw), 1)
    for g in range(SSD_GROUPS):
        gs = slice(g * gw, (g + 1) * gw)
        expand = (((hcol >> 6) + g * hg) == hrow).astype(BF16)
        ex = jnp.dot(heads6, expand, preferred_element_type=F32)
        ex = ex[0:3 * ROWS, :] + ex[3 * ROWS:6 * ROWS, :]
        dtx = ex[0:ROWS, :]
        eax = ex[ROWS:2 * ROWS, :]
        tex = ex[2 * ROWS:3 * ROWS, :]
        yield
        xs = xa_ref[:, gs]
        xdt = xs * dtx
        xdt_b = xdt.astype(BF16)
        xend_b = (xdt * tex).astype(BF16)
        bm = xa_ref[:, SSD_WIDTH + g * SSD_STATE:SSD_WIDTH + (g + 1) * SSD_STATE]
        cm = xa_ref[:, SSD_WIDTH + SSD_BC + g * SSD_STATE:SSD_WIDTH + SSD_BC + (g + 1) * SSD_STATE]
        cm_b = cm.astype(BF16)
        cb = _dot_nt(cm_b, bm)
        yield
        yd = []
        for pair in range(hg // 2):
            res = []
            for sub in range(2):
                h = g * hg + pair * 2 + sub
                seg = acum[:, h:h + 1] - acumt[h:h + 1, :]
                lmat = cb * jnp.exp(jnp.where(causal, seg, -jnp.inf))
                c0 = pair * 2 * SSD_HEAD_DIM
                res.append(jnp.dot(lmat.astype(BF16), xdt_b[:, c0:c0 + LANES],
                                   preferred_element_type=F32))
            yd.append(jnp.where(lane_lo, res[0], res[1]))
            yield
        yd = jnp.concatenate(yd, axis=1)
        st = st_ref[g]
        yoff = jnp.dot(cm_b, st.astype(BF16), preferred_element_type=F32) * eax
        st_ref[g] = (st * eax[ROWS - 1:ROWS, :]
                     + jnp.dot(bm.T.astype(BF16), xend_b, preferred_element_type=F32))
        yield
        y = yd + yoff + xs * dsk_ref[:, gs]
        y = y * _silu(z_ref[:, gs])
        ms = jnp.mean(y * y, axis=-1, keepdims=True)
        y_ref[:, MIX_SSD + g * gw:MIX_SSD + (g + 1) * gw] = (
            y * lax.rsqrt(ms + 1e-6) * nw_ref[:, gs]).astype(y_ref.dtype)
        yield


def _gdn_stages(qkv_ref, gate_ref, sm_ref, cw_ref, hp_ref, hpt_ref, nw_ref, y_ref, xpad_ref, xa_ref, st_ref):
    yield from _conv_silu_stages(qkv_ref, xpad_ref, xa_ref, cw_ref, None)

    row = _iota2((ROWS, ROWS), 0)
    col = _iota2((ROWS, ROWS), 1)
    incl = row >= col
    strict = row > col
    tril = incl.astype(F32)
    triu = (row <= col).astype(F32)

    sm = sm_ref[...]
    la = -jnp.exp(hp_ref[1:2, :]) * _softplus(sm + hp_ref[0:1, :])
    gcum = _dot_01(tril, la)
    smt = sm.T
    lat = (-jnp.exp(hpt_ref[SM_B:SM_B + SUBLANES, 1:2])
           * _softplus(smt[SM_B:SM_B + SUBLANES, :] + hpt_ref[SM_B:SM_B + SUBLANES, 0:1]))
    gcumt = _dot_x01(lat, triu)
    beta_all = _sigmoid(sm)
    yield

    dh = GDN_HEAD_DIM
    heads = range(GDN_HEADS)
    qs, ks, xs, rhss, decays, gs = [], [], [], [], [], []
    for h in heads:
        q = xa_ref[:, h * dh:(h + 1) * dh]
        k = xa_ref[:, GDN_WIDTH + h * dh:GDN_WIDTH + (h + 1) * dh]
        v = xa_ref[:, 2 * GDN_WIDTH + h * dh:2 * GDN_WIDTH + (h + 1) * dh]
        q = q * lax.rsqrt(jnp.sum(q * q, axis=-1, keepdims=True) + 1e-6) * (dh ** -0.5)
        k = k * lax.rsqrt(jnp.sum(k * k, axis=-1, keepdims=True) + 1e-6)
        g = gcum[:, SM_A + h:SM_A + h + 1]
        gt = gcumt[SM_A - SM_B + h:SM_A - SM_B + h + 1, :]
        beta = beta_all[:, SM_B + h:SM_B + h + 1]
        decay = jnp.exp(jnp.where(incl, g - gt, -jnp.inf))
        kb = k * beta
        xs.append(-jnp.where(strict, _dot_nt(kb, k) * decay, 0.0))
        rhss.append(jnp.concatenate([v * beta, kb * jnp.exp(g)], axis=1))
        qs.append(q)
        ks.append(k)
        decays.append(decay)
        gs.append(g)
        yield
    ps = xs
    ns = xs
    levels = ROWS.bit_length() - 2
    for j in range(levels):
        if j < GDN_HI_LEVELS:
            pp = [_split(p) for p in ps]
            ps = [_dot3(p, p) for p in pp]
            yield
            pp = [_split(p) for p in ps]
            ns = [n + p + _dot3(_split(n), p2) for n, p, p2 in zip(ns, ps, pp)]
        else:
            ps = [_dot(p, p) for p in ps]
            yield
            ns = [n + p + _dot(n, p) for n, p in zip(ns, ps)]
        yield
    sols = [r + _dot(n, r) for n, r in zip(ns, rhss)]
    yield
    qks = [_dot_nt(q, k) * d for q, k, d in zip(qs, ks, decays)]
    sts = [st_ref[h] for h in heads]
    v_news = [s[:, 0:dh] - _dot(s[:, dh:2 * dh], st) for s, st in zip(sols, sts)]
    yield
    os_ = [_dot(q * jnp.exp(g), st) + _dot(qk, vn) for q, g, st, qk, vn in zip(qs, gs, sts, qks, v_news)]
    yield
    for h in heads:
        glast = gs[h][ROWS - 1:ROWS, :]
        kd = ks[h] * jnp.exp(glast - gs[h])
        st_ref[h] = sts[h] * jnp.exp(glast) + _dot(kd.T, v_news[h])
    yield
    for h in heads:
        o = os_[h]
        o = o * lax.rsqrt(jnp.mean(o * o, axis=-1, keepdims=True) + 1e-6)
        y_ref[:, MIX_GDN + h * dh:MIX_GDN + (h + 1) * dh] = (
            o * nw_ref[...] * _silu(gate_ref[:, h * dh:(h + 1) * dh])).astype(y_ref.dtype)
        yield


def _hgrn_stages(q_ref, f_ref, i_ref, gate_ref, lb_ref, nw_ref, y_ref, st_ref):
    row = _iota2((ROWS, ROWS), 0)
    col = _iota2((ROWS, ROWS), 1)
    tril = (row >= col).astype(F32)
    rmod = _iota2((ROWS, LANES), 0) & (SUBLANES - 1)
    rfull = _iota2((ROWS, LANES), 0)
    dk = HG_HEAD_DIM
    level_masks = {}
    m = 1
    while m < ROWS:
        right = (rfull & m) != 0
        parent = ~(2 * m - 1)
        keep = (((row & m) ^ m) | (col & m) | ((row ^ col) & parent)) == 0
        level_masks[m] = (right, jnp.where(right, LOG2E, -LOG2E), keep)
        m *= 2
    for h in range(HG_HEADS):
        sl = slice(h * dk, (h + 1) * dk)
        lb = lb_ref[:, sl]
        fr = f_ref[:, sl]
        log_sig = jnp.minimum(fr, 0.0) - jnp.log(1.0 + jnp.exp(-jnp.abs(fr)))
        a = jnp.log(lb)
        y = jnp.log1p(-lb) + log_sig
        log_f = jnp.maximum(a, y) + jnp.log(1.0 + jnp.exp(-jnp.abs(a - y)))
        k = (1.0 - lb) * _sigmoid(-fr)
        q = _silu(q_ref[:, sl])
        v = i_ref[:, sl]
        b = _dot_01(tril, log_f)
        yield

        st = st_ref[h]
        o = _dot_nt(q * jnp.exp(b), st)
        blast = b[ROWS - 1:ROWS, :]
        st_ref[h] = st * jnp.exp(blast) + _dot_tn(v, k * jnp.exp(blast - b))
        yield

        pmat = jnp.zeros((ROWS, ROWS), F32)
        b3 = b.reshape(ROWS // SUBLANES, SUBLANES, dk)
        m = 1
        while m < ROWS:
            if m < SUBLANES:
                ref = None
                for p0 in range(0, SUBLANES, 2 * m):
                    cand = jnp.broadcast_to(b3[:, p0 + m - 1:p0 + m, :], b3.shape).reshape(ROWS, dk)
                    ref = cand if ref is None else jnp.where(rmod >= p0, cand, ref)
            else:
                ref = jnp.concatenate(
                    [jnp.broadcast_to(b[p0 + m - 1:p0 + m, :], (2 * m, dk)) for p0 in range(0, ROWS, 2 * m)],
                    axis=0)
            right, scale, keep = level_masks[m]
            z = jnp.where(right, q, k) * jnp.exp2((b - ref) * scale)
            zb = z.astype(BF16)
            pmat = pmat + jnp.where(keep, _dot_nt(zb, zb), 0.0)
            m *= 2
            yield
        o = o + _dot(pmat, v) + jnp.sum(q * k, axis=-1, keepdims=True) * v
        yield

        o = o * lax.rsqrt(jnp.mean(o * o, axis=-1, keepdims=True) + 1e-6)
        y_ref[:, MIX_HG + h * dk:MIX_HG + (h + 1) * dk] = (
            o * nw_ref[...] * _silu(gate_ref[:, sl])).astype(y_ref.dtype)
        yield


def _in_proj_stages(hb_ref, w_ref, out_ref):
    hb = hb_ref[...]
    for c0 in range(0, PROJ_COLS, PROJ_SLAB):
        out_ref[:, c0:c0 + PROJ_SLAB] = jnp.dot(hb, w_ref[:, c0:c0 + PROJ_SLAB], preferred_element_type=F32)
        yield


def _mixer_kernel(hb0_ref, hbn_ref, w_ref,
                  scw_ref, scb_ref, shp_ref, shpt_ref, dsk_ref, snw_ref,
                  gcw_ref, ghp_ref, ghpt_ref, gnw_ref, lb_ref, hnw_ref,
                  y_ref, proj_ref, sxpad_ref, sxa_ref, sst_ref, gxpad_ref, gxa_ref, gst_ref, hst_ref):
    c = pl.program_id(1)

    @pl.when(c == 0)
    def _():
        sst_ref[...] = jnp.zeros(sst_ref.shape, F32)
        gst_ref[...] = jnp.zeros(gst_ref.shape, F32)
        hst_ref[...] = jnp.zeros(hst_ref.shape, F32)
        sxpad_ref[0:SUBLANES, :] = jnp.zeros((SUBLANES, sxpad_ref.shape[1]), F32)
        gxpad_ref[0:SUBLANES, :] = jnp.zeros((SUBLANES, gxpad_ref.shape[1]), F32)
        for _ in _in_proj_stages(hb0_ref, w_ref, proj_ref.at[0]):
            pass

    cur = proj_ref.at[c % 2]
    nxt = proj_ref.at[(c + 1) % 2]

    def seg(off, width):
        return cur.at[:, off:off + width]

    sm_ref = seg(OFF_SMALL, LANES)
    streams = [
        _gdn_stages(seg(OFF_QKV, 3 * GDN_WIDTH), seg(OFF_GGATE, GDN_WIDTH), sm_ref, gcw_ref, ghp_ref, ghpt_ref,
                    gnw_ref, y_ref, gxpad_ref, gxa_ref, gst_ref),
        _ssd_stages(seg(OFF_Z, SSD_WIDTH), seg(OFF_XBC, SSD_XBC), sm_ref, scw_ref, scb_ref, shp_ref, shpt_ref,
                    dsk_ref, snw_ref, y_ref, sxpad_ref, sxa_ref, sst_ref),
        _hgrn_stages(seg(OFF_HQ, HG_WIDTH), seg(OFF_HF, HG_WIDTH), seg(OFF_HI, HG_WIDTH),
                     seg(OFF_HGATE, HG_WIDTH), lb_ref, hnw_ref, y_ref, hst_ref),
        _in_proj_stages(hbn_ref, w_ref, nxt),
    ]
    live = [True] * len(streams)
    rnd = 0
    while any(live):
        advance = [1, 1, 1 if rnd < HG_SLOW_ROUNDS else 2, 1]
        rnd += 1
        for s, stream in enumerate(streams):
            for _ in range(advance[s]):
                if live[s]:
                    try:
                        next(stream)
                    except StopIteration:
                        live[s] = False


def _mixers(hb, w_proj, layer, nb, seq, ssd_conv_w, ssd_conv_b, ssd_dt_bias, ssd_a_log, ssd_d, ssd_norm_w,
            gdn_conv_w, gdn_dt_bias, gdn_a_log, gdn_norm_w, lb, hg_norm_w):
    nblk = seq // ROWS
    shp = jnp.zeros((SUBLANES, LANES), F32)
    shp = shp.at[0, SM_DT:SM_DT + SSD_HEADS].set(ssd_dt_bias).at[1, SM_DT:SM_DT + SSD_HEADS].set(ssd_a_log)
    ghp = jnp.zeros((SUBLANES, LANES), F32)
    ghp = ghp.at[0, SM_A:SM_A + GDN_HEADS].set(gdn_dt_bias).at[1, SM_A:SM_A + GDN_HEADS].set(gdn_a_log)
    dsk = jnp.repeat(ssd_d, SSD_HEAD_DIM)[None, :]
    const = lambda b, c: (0, 0)

    def whole(shape):
        return pl.BlockSpec(shape, const)

    return pl.pallas_call(
        _mixer_kernel,
        out_shape=jax.ShapeDtypeStruct((nb * seq, D_MIX), BF16),
        grid=(nb, nblk),
        in_specs=[
            pl.BlockSpec((ROWS, D_MODEL), lambda b, c: (b * nblk + c, 0)),
            pl.BlockSpec((ROWS, D_MODEL), lambda b, c: (b * nblk + jnp.minimum(c + 1, nblk - 1), 0)),
            pl.BlockSpec((None, D_MODEL, PROJ_COLS), lambda b, c: (layer, 0, 0), pipeline_mode=pl.Buffered(1)),
            whole((SSD_CONV, SSD_XBC)), whole((1, SSD_XBC)), whole((SUBLANES, LANES)), whole((LANES, SUBLANES)),
            whole((1, SSD_WIDTH)), whole((1, SSD_WIDTH)),
            whole((GDN_CONV, 3 * GDN_WIDTH)), whole((SUBLANES, LANES)), whole((LANES, SUBLANES)),
            whole((1, GDN_HEAD_DIM)), whole((1, HG_WIDTH)), whole((1, HG_HEAD_DIM)),
        ],
        out_specs=pl.BlockSpec((ROWS, D_MIX), lambda b, c: (b * nblk + c, 0)),
        scratch_shapes=[pltpu.VMEM((2, ROWS, PROJ_COLS), F32),
                        pltpu.VMEM((ROWS + SUBLANES, SSD_XBC), F32),
                        pltpu.VMEM((ROWS, SSD_XBC), F32),
                        pltpu.VMEM((SSD_GROUPS, SSD_STATE, SSD_WIDTH // SSD_GROUPS), F32),
                        pltpu.VMEM((ROWS + SUBLANES, 3 * GDN_WIDTH), F32),
                        pltpu.VMEM((ROWS, 3 * GDN_WIDTH), F32),
                        pltpu.VMEM((GDN_HEADS, GDN_HEAD_DIM, GDN_HEAD_DIM), F32),
                        pltpu.VMEM((HG_HEADS, HG_HEAD_DIM, HG_HEAD_DIM), F32)],
        compiler_params=_params(("parallel", "arbitrary")),
        name="in_proj_mixers",
    )(hb, hb, w_proj,
      ssd_conv_w, ssd_conv_b[None, :], shp, shp.T, dsk, ssd_norm_w[None, :],
      gdn_conv_w, ghp, ghp.T, gdn_norm_w[None, :], lb[None, :], hg_norm_w[None, :])


def _layer_norm(x, g, b):
    mu = jnp.mean(x, axis=-1, keepdims=True)
    xc = x - mu
    var = jnp.mean(xc * xc, axis=-1, keepdims=True)
    return xc * lax.rsqrt(var + 1e-5) * g + b


ROUTER_ROWS = 32


def _outproj_kernel(y_ref, h_ref, w_ref, g_ref, b_ref, wr_ref, br_ref, h1_ref, h1b_ref, comb_ref):
    mix = jnp.dot(y_ref[...], w_ref[...], preferred_element_type=F32)
    h1 = _layer_norm(DN_ALPHA * h_ref[...] + mix, g_ref[...], b_ref[...])
    h1_ref[...] = h1
    h1b_ref[...] = h1.astype(BF16)

    h_hi, h_lo = _split(h1)
    wr = wr_ref[...]
    lt = _dot_nt(wr, h_hi) + _dot_nt(wr, h_lo)
    lt = lt[0:ROUTER_ROWS, :] + lt[ROUTER_ROWS:2 * ROUTER_ROWS, :] + br_ref[...]
    glog = [lt[g:g + 1, :] for g in range(MOE_GROUPS)]
    elog = [lt[MOE_GROUPS + e:MOE_GROUPS + e + 1, :] for e in range(N_EXPERTS)]

    def softmax(rows):
        m = rows[0]
        for r in rows[1:]:
            m = jnp.maximum(m, r)
        ex = [jnp.exp(r - m) for r in rows]
        tot = ex[0]
        for r in ex[1:]:
            tot = tot + r
        return [r / tot for r in ex]

    def top1(rows):
        best = rows[0]
        for r in rows[1:]:
            best = jnp.maximum(best, r)
        idx = jnp.full_like(best, float(len(rows) - 1))
        for j in range(len(rows) - 2, -1, -1):
            idx = jnp.where(rows[j] == best, float(j), idx)
        return best, idx

    g_p, g_idx = top1(softmax(glog))
    chosen = []
    for k in range(EXPERTS_PER_GROUP):
        acc = jnp.zeros_like(g_p)
        for g in range(MOE_GROUPS):
            acc = jnp.where(g_idx == float(g), elog[g * EXPERTS_PER_GROUP + k], acc)
        chosen.append(acc)
    eprob = softmax(chosen)
    p1, i1 = top1(eprob)
    p2, i2 = top1([jnp.where(i1 == float(k), -1.0, eprob[k]) for k in range(EXPERTS_PER_GROUP)])
    denom = p1 + p2
    w1 = g_p * p1 / denom
    w2 = g_p * p2 / denom
    wk = [jnp.where(i1 == float(k), w1, 0.0) + jnp.where(i2 == float(k), w2, 0.0)
          for k in range(EXPERTS_PER_GROUP)]
    rid = _iota2((ROUTER_ROWS, lt.shape[1]), 0)
    comb_t = jnp.where(rid == N_EXPERTS, g_idx, 0.0)
    for g in range(MOE_GROUPS):
        for k in range(EXPERTS_PER_GROUP):
            comb_t = jnp.where(rid == g * EXPERTS_PER_GROUP + k,
                               jnp.where(g_idx == float(g), wk[k], 0.0), comb_t)
    comb_t = jnp.concatenate([comb_t, jnp.zeros((LANES - ROUTER_ROWS, lt.shape[1]), F32)], axis=0)
    comb_ref[...] = comb_t.T


def _router_params(w_group, b_group, w_expert, b_expert):
    pad = ROUTER_ROWS - MOE_GROUPS - N_EXPERTS
    w_t = jnp.pad(jnp.concatenate([w_group, w_expert], axis=1).T, ((0, pad), (0, 0)))
    b_col = jnp.pad(jnp.concatenate([b_group, b_expert]), (0, pad))[:, None]
    return jnp.concatenate(_split(w_t), axis=0), b_col


def _outproj(y, h, w_out, layer, ln_g, ln_b, w_router, b_router, tm):
    t = h.shape[0]
    rowmap = lambda i: (i, 0)
    const = lambda i: (0, 0)
    return pl.pallas_call(
        _outproj_kernel,
        out_shape=(jax.ShapeDtypeStruct((t, D_MODEL), F32),
                   jax.ShapeDtypeStruct((t, D_MODEL), BF16),
                   jax.ShapeDtypeStruct((t, LANES), F32)),
        grid=(t // tm,),
        in_specs=[pl.BlockSpec((tm, D_MIX), rowmap),
                  pl.BlockSpec((tm, D_MODEL), rowmap),
                  pl.BlockSpec((None, D_MIX, D_MODEL), lambda i: (layer, 0, 0)),
                  pl.BlockSpec((1, D_MODEL), const),
                  pl.BlockSpec((1, D_MODEL), const),
                  pl.BlockSpec((2 * ROUTER_ROWS, D_MODEL), const),
                  pl.BlockSpec((ROUTER_ROWS, 1), const)],
        out_specs=(pl.BlockSpec((tm, D_MODEL), rowmap),
                   pl.BlockSpec((tm, D_MODEL), rowmap),
                   pl.BlockSpec((tm, LANES), rowmap)),
        compiler_params=_params(("parallel",)),
        name="out_proj_ln_router",
    )(y, h, w_out, ln_g[None, :], ln_b[None, :], w_router, b_router)


def _prefix_lanes(v, idx):
    axis = 1 if v.shape[0] == 1 else 0
    out = jnp.zeros_like(v)
    for g in range(MOE_GROUPS - 1):
        vg = v[:, g:g + 1] if axis == 1 else v[g:g + 1, :]
        out = out + jnp.where(idx > g, vg, 0.0)
    return out


def _moe_kernel(hb_ref, h_ref, comb_ref, wgu_ref, wdn_ref, g_ref, b_ref, o_ref, ob_ref,
                xs_ref, cw_ref, ys_ref):
    tm = hb_ref.shape[0]
    comb = comb_ref[...]
    lane = _iota2((tm, LANES), 1).astype(F32)
    gid = comb[:, N_EXPERTS:N_EXPERTS + 1]
    gsel = (lane == gid).astype(F32)
    tr = _iota2((tm, tm), 0)
    tc = _iota2((tm, tm), 1)

    ns = xs_ref.shape[0] - MOE_CHUNK

    def aligned(cnt):
        return jnp.floor((cnt + (MOE_ALIGN - 1)) * (1.0 / MOE_ALIGN)) * MOE_ALIGN

    rank_c = _dot((tc < tr).astype(F32), gsel)
    cnt_r = jnp.sum(gsel, axis=0, keepdims=True)
    start_r = _prefix_lanes(aligned(cnt_r), _iota2((1, LANES), 1))
    dest_c = jnp.sum(gsel * (start_r + rank_c), axis=1, keepdims=True)
    eye8 = (_iota2((SUBLANES, LANES), 0) == _iota2((SUBLANES, LANES), 1)).astype(F32)
    gsel_t = _dot_nt(eye8, gsel)
    rank_r = _dot(gsel_t, (tr < tc).astype(F32))
    cnt_c = jnp.sum(gsel_t, axis=1, keepdims=True)
    start_c = _prefix_lanes(aligned(cnt_c), _iota2((SUBLANES, 1), 0))
    dest_r = jnp.sum(gsel_t * (start_c + rank_r), axis=0, keepdims=True)

    perm = (dest_r == _iota2((ns, tm), 0).astype(F32)).astype(BF16)
    xs_ref[0:ns, :] = jnp.dot(perm, hb_ref[...], preferred_element_type=F32).astype(BF16)
    xs_ref[ns:ns + MOE_CHUNK, :] = jnp.zeros((MOE_CHUNK, D_MODEL), BF16)
    c_hi, c_lo = _split(comb)
    cw2 = jnp.dot(perm, jnp.concatenate([c_hi, c_lo], axis=1), preferred_element_type=F32)
    cw_ref[0:ns, :] = cw2[:, 0:LANES] + cw2[:, LANES:2 * LANES]
    cw_ref[ns:ns + MOE_CHUNK, :] = jnp.zeros((MOE_CHUNK, LANES), F32)
    ys_ref[...] = jnp.zeros(ys_ref.shape, F32)

    lane1 = _iota2((1, LANES), 1)
    starts = [jnp.sum(jnp.where(lane1 == g, start_r, 0.0)).astype(jnp.int32) for g in range(MOE_GROUPS)]
    cnts = [jnp.sum(jnp.where(lane1 == g, cnt_r, 0.0)) for g in range(MOE_GROUPS)]
    for j in range(-(-tm // MOE_CHUNK)):
        for g in range(MOE_GROUPS):
            def block(g=g, j=j):
                rows = pl.ds(pl.multiple_of(starts[g] + j * MOE_CHUNK, MOE_ALIGN), MOE_CHUNK)
                x = xs_ref[rows, :]
                cw = cw_ref[rows, :]
                hm = []
                for e in range(EXPERTS_PER_GROUP):
                    ex = g * EXPERTS_PER_GROUP + e
                    gu = jnp.dot(x, wgu_ref[ex], preferred_element_type=F32)
                    hm.append((_silu(gu[:, 0:D_EXPERT]) * gu[:, D_EXPERT:2 * D_EXPERT]
                               * cw[:, ex:ex + 1]).astype(BF16))
                ys_ref[rows, :] += jnp.dot(jnp.concatenate(hm, axis=1), wdn_ref[g],
                                           preferred_element_type=F32)

            if j == 0:
                block()
            else:
                pl.when(cnts[g] > j * MOE_CHUNK)(block)

    unperm = (dest_c == _iota2((tm, ns), 1).astype(F32)).astype(BF16)
    y = jnp.dot(unperm, ys_ref[0:ns, :].astype(BF16), preferred_element_type=F32)
    h2 = _layer_norm(DN_ALPHA * h_ref[...] + y, g_ref[...], b_ref[...])
    o_ref[...] = h2
    ob_ref[...] = h2.astype(BF16)


def _moe(hb, h, comb, w_gu, w_dn, layer, ln_g, ln_b, tm):
    t = h.shape[0]
    rowmap = lambda i: (i, 0)
    const = lambda i: (0, 0)
    const3 = lambda i: (layer, 0, 0, 0)
    sorted_rows = -(-(tm + MOE_GROUPS * MOE_ALIGN) // LANES) * LANES + MOE_CHUNK
    return pl.pallas_call(
        _moe_kernel,
        out_shape=(jax.ShapeDtypeStruct((t, D_MODEL), F32),
                   jax.ShapeDtypeStruct((t, D_MODEL), BF16)),
        grid=(t // tm,),
        in_specs=[pl.BlockSpec((tm, D_MODEL), rowmap),
                  pl.BlockSpec((tm, D_MODEL), rowmap),
                  pl.BlockSpec((tm, LANES), rowmap),
                  pl.BlockSpec((None, N_EXPERTS, D_MODEL, 2 * D_EXPERT), const3, pipeline_mode=pl.Buffered(1)),
                  pl.BlockSpec((None, MOE_GROUPS, EXPERTS_PER_GROUP * D_EXPERT, D_MODEL), const3,
                               pipeline_mode=pl.Buffered(1)),
                  pl.BlockSpec((1, D_MODEL), const),
                  pl.BlockSpec((1, D_MODEL), const)],
        out_specs=(pl.BlockSpec((tm, D_MODEL), rowmap),
                   pl.BlockSpec((tm, D_MODEL), rowmap)),
        scratch_shapes=[pltpu.VMEM((sorted_rows, D_MODEL), BF16),
                        pltpu.VMEM((sorted_rows, LANES), F32),
                        pltpu.VMEM((sorted_rows, D_MODEL), F32)],
        compiler_params=pltpu.CompilerParams(dimension_semantics=("parallel",),
                                             vmem_limit_bytes=MOE_VMEM_LIMIT),
        name="moe_ln",
    )(hb, h, comb, w_gu, w_dn, ln_g[None, :], ln_b[None, :])


W_IN_SEGMENTS = ((SSD_WIDTH, OFF_Z), (SSD_XBC, OFF_XBC), (SSD_HEADS, OFF_SMALL + SM_DT),
                 (3 * GDN_WIDTH, OFF_QKV), (GDN_WIDTH, OFF_GGATE), (GDN_HEADS, OFF_SMALL + SM_B),
                 (GDN_HEADS, OFF_SMALL + SM_A), (HG_WIDTH, OFF_HQ), (HG_WIDTH, OFF_HF),
                 (HG_WIDTH, OFF_HI), (HG_WIDTH, OFF_HGATE))
W_T_BLOCK = 256
W_MAIN_BLOCKS = OFF_SMALL // W_T_BLOCK


def _w_in_kernel(tbl_ref, w_ref, small_ref, o_ref):
    j = pl.program_id(1)

    @pl.when(j < W_MAIN_BLOCKS)
    def _():
        o_ref[...] = w_ref[0].T.astype(BF16)

    @pl.when(j >= W_MAIN_BLOCKS)
    def _():
        o_ref[...] = small_ref[...].T.astype(BF16)


def _split_w_in(w_in):
    depth = w_in.shape[0]
    w_t = jnp.swapaxes(w_in, 1, 2)
    src_of, small_rows, src = {}, [], 0
    for width, dst in W_IN_SEGMENTS:
        if width >= W_T_BLOCK:
            for off in range(0, width, W_T_BLOCK):
                src_of[(dst + off) // W_T_BLOCK] = src + off
        else:
            small_rows.append(w_t[:, src:src + width, :])
        src += width
    table = jnp.array([src_of.get(j, 0) for j in range(PROJ_COLS // W_T_BLOCK)], jnp.int32)
    small = jnp.concatenate(small_rows, axis=1)
    small = jnp.pad(small, ((0, 0), (0, PROJ_COLS - OFF_SMALL - small.shape[1]), (0, 0)))
    return pl.pallas_call(
        _w_in_kernel,
        out_shape=jax.ShapeDtypeStruct((depth, D_MODEL, PROJ_COLS), BF16),
        grid_spec=pltpu.PrefetchScalarGridSpec(
            num_scalar_prefetch=1,
            grid=(depth, PROJ_COLS // W_T_BLOCK),
            in_specs=[pl.BlockSpec((pl.Element(1), pl.Element(W_T_BLOCK), pl.Element(D_MODEL)),
                                   lambda l, j, tbl: (l, pl.multiple_of(tbl[j], SUBLANES), 0)),
                      pl.BlockSpec((None, PROJ_COLS - OFF_SMALL, D_MODEL), lambda l, j, tbl: (l, 0, 0))],
            out_specs=pl.BlockSpec((None, D_MODEL, W_T_BLOCK), lambda l, j, tbl: (l, 0, j))),
        compiler_params=_params(("parallel", "arbitrary")),
        name="w_in_relayout",
    )(table, w_t, small)


def kernel(x, w_in, ssd_conv_w, ssd_conv_b, ssd_dt_bias, ssd_a_log, ssd_d, ssd_norm_w, gdn_conv_w, gdn_dt_bias, gdn_a_log, gdn_norm_w, hg_lb_logits, hg_norm_w, w_out, ln1_g, ln1_b, w_router_group, b_router_group, w_router_expert, b_router_expert, w_expert_gate_up, w_expert_down, ln2_g, ln2_b):
    nb, seq, d = x.shape
    t = nb * seq
    lb_cum = jnp.cumsum(jax.nn.softmax(hg_lb_logits.astype(F32), axis=0), axis=0)
    lb_all = lb_cum - lb_cum[0:1]
    h = x.reshape(t, d)
    hb = h.astype(BF16)
    w_proj = _split_w_in(w_in)
    w_out_b = w_out.astype(BF16)
    w_gu_b = w_expert_gate_up.astype(BF16)
    w_dn_b = w_expert_down.astype(BF16).reshape(DEPTH, MOE_GROUPS, EXPERTS_PER_GROUP * D_EXPERT, D_MODEL)
    for l in range(DEPTH):
        y = _mixers(hb, w_proj, l, nb, seq, ssd_conv_w[l], ssd_conv_b[l], ssd_dt_bias[l], ssd_a_log[l], ssd_d[l],
                    ssd_norm_w[l], gdn_conv_w[l], gdn_dt_bias[l], gdn_a_log[l], gdn_norm_w[l],
                    lb_all[l], hg_norm_w[l])
        w_router, b_router = _router_params(w_router_group[l], b_router_group[l],
                                            w_router_expert[l], b_router_expert[l])
        h1, h1b, comb = _outproj(y, h, w_out_b, l, ln1_g[l], ln1_b[l], w_router, b_router, 512)
        h, hb = _moe(h1b, h1, comb, w_gu_b, w_dn_b, l, ln2_g[l], ln2_b[l], 512)
    return h.reshape(nb, seq, d)
```

```python
import jax
import jax.numpy as jnp
from jax import lax
from jax.experimental import pallas as pl
from jax.experimental.pallas import tpu as pltpu

F32 = jnp.float32
BF16 = jnp.bfloat16

D_MODEL = 1024
DEPTH = 4
SSD_HEADS = 16
SSD_HEAD_DIM = 64
SSD_WIDTH = 1024
SSD_GROUPS = 2
SSD_STATE = 128
SSD_BC = 256
SSD_XBC = 1536
SSD_CONV = 4
GDN_HEADS = 4
GDN_HEAD_DIM = 128
GDN_WIDTH = 512
GDN_CONV = 4
HG_HEADS = 4
HG_HEAD_DIM = 128
HG_WIDTH = 512
D_MIX = 2048
MOE_GROUPS = 4
EXPERTS_PER_GROUP = 4
N_EXPERTS = 16
D_EXPERT = 256
DN_ALPHA = (2 * DEPTH) ** 0.25
LOG2E = 1.4426950408889634

LANES = 128
SUBLANES = 8
ROWS = 128
VMEM_LIMIT = 48 * 1024 * 1024
MOE_VMEM_LIMIT = 56 * 1024 * 1024
CONV_SLAB = 256
GDN_HI_LEVELS = 2
MOE_CHUNK = 160
MOE_ALIGN = 16
PROJ_SLAB = 768
HG_SLOW_ROUNDS = 12

OFF_XBC = 0
OFF_QKV = 1536
OFF_Z = 3072
OFF_GGATE = 4096
OFF_HQ = 4608
OFF_HF = 5120
OFF_HI = 5632
OFF_HGATE = 6144
OFF_SMALL = 6656
PROJ_COLS = 6912
SM_DT = 0
SM_B = 16
SM_A = 20
MIX_SSD = 0
MIX_GDN = SSD_WIDTH
MIX_HG = SSD_WIDTH + GDN_WIDTH


def _dot(a, b):
    return jnp.dot(a.astype(BF16), b.astype(BF16), preferred_element_type=F32)


def _dot_nt(a, b):
    return lax.dot_general(a.astype(BF16), b.astype(BF16), (((1,), (1,)), ((), ())),
                           preferred_element_type=F32)


def _dot_tn(a, b):
    return lax.dot_general(a.astype(BF16), b.astype(BF16), (((0,), (0,)), ((), ())),
                           preferred_element_type=F32)


def _split3(a):
    p1 = a.astype(BF16)
    r1 = a - p1.astype(F32)
    p2 = r1.astype(BF16)
    return p1, p2, (r1 - p2.astype(F32)).astype(BF16)


def _dot_01(m01, x):
    m = m01.astype(BF16)
    p1, p2, p3 = _split3(x)
    return (jnp.dot(m, p1, preferred_element_type=F32) + jnp.dot(m, p2, preferred_element_type=F32)
            + jnp.dot(m, p3, preferred_element_type=F32))


def _dot_x01(x, m01):
    m = m01.astype(BF16)
    p1, p2, p3 = _split3(x)
    return (jnp.dot(p1, m, preferred_element_type=F32) + jnp.dot(p2, m, preferred_element_type=F32)
            + jnp.dot(p3, m, preferred_element_type=F32))


def _split(a):
    hi = a.astype(BF16)
    return hi, (a - hi.astype(F32)).astype(BF16)


def _dot3(a, b):
    (ah, al), (bh, bl) = a, b
    return (jnp.dot(ah, bh, preferred_element_type=F32) + jnp.dot(ah, bl, preferred_element_type=F32)
            + jnp.dot(al, bh, preferred_element_type=F32))


def _sigmoid(x):
    return 1.0 / (1.0 + jnp.exp(-x))


def _silu(x):
    return x * _sigmoid(x)


def _softplus(x):
    return jnp.maximum(x, 0.0) + jnp.log(1.0 + jnp.exp(-jnp.abs(x)))


def _iota2(shape, dim):
    return lax.broadcasted_iota(jnp.int32, shape, dim)


def _params(sem):
    return pltpu.CompilerParams(dimension_semantics=sem, vmem_limit_bytes=VMEM_LIMIT)


def _matmul_kernel(x_ref, w_ref, o_ref):
    o_ref[...] = jnp.dot(x_ref[...], w_ref[...], preferred_element_type=F32)


def _matmul(x, w, tm, tn):
    t, k = x.shape
    n = w.shape[1]
    return pl.pallas_call(
        _matmul_kernel,
        out_shape=jax.ShapeDtypeStruct((t, n), F32),
        grid=(n // tn, t // tm),
        in_specs=[pl.BlockSpec((tm, k), lambda j, i: (i, 0)),
                  pl.BlockSpec((k, tn), lambda j, i: (0, j))],
        out_specs=pl.BlockSpec((tm, tn), lambda j, i: (i, j)),
        compiler_params=_params(("parallel", "arbitrary")),
        name="in_proj",
    )(x, w)


def _conv_silu_stages(x_ref, xpad_ref, xa_ref, cw_ref, cb_ref):
    rows, cols = x_ref.shape
    k = cw_ref.shape[0]
    for c0 in range(0, cols, CONV_SLAB):
        cs = slice(c0, c0 + CONV_SLAB)
        x = x_ref[:, cs]
        xpad_ref[SUBLANES:SUBLANES + rows, cs] = x
        acc = x * cw_ref[k - 1:k, cs]
        if cb_ref is not None:
            acc = acc + cb_ref[:, cs]
        for i in range(k - 1):
            off = SUBLANES - (k - 1) + i
            acc = acc + xpad_ref[off:off + rows, cs] * cw_ref[i:i + 1, cs]
        xa_ref[:, cs] = _silu(acc)
        xpad_ref[0:SUBLANES, cs] = xpad_ref[rows:rows + SUBLANES, cs]
        yield


def _ssd_stages(z_ref, xbc_ref, sm_ref, cw_ref, cb_ref, hp_ref, hpt_ref, dsk_ref, nw_ref,
                y_ref, xpad_ref, xa_ref, st_ref):
    yield from _conv_silu_stages(xbc_ref, xpad_ref, xa_ref, cw_ref, cb_ref)

    row = _iota2((ROWS, ROWS), 0)
    col = _iota2((ROWS, ROWS), 1)
    causal = row >= col
    tril = causal.astype(F32)
    triu = (row <= col).astype(F32)

    sm = sm_ref[...]
    dt = _softplus(sm + hp_ref[0:1, :])
    da = dt * (-jnp.exp(hp_ref[1:2, :]))
    acum = _dot_01(tril, da)
    smt = sm.T
    dtt = _softplus(smt[0:SSD_HEADS, :] + hpt_ref[0:SSD_HEADS, 0:1])
    dat = dtt * (-jnp.exp(hpt_ref[0:SSD_HEADS, 1:2]))
    acumt = _dot_x01(dat, triu)

    heads3 = jnp.concatenate([dt, jnp.exp(acum), jnp.exp(acum[ROWS - 1:ROWS, :] - acum)], axis=0)
    h_hi, h_lo = _split(heads3)
    heads6 = jnp.concatenate([h_hi, h_lo], axis=0)
    lane_lo = _iota2((ROWS, LANES), 1) < SSD_HEAD_DIM
    yield

    hg = SSD_HEADS // SSD_GROUPS
    gw = SSD_WIDTH // SSD_GROUPS
    hrow = _iota2((LANES, gw), 0)
    hcol = _iota2((LANES, gw), 1)
    for g in range(SSD_GROUPS):
        gs = slice(g * gw, (g + 1) * gw)
        expand = (((hcol >> 6) + g * hg) == hrow).astype(BF16)
        ex = jnp.dot(heads6, expand, preferred_element_type=F32)
        ex = ex[0:3 * ROWS, :] + ex[3 * ROWS:6 * ROWS, :]
        dtx = ex[0:ROWS, :]
        eax = ex[ROWS:2 * ROWS, :]
        tex = ex[2 * ROWS:3 * ROWS, :]
        yield
        xs = xa_ref[:, gs]
        xdt = xs * dtx
        xdt_b = xdt.astype(BF16)
        xend_b = (xdt * tex).astype(BF16)
        bm = xa_ref[:, SSD_WIDTH + g * SSD_STATE:SSD_WIDTH + (g + 1) * SSD_STATE]
        cm = xa_ref[:, SSD_WIDTH + SSD_BC + g * SSD_STATE:SSD_WIDTH + SSD_BC + (g + 1) * SSD_STATE]
        cm_b = cm.astype(BF16)
        cb = _dot_nt(cm_b, bm)
        yield
        yd = []
        for pair in range(hg // 2):
            res = []
            for sub in range(2):
                h = g * hg + pair * 2 + sub
                seg = acum[:, h:h + 1] - acumt[h:h + 1, :]
                lmat = cb * jnp.exp(jnp.where(causal, seg, -jnp.inf))
                c0 = pair * 2 * SSD_HEAD_DIM
                res.append(jnp.dot(lmat.astype(BF16), xdt_b[:, c0:c0 + LANES],
                                   preferred_element_type=F32))
            yd.append(jnp.where(lane_lo, res[0], res[1]))
            yield
        yd = jnp.concatenate(yd, axis=1)
        st = st_ref[g]
        yoff = jnp.dot(cm_b, st.astype(BF16), preferred_element_type=F32) * eax
        st_ref[g] = (st * eax[ROWS - 1:ROWS, :]
                     + jnp.dot(bm.T.astype(BF16), xend_b, preferred_element_type=F32))
        yield
        y = yd + yoff + xs * dsk_ref[:, gs]
        y = y * _silu(z_ref[:, gs])
        ms = jnp.mean(y * y, axis=-1, keepdims=True)
        y_ref[:, MIX_SSD + g * gw:MIX_SSD + (g + 1) * gw] = (
            y * lax.rsqrt(ms + 1e-6) * nw_ref[:, gs]).astype(y_ref.dtype)
        yield


def _gdn_stages(qkv_ref, gate_ref, sm_ref, cw_ref, hp_ref, hpt_ref, nw_ref, y_ref, xpad_ref, xa_ref, st_ref):
    yield from _conv_silu_stages(qkv_ref, xpad_ref, xa_ref, cw_ref, None)

    row = _iota2((ROWS, ROWS), 0)
    col = _iota2((ROWS, ROWS), 1)
    incl = row >= col
    strict = row > col
    tril = incl.astype(F32)
    triu = (row <= col).astype(F32)

    sm = sm_ref[...]
    la = -jnp.exp(hp_ref[1:2, :]) * _softplus(sm + hp_ref[0:1, :])
    gcum = _dot_01(tril, la)
    smt = sm.T
    lat = (-jnp.exp(hpt_ref[SM_B:SM_B + SUBLANES, 1:2])
           * _softplus(smt[SM_B:SM_B + SUBLANES, :] + hpt_ref[SM_B:SM_B + SUBLANES, 0:1]))
    gcumt = _dot_x01(lat, triu)
    beta_all = _sigmoid(sm)
    yield

    dh = GDN_HEAD_DIM
    heads = range(GDN_HEADS)
    qs, ks, xs, rhss, decays, gs = [], [], [], [], [], []
    for h in heads:
        q = xa_ref[:, h * dh:(h + 1) * dh]
        k = xa_ref[:, GDN_WIDTH + h * dh:GDN_WIDTH + (h + 1) * dh]
        v = xa_ref[:, 2 * GDN_WIDTH + h * dh:2 * GDN_WIDTH + (h + 1) * dh]
        q = q * lax.rsqrt(jnp.sum(q * q, axis=-1, keepdims=True) + 1e-6) * (dh ** -0.5)
        k = k * lax.rsqrt(jnp.sum(k * k, axis=-1, keepdims=True) + 1e-6)
        g = gcum[:, SM_A + h:SM_A + h + 1]
        gt = gcumt[SM_A - SM_B + h:SM_A - SM_B + h + 1, :]
        beta = beta_all[:, SM_B + h:SM_B + h + 1]
        decay = jnp.exp(jnp.where(incl, g - gt, -jnp.inf))
        kb = k * beta
        xs.append(-jnp.where(strict, _dot_nt(kb, k) * decay, 0.0))
        rhss.append(jnp.concatenate([v * beta, kb * jnp.exp(g)], axis=1))
        qs.append(q)
        ks.append(k)
        decays.append(decay)
        gs.append(g)
        yield
    ps = xs
    ns = xs
    levels = ROWS.bit_length() - 2
    for j in range(levels):
        if j < GDN_HI_LEVELS:
            pp = [_split(p) for p in ps]
            ps = [_dot3(p, p) for p in pp]
            yield
            pp = [_split(p) for p in ps]
            ns = [n + p + _dot3(_split(n), p2) for n, p, p2 in zip(ns, ps, pp)]
        else:
            ps = [_dot(p, p) for p in ps]
            yield
            ns = [n + p + _dot(n, p) for n, p in zip(ns, ps)]
        yield
    sols = [r + _dot(n, r) for n, r in zip(ns, rhss)]
    yield
    qks = [_dot_nt(q, k) * d for q, k, d in zip(qs, ks, decays)]
    sts = [st_ref[h] for h in heads]
    v_news = [s[:, 0:dh] - _dot(s[:, dh:2 * dh], st) for s, st in zip(sols, sts)]
    yield
    os_ = [_dot(q * jnp.exp(g), st) + _dot(qk, vn) for q, g, st, qk, vn in zip(qs, gs, sts, qks, v_news)]
    yield
    for h in heads:
        glast = gs[h][ROWS - 1:ROWS, :]
        kd = ks[h] * jnp.exp(glast - gs[h])
        st_ref[h] = sts[h] * jnp.exp(glast) + _dot(kd.T, v_news[h])
    yield
    for h in heads:
        o = os_[h]
        o = o * lax.rsqrt(jnp.mean(o * o, axis=-1, keepdims=True) + 1e-6)
        y_ref[:, MIX_GDN + h * dh:MIX_GDN + (h + 1) * dh] = (
            o * nw_ref[...] * _silu(gate_ref[:, h * dh:(h + 1) * dh])).astype(y_ref.dtype)
        yield


def _hgrn_stages(q_ref, f_ref, i_ref, gate_ref, lb_ref, nw_ref, y_ref, st_ref):
    row = _iota2((ROWS, ROWS), 0)
    col = _iota2((ROWS, ROWS), 1)
    tril = (row >= col).astype(F32)
    rmod = _iota2((ROWS, LANES), 0) & (SUBLANES - 1)
    rfull = _iota2((ROWS, LANES), 0)
    dk = HG_HEAD_DIM
    level_masks = {}
    m = 1
    while m < ROWS:
        right = (rfull & m) != 0
        parent = ~(2 * m - 1)
        keep = (((row & m) ^ m) | (col & m) | ((row ^ col) & parent)) == 0
        level_masks[m] = (right, jnp.where(right, LOG2E, -LOG2E), keep)
        m *= 2
    for h in range(HG_HEADS):
        sl = slice(h * dk, (h + 1) * dk)
        lb = lb_ref[:, sl]
        fr = f_ref[:, sl]
        log_sig = jnp.minimum(fr, 0.0) - jnp.log(1.0 + jnp.exp(-jnp.abs(fr)))
        a = jnp.log(lb)
        y = jnp.log1p(-lb) + log_sig
        log_f = jnp.maximum(a, y) + jnp.log(1.0 + jnp.exp(-jnp.abs(a - y)))
        k = (1.0 - lb) * _sigmoid(-fr)
        q = _silu(q_ref[:, sl])
        v = i_ref[:, sl]
        b = _dot_01(tril, log_f)
        yield

        st = st_ref[h]
        o = _dot_nt(q * jnp.exp(b), st)
        blast = b[ROWS - 1:ROWS, :]
        st_ref[h] = st * jnp.exp(blast) + _dot_tn(v, k * jnp.exp(blast - b))
        yield

        pmat = jnp.zeros((ROWS, ROWS), F32)
        b3 = b.reshape(ROWS // SUBLANES, SUBLANES, dk)
        m = 1
        while m < ROWS:
            if m < SUBLANES:
                ref = None
                for p0 in range(0, SUBLANES, 2 * m):
                    cand = jnp.broadcast_to(b3[:, p0 + m - 1:p0 + m, :], b3.shape).reshape(ROWS, dk)
                    ref = cand if ref is None else jnp.where(rmod >= p0, cand, ref)
            else:
                ref = jnp.concatenate(
                    [jnp.broadcast_to(b[p0 + m - 1:p0 + m, :], (2 * m, dk)) for p0 in range(0, ROWS, 2 * m)],
                    axis=0)
            right, scale, keep = level_masks[m]
            z = jnp.where(right, q, k) * jnp.exp2((b - ref) * scale)
            zb = z.astype(BF16)
            pmat = pmat + jnp.where(keep, _dot_nt(zb, zb), 0.0)
            m *= 2
            yield
        o = o + _dot(pmat, v) + jnp.sum(q * k, axis=-1, keepdims=True) * v
        yield

        o = o * lax.rsqrt(jnp.mean(o * o, axis=-1, keepdims=True) + 1e-6)
        y_ref[:, MIX_HG + h * dk:MIX_HG + (h + 1) * dk] = (
            o * nw_ref[...] * _silu(gate_ref[:, sl])).astype(y_ref.dtype)
        yield


def _in_proj_stages(hb_ref, w_ref, out_ref):
    hb = hb_ref[...]
    for c0 in range(0, PROJ_COLS, PROJ_SLAB):
        out_ref[:, c0:c0 + PROJ_SLAB] = jnp.dot(hb, w_ref[:, c0:c0 + PROJ_SLAB], preferred_element_type=F32)
        yield


def _mixer_kernel(hb0_ref, hbn_ref, w_ref,
                  scw_ref, scb_ref, shp_ref, shpt_ref, dsk_ref, snw_ref,
                  gcw_ref, ghp_ref, ghpt_ref, gnw_ref, lb_ref, hnw_ref,
                  y_ref, proj_ref, sxpad_ref, sxa_ref, sst_ref, gxpad_ref, gxa_ref, gst_ref, hst_ref):
    c = pl.program_id(1)

    @pl.when(c == 0)
    def _():
        sst_ref[...] = jnp.zeros(sst_ref.shape, F32)
        gst_ref[...] = jnp.zeros(gst_ref.shape, F32)
        hst_ref[...] = jnp.zeros(hst_ref.shape, F32)
        sxpad_ref[0:SUBLANES, :] = jnp.zeros((SUBLANES, sxpad_ref.shape[1]), F32)
        gxpad_ref[0:SUBLANES, :] = jnp.zeros((SUBLANES, gxpad_ref.shape[1]), F32)
        for _ in _in_proj_stages(hb0_ref, w_ref, proj_ref.at[0]):
            pass

    cur = proj_ref.at[c % 2]
    nxt = proj_ref.at[(c + 1) % 2]

    def seg(off, width):
        return cur.at[:, off:off + width]

    sm_ref = seg(OFF_SMALL, LANES)
    streams = [
        _gdn_stages(seg(OFF_QKV, 3 * GDN_WIDTH), seg(OFF_GGATE, GDN_WIDTH), sm_ref, gcw_ref, ghp_ref, ghpt_ref,
                    gnw_ref, y_ref, gxpad_ref, gxa_ref, gst_ref),
        _ssd_stages(seg(OFF_Z, SSD_WIDTH), seg(OFF_XBC, SSD_XBC), sm_ref, scw_ref, scb_ref, shp_ref, shpt_ref,
                    dsk_ref, snw_ref, y_ref, sxpad_ref, sxa_ref, sst_ref),
        _hgrn_stages(seg(OFF_HQ, HG_WIDTH), seg(OFF_HF, HG_WIDTH), seg(OFF_HI, HG_WIDTH),
                     seg(OFF_HGATE, HG_WIDTH), lb_ref, hnw_ref, y_ref, hst_ref),
        _in_proj_stages(hbn_ref, w_ref, nxt),
    ]
    live = [True] * len(streams)
    rnd = 0
    while any(live):
        advance = [1, 1, 1 if rnd < HG_SLOW_ROUNDS else 2, 1]
        rnd += 1
        for s, stream in enumerate(streams):
            for _ in range(advance[s]):
                if live[s]:
                    try:
                        next(stream)
                    except StopIteration:
                        live[s] = False


def _mixers(hb, w_proj, layer, nb, seq, ssd_conv_w, ssd_conv_b, ssd_dt_bias, ssd_a_log, ssd_d, ssd_norm_w,
            gdn_conv_w, gdn_dt_bias, gdn_a_log, gdn_norm_w, lb, hg_norm_w):
    nblk = seq // ROWS
    shp = jnp.zeros((SUBLANES, LANES), F32)
    shp = shp.at[0, SM_DT:SM_DT + SSD_HEADS].set(ssd_dt_bias).at[1, SM_DT:SM_DT + SSD_HEADS].set(ssd_a_log)
    ghp = jnp.zeros((SUBLANES, LANES), F32)
    ghp = ghp.at[0, SM_A:SM_A + GDN_HEADS].set(gdn_dt_bias).at[1, SM_A:SM_A + GDN_HEADS].set(gdn_a_log)
    dsk = jnp.repeat(ssd_d, SSD_HEAD_DIM)[None, :]
    const = lambda b, c: (0, 0)

    def whole(shape):
        return pl.BlockSpec(shape, const)

    return pl.pallas_call(
        _mixer_kernel,
        out_shape=jax.ShapeDtypeStruct((nb * seq, D_MIX), BF16),
        grid=(nb, nblk),
        in_specs=[
            pl.BlockSpec((ROWS, D_MODEL), lambda b, c: (b * nblk + c, 0)),
            pl.BlockSpec((ROWS, D_MODEL), lambda b, c: (b * nblk + jnp.minimum(c + 1, nblk - 1), 0)),
            pl.BlockSpec((None, D_MODEL, PROJ_COLS), lambda b, c: (layer, 0, 0), pipeline_mode=pl.Buffered(1)),
            whole((SSD_CONV, SSD_XBC)), whole((1, SSD_XBC)), whole((SUBLANES, LANES)), whole((LANES, SUBLANES)),
            whole((1, SSD_WIDTH)), whole((1, SSD_WIDTH)),
            whole((GDN_CONV, 3 * GDN_WIDTH)), whole((SUBLANES, LANES)), whole((LANES, SUBLANES)),
            whole((1, GDN_HEAD_DIM)), whole((1, HG_WIDTH)), whole((1, HG_HEAD_DIM)),
        ],
        out_specs=pl.BlockSpec((ROWS, D_MIX), lambda b, c: (b * nblk + c, 0)),
        scratch_shapes=[pltpu.VMEM((2, ROWS, PROJ_COLS), F32),
                        pltpu.VMEM((ROWS + SUBLANES, SSD_XBC), F32),
                        pltpu.VMEM((ROWS, SSD_XBC), F32),
                        pltpu.VMEM((SSD_GROUPS, SSD_STATE, SSD_WIDTH // SSD_GROUPS), F32),
                        pltpu.VMEM((ROWS + SUBLANES, 3 * GDN_WIDTH), F32),
                        pltpu.VMEM((ROWS, 3 * GDN_WIDTH), F32),
                        pltpu.VMEM((GDN_HEADS, GDN_HEAD_DIM, GDN_HEAD_DIM), F32),
                        pltpu.VMEM((HG_HEADS, HG_HEAD_DIM, HG_HEAD_DIM), F32)],
        compiler_params=_params(("parallel", "arbitrary")),
        name="in_proj_mixers",
    )(hb, hb, w_proj,
      ssd_conv_w, ssd_conv_b[None, :], shp, shp.T, dsk, ssd_norm_w[None, :],
      gdn_conv_w, ghp, ghp.T, gdn_norm_w[None, :], lb[None, :], hg_norm_w[None, :])


def _layer_norm(x, g, b):
    mu = jnp.mean(x, axis=-1, keepdims=True)
    xc = x - mu
    var = jnp.mean(xc * xc, axis=-1, keepdims=True)
    return xc * lax.rsqrt(var + 1e-5) * g + b


ROUTER_ROWS = 32
OUT_SLAB = 256


def _outproj_rows(mix, h, g_ref, b_ref, wr_ref, br_ref):
    h1 = _layer_norm(DN_ALPHA * h + mix, g_ref[...], b_ref[...])

    h_hi, h_lo = _split(h1)
    wr = wr_ref[...]
    lt = _dot_nt(wr, h_hi) + _dot_nt(wr, h_lo)
    lt = lt[0:ROUTER_ROWS, :] + lt[ROUTER_ROWS:2 * ROUTER_ROWS, :] + br_ref[...]
    glog = [lt[g:g + 1, :] for g in range(MOE_GROUPS)]
    elog = [lt[MOE_GROUPS + e:MOE_GROUPS + e + 1, :] for e in range(N_EXPERTS)]

    def softmax(rows):
        m = rows[0]
        for r in rows[1:]:
            m = jnp.maximum(m, r)
        ex = [jnp.exp(r - m) for r in rows]
        tot = ex[0]
        for r in ex[1:]:
            tot = tot + r
        return [r / tot for r in ex]

    def top1(rows):
        best = rows[0]
        for r in rows[1:]:
            best = jnp.maximum(best, r)
        idx = jnp.full_like(best, float(len(rows) - 1))
        for j in range(len(rows) - 2, -1, -1):
            idx = jnp.where(rows[j] == best, float(j), idx)
        return best, idx

    g_p, g_idx = top1(softmax(glog))
    chosen = []
    for k in range(EXPERTS_PER_GROUP):
        acc = jnp.zeros_like(g_p)
        for g in range(MOE_GROUPS):
            acc = jnp.where(g_idx == float(g), elog[g * EXPERTS_PER_GROUP + k], acc)
        chosen.append(acc)
    eprob = softmax(chosen)
    p1, i1 = top1(eprob)
    p2, i2 = top1([jnp.where(i1 == float(k), -1.0, eprob[k]) for k in range(EXPERTS_PER_GROUP)])
    denom = p1 + p2
    w1 = g_p * p1 / denom
    w2 = g_p * p2 / denom
    wk = [jnp.where(i1 == float(k), w1, 0.0) + jnp.where(i2 == float(k), w2, 0.0)
          for k in range(EXPERTS_PER_GROUP)]
    rid = _iota2((ROUTER_ROWS, lt.shape[1]), 0)
    comb_t = jnp.where(rid == N_EXPERTS, g_idx, 0.0)
    for g in range(MOE_GROUPS):
        for k in range(EXPERTS_PER_GROUP):
            comb_t = jnp.where(rid == g * EXPERTS_PER_GROUP + k,
                               jnp.where(g_idx == float(g), wk[k], 0.0), comb_t)
    comb_t = jnp.concatenate([comb_t, jnp.zeros((LANES - ROUTER_ROWS, lt.shape[1]), F32)], axis=0)
    return h1, comb_t.T


def _outproj_kernel(y_ref, h_ref, w_ref, g_ref, b_ref, wr_ref, br_ref, h1_ref, h1b_ref, comb_ref):
    slab = min(OUT_SLAB, y_ref.shape[0])
    slabs = [slice(r0, r0 + slab) for r0 in range(0, y_ref.shape[0], slab)]
    mixes = [jnp.dot(y_ref[rows, :], w_ref[...], preferred_element_type=F32) for rows in slabs]
    for rows, mix in zip(slabs, mixes):
        h1, comb = _outproj_rows(mix, h_ref[rows, :], g_ref, b_ref, wr_ref, br_ref)
        h1_ref[rows, :] = h1
        h1b_ref[rows, :] = h1.astype(BF16)
        comb_ref[rows, :] = comb


def _router_params(w_group, b_group, w_expert, b_expert):
    pad = ROUTER_ROWS - MOE_GROUPS - N_EXPERTS
    w_t = jnp.pad(jnp.concatenate([w_group, w_expert], axis=1).T, ((0, pad), (0, 0)))
    b_col = jnp.pad(jnp.concatenate([b_group, b_expert]), (0, pad))[:, None]
    return jnp.concatenate(_split(w_t), axis=0), b_col


def _outproj(y, h, w_out, layer, ln_g, ln_b, w_router, b_router, tm):
    t = h.shape[0]
    rowmap = lambda i: (i, 0)
    const = lambda i: (0, 0)
    return pl.pallas_call(
        _outproj_kernel,
        out_shape=(jax.ShapeDtypeStruct((t, D_MODEL), F32),
                   jax.ShapeDtypeStruct((t, D_MODEL), BF16),
                   jax.ShapeDtypeStruct((t, LANES), F32)),
        grid=(t // tm,),
        in_specs=[pl.BlockSpec((tm, D_MIX), rowmap),
                  pl.BlockSpec((tm, D_MODEL), rowmap),
                  pl.BlockSpec((None, D_MIX, D_MODEL), lambda i: (layer, 0, 0)),
                  pl.BlockSpec((1, D_MODEL), const),
                  pl.BlockSpec((1, D_MODEL), const),
                  pl.BlockSpec((2 * ROUTER_ROWS, D_MODEL), const),
                  pl.BlockSpec((ROUTER_ROWS, 1), const)],
        out_specs=(pl.BlockSpec((tm, D_MODEL), rowmap),
                   pl.BlockSpec((tm, D_MODEL), rowmap),
                   pl.BlockSpec((tm, LANES), rowmap)),
        compiler_params=_params(("parallel",)),
        name="out_proj_ln_router",
    )(y, h, w_out, ln_g[None, :], ln_b[None, :], w_router, b_router)


def _prefix_lanes(v, idx):
    axis = 1 if v.shape[0] == 1 else 0
    out = jnp.zeros_like(v)
    for g in range(MOE_GROUPS - 1):
        vg = v[:, g:g + 1] if axis == 1 else v[g:g + 1, :]
        out = out + jnp.where(idx > g, vg, 0.0)
    return out


def _moe_kernel(hb_ref, h_ref, comb_ref, wgu_ref, wdn_ref, g_ref, b_ref, o_ref, ob_ref,
                xs_ref, cw_ref, ys_ref):
    tm = hb_ref.shape[0]
    comb = comb_ref[...]
    lane = _iota2((tm, LANES), 1).astype(F32)
    gid = comb[:, N_EXPERTS:N_EXPERTS + 1]
    gsel = (lane == gid).astype(F32)
    tr = _iota2((tm, tm), 0)
    tc = _iota2((tm, tm), 1)

    ns = xs_ref.shape[0] - MOE_CHUNK

    def aligned(cnt):
        return jnp.floor((cnt + (MOE_ALIGN - 1)) * (1.0 / MOE_ALIGN)) * MOE_ALIGN

    rank_c = _dot((tc < tr).astype(F32), gsel)
    cnt_r = jnp.sum(gsel, axis=0, keepdims=True)
    start_r = _prefix_lanes(aligned(cnt_r), _iota2((1, LANES), 1))
    dest_c = jnp.sum(gsel * (start_r + rank_c), axis=1, keepdims=True)
    eye8 = (_iota2((SUBLANES, LANES), 0) == _iota2((SUBLANES, LANES), 1)).astype(F32)
    gsel_t = _dot_nt(eye8, gsel)
    rank_r = _dot(gsel_t, (tr < tc).astype(F32))
    cnt_c = jnp.sum(gsel_t, axis=1, keepdims=True)
    start_c = _prefix_lanes(aligned(cnt_c), _iota2((SUBLANES, 1), 0))
    dest_r = jnp.sum(gsel_t * (start_c + rank_r), axis=0, keepdims=True)

    perm = (dest_r == _iota2((ns, tm), 0).astype(F32)).astype(BF16)
    xs_ref[0:ns, :] = jnp.dot(perm, hb_ref[...], preferred_element_type=F32).astype(BF16)
    xs_ref[ns:ns + MOE_CHUNK, :] = jnp.zeros((MOE_CHUNK, D_MODEL), BF16)
    c_hi, c_lo = _split(comb)
    cw2 = jnp.dot(perm, jnp.concatenate([c_hi, c_lo], axis=1), preferred_element_type=F32)
    cw_ref[0:ns, :] = cw2[:, 0:LANES] + cw2[:, LANES:2 * LANES]
    cw_ref[ns:ns + MOE_CHUNK, :] = jnp.zeros((MOE_CHUNK, LANES), F32)
    ys_ref[...] = jnp.zeros(ys_ref.shape, F32)

    lane1 = _iota2((1, LANES), 1)
    starts = [jnp.sum(jnp.where(lane1 == g, start_r, 0.0)).astype(jnp.int32) for g in range(MOE_GROUPS)]
    cnts = [jnp.sum(jnp.where(lane1 == g, cnt_r, 0.0)) for g in range(MOE_GROUPS)]
    def block(g, j):
        rows = pl.ds(pl.multiple_of(starts[g] + j * MOE_CHUNK, MOE_ALIGN), MOE_CHUNK)
        x = xs_ref[rows, :]
        cw = cw_ref[rows, :]
        hm = []
        for e in range(EXPERTS_PER_GROUP):
            ex = g * EXPERTS_PER_GROUP + e
            gu = jnp.dot(x, wgu_ref[ex], preferred_element_type=F32)
            hm.append((_silu(gu[:, 0:D_EXPERT]) * gu[:, D_EXPERT:2 * D_EXPERT] * cw[:, ex:ex + 1]).astype(BF16))
        ys_ref[rows, :] += jnp.dot(jnp.concatenate(hm, axis=1), wdn_ref[g], preferred_element_type=F32)

    for g in range(MOE_GROUPS):
        block(g, 0)

    most = cnts[0]
    for g in range(1, MOE_GROUPS):
        most = jnp.maximum(most, cnts[g])

    @pl.when(most > MOE_CHUNK)
    def _():
        for j in range(1, -(-tm // MOE_CHUNK)):
            for g in range(MOE_GROUPS):
                pl.when(cnts[g] > j * MOE_CHUNK)(lambda g=g, j=j: block(g, j))

    unperm = (dest_c == _iota2((tm, ns), 1).astype(F32)).astype(BF16)
    y = jnp.dot(unperm, ys_ref[0:ns, :].astype(BF16), preferred_element_type=F32)
    h2 = _layer_norm(DN_ALPHA * h_ref[...] + y, g_ref[...], b_ref[...])
    o_ref[...] = h2
    ob_ref[...] = h2.astype(BF16)


def _moe(hb, h, comb, w_gu, w_dn, layer, ln_g, ln_b, tm):
    t = h.shape[0]
    rowmap = lambda i: (i, 0)
    const = lambda i: (0, 0)
    const3 = lambda i: (layer, 0, 0, 0)
    sorted_rows = -(-(tm + MOE_GROUPS * MOE_ALIGN) // LANES) * LANES + MOE_CHUNK
    return pl.pallas_call(
        _moe_kernel,
        out_shape=(jax.ShapeDtypeStruct((t, D_MODEL), F32),
                   jax.ShapeDtypeStruct((t, D_MODEL), BF16)),
        grid=(t // tm,),
        in_specs=[pl.BlockSpec((tm, D_MODEL), rowmap),
                  pl.BlockSpec((tm, D_MODEL), rowmap),
                  pl.BlockSpec((tm, LANES), rowmap),
                  pl.BlockSpec((None, N_EXPERTS, D_MODEL, 2 * D_EXPERT), const3, pipeline_mode=pl.Buffered(1)),
                  pl.BlockSpec((None, MOE_GROUPS, EXPERTS_PER_GROUP * D_EXPERT, D_MODEL), const3,
                               pipeline_mode=pl.Buffered(1)),
                  pl.BlockSpec((1, D_MODEL), const),
                  pl.BlockSpec((1, D_MODEL), const)],
        out_specs=(pl.BlockSpec((tm, D_MODEL), rowmap),
                   pl.BlockSpec((tm, D_MODEL), rowmap)),
        scratch_shapes=[pltpu.VMEM((sorted_rows, D_MODEL), BF16),
                        pltpu.VMEM((sorted_rows, LANES), F32),
                        pltpu.VMEM((sorted_rows, D_MODEL), F32)],
        compiler_params=pltpu.CompilerParams(dimension_semantics=("parallel",),
                                             vmem_limit_bytes=MOE_VMEM_LIMIT),
        name="moe_ln",
    )(hb, h, comb, w_gu, w_dn, ln_g[None, :], ln_b[None, :])


W_IN_SEGMENTS = ((SSD_WIDTH, OFF_Z), (SSD_XBC, OFF_XBC), (SSD_HEADS, OFF_SMALL + SM_DT),
                 (3 * GDN_WIDTH, OFF_QKV), (GDN_WIDTH, OFF_GGATE), (GDN_HEADS, OFF_SMALL + SM_B),
                 (GDN_HEADS, OFF_SMALL + SM_A), (HG_WIDTH, OFF_HQ), (HG_WIDTH, OFF_HF),
                 (HG_WIDTH, OFF_HI), (HG_WIDTH, OFF_HGATE))
W_T_BLOCK = 256
W_MAIN_BLOCKS = OFF_SMALL // W_T_BLOCK


def _w_in_kernel(tbl_ref, w_ref, small_ref, o_ref):
    j = pl.program_id(1)

    @pl.when(j < W_MAIN_BLOCKS)
    def _():
        o_ref[...] = w_ref[0].T.astype(BF16)

    @pl.when(j >= W_MAIN_BLOCKS)
    def _():
        o_ref[...] = small_ref[...].T.astype(BF16)


def _split_w_in(w_in):
    depth = w_in.shape[0]
    w_t = jnp.swapaxes(w_in, 1, 2)
    src_of, small_rows, src = {}, [], 0
    for width, dst in W_IN_SEGMENTS:
        if width >= W_T_BLOCK:
            for off in range(0, width, W_T_BLOCK):
                src_of[(dst + off) // W_T_BLOCK] = src + off
        else:
            small_rows.append(w_t[:, src:src + width, :])
        src += width
    table = jnp.array([src_of.get(j, 0) for j in range(PROJ_COLS // W_T_BLOCK)], jnp.int32)
    small = jnp.concatenate(small_rows, axis=1)
    small = jnp.pad(small, ((0, 0), (0, PROJ_COLS - OFF_SMALL - small.shape[1]), (0, 0)))
    return pl.pallas_call(
        _w_in_kernel,
        out_shape=jax.ShapeDtypeStruct((depth, D_MODEL, PROJ_COLS), BF16),
        grid_spec=pltpu.PrefetchScalarGridSpec(
            num_scalar_prefetch=1,
            grid=(depth, PROJ_COLS // W_T_BLOCK),
            in_specs=[pl.BlockSpec((pl.Element(1), pl.Element(W_T_BLOCK), pl.Element(D_MODEL)),
                                   lambda l, j, tbl: (l, pl.multiple_of(tbl[j], SUBLANES), 0)),
                      pl.BlockSpec((None, PROJ_COLS - OFF_SMALL, D_MODEL), lambda l, j, tbl: (l, 0, 0))],
            out_specs=pl.BlockSpec((None, D_MODEL, W_T_BLOCK), lambda l, j, tbl: (l, 0, j))),
        compiler_params=_params(("parallel", "arbitrary")),
        name="w_in_relayout",
    )(table, w_t, small)


def kernel(x, w_in, ssd_conv_w, ssd_conv_b, ssd_dt_bias, ssd_a_log, ssd_d, ssd_norm_w, gdn_conv_w, gdn_dt_bias, gdn_a_log, gdn_norm_w, hg_lb_logits, hg_norm_w, w_out, ln1_g, ln1_b, w_router_group, b_router_group, w_router_expert, b_router_expert, w_expert_gate_up, w_expert_down, ln2_g, ln2_b):
    nb, seq, d = x.shape
    t = nb * seq
    lb_cum = jnp.cumsum(jax.nn.softmax(hg_lb_logits.astype(F32), axis=0), axis=0)
    lb_all = lb_cum - lb_cum[0:1]
    h = x.reshape(t, d)
    hb = h.astype(BF16)
    w_proj = _split_w_in(w_in)
    w_out_b = w_out.astype(BF16)
    w_gu_b = w_expert_gate_up.astype(BF16)
    w_dn_b = w_expert_down.astype(BF16).reshape(DEPTH, MOE_GROUPS, EXPERTS_PER_GROUP * D_EXPERT, D_MODEL)
    for l in range(DEPTH):
        y = _mixers(hb, w_proj, l, nb, seq, ssd_conv_w[l], ssd_conv_b[l], ssd_dt_bias[l], ssd_a_log[l], ssd_d[l],
                    ssd_norm_w[l], gdn_conv_w[l], gdn_dt_bias[l], gdn_a_log[l], gdn_norm_w[l],
                    lb_all[l], hg_norm_w[l])
        w_router, b_router = _router_params(w_router_group[l], b_router_group[l],
                                            w_router_expert[l], b_router_expert[l])
        h1, h1b, comb = _outproj(y, h, w_out_b, l, ln1_g[l], ln1_b[l], w_router, b_router, 1024)
        h, hb = _moe(h1b, h1, comb, w_gu_b, w_dn_b, l, ln2_g[l], ln2_b[l], 512)
    return h.reshape(nb, seq, d)
```

```python
import jax
import jax.numpy as jnp
from jax import lax
from jax.experimental import pallas as pl
from jax.experimental.pallas import tpu as pltpu

F32 = jnp.float32
BF16 = jnp.bfloat16

D_MODEL = 1024
DEPTH = 4
SSD_HEADS = 16
SSD_HEAD_DIM = 64
SSD_WIDTH = 1024
SSD_GROUPS = 2
SSD_STATE = 128
SSD_BC = 256
SSD_XBC = 1536
SSD_CONV = 4
GDN_HEADS = 4
GDN_HEAD_DIM = 128
GDN_WIDTH = 512
GDN_CONV = 4
HG_HEADS = 4
HG_HEAD_DIM = 128
HG_WIDTH = 512
D_MIX = 2048
MOE_GROUPS = 4
EXPERTS_PER_GROUP = 4
N_EXPERTS = 16
D_EXPERT = 256
DN_ALPHA = (2 * DEPTH) ** 0.25
LOG2E = 1.4426950408889634

LANES = 128
SUBLANES = 8
ROWS = 128
VMEM_LIMIT = 48 * 1024 * 1024
MOE_VMEM_LIMIT = 56 * 1024 * 1024
CONV_SLAB = 256
GDN_HI_LEVELS = 2
MOE_CHUNK = 160
MOE_ALIGN = 16
PROJ_SLAB = 768
HG_SLOW_ROUNDS = 12

OFF_XBC = 0
OFF_QKV = 1536
OFF_Z = 3072
OFF_GGATE = 4096
OFF_HQ = 4608
OFF_HF = 5120
OFF_HI = 5632
OFF_HGATE = 6144
OFF_SMALL = 6656
PROJ_COLS = 6912
SM_DT = 0
SM_B = 16
SM_A = 20
MIX_SSD = 0
MIX_GDN = SSD_WIDTH
MIX_HG = SSD_WIDTH + GDN_WIDTH


def _dot(a, b):
    return jnp.dot(a.astype(BF16), b.astype(BF16), preferred_element_type=F32)


def _dot_nt(a, b):
    return lax.dot_general(a.astype(BF16), b.astype(BF16), (((1,), (1,)), ((), ())),
                           preferred_element_type=F32)


def _dot_tn(a, b):
    return lax.dot_general(a.astype(BF16), b.astype(BF16), (((0,), (0,)), ((), ())),
                           preferred_element_type=F32)


def _split3(a):
    p1 = a.astype(BF16)
    r1 = a - p1.astype(F32)
    p2 = r1.astype(BF16)
    return p1, p2, (r1 - p2.astype(F32)).astype(BF16)


def _dot_01(m01, x):
    m = m01.astype(BF16)
    p1, p2, p3 = _split3(x)
    return (jnp.dot(m, p1, preferred_element_type=F32) + jnp.dot(m, p2, preferred_element_type=F32)
            + jnp.dot(m, p3, preferred_element_type=F32))


def _dot_x01(x, m01):
    m = m01.astype(BF16)
    p1, p2, p3 = _split3(x)
    return (jnp.dot(p1, m, preferred_element_type=F32) + jnp.dot(p2, m, preferred_element_type=F32)
            + jnp.dot(p3, m, preferred_element_type=F32))


def _split(a):
    hi = a.astype(BF16)
    return hi, (a - hi.astype(F32)).astype(BF16)


def _dot3(a, b):
    (ah, al), (bh, bl) = a, b
    return (jnp.dot(ah, bh, preferred_element_type=F32) + jnp.dot(ah, bl, preferred_element_type=F32)
            + jnp.dot(al, bh, preferred_element_type=F32))


def _sigmoid(x):
    return 1.0 / (1.0 + jnp.exp(-x))


def _silu(x):
    return x * _sigmoid(x)


def _softplus(x):
    return jnp.maximum(x, 0.0) + jnp.log(1.0 + jnp.exp(-jnp.abs(x)))


def _iota2(shape, dim):
    return lax.broadcasted_iota(jnp.int32, shape, dim)


def _params(sem):
    return pltpu.CompilerParams(dimension_semantics=sem, vmem_limit_bytes=VMEM_LIMIT)


def _matmul_kernel(x_ref, w_ref, o_ref):
    o_ref[...] = jnp.dot(x_ref[...], w_ref[...], preferred_element_type=F32)


def _matmul(x, w, tm, tn):
    t, k = x.shape
    n = w.shape[1]
    return pl.pallas_call(
        _matmul_kernel,
        out_shape=jax.ShapeDtypeStruct((t, n), F32),
        grid=(n // tn, t // tm),
        in_specs=[pl.BlockSpec((tm, k), lambda j, i: (i, 0)),
                  pl.BlockSpec((k, tn), lambda j, i: (0, j))],
        out_specs=pl.BlockSpec((tm, tn), lambda j, i: (i, j)),
        compiler_params=_params(("parallel", "arbitrary")),
        name="in_proj",
    )(x, w)


def _conv_silu_stages(x_ref, xpad_ref, xa_ref, cw_ref, cb_ref):
    rows, cols = x_ref.shape
    k = cw_ref.shape[0]
    for c0 in range(0, cols, CONV_SLAB):
        cs = slice(c0, c0 + CONV_SLAB)
        x = x_ref[:, cs]
        xpad_ref[SUBLANES:SUBLANES + rows, cs] = x
        acc = x * cw_ref[k - 1:k, cs]
        if cb_ref is not None:
            acc = acc + cb_ref[:, cs]
        for i in range(k - 1):
            off = SUBLANES - (k - 1) + i
            acc = acc + xpad_ref[off:off + rows, cs] * cw_ref[i:i + 1, cs]
        xa_ref[:, cs] = _silu(acc)
        xpad_ref[0:SUBLANES, cs] = xpad_ref[rows:rows + SUBLANES, cs]
        yield


def _ssd_stages(z_ref, xbc_ref, sm_ref, cw_ref, cb_ref, hp_ref, hpt_ref, dsk_ref, nw_ref,
                y_ref, xpad_ref, xa_ref, st_ref):
    yield from _conv_silu_stages(xbc_ref, xpad_ref, xa_ref, cw_ref, cb_ref)

    row = _iota2((ROWS, ROWS), 0)
    col = _iota2((ROWS, ROWS), 1)
    causal = row >= col
    tril = causal.astype(F32)
    triu = (row <= col).astype(F32)

    sm = sm_ref[...]
    dt = _softplus(sm + hp_ref[0:1, :])
    da = dt * (-jnp.exp(hp_ref[1:2, :]))
    acum = _dot_01(tril, da)
    smt = sm.T
    dtt = _softplus(smt[0:SSD_HEADS, :] + hpt_ref[0:SSD_HEADS, 0:1])
    dat = dtt * (-jnp.exp(hpt_ref[0:SSD_HEADS, 1:2]))
    acumt = _dot_x01(dat, triu)

    heads3 = jnp.concatenate([dt, jnp.exp(acum), jnp.exp(acum[ROWS - 1:ROWS, :] - acum)], axis=0)
    h_hi, h_lo = _split(heads3)
    heads6 = jnp.concatenate([h_hi, h_lo], axis=0)
    lane_lo = _iota2((ROWS, LANES), 1) < SSD_HEAD_DIM
    yield

    hg = SSD_HEADS // SSD_GROUPS
    gw = SSD_WIDTH // SSD_GROUPS
    hrow = _iota2((LANES, gw), 0)
    hcol = _iota2((LANES, gw), 1)
    for g in range(SSD_GROUPS):
        gs = slice(g * gw, (g + 1) * gw)
        expand = (((hcol >> 6) + g * hg) == hrow).astype(BF16)
        ex = jnp.dot(heads6, expand, preferred_element_type=F32)
        ex = ex[0:3 * ROWS, :] + ex[3 * ROWS:6 * ROWS, :]
        dtx = ex[0:ROWS, :]
        eax = ex[ROWS:2 * ROWS, :]
        tex = ex[2 * ROWS:3 * ROWS, :]
        yield
        xs = xa_ref[:, gs]
        xdt = xs * dtx
        xdt_b = xdt.astype(BF16)
        xend_b = (xdt * tex).astype(BF16)
        bm = xa_ref[:, SSD_WIDTH + g * SSD_STATE:SSD_WIDTH + (g + 1) * SSD_STATE]
        cm = xa_ref[:, SSD_WIDTH + SSD_BC + g * SSD_STATE:SSD_WIDTH + SSD_BC + (g + 1) * SSD_STATE]
        cm_b = cm.astype(BF16)
        cb = _dot_nt(cm_b, bm)
        yield
        yd = []
        for pair in range(hg // 2):
            res = []
            for sub in range(2):
                h = g * hg + pair * 2 + sub
                seg = acum[:, h:h + 1] - acumt[h:h + 1, :]
                lmat = cb * jnp.exp(jnp.where(causal, seg, -jnp.inf))
                c0 = pair * 2 * SSD_HEAD_DIM
                res.append(jnp.dot(lmat.astype(BF16), xdt_b[:, c0:c0 + LANES],
                                   preferred_element_type=F32))
            yd.append(jnp.where(lane_lo, res[0], res[1]))
            yield
        yd = jnp.concatenate(yd, axis=1)
        st = st_ref[g]
        yoff = jnp.dot(cm_b, st.astype(BF16), preferred_element_type=F32) * eax
        st_ref[g] = (st * eax[ROWS - 1:ROWS, :]
                     + jnp.dot(bm.T.astype(BF16), xend_b, preferred_element_type=F32))
        yield
        y = yd + yoff + xs * dsk_ref[:, gs]
        y = y * _silu(z_ref[:, gs])
        ms = jnp.mean(y * y, axis=-1, keepdims=True)
        y_ref[:, MIX_SSD + g * gw:MIX_SSD + (g + 1) * gw] = (
            y * lax.rsqrt(ms + 1e-6) * nw_ref[:, gs]).astype(y_ref.dtype)
        yield


def _gdn_stages(qkv_ref, gate_ref, sm_ref, cw_ref, hp_ref, hpt_ref, nw_ref, y_ref, xpad_ref, xa_ref, st_ref):
    yield from _conv_silu_stages(qkv_ref, xpad_ref, xa_ref, cw_ref, None)

    row = _iota2((ROWS, ROWS), 0)
    col = _iota2((ROWS, ROWS), 1)
    incl = row >= col
    strict = row > col
    tril = incl.astype(F32)
    triu = (row <= col).astype(F32)

    sm = sm_ref[...]
    la = -jnp.exp(hp_ref[1:2, :]) * _softplus(sm + hp_ref[0:1, :])
    gcum = _dot_01(tril, la)
    smt = sm.T
    lat = (-jnp.exp(hpt_ref[SM_B:SM_B + SUBLANES, 1:2])
           * _softplus(smt[SM_B:SM_B + SUBLANES, :] + hpt_ref[SM_B:SM_B + SUBLANES, 0:1]))
    gcumt = _dot_x01(lat, triu)
    beta_all = _sigmoid(sm)
    yield

    dh = GDN_HEAD_DIM
    heads = range(GDN_HEADS)
    qs, ks, xs, rhss, decays, gs = [], [], [], [], [], []
    for h in heads:
        q = xa_ref[:, h * dh:(h + 1) * dh]
        k = xa_ref[:, GDN_WIDTH + h * dh:GDN_WIDTH + (h + 1) * dh]
        v = xa_ref[:, 2 * GDN_WIDTH + h * dh:2 * GDN_WIDTH + (h + 1) * dh]
        q = q * lax.rsqrt(jnp.sum(q * q, axis=-1, keepdims=True) + 1e-6) * (dh ** -0.5)
        k = k * lax.rsqrt(jnp.sum(k * k, axis=-1, keepdims=True) + 1e-6)
        g = gcum[:, SM_A + h:SM_A + h + 1]
        gt = gcumt[SM_A - SM_B + h:SM_A - SM_B + h + 1, :]
        beta = beta_all[:, SM_B + h:SM_B + h + 1]
        decay = jnp.exp(jnp.where(incl, g - gt, -jnp.inf))
        kb = k * beta
        xs.append(-jnp.where(strict, _dot_nt(kb, k) * decay, 0.0))
        rhss.append(jnp.concatenate([v * beta, kb * jnp.exp(g)], axis=1))
        qs.append(q)
        ks.append(k)
        decays.append(decay)
        gs.append(g)
        yield
    ps = xs
    ns = xs
    levels = ROWS.bit_length() - 2
    for j in range(levels):
        if j < GDN_HI_LEVELS:
            pp = [_split(p) for p in ps]
            ps = [_dot3(p, p) for p in pp]
            yield
            pp = [_split(p) for p in ps]
            ns = [n + p + _dot3(_split(n), p2) for n, p, p2 in zip(ns, ps, pp)]
        else:
            ps = [_dot(p, p) for p in ps]
            yield
            ns = [n + p + _dot(n, p) for n, p in zip(ns, ps)]
        yield
    sols = [r + _dot(n, r) for n, r in zip(ns, rhss)]
    yield
    qks = [_dot_nt(q, k) * d for q, k, d in zip(qs, ks, decays)]
    sts = [st_ref[h] for h in heads]
    v_news = [s[:, 0:dh] - _dot(s[:, dh:2 * dh], st) for s, st in zip(sols, sts)]
    yield
    os_ = [_dot(q * jnp.exp(g), st) + _dot(qk, vn) for q, g, st, qk, vn in zip(qs, gs, sts, qks, v_news)]
    yield
    for h in heads:
        glast = gs[h][ROWS - 1:ROWS, :]
        kd = ks[h] * jnp.exp(glast - gs[h])
        st_ref[h] = sts[h] * jnp.exp(glast) + _dot(kd.T, v_news[h])
    yield
    for h in heads:
        o = os_[h]
        o = o * lax.rsqrt(jnp.mean(o * o, axis=-1, keepdims=True) + 1e-6)
        y_ref[:, MIX_GDN + h * dh:MIX_GDN + (h + 1) * dh] = (
            o * nw_ref[...] * _silu(gate_ref[:, h * dh:(h + 1) * dh])).astype(y_ref.dtype)
        yield


def _hgrn_stages(q_ref, f_ref, i_ref, gate_ref, lb_ref, nw_ref, y_ref, st_ref):
    row = _iota2((ROWS, ROWS), 0)
    col = _iota2((ROWS, ROWS), 1)
    tril = (row >= col).astype(F32)
    rmod = _iota2((ROWS, LANES), 0) & (SUBLANES - 1)
    rfull = _iota2((ROWS, LANES), 0)
    dk = HG_HEAD_DIM
    level_masks = {}
    m = 1
    while m < ROWS:
        right = (rfull & m) != 0
        parent = ~(2 * m - 1)
        keep = (((row & m) ^ m) | (col & m) | ((row ^ col) & parent)) == 0
        level_masks[m] = (right, jnp.where(right, LOG2E, -LOG2E), keep)
        m *= 2
    for h in range(HG_HEADS):
        sl = slice(h * dk, (h + 1) * dk)
        lb = lb_ref[:, sl]
        fr = f_ref[:, sl]
        log_sig = jnp.minimum(fr, 0.0) - jnp.log(1.0 + jnp.exp(-jnp.abs(fr)))
        a = jnp.log(lb)
        y = jnp.log1p(-lb) + log_sig
        log_f = jnp.maximum(a, y) + jnp.log(1.0 + jnp.exp(-jnp.abs(a - y)))
        k = (1.0 - lb) * _sigmoid(-fr)
        q = _silu(q_ref[:, sl])
        v = i_ref[:, sl]
        b = _dot_01(tril, log_f)
        yield

        st = st_ref[h]
        o = _dot_nt(q * jnp.exp(b), st)
        blast = b[ROWS - 1:ROWS, :]
        st_ref[h] = st * jnp.exp(blast) + _dot_tn(v, k * jnp.exp(blast - b))
        yield

        pmat = jnp.zeros((ROWS, ROWS), F32)
        b3 = b.reshape(ROWS // SUBLANES, SUBLANES, dk)
        m = 1
        while m < ROWS:
            if m < SUBLANES:
                ref = None
                for p0 in range(0, SUBLANES, 2 * m):
                    cand = jnp.broadcast_to(b3[:, p0 + m - 1:p0 + m, :], b3.shape).reshape(ROWS, dk)
                    ref = cand if ref is None else jnp.where(rmod >= p0, cand, ref)
            else:
                ref = jnp.concatenate(
                    [jnp.broadcast_to(b[p0 + m - 1:p0 + m, :], (2 * m, dk)) for p0 in range(0, ROWS, 2 * m)],
                    axis=0)
            right, scale, keep = level_masks[m]
            z = jnp.where(right, q, k) * jnp.exp2((b - ref) * scale)
            zb = z.astype(BF16)
            pmat = pmat + jnp.where(keep, _dot_nt(zb, zb), 0.0)
            m *= 2
            yield
        o = o + _dot(pmat, v) + jnp.sum(q * k, axis=-1, keepdims=True) * v
        yield

        o = o * lax.rsqrt(jnp.mean(o * o, axis=-1, keepdims=True) + 1e-6)
        y_ref[:, MIX_HG + h * dk:MIX_HG + (h + 1) * dk] = (
            o * nw_ref[...] * _silu(gate_ref[:, sl])).astype(y_ref.dtype)
        yield


def _in_proj_stages(hb_ref, w_ref, out_ref):
    hb = hb_ref[...]
    for c0 in range(0, PROJ_COLS, PROJ_SLAB):
        out_ref[:, c0:c0 + PROJ_SLAB] = jnp.dot(hb, w_ref[:, c0:c0 + PROJ_SLAB], preferred_element_type=F32)
        yield


def _mixer_kernel(hb0_ref, hbn_ref, w_ref,
                  scw_ref, scb_ref, shp_ref, shpt_ref, dsk_ref, snw_ref,
                  gcw_ref, ghp_ref, ghpt_ref, gnw_ref, lb_ref, hnw_ref,
                  y_ref, proj_ref, sxpad_ref, sxa_ref, sst_ref, gxpad_ref, gxa_ref, gst_ref, hst_ref):
    c = pl.program_id(1)

    @pl.when(c == 0)
    def _():
        sst_ref[...] = jnp.zeros(sst_ref.shape, F32)
        gst_ref[...] = jnp.zeros(gst_ref.shape, F32)
        hst_ref[...] = jnp.zeros(hst_ref.shape, F32)
        sxpad_ref[0:SUBLANES, :] = jnp.zeros((SUBLANES, sxpad_ref.shape[1]), F32)
        gxpad_ref[0:SUBLANES, :] = jnp.zeros((SUBLANES, gxpad_ref.shape[1]), F32)
        for _ in _in_proj_stages(hb0_ref, w_ref, proj_ref.at[0]):
            pass

    cur = proj_ref.at[c % 2]
    nxt = proj_ref.at[(c + 1) % 2]

    def seg(off, width):
        return cur.at[:, off:off + width]

    sm_ref = seg(OFF_SMALL, LANES)
    streams = [
        _gdn_stages(seg(OFF_QKV, 3 * GDN_WIDTH), seg(OFF_GGATE, GDN_WIDTH), sm_ref, gcw_ref, ghp_ref, ghpt_ref,
                    gnw_ref, y_ref, gxpad_ref, gxa_ref, gst_ref),
        _ssd_stages(seg(OFF_Z, SSD_WIDTH), seg(OFF_XBC, SSD_XBC), sm_ref, scw_ref, scb_ref, shp_ref, shpt_ref,
                    dsk_ref, snw_ref, y_ref, sxpad_ref, sxa_ref, sst_ref),
        _hgrn_stages(seg(OFF_HQ, HG_WIDTH), seg(OFF_HF, HG_WIDTH), seg(OFF_HI, HG_WIDTH),
                     seg(OFF_HGATE, HG_WIDTH), lb_ref, hnw_ref, y_ref, hst_ref),
        _in_proj_stages(hbn_ref, w_ref, nxt),
    ]
    live = [True] * len(streams)
    rnd = 0
    while any(live):
        advance = [1, 1, 1 if rnd < HG_SLOW_ROUNDS else 2, 1]
        rnd += 1
        for s, stream in enumerate(streams):
            for _ in range(advance[s]):
                if live[s]:
                    try:
                        next(stream)
                    except StopIteration:
                        live[s] = False


def _mixers(hb, w_proj, layer, nb, seq, ssd_conv_w, ssd_conv_b, ssd_dt_bias, ssd_a_log, ssd_d, ssd_norm_w,
            gdn_conv_w, gdn_dt_bias, gdn_a_log, gdn_norm_w, lb, hg_norm_w):
    nblk = seq // ROWS
    shp = jnp.zeros((SUBLANES, LANES), F32)
    shp = shp.at[0, SM_DT:SM_DT + SSD_HEADS].set(ssd_dt_bias).at[1, SM_DT:SM_DT + SSD_HEADS].set(ssd_a_log)
    ghp = jnp.zeros((SUBLANES, LANES), F32)
    ghp = ghp.at[0, SM_A:SM_A + GDN_HEADS].set(gdn_dt_bias).at[1, SM_A:SM_A + GDN_HEADS].set(gdn_a_log)
    dsk = jnp.repeat(ssd_d, SSD_HEAD_DIM)[None, :]
    const = lambda b, c: (0, 0)

    def whole(shape):
        return pl.BlockSpec(shape, const)

    return pl.pallas_call(
        _mixer_kernel,
        out_shape=jax.ShapeDtypeStruct((nb * seq, D_MIX), BF16),
        grid=(nb, nblk),
        in_specs=[
            pl.BlockSpec((ROWS, D_MODEL), lambda b, c: (b * nblk + c, 0)),
            pl.BlockSpec((ROWS, D_MODEL), lambda b, c: (b * nblk + jnp.minimum(c + 1, nblk - 1), 0)),
            pl.BlockSpec((None, D_MODEL, PROJ_COLS), lambda b, c: (layer, 0, 0), pipeline_mode=pl.Buffered(1)),
            whole((SSD_CONV, SSD_XBC)), whole((1, SSD_XBC)), whole((SUBLANES, LANES)), whole((LANES, SUBLANES)),
            whole((1, SSD_WIDTH)), whole((1, SSD_WIDTH)),
            whole((GDN_CONV, 3 * GDN_WIDTH)), whole((SUBLANES, LANES)), whole((LANES, SUBLANES)),
            whole((1, GDN_HEAD_DIM)), whole((1, HG_WIDTH)), whole((1, HG_HEAD_DIM)),
        ],
        out_specs=pl.BlockSpec((ROWS, D_MIX), lambda b, c: (b * nblk + c, 0)),
        scratch_shapes=[pltpu.VMEM((2, ROWS, PROJ_COLS), F32),
                        pltpu.VMEM((ROWS + SUBLANES, SSD_XBC), F32),
                        pltpu.VMEM((ROWS, SSD_XBC), F32),
                        pltpu.VMEM((SSD_GROUPS, SSD_STATE, SSD_WIDTH // SSD_GROUPS), F32),
                        pltpu.VMEM((ROWS + SUBLANES, 3 * GDN_WIDTH), F32),
                        pltpu.VMEM((ROWS, 3 * GDN_WIDTH), F32),
                        pltpu.VMEM((GDN_HEADS, GDN_HEAD_DIM, GDN_HEAD_DIM), F32),
                        pltpu.VMEM((HG_HEADS, HG_HEAD_DIM, HG_HEAD_DIM), F32)],
        compiler_params=_params(("parallel", "arbitrary")),
        name="in_proj_mixers",
    )(hb, hb, w_proj,
      ssd_conv_w, ssd_conv_b[None, :], shp, shp.T, dsk, ssd_norm_w[None, :],
      gdn_conv_w, ghp, ghp.T, gdn_norm_w[None, :], lb[None, :], hg_norm_w[None, :])


def _layer_norm(x, g, b):
    mu = jnp.mean(x, axis=-1, keepdims=True)
    xc = x - mu
    var = jnp.mean(xc * xc, axis=-1, keepdims=True)
    return xc * lax.rsqrt(var + 1e-5) * g + b


ROUTER_ROWS = 32
OUT_SLAB = 256


def _outproj_rows(mix, h, g_ref, b_ref, wr_ref, br_ref):
    h1 = _layer_norm(DN_ALPHA * h + mix, g_ref[...], b_ref[...])

    h_hi, h_lo = _split(h1)
    wr = wr_ref[...]
    lt = _dot_nt(wr, h_hi) + _dot_nt(wr, h_lo)
    lt = lt[0:ROUTER_ROWS, :] + lt[ROUTER_ROWS:2 * ROUTER_ROWS, :] + br_ref[...]
    glog = [lt[g:g + 1, :] for g in range(MOE_GROUPS)]
    elog = [lt[MOE_GROUPS + e:MOE_GROUPS + e + 1, :] for e in range(N_EXPERTS)]

    def softmax(rows):
        m = rows[0]
        for r in rows[1:]:
            m = jnp.maximum(m, r)
        ex = [jnp.exp(r - m) for r in rows]
        tot = ex[0]
        for r in ex[1:]:
            tot = tot + r
        return [r / tot for r in ex]

    def top1(rows):
        best = rows[0]
        for r in rows[1:]:
            best = jnp.maximum(best, r)
        idx = jnp.full_like(best, float(len(rows) - 1))
        for j in range(len(rows) - 2, -1, -1):
            idx = jnp.where(rows[j] == best, float(j), idx)
        return best, idx

    g_p, g_idx = top1(softmax(glog))
    chosen = []
    for k in range(EXPERTS_PER_GROUP):
        acc = jnp.zeros_like(g_p)
        for g in range(MOE_GROUPS):
            acc = jnp.where(g_idx == float(g), elog[g * EXPERTS_PER_GROUP + k], acc)
        chosen.append(acc)
    eprob = softmax(chosen)
    p1, i1 = top1(eprob)
    p2, i2 = top1([jnp.where(i1 == float(k), -1.0, eprob[k]) for k in range(EXPERTS_PER_GROUP)])
    denom = p1 + p2
    w1 = g_p * p1 / denom
    w2 = g_p * p2 / denom
    wk = [jnp.where(i1 == float(k), w1, 0.0) + jnp.where(i2 == float(k), w2, 0.0)
          for k in range(EXPERTS_PER_GROUP)]
    rid = _iota2((ROUTER_ROWS, lt.shape[1]), 0)
    comb_t = jnp.where(rid == N_EXPERTS, g_idx, 0.0)
    for g in range(MOE_GROUPS):
        for k in range(EXPERTS_PER_GROUP):
            comb_t = jnp.where(rid == g * EXPERTS_PER_GROUP + k,
                               jnp.where(g_idx == float(g), wk[k], 0.0), comb_t)
    comb_t = jnp.concatenate([comb_t, jnp.zeros((LANES - ROUTER_ROWS, lt.shape[1]), F32)], axis=0)
    return h1, comb_t.T


def _outproj_kernel(y_ref, h_ref, w_ref, g_ref, b_ref, wr_ref, br_ref, h1_ref, h1b_ref, comb_ref):
    slab = min(OUT_SLAB, y_ref.shape[0])
    slabs = [slice(r0, r0 + slab) for r0 in range(0, y_ref.shape[0], slab)]
    mixes = [jnp.dot(y_ref[rows, :], w_ref[...], preferred_element_type=F32) for rows in slabs]
    for rows, mix in zip(slabs, mixes):
        h1, comb = _outproj_rows(mix, h_ref[rows, :], g_ref, b_ref, wr_ref, br_ref)
        h1_ref[rows, :] = h1
        h1b_ref[rows, :] = h1.astype(BF16)
        comb_ref[rows, :] = comb


def _router_params(w_group, b_group, w_expert, b_expert):
    pad = ROUTER_ROWS - MOE_GROUPS - N_EXPERTS
    w_t = jnp.pad(jnp.concatenate([w_group, w_expert], axis=1).T, ((0, pad), (0, 0)))
    b_col = jnp.pad(jnp.concatenate([b_group, b_expert]), (0, pad))[:, None]
    return jnp.concatenate(_split(w_t), axis=0), b_col


def _outproj(y, h, w_out, layer, ln_g, ln_b, w_router, b_router, tm):
    t = h.shape[0]
    rowmap = lambda i: (i, 0)
    const = lambda i: (0, 0)
    return pl.pallas_call(
        _outproj_kernel,
        out_shape=(jax.ShapeDtypeStruct((t, D_MODEL), F32),
                   jax.ShapeDtypeStruct((t, D_MODEL), BF16),
                   jax.ShapeDtypeStruct((t, LANES), F32)),
        grid=(t // tm,),
        in_specs=[pl.BlockSpec((tm, D_MIX), rowmap),
                  pl.BlockSpec((tm, D_MODEL), rowmap),
                  pl.BlockSpec((None, D_MIX, D_MODEL), lambda i: (layer, 0, 0)),
                  pl.BlockSpec((1, D_MODEL), const),
                  pl.BlockSpec((1, D_MODEL), const),
                  pl.BlockSpec((2 * ROUTER_ROWS, D_MODEL), const),
                  pl.BlockSpec((ROUTER_ROWS, 1), const)],
        out_specs=(pl.BlockSpec((tm, D_MODEL), rowmap),
                   pl.BlockSpec((tm, D_MODEL), rowmap),
                   pl.BlockSpec((tm, LANES), rowmap)),
        compiler_params=_params(("parallel",)),
        name="out_proj_ln_router",
    )(y, h, w_out, ln_g[None, :], ln_b[None, :], w_router, b_router)


def _prefix_lanes(v, idx):
    axis = 1 if v.shape[0] == 1 else 0
    out = jnp.zeros_like(v)
    for g in range(MOE_GROUPS - 1):
        vg = v[:, g:g + 1] if axis == 1 else v[g:g + 1, :]
        out = out + jnp.where(idx > g, vg, 0.0)
    return out


def _moe_kernel(hb_ref, h_ref, comb_ref, wgu_ref, wdn_ref, g_ref, b_ref, o_ref, ob_ref,
                xs_ref, cw_ref, ys_ref):
    tm = hb_ref.shape[0]
    comb = comb_ref[...]
    lane = _iota2((tm, LANES), 1).astype(F32)
    gid = comb[:, N_EXPERTS:N_EXPERTS + 1]
    gsel = (lane == gid).astype(F32)
    tr = _iota2((tm, tm), 0)
    tc = _iota2((tm, tm), 1)

    ns = xs_ref.shape[0] - MOE_CHUNK

    def aligned(cnt):
        return jnp.floor((cnt + (MOE_ALIGN - 1)) * (1.0 / MOE_ALIGN)) * MOE_ALIGN

    rank_c = _dot((tc < tr).astype(F32), gsel)
    cnt_r = jnp.sum(gsel, axis=0, keepdims=True)
    start_r = _prefix_lanes(aligned(cnt_r), _iota2((1, LANES), 1))
    dest_c = jnp.sum(gsel * (start_r + rank_c), axis=1, keepdims=True)
    eye8 = (_iota2((SUBLANES, LANES), 0) == _iota2((SUBLANES, LANES), 1)).astype(F32)
    gsel_t = _dot_nt(eye8, gsel)
    rank_r = _dot(gsel_t, (tr < tc).astype(F32))
    cnt_c = jnp.sum(gsel_t, axis=1, keepdims=True)
    start_c = _prefix_lanes(aligned(cnt_c), _iota2((SUBLANES, 1), 0))
    dest_r = jnp.sum(gsel_t * (start_c + rank_r), axis=0, keepdims=True)

    perm = (dest_r == _iota2((ns, tm), 0).astype(F32)).astype(BF16)
    xs_ref[0:ns, :] = jnp.dot(perm, hb_ref[...], preferred_element_type=F32).astype(BF16)
    xs_ref[ns:ns + MOE_CHUNK, :] = jnp.zeros((MOE_CHUNK, D_MODEL), BF16)
    c_hi, c_lo = _split(comb)
    cw2 = jnp.dot(perm, jnp.concatenate([c_hi, c_lo], axis=1), preferred_element_type=F32)
    cw_ref[0:ns, :] = cw2[:, 0:LANES] + cw2[:, LANES:2 * LANES]
    cw_ref[ns:ns + MOE_CHUNK, :] = jnp.zeros((MOE_CHUNK, LANES), F32)
    ys_ref[...] = jnp.zeros(ys_ref.shape, F32)

    lane1 = _iota2((1, LANES), 1)
    starts = [jnp.sum(jnp.where(lane1 == g, start_r, 0.0)).astype(jnp.int32) for g in range(MOE_GROUPS)]
    cnts = [jnp.sum(jnp.where(lane1 == g, cnt_r, 0.0)) for g in range(MOE_GROUPS)]
    def block(g, j):
        rows = pl.ds(pl.multiple_of(starts[g] + j * MOE_CHUNK, MOE_ALIGN), MOE_CHUNK)
        x = xs_ref[rows, :]
        cw = cw_ref[rows, :]
        hm = []
        for e in range(EXPERTS_PER_GROUP):
            ex = g * EXPERTS_PER_GROUP + e
            gu = jnp.dot(x, wgu_ref[ex], preferred_element_type=F32)
            hm.append((_silu(gu[:, 0:D_EXPERT]) * gu[:, D_EXPERT:2 * D_EXPERT] * cw[:, ex:ex + 1]).astype(BF16))
        ys_ref[rows, :] += jnp.dot(jnp.concatenate(hm, axis=1), wdn_ref[g], preferred_element_type=F32)

    for g in range(MOE_GROUPS):
        block(g, 0)

    most = cnts[0]
    for g in range(1, MOE_GROUPS):
        most = jnp.maximum(most, cnts[g])

    def finish():
        unperm = (dest_c == _iota2((tm, ns), 1).astype(F32)).astype(BF16)
        y = jnp.dot(unperm, ys_ref[0:ns, :].astype(BF16), preferred_element_type=F32)
        h2 = _layer_norm(DN_ALPHA * h_ref[...] + y, g_ref[...], b_ref[...])
        o_ref[...] = h2
        ob_ref[...] = h2.astype(BF16)

    finish()

    @pl.when(most > MOE_CHUNK)
    def _():
        for j in range(1, -(-tm // MOE_CHUNK)):
            for g in range(MOE_GROUPS):
                pl.when(cnts[g] > j * MOE_CHUNK)(lambda g=g, j=j: block(g, j))
        finish()


def _moe(hb, h, comb, w_gu, w_dn, layer, ln_g, ln_b, tm):
    t = h.shape[0]
    rowmap = lambda i: (i, 0)
    const = lambda i: (0, 0)
    const3 = lambda i: (layer, 0, 0, 0)
    sorted_rows = -(-(tm + MOE_GROUPS * MOE_ALIGN) // LANES) * LANES + MOE_CHUNK
    return pl.pallas_call(
        _moe_kernel,
        out_shape=(jax.ShapeDtypeStruct((t, D_MODEL), F32),
                   jax.ShapeDtypeStruct((t, D_MODEL), BF16)),
        grid=(t // tm,),
        in_specs=[pl.BlockSpec((tm, D_MODEL), rowmap),
                  pl.BlockSpec((tm, D_MODEL), rowmap),
                  pl.BlockSpec((tm, LANES), rowmap),
                  pl.BlockSpec((None, N_EXPERTS, D_MODEL, 2 * D_EXPERT), const3, pipeline_mode=pl.Buffered(1)),
                  pl.BlockSpec((None, MOE_GROUPS, EXPERTS_PER_GROUP * D_EXPERT, D_MODEL), const3,
                               pipeline_mode=pl.Buffered(1)),
                  pl.BlockSpec((1, D_MODEL), const),
                  pl.BlockSpec((1, D_MODEL), const)],
        out_specs=(pl.BlockSpec((tm, D_MODEL), rowmap),
                   pl.BlockSpec((tm, D_MODEL), rowmap)),
        scratch_shapes=[pltpu.VMEM((sorted_rows, D_MODEL), BF16),
                        pltpu.VMEM((sorted_rows, LANES), F32),
                        pltpu.VMEM((sorted_rows, D_MODEL), F32)],
        compiler_params=pltpu.CompilerParams(dimension_semantics=("parallel",),
                                             vmem_limit_bytes=MOE_VMEM_LIMIT),
        name="moe_ln",
    )(hb, h, comb, w_gu, w_dn, ln_g[None, :], ln_b[None, :])


W_IN_SEGMENTS = ((SSD_WIDTH, OFF_Z), (SSD_XBC, OFF_XBC), (SSD_HEADS, OFF_SMALL + SM_DT),
                 (3 * GDN_WIDTH, OFF_QKV), (GDN_WIDTH, OFF_GGATE), (GDN_HEADS, OFF_SMALL + SM_B),
                 (GDN_HEADS, OFF_SMALL + SM_A), (HG_WIDTH, OFF_HQ), (HG_WIDTH, OFF_HF),
                 (HG_WIDTH, OFF_HI), (HG_WIDTH, OFF_HGATE))
W_T_BLOCK = 256
W_MAIN_BLOCKS = OFF_SMALL // W_T_BLOCK


def _w_in_kernel(tbl_ref, w_ref, small_ref, o_ref):
    j = pl.program_id(1)

    @pl.when(j < W_MAIN_BLOCKS)
    def _():
        o_ref[...] = w_ref[0].T.astype(BF16)

    @pl.when(j >= W_MAIN_BLOCKS)
    def _():
        o_ref[...] = small_ref[...].T.astype(BF16)


def _split_w_in(w_in):
    depth = w_in.shape[0]
    w_t = jnp.swapaxes(w_in, 1, 2)
    src_of, small_rows, src = {}, [], 0
    for width, dst in W_IN_SEGMENTS:
        if width >= W_T_BLOCK:
            for off in range(0, width, W_T_BLOCK):
                src_of[(dst + off) // W_T_BLOCK] = src + off
        else:
            small_rows.append(w_t[:, src:src + width, :])
        src += width
    table = jnp.array([src_of.get(j, 0) for j in range(PROJ_COLS // W_T_BLOCK)], jnp.int32)
    small = jnp.concatenate(small_rows, axis=1)
    small = jnp.pad(small, ((0, 0), (0, PROJ_COLS - OFF_SMALL - small.shape[1]), (0, 0)))
    return pl.pallas_call(
        _w_in_kernel,
        out_shape=jax.ShapeDtypeStruct((depth, D_MODEL, PROJ_COLS), BF16),
        grid_spec=pltpu.PrefetchScalarGridSpec(
            num_scalar_prefetch=1,
            grid=(depth, PROJ_COLS // W_T_BLOCK),
            in_specs=[pl.BlockSpec((pl.Element(1), pl.Element(W_T_BLOCK), pl.Element(D_MODEL)),
                                   lambda l, j, tbl: (l, pl.multiple_of(tbl[j], SUBLANES), 0)),
                      pl.BlockSpec((None, PROJ_COLS - OFF_SMALL, D_MODEL), lambda l, j, tbl: (l, 0, 0))],
            out_specs=pl.BlockSpec((None, D_MODEL, W_T_BLOCK), lambda l, j, tbl: (l, 0, j))),
        compiler_params=_params(("parallel", "arbitrary")),
        name="w_in_relayout",
    )(table, w_t, small)


def kernel(x, w_in, ssd_conv_w, ssd_conv_b, ssd_dt_bias, ssd_a_log, ssd_d, ssd_norm_w, gdn_conv_w, gdn_dt_bias, gdn_a_log, gdn_norm_w, hg_lb_logits, hg_norm_w, w_out, ln1_g, ln1_b, w_router_group, b_router_group, w_router_expert, b_router_expert, w_expert_gate_up, w_expert_down, ln2_g, ln2_b):
    nb, seq, d = x.shape
    t = nb * seq
    lb_cum = jnp.cumsum(jax.nn.softmax(hg_lb_logits.astype(F32), axis=0), axis=0)
    lb_all = lb_cum - lb_cum[0:1]
    h = x.reshape(t, d)
    hb = h.astype(BF16)
    w_proj = _split_w_in(w_in)
    w_out_b = w_out.astype(BF16)
    w_gu_b = w_expert_gate_up.astype(BF16)
    w_dn_b = w_expert_down.astype(BF16).reshape(DEPTH, MOE_GROUPS, EXPERTS_PER_GROUP * D_EXPERT, D_MODEL)
    for l in range(DEPTH):
        y = _mixers(hb, w_proj, l, nb, seq, ssd_conv_w[l], ssd_conv_b[l], ssd_dt_bias[l], ssd_a_log[l], ssd_d[l],
                    ssd_norm_w[l], gdn_conv_w[l], gdn_dt_bias[l], gdn_a_log[l], gdn_norm_w[l],
                    lb_all[l], hg_norm_w[l])
        w_router, b_router = _router_params(w_router_group[l], b_router_group[l],
                                            w_router_expert[l], b_router_expert[l])
        h1, h1b, comb = _outproj(y, h, w_out_b, l, ln1_g[l], ln1_b[l], w_router, b_router, 1024)
        h, hb = _moe(h1b, h1, comb, w_gu_b, w_dn_b, l, ln2_g[l], ln2_b[l], 512)
    return h.reshape(nb, seq, d)
```

```python
import jax
import jax.numpy as jnp
from jax import lax
from jax.experimental import pallas as pl
from jax.experimental.pallas import tpu as pltpu

F32 = jnp.float32
BF16 = jnp.bfloat16

D_MODEL = 1024
DEPTH = 4
SSD_HEADS = 16
SSD_HEAD_DIM = 64
SSD_WIDTH = 1024
SSD_GROUPS = 2
SSD_STATE = 128
SSD_BC = 256
SSD_XBC = 1536
SSD_CONV = 4
GDN_HEADS = 4
GDN_HEAD_DIM = 128
GDN_WIDTH = 512
GDN_CONV = 4
HG_HEADS = 4
HG_HEAD_DIM = 128
HG_WIDTH = 512
D_MIX = 2048
MOE_GROUPS = 4
EXPERTS_PER_GROUP = 4
N_EXPERTS = 16
D_EXPERT = 256
DN_ALPHA = (2 * DEPTH) ** 0.25
LOG2E = 1.4426950408889634

LANES = 128
SUBLANES = 8
ROWS = 128
VMEM_LIMIT = 48 * 1024 * 1024
MOE_VMEM_LIMIT = 56 * 1024 * 1024
CONV_SLAB = 256
GDN_HI_LEVELS = 2
MOE_CHUNK = 160
MOE_ALIGN = 16
PROJ_SLAB = 768
HG_SLOW_ROUNDS = 12

OFF_XBC = 0
OFF_QKV = 1536
OFF_Z = 3072
OFF_GGATE = 4096
OFF_HQ = 4608
OFF_HF = 5120
OFF_HI = 5632
OFF_HGATE = 6144
OFF_SMALL = 6656
PROJ_COLS = 6912
SM_DT = 0
SM_B = 16
SM_A = 20
MIX_SSD = 0
MIX_GDN = SSD_WIDTH
MIX_HG = SSD_WIDTH + GDN_WIDTH


def _dot(a, b):
    return jnp.dot(a.astype(BF16), b.astype(BF16), preferred_element_type=F32)


def _dot_nt(a, b):
    return lax.dot_general(a.astype(BF16), b.astype(BF16), (((1,), (1,)), ((), ())),
                           preferred_element_type=F32)


def _dot_tn(a, b):
    return lax.dot_general(a.astype(BF16), b.astype(BF16), (((0,), (0,)), ((), ())),
                           preferred_element_type=F32)


def _split3(a):
    p1 = a.astype(BF16)
    r1 = a - p1.astype(F32)
    p2 = r1.astype(BF16)
    return p1, p2, (r1 - p2.astype(F32)).astype(BF16)


def _dot_01(m01, x):
    m = m01.astype(BF16)
    p1, p2, p3 = _split3(x)
    return (jnp.dot(m, p1, preferred_element_type=F32) + jnp.dot(m, p2, preferred_element_type=F32)
            + jnp.dot(m, p3, preferred_element_type=F32))


def _dot_x01(x, m01):
    m = m01.astype(BF16)
    p1, p2, p3 = _split3(x)
    return (jnp.dot(p1, m, preferred_element_type=F32) + jnp.dot(p2, m, preferred_element_type=F32)
            + jnp.dot(p3, m, preferred_element_type=F32))


def _split(a):
    hi = a.astype(BF16)
    return hi, (a - hi.astype(F32)).astype(BF16)


def _dot3(a, b):
    (ah, al), (bh, bl) = a, b
    return (jnp.dot(ah, bh, preferred_element_type=F32) + jnp.dot(ah, bl, preferred_element_type=F32)
            + jnp.dot(al, bh, preferred_element_type=F32))


def _sigmoid(x):
    return 1.0 / (1.0 + jnp.exp(-x))


def _silu(x):
    return x * _sigmoid(x)


def _softplus(x):
    return jnp.maximum(x, 0.0) + jnp.log(1.0 + jnp.exp(-jnp.abs(x)))


def _iota2(shape, dim):
    return lax.broadcasted_iota(jnp.int32, shape, dim)


def _params(sem):
    return pltpu.CompilerParams(dimension_semantics=sem, vmem_limit_bytes=VMEM_LIMIT)


def _matmul_kernel(x_ref, w_ref, o_ref):
    o_ref[...] = jnp.dot(x_ref[...], w_ref[...], preferred_element_type=F32)


def _matmul(x, w, tm, tn):
    t, k = x.shape
    n = w.shape[1]
    return pl.pallas_call(
        _matmul_kernel,
        out_shape=jax.ShapeDtypeStruct((t, n), F32),
        grid=(n // tn, t // tm),
        in_specs=[pl.BlockSpec((tm, k), lambda j, i: (i, 0)),
                  pl.BlockSpec((k, tn), lambda j, i: (0, j))],
        out_specs=pl.BlockSpec((tm, tn), lambda j, i: (i, j)),
        compiler_params=_params(("parallel", "arbitrary")),
        name="in_proj",
    )(x, w)


def _conv_silu_stages(x_ref, xpad_ref, xa_ref, cw_ref, cb_ref):
    rows, cols = x_ref.shape
    k = cw_ref.shape[0]
    base = SUBLANES
    for c0 in range(0, cols, CONV_SLAB):
        cs = slice(c0, c0 + CONV_SLAB)
        x = x_ref[:, cs]
        acc = x * cw_ref[k - 1:k, cs]
        if cb_ref is not None:
            acc = acc + cb_ref[:, cs]
        for s in range(1, k):
            buf = xpad_ref.at[s - 1]
            buf[base:base + SUBLANES, cs] = buf[base + rows:base + rows + SUBLANES, cs]
            buf[base + s:base + s + rows, cs] = x
            acc = acc + buf[base:base + rows, cs] * cw_ref[k - 1 - s:k - s, cs]
        xa_ref[:, cs] = _silu(acc)
        yield


def _ssd_stages(z_ref, xbc_ref, sm_ref, cw_ref, cb_ref, hp_ref, hpt_ref, dsk_ref, nw_ref,
                y_ref, xpad_ref, xa_ref, st_ref):
    yield from _conv_silu_stages(xbc_ref, xpad_ref, xa_ref, cw_ref, cb_ref)

    row = _iota2((ROWS, ROWS), 0)
    col = _iota2((ROWS, ROWS), 1)
    causal = row >= col
    tril = causal.astype(F32)
    triu = (row <= col).astype(F32)

    sm = sm_ref[...]
    dt = _softplus(sm + hp_ref[0:1, :])
    da = dt * (-jnp.exp(hp_ref[1:2, :]))
    acum = _dot_01(tril, da)
    smt = sm.T
    dtt = _softplus(smt[0:SSD_HEADS, :] + hpt_ref[0:SSD_HEADS, 0:1])
    dat = dtt * (-jnp.exp(hpt_ref[0:SSD_HEADS, 1:2]))
    acumt = _dot_x01(dat, triu)

    heads3 = jnp.concatenate([dt, jnp.exp(acum), jnp.exp(acum[ROWS - 1:ROWS, :] - acum)], axis=0)
    h_hi, h_lo = _split(heads3)
    heads6 = jnp.concatenate([h_hi, h_lo], axis=0)
    lane_lo = _iota2((ROWS, LANES), 1) < SSD_HEAD_DIM
    yield

    hg = SSD_HEADS // SSD_GROUPS
    gw = SSD_WIDTH // SSD_GROUPS
    hrow = _iota2((LANES, gw), 0)
    hcol = _iota2((LANES, gw), 1)
    for g in range(SSD_GROUPS):
        gs = slice(g * gw, (g + 1) * gw)
        expand = (((hcol >> 6) + g * hg) == hrow).astype(BF16)
        ex = jnp.dot(heads6, expand, preferred_element_type=F32)
        ex = ex[0:3 * ROWS, :] + ex[3 * ROWS:6 * ROWS, :]
        dtx = ex[0:ROWS, :]
        eax = ex[ROWS:2 * ROWS, :]
        tex = ex[2 * ROWS:3 * ROWS, :]
        yield
        xs = xa_ref[:, gs]
        xdt = xs * dtx
        xdt_b = xdt.astype(BF16)
        xend_b = (xdt * tex).astype(BF16)
        bm = xa_ref[:, SSD_WIDTH + g * SSD_STATE:SSD_WIDTH + (g + 1) * SSD_STATE]
        cm = xa_ref[:, SSD_WIDTH + SSD_BC + g * SSD_STATE:SSD_WIDTH + SSD_BC + (g + 1) * SSD_STATE]
        cm_b = cm.astype(BF16)
        cb = _dot_nt(cm_b, bm)
        yield
        yd = []
        for pair in range(hg // 2):
            res = []
            for sub in range(2):
                h = g * hg + pair * 2 + sub
                seg = acum[:, h:h + 1] - acumt[h:h + 1, :]
                lmat = cb * jnp.exp(jnp.where(causal, seg, -jnp.inf))
                c0 = pair * 2 * SSD_HEAD_DIM
                res.append(jnp.dot(lmat.astype(BF16), xdt_b[:, c0:c0 + LANES],
                                   preferred_element_type=F32))
            yd.append(jnp.where(lane_lo, res[0], res[1]))
            yield
        yd = jnp.concatenate(yd, axis=1)
        st = st_ref[g]
        yoff = jnp.dot(cm_b, st.astype(BF16), preferred_element_type=F32) * eax
        st_ref[g] = (st * eax[ROWS - 1:ROWS, :]
                     + jnp.dot(bm.T.astype(BF16), xend_b, preferred_element_type=F32))
        yield
        y = yd + yoff + xs * dsk_ref[:, gs]
        y = y * _silu(z_ref[:, gs])
        ms = jnp.mean(y * y, axis=-1, keepdims=True)
        y_ref[:, MIX_SSD + g * gw:MIX_SSD + (g + 1) * gw] = (
            y * lax.rsqrt(ms + 1e-6) * nw_ref[:, gs]).astype(y_ref.dtype)
        yield


def _gdn_stages(qkv_ref, gate_ref, sm_ref, cw_ref, hp_ref, hpt_ref, nw_ref, y_ref, xpad_ref, xa_ref, st_ref):
    yield from _conv_silu_stages(qkv_ref, xpad_ref, xa_ref, cw_ref, None)

    row = _iota2((ROWS, ROWS), 0)
    col = _iota2((ROWS, ROWS), 1)
    incl = row >= col
    strict = row > col
    tril = incl.astype(F32)
    triu = (row <= col).astype(F32)

    sm = sm_ref[...]
    la = -jnp.exp(hp_ref[1:2, :]) * _softplus(sm + hp_ref[0:1, :])
    gcum = _dot_01(tril, la)
    smt = sm.T
    lat = (-jnp.exp(hpt_ref[SM_B:SM_B + SUBLANES, 1:2])
           * _softplus(smt[SM_B:SM_B + SUBLANES, :] + hpt_ref[SM_B:SM_B + SUBLANES, 0:1]))
    gcumt = _dot_x01(lat, triu)
    beta_all = _sigmoid(sm)
    yield

    dh = GDN_HEAD_DIM
    heads = range(GDN_HEADS)
    qs, ks, xs, rhss, decays, gs = [], [], [], [], [], []
    for h in heads:
        q = xa_ref[:, h * dh:(h + 1) * dh]
        k = xa_ref[:, GDN_WIDTH + h * dh:GDN_WIDTH + (h + 1) * dh]
        v = xa_ref[:, 2 * GDN_WIDTH + h * dh:2 * GDN_WIDTH + (h + 1) * dh]
        q = q * lax.rsqrt(jnp.sum(q * q, axis=-1, keepdims=True) + 1e-6) * (dh ** -0.5)
        k = k * lax.rsqrt(jnp.sum(k * k, axis=-1, keepdims=True) + 1e-6)
        g = gcum[:, SM_A + h:SM_A + h + 1]
        gt = gcumt[SM_A - SM_B + h:SM_A - SM_B + h + 1, :]
        beta = beta_all[:, SM_B + h:SM_B + h + 1]
        decay = jnp.exp(jnp.where(incl, g - gt, -jnp.inf))
        kb = k * beta
        xs.append(-jnp.where(strict, _dot_nt(kb, k) * decay, 0.0))
        rhss.append(jnp.concatenate([v * beta, kb * jnp.exp(g)], axis=1))
        qs.append(q)
        ks.append(k)
        decays.append(decay)
        gs.append(g)
        yield
    ps = xs
    ns = xs
    levels = ROWS.bit_length() - 2
    for j in range(levels):
        if j < GDN_HI_LEVELS:
            pp = [_split(p) for p in ps]
            ps = [_dot3(p, p) for p in pp]
            yield
            pp = [_split(p) for p in ps]
            ns = [n + p + _dot3(_split(n), p2) for n, p, p2 in zip(ns, ps, pp)]
        else:
            ps = [_dot(p, p) for p in ps]
            yield
            ns = [n + p + _dot(n, p) for n, p in zip(ns, ps)]
        yield
    sols = [r + _dot(n, r) for n, r in zip(ns, rhss)]
    yield
    qks = [_dot_nt(q, k) * d for q, k, d in zip(qs, ks, decays)]
    sts = [st_ref[h] for h in heads]
    v_news = [s[:, 0:dh] - _dot(s[:, dh:2 * dh], st) for s, st in zip(sols, sts)]
    yield
    os_ = [_dot(q * jnp.exp(g), st) + _dot(qk, vn) for q, g, st, qk, vn in zip(qs, gs, sts, qks, v_news)]
    yield
    for h in heads:
        glast = gs[h][ROWS - 1:ROWS, :]
        kd = ks[h] * jnp.exp(glast - gs[h])
        st_ref[h] = sts[h] * jnp.exp(glast) + _dot(kd.T, v_news[h])
    yield
    for h in heads:
        o = os_[h]
        o = o * lax.rsqrt(jnp.mean(o * o, axis=-1, keepdims=True) + 1e-6)
        y_ref[:, MIX_GDN + h * dh:MIX_GDN + (h + 1) * dh] = (
            o * nw_ref[...] * _silu(gate_ref[:, h * dh:(h + 1) * dh])).astype(y_ref.dtype)
        yield


HG_LEVELS = ROWS.bit_length() - 1


def _hgrn_level_masks(mask_ref):
    row = _iota2((ROWS, ROWS), 0)
    col = _iota2((ROWS, ROWS), 1)
    for li in range(HG_LEVELS):
        m = 1 << li
        right = (row & m) != 0
        parent = ~(2 * m - 1)
        keep = (((row & m) ^ m) | (col & m) | ((row ^ col) & parent)) == 0
        mask_ref[0, li] = right.astype(F32)
        mask_ref[1, li] = jnp.where(right, LOG2E, -LOG2E)
        mask_ref[2, li] = keep.astype(F32)


def _hgrn_stages(q_ref, f_ref, i_ref, gate_ref, lb_ref, nw_ref, y_ref, st_ref, mask_ref):
    row = _iota2((ROWS, ROWS), 0)
    col = _iota2((ROWS, ROWS), 1)
    tril = (row >= col).astype(F32)
    rmod = _iota2((ROWS, LANES), 0) & (SUBLANES - 1)
    dk = HG_HEAD_DIM
    for h in range(HG_HEADS):
        sl = slice(h * dk, (h + 1) * dk)
        lb = lb_ref[:, sl]
        fr = f_ref[:, sl]
        log_sig = jnp.minimum(fr, 0.0) - jnp.log(1.0 + jnp.exp(-jnp.abs(fr)))
        a = jnp.log(lb)
        y = jnp.log1p(-lb) + log_sig
        log_f = jnp.maximum(a, y) + jnp.log(1.0 + jnp.exp(-jnp.abs(a - y)))
        k = (1.0 - lb) * _sigmoid(-fr)
        q = _silu(q_ref[:, sl])
        v = i_ref[:, sl]
        b = _dot_01(tril, log_f)
        yield

        st = st_ref[h]
        o = _dot_nt(q * jnp.exp(b), st)
        blast = b[ROWS - 1:ROWS, :]
        st_ref[h] = st * jnp.exp(blast) + _dot_tn(v, k * jnp.exp(blast - b))
        yield

        pmat = jnp.zeros((ROWS, ROWS), F32)
        b3 = b.reshape(ROWS // SUBLANES, SUBLANES, dk)
        m = 1
        while m < ROWS:
            if m < SUBLANES:
                ref = None
                for p0 in range(0, SUBLANES, 2 * m):
                    cand = jnp.broadcast_to(b3[:, p0 + m - 1:p0 + m, :], b3.shape).reshape(ROWS, dk)
                    ref = cand if ref is None else jnp.where(rmod >= p0, cand, ref)
            else:
                ref = jnp.concatenate(
                    [jnp.broadcast_to(b[p0 + m - 1:p0 + m, :], (2 * m, dk)) for p0 in range(0, ROWS, 2 * m)],
                    axis=0)
            li = m.bit_length() - 1
            z = jnp.where(mask_ref[0, li] != 0.0, q, k) * jnp.exp2((b - ref) * mask_ref[1, li])
            zb = z.astype(BF16)
            pmat = pmat + mask_ref[2, li] * _dot_nt(zb, zb)
            m *= 2
            yield
        o = o + _dot(pmat, v) + jnp.sum(q * k, axis=-1, keepdims=True) * v
        yield

        o = o * lax.rsqrt(jnp.mean(o * o, axis=-1, keepdims=True) + 1e-6)
        y_ref[:, MIX_HG + h * dk:MIX_HG + (h + 1) * dk] = (
            o * nw_ref[...] * _silu(gate_ref[:, sl])).astype(y_ref.dtype)
        yield


def _in_proj_stages(hb_ref, w_ref, out_ref):
    hb = hb_ref[...]
    for c0 in range(0, PROJ_COLS, PROJ_SLAB):
        out_ref[:, c0:c0 + PROJ_SLAB] = jnp.dot(hb, w_ref[:, c0:c0 + PROJ_SLAB], preferred_element_type=F32)
        yield


def _mixer_kernel(hb0_ref, hbn_ref, w_ref,
                  scw_ref, scb_ref, shp_ref, shpt_ref, dsk_ref, snw_ref,
                  gcw_ref, ghp_ref, ghpt_ref, gnw_ref, lb_ref, hnw_ref,
                  y_ref, proj_ref, sxpad_ref, sxa_ref, sst_ref, gxpad_ref, gxa_ref, gst_ref, hst_ref, hmask_ref):
    c = pl.program_id(1)

    @pl.when(c == 0)
    def _():
        sst_ref[...] = jnp.zeros(sst_ref.shape, F32)
        gst_ref[...] = jnp.zeros(gst_ref.shape, F32)
        hst_ref[...] = jnp.zeros(hst_ref.shape, F32)
        for pad_ref in (sxpad_ref, gxpad_ref):
            taps, _, cols = pad_ref.shape
            pad_ref[:, SUBLANES + ROWS:2 * SUBLANES + ROWS, :] = jnp.zeros((taps, SUBLANES, cols), F32)
        _hgrn_level_masks(hmask_ref)
        for _ in _in_proj_stages(hb0_ref, w_ref, proj_ref.at[0]):
            pass

    cur = proj_ref.at[c % 2]
    nxt = proj_ref.at[(c + 1) % 2]

    def seg(off, width):
        return cur.at[:, off:off + width]

    sm_ref = seg(OFF_SMALL, LANES)
    streams = [
        _gdn_stages(seg(OFF_QKV, 3 * GDN_WIDTH), seg(OFF_GGATE, GDN_WIDTH), sm_ref, gcw_ref, ghp_ref, ghpt_ref,
                    gnw_ref, y_ref, gxpad_ref, gxa_ref, gst_ref),
        _ssd_stages(seg(OFF_Z, SSD_WIDTH), seg(OFF_XBC, SSD_XBC), sm_ref, scw_ref, scb_ref, shp_ref, shpt_ref,
                    dsk_ref, snw_ref, y_ref, sxpad_ref, sxa_ref, sst_ref),
        _hgrn_stages(seg(OFF_HQ, HG_WIDTH), seg(OFF_HF, HG_WIDTH), seg(OFF_HI, HG_WIDTH),
                     seg(OFF_HGATE, HG_WIDTH), lb_ref, hnw_ref, y_ref, hst_ref, hmask_ref),
        _in_proj_stages(hbn_ref, w_ref, nxt),
    ]
    live = [True] * len(streams)
    rnd = 0
    while any(live):
        advance = [1, 1, 1 if rnd < HG_SLOW_ROUNDS else 2, 1]
        rnd += 1
        for s, stream in enumerate(streams):
            for _ in range(advance[s]):
                if live[s]:
                    try:
                        next(stream)
                    except StopIteration:
                        live[s] = False


def _mixers(hb, w_proj, layer, nb, seq, ssd_conv_w, ssd_conv_b, ssd_dt_bias, ssd_a_log, ssd_d, ssd_norm_w,
            gdn_conv_w, gdn_dt_bias, gdn_a_log, gdn_norm_w, lb, hg_norm_w):
    nblk = seq // ROWS
    shp = jnp.zeros((SUBLANES, LANES), F32)
    shp = shp.at[0, SM_DT:SM_DT + SSD_HEADS].set(ssd_dt_bias).at[1, SM_DT:SM_DT + SSD_HEADS].set(ssd_a_log)
    ghp = jnp.zeros((SUBLANES, LANES), F32)
    ghp = ghp.at[0, SM_A:SM_A + GDN_HEADS].set(gdn_dt_bias).at[1, SM_A:SM_A + GDN_HEADS].set(gdn_a_log)
    dsk = jnp.repeat(ssd_d, SSD_HEAD_DIM)[None, :]
    const = lambda b, c: (0, 0)

    def whole(shape):
        return pl.BlockSpec(shape, const)

    return pl.pallas_call(
        _mixer_kernel,
        out_shape=jax.ShapeDtypeStruct((nb * seq, D_MIX), BF16),
        grid=(nb, nblk),
        in_specs=[
            pl.BlockSpec((ROWS, D_MODEL), lambda b, c: (b * nblk + c, 0)),
            pl.BlockSpec((ROWS, D_MODEL), lambda b, c: (b * nblk + jnp.minimum(c + 1, nblk - 1), 0)),
            pl.BlockSpec((None, D_MODEL, PROJ_COLS), lambda b, c: (layer, 0, 0), pipeline_mode=pl.Buffered(1)),
            whole((SSD_CONV, SSD_XBC)), whole((1, SSD_XBC)), whole((SUBLANES, LANES)), whole((LANES, SUBLANES)),
            whole((1, SSD_WIDTH)), whole((1, SSD_WIDTH)),
            whole((GDN_CONV, 3 * GDN_WIDTH)), whole((SUBLANES, LANES)), whole((LANES, SUBLANES)),
            whole((1, GDN_HEAD_DIM)), whole((1, HG_WIDTH)), whole((1, HG_HEAD_DIM)),
        ],
        out_specs=pl.BlockSpec((ROWS, D_MIX), lambda b, c: (b * nblk + c, 0)),
        scratch_shapes=[pltpu.VMEM((2, ROWS, PROJ_COLS), F32),
                        pltpu.VMEM((SSD_CONV - 1, ROWS + 2 * SUBLANES, SSD_XBC), F32),
                        pltpu.VMEM((ROWS, SSD_XBC), F32),
                        pltpu.VMEM((SSD_GROUPS, SSD_STATE, SSD_WIDTH // SSD_GROUPS), F32),
                        pltpu.VMEM((GDN_CONV - 1, ROWS + 2 * SUBLANES, 3 * GDN_WIDTH), F32),
                        pltpu.VMEM((ROWS, 3 * GDN_WIDTH), F32),
                        pltpu.VMEM((GDN_HEADS, GDN_HEAD_DIM, GDN_HEAD_DIM), F32),
                        pltpu.VMEM((HG_HEADS, HG_HEAD_DIM, HG_HEAD_DIM), F32),
                        pltpu.VMEM((3, HG_LEVELS, ROWS, ROWS), F32)],
        compiler_params=_params(("parallel", "arbitrary")),
        name="in_proj_mixers",
    )(hb, hb, w_proj,
      ssd_conv_w, ssd_conv_b[None, :], shp, shp.T, dsk, ssd_norm_w[None, :],
      gdn_conv_w, ghp, ghp.T, gdn_norm_w[None, :], lb[None, :], hg_norm_w[None, :])


def _layer_norm(x, g, b):
    mu = jnp.mean(x, axis=-1, keepdims=True)
    xc = x - mu
    var = jnp.mean(xc * xc, axis=-1, keepdims=True)
    return xc * lax.rsqrt(var + 1e-5) * g + b


ROUTER_ROWS = 32
OUT_SLAB = 256


def _outproj_rows(mix, h, g_ref, b_ref, wr_ref, br_ref):
    h1 = _layer_norm(DN_ALPHA * h + mix, g_ref[...], b_ref[...])

    h_hi, h_lo = _split(h1)
    wr = wr_ref[...]
    lt = _dot_nt(wr, h_hi) + _dot_nt(wr, h_lo)
    lt = lt[0:ROUTER_ROWS, :] + lt[ROUTER_ROWS:2 * ROUTER_ROWS, :] + br_ref[...]
    glog = [lt[g:g + 1, :] for g in range(MOE_GROUPS)]
    elog = [lt[MOE_GROUPS + e:MOE_GROUPS + e + 1, :] for e in range(N_EXPERTS)]

    def softmax(rows):
        m = rows[0]
        for r in rows[1:]:
            m = jnp.maximum(m, r)
        ex = [jnp.exp(r - m) for r in rows]
        tot = ex[0]
        for r in ex[1:]:
            tot = tot + r
        return [r / tot for r in ex]

    def top1(rows):
        best = rows[0]
        for r in rows[1:]:
            best = jnp.maximum(best, r)
        idx = jnp.full_like(best, float(len(rows) - 1))
        for j in range(len(rows) - 2, -1, -1):
            idx = jnp.where(rows[j] == best, float(j), idx)
        return best, idx

    g_p, g_idx = top1(softmax(glog))
    chosen = []
    for k in range(EXPERTS_PER_GROUP):
        acc = jnp.zeros_like(g_p)
        for g in range(MOE_GROUPS):
            acc = jnp.where(g_idx == float(g), elog[g * EXPERTS_PER_GROUP + k], acc)
        chosen.append(acc)
    eprob = softmax(chosen)
    p1, i1 = top1(eprob)
    p2, i2 = top1([jnp.where(i1 == float(k), -1.0, eprob[k]) for k in range(EXPERTS_PER_GROUP)])
    denom = p1 + p2
    w1 = g_p * p1 / denom
    w2 = g_p * p2 / denom
    wk = [jnp.where(i1 == float(k), w1, 0.0) + jnp.where(i2 == float(k), w2, 0.0)
          for k in range(EXPERTS_PER_GROUP)]
    rid = _iota2((ROUTER_ROWS, lt.shape[1]), 0)
    comb_t = jnp.where(rid == N_EXPERTS, g_idx, 0.0)
    for g in range(MOE_GROUPS):
        for k in range(EXPERTS_PER_GROUP):
            comb_t = jnp.where(rid == g * EXPERTS_PER_GROUP + k,
                               jnp.where(g_idx == float(g), wk[k], 0.0), comb_t)
    comb_t = jnp.concatenate([comb_t, jnp.zeros((LANES - ROUTER_ROWS, lt.shape[1]), F32)], axis=0)
    return h1, comb_t.T


def _outproj_kernel(y_ref, h_ref, w_ref, g_ref, b_ref, wr_ref, br_ref, h1_ref, h1b_ref, comb_ref):
    slab = min(OUT_SLAB, y_ref.shape[0])
    slabs = [slice(r0, r0 + slab) for r0 in range(0, y_ref.shape[0], slab)]
    mixes = [jnp.dot(y_ref[rows, :], w_ref[...], preferred_element_type=F32) for rows in slabs]
    for rows, mix in zip(slabs, mixes):
        h1, comb = _outproj_rows(mix, h_ref[rows, :], g_ref, b_ref, wr_ref, br_ref)
        h1_ref[rows, :] = h1
        h1b_ref[rows, :] = h1.astype(BF16)
        comb_ref[rows, :] = comb


def _router_params(w_group, b_group, w_expert, b_expert):
    pad = ROUTER_ROWS - MOE_GROUPS - N_EXPERTS
    w_t = jnp.pad(jnp.concatenate([w_group, w_expert], axis=1).T, ((0, pad), (0, 0)))
    b_col = jnp.pad(jnp.concatenate([b_group, b_expert]), (0, pad))[:, None]
    return jnp.concatenate(_split(w_t), axis=0), b_col


def _outproj(y, h, w_out, layer, ln_g, ln_b, w_router, b_router, tm):
    t = h.shape[0]
    rowmap = lambda i: (i, 0)
    const = lambda i: (0, 0)
    return pl.pallas_call(
        _outproj_kernel,
        out_shape=(jax.ShapeDtypeStruct((t, D_MODEL), F32),
                   jax.ShapeDtypeStruct((t, D_MODEL), BF16),
                   jax.ShapeDtypeStruct((t, LANES), F32)),
        grid=(t // tm,),
        in_specs=[pl.BlockSpec((tm, D_MIX), rowmap),
                  pl.BlockSpec((tm, D_MODEL), rowmap),
                  pl.BlockSpec((None, D_MIX, D_MODEL), lambda i: (layer, 0, 0)),
                  pl.BlockSpec((1, D_MODEL), const),
                  pl.BlockSpec((1, D_MODEL), const),
                  pl.BlockSpec((2 * ROUTER_ROWS, D_MODEL), const),
                  pl.BlockSpec((ROUTER_ROWS, 1), const)],
        out_specs=(pl.BlockSpec((tm, D_MODEL), rowmap),
                   pl.BlockSpec((tm, D_MODEL), rowmap),
                   pl.BlockSpec((tm, LANES), rowmap)),
        compiler_params=_params(("parallel",)),
        name="out_proj_ln_router",
    )(y, h, w_out, ln_g[None, :], ln_b[None, :], w_router, b_router)


def _prefix_lanes(v, idx):
    axis = 1 if v.shape[0] == 1 else 0
    out = jnp.zeros_like(v)
    for g in range(MOE_GROUPS - 1):
        vg = v[:, g:g + 1] if axis == 1 else v[g:g + 1, :]
        out = out + jnp.where(idx > g, vg, 0.0)
    return out


def _moe_kernel(hb_ref, h_ref, comb_ref, wgu_ref, wdn_ref, g_ref, b_ref, o_ref, ob_ref,
                xs_ref, cw_ref, ys_ref):
    tm = hb_ref.shape[0]
    comb = comb_ref[...]
    lane = _iota2((tm, LANES), 1).astype(F32)
    gid = comb[:, N_EXPERTS:N_EXPERTS + 1]
    gsel = (lane == gid).astype(F32)
    tr = _iota2((tm, tm), 0)
    tc = _iota2((tm, tm), 1)

    ns = xs_ref.shape[0] - MOE_CHUNK

    def aligned(cnt):
        return jnp.floor((cnt + (MOE_ALIGN - 1)) * (1.0 / MOE_ALIGN)) * MOE_ALIGN

    rank_c = _dot((tc < tr).astype(F32), gsel)
    cnt_r = jnp.sum(gsel, axis=0, keepdims=True)
    start_r = _prefix_lanes(aligned(cnt_r), _iota2((1, LANES), 1))
    dest_c = jnp.sum(gsel * (start_r + rank_c), axis=1, keepdims=True)
    eye8 = (_iota2((SUBLANES, LANES), 0) == _iota2((SUBLANES, LANES), 1)).astype(F32)
    gsel_t = _dot_nt(eye8, gsel)
    rank_r = _dot(gsel_t, (tr < tc).astype(F32))
    cnt_c = jnp.sum(gsel_t, axis=1, keepdims=True)
    start_c = _prefix_lanes(aligned(cnt_c), _iota2((SUBLANES, 1), 0))
    dest_r = jnp.sum(gsel_t * (start_c + rank_r), axis=0, keepdims=True)

    perm = (dest_r == _iota2((ns, tm), 0).astype(F32)).astype(BF16)
    xs_ref[0:ns, :] = jnp.dot(perm, hb_ref[...], preferred_element_type=F32).astype(BF16)
    xs_ref[ns:ns + MOE_CHUNK, :] = jnp.zeros((MOE_CHUNK, D_MODEL), BF16)
    c_hi, c_lo = _split(comb)
    cw2 = jnp.dot(perm, jnp.concatenate([c_hi, c_lo], axis=1), preferred_element_type=F32)
    cw_ref[0:ns, :] = cw2[:, 0:LANES] + cw2[:, LANES:2 * LANES]
    cw_ref[ns:ns + MOE_CHUNK, :] = jnp.zeros((MOE_CHUNK, LANES), F32)
    ys_ref[...] = jnp.zeros(ys_ref.shape, F32)

    lane1 = _iota2((1, LANES), 1)
    starts = [jnp.sum(jnp.where(lane1 == g, start_r, 0.0)).astype(jnp.int32) for g in range(MOE_GROUPS)]
    cnts = [jnp.sum(jnp.where(lane1 == g, cnt_r, 0.0)) for g in range(MOE_GROUPS)]
    def block(g, j):
        rows = pl.ds(pl.multiple_of(starts[g] + j * MOE_CHUNK, MOE_ALIGN), MOE_CHUNK)
        x = xs_ref[rows, :]
        cw = cw_ref[rows, :]
        hm = []
        for e in range(EXPERTS_PER_GROUP):
            ex = g * EXPERTS_PER_GROUP + e
            gu = jnp.dot(x, wgu_ref[ex], preferred_element_type=F32)
            hm.append((_silu(gu[:, 0:D_EXPERT]) * gu[:, D_EXPERT:2 * D_EXPERT] * cw[:, ex:ex + 1]).astype(BF16))
        ys_ref[rows, :] += jnp.dot(jnp.concatenate(hm, axis=1), wdn_ref[g], preferred_element_type=F32)

    for g in range(MOE_GROUPS):
        block(g, 0)

    most = cnts[0]
    for g in range(1, MOE_GROUPS):
        most = jnp.maximum(most, cnts[g])

    @pl.when(most > MOE_CHUNK)
    def _():
        for j in range(1, -(-tm // MOE_CHUNK)):
            for g in range(MOE_GROUPS):
                pl.when(cnts[g] > j * MOE_CHUNK)(lambda g=g, j=j: block(g, j))

    unperm = (dest_c == _iota2((tm, ns), 1).astype(F32)).astype(BF16)
    y = jnp.dot(unperm, ys_ref[0:ns, :].astype(BF16), preferred_element_type=F32)
    h2 = _layer_norm(DN_ALPHA * h_ref[...] + y, g_ref[...], b_ref[...])
    o_ref[...] = h2
    ob_ref[...] = h2.astype(BF16)


def _moe(hb, h, comb, w_gu, w_dn, layer, ln_g, ln_b, tm):
    t = h.shape[0]
    rowmap = lambda i: (i, 0)
    const = lambda i: (0, 0)
    const3 = lambda i: (layer, 0, 0, 0)
    sorted_rows = -(-(tm + MOE_GROUPS * MOE_ALIGN) // LANES) * LANES + MOE_CHUNK
    return pl.pallas_call(
        _moe_kernel,
        out_shape=(jax.ShapeDtypeStruct((t, D_MODEL), F32),
                   jax.ShapeDtypeStruct((t, D_MODEL), BF16)),
        grid=(t // tm,),
        in_specs=[pl.BlockSpec((tm, D_MODEL), rowmap),
                  pl.BlockSpec((tm, D_MODEL), rowmap),
                  pl.BlockSpec((tm, LANES), rowmap),
                  pl.BlockSpec((None, N_EXPERTS, D_MODEL, 2 * D_EXPERT), const3, pipeline_mode=pl.Buffered(1)),
                  pl.BlockSpec((None, MOE_GROUPS, EXPERTS_PER_GROUP * D_EXPERT, D_MODEL), const3,
                               pipeline_mode=pl.Buffered(1)),
                  pl.BlockSpec((1, D_MODEL), const),
                  pl.BlockSpec((1, D_MODEL), const)],
        out_specs=(pl.BlockSpec((tm, D_MODEL), rowmap),
                   pl.BlockSpec((tm, D_MODEL), rowmap)),
        scratch_shapes=[pltpu.VMEM((sorted_rows, D_MODEL), BF16),
                        pltpu.VMEM((sorted_rows, LANES), F32),
                        pltpu.VMEM((sorted_rows, D_MODEL), F32)],
        compiler_params=pltpu.CompilerParams(dimension_semantics=("parallel",),
                                             vmem_limit_bytes=MOE_VMEM_LIMIT),
        name="moe_ln",
    )(hb, h, comb, w_gu, w_dn, ln_g[None, :], ln_b[None, :])


W_IN_SEGMENTS = ((SSD_WIDTH, OFF_Z), (SSD_XBC, OFF_XBC), (SSD_HEADS, OFF_SMALL + SM_DT),
                 (3 * GDN_WIDTH, OFF_QKV), (GDN_WIDTH, OFF_GGATE), (GDN_HEADS, OFF_SMALL + SM_B),
                 (GDN_HEADS, OFF_SMALL + SM_A), (HG_WIDTH, OFF_HQ), (HG_WIDTH, OFF_HF),
                 (HG_WIDTH, OFF_HI), (HG_WIDTH, OFF_HGATE))
W_T_BLOCK = 256
W_MAIN_BLOCKS = OFF_SMALL // W_T_BLOCK


def _w_in_kernel(tbl_ref, w_ref, small_ref, o_ref):
    j = pl.program_id(1)

    @pl.when(j < W_MAIN_BLOCKS)
    def _():
        o_ref[...] = w_ref[0].T.astype(BF16)

    @pl.when(j >= W_MAIN_BLOCKS)
    def _():
        o_ref[...] = small_ref[...].T.astype(BF16)


def _split_w_in(w_in):
    depth = w_in.shape[0]
    w_t = jnp.swapaxes(w_in, 1, 2)
    src_of, small_rows, src = {}, [], 0
    for width, dst in W_IN_SEGMENTS:
        if width >= W_T_BLOCK:
            for off in range(0, width, W_T_BLOCK):
                src_of[(dst + off) // W_T_BLOCK] = src + off
        else:
            small_rows.append(w_t[:, src:src + width, :])
        src += width
    table = jnp.array([src_of.get(j, 0) for j in range(PROJ_COLS // W_T_BLOCK)], jnp.int32)
    small = jnp.concatenate(small_rows, axis=1)
    small = jnp.pad(small, ((0, 0), (0, PROJ_COLS - OFF_SMALL - small.shape[1]), (0, 0)))
    return pl.pallas_call(
        _w_in_kernel,
        out_shape=jax.ShapeDtypeStruct((depth, D_MODEL, PROJ_COLS), BF16),
        grid_spec=pltpu.PrefetchScalarGridSpec(
            num_scalar_prefetch=1,
            grid=(depth, PROJ_COLS // W_T_BLOCK),
            in_specs=[pl.BlockSpec((pl.Element(1), pl.Element(W_T_BLOCK), pl.Element(D_MODEL)),
                                   lambda l, j, tbl: (l, pl.multiple_of(tbl[j], SUBLANES), 0)),
                      pl.BlockSpec((None, PROJ_COLS - OFF_SMALL, D_MODEL), lambda l, j, tbl: (l, 0, 0))],
            out_specs=pl.BlockSpec((None, D_MODEL, W_T_BLOCK), lambda l, j, tbl: (l, 0, j))),
        compiler_params=_params(("parallel", "arbitrary")),
        name="w_in_relayout",
    )(table, w_t, small)


def kernel(x, w_in, ssd_conv_w, ssd_conv_b, ssd_dt_bias, ssd_a_log, ssd_d, ssd_norm_w, gdn_conv_w, gdn_dt_bias, gdn_a_log, gdn_norm_w, hg_lb_logits, hg_norm_w, w_out, ln1_g, ln1_b, w_router_group, b_router_group, w_router_expert, b_router_expert, w_expert_gate_up, w_expert_down, ln2_g, ln2_b):
    nb, seq, d = x.shape
    t = nb * seq
    lb_cum = jnp.cumsum(jax.nn.softmax(hg_lb_logits.astype(F32), axis=0), axis=0)
    lb_all = lb_cum - lb_cum[0:1]
    h = x.reshape(t, d)
    hb = h.astype(BF16)
    w_proj = _split_w_in(w_in)
    w_out_b = w_out.astype(BF16)
    w_gu_b = w_expert_gate_up.astype(BF16)
    w_dn_b = w_expert_down.astype(BF16).reshape(DEPTH, MOE_GROUPS, EXPERTS_PER_GROUP * D_EXPERT, D_MODEL)
    for l in range(DEPTH):
        y = _mixers(hb, w_proj, l, nb, seq, ssd_conv_w[l], ssd_conv_b[l], ssd_dt_bias[l], ssd_a_log[l], ssd_d[l],
                    ssd_norm_w[l], gdn_conv_w[l], gdn_dt_bias[l], gdn_a_log[l], gdn_norm_w[l],
                    lb_all[l], hg_norm_w[l])
        w_router, b_router = _router_params(w_router_group[l], b_router_group[l],
                                            w_router_expert[l], b_router_expert[l])
        h1, h1b, comb = _outproj(y, h, w_out_b, l, ln1_g[l], ln1_b[l], w_router, b_router, 1024)
        h, hb = _moe(h1b, h1, comb, w_gu_b, w_dn_b, l, ln2_g[l], ln2_b[l], 512)
    return h.reshape(nb, seq, d)
```

```python
import jax
import jax.numpy as jnp
from jax import lax
from jax.experimental import pallas as pl
from jax.experimental.pallas import tpu as pltpu

F32 = jnp.float32
BF16 = jnp.bfloat16

D_MODEL = 1024
DEPTH = 4
SSD_HEADS = 16
SSD_HEAD_DIM = 64
SSD_WIDTH = 1024
SSD_GROUPS = 2
SSD_STATE = 128
SSD_BC = 256
SSD_XBC = 1536
SSD_CONV = 4
GDN_HEADS = 4
GDN_HEAD_DIM = 128
GDN_WIDTH = 512
GDN_CONV = 4
HG_HEADS = 4
HG_HEAD_DIM = 128
HG_WIDTH = 512
D_MIX = 2048
MOE_GROUPS = 4
EXPERTS_PER_GROUP = 4
N_EXPERTS = 16
D_EXPERT = 256
DN_ALPHA = (2 * DEPTH) ** 0.25
LOG2E = 1.4426950408889634

LANES = 128
SUBLANES = 8
ROWS = 128
VMEM_LIMIT = 48 * 1024 * 1024
MOE_VMEM_LIMIT = 56 * 1024 * 1024
CONV_SLAB = 256
GDN_HI_LEVELS = 2
MOE_CHUNK = 160
MOE_ALIGN = 16
PROJ_SLAB = 768
HG_SLOW_ROUNDS = 12

OFF_XBC = 0
OFF_QKV = 1536
OFF_Z = 3072
OFF_GGATE = 4096
OFF_HQ = 4608
OFF_HF = 5120
OFF_HI = 5632
OFF_HGATE = 6144
OFF_SMALL = 6656
PROJ_COLS = 6912
SM_DT = 0
SM_B = 16
SM_A = 20
MIX_SSD = 0
MIX_GDN = SSD_WIDTH
MIX_HG = SSD_WIDTH + GDN_WIDTH


def _dot(a, b):
    return jnp.dot(a.astype(BF16), b.astype(BF16), preferred_element_type=F32)


def _dot_nt(a, b):
    return lax.dot_general(a.astype(BF16), b.astype(BF16), (((1,), (1,)), ((), ())),
                           preferred_element_type=F32)


def _dot_tn(a, b):
    return lax.dot_general(a.astype(BF16), b.astype(BF16), (((0,), (0,)), ((), ())),
                           preferred_element_type=F32)


def _split3(a):
    p1 = a.astype(BF16)
    r1 = a - p1.astype(F32)
    p2 = r1.astype(BF16)
    return p1, p2, (r1 - p2.astype(F32)).astype(BF16)


def _dot_01(m01, x):
    m = m01.astype(BF16)
    p1, p2, p3 = _split3(x)
    return (jnp.dot(m, p1, preferred_element_type=F32) + jnp.dot(m, p2, preferred_element_type=F32)
            + jnp.dot(m, p3, preferred_element_type=F32))


def _dot_x01(x, m01):
    m = m01.astype(BF16)
    p1, p2, p3 = _split3(x)
    return (jnp.dot(p1, m, preferred_element_type=F32) + jnp.dot(p2, m, preferred_element_type=F32)
            + jnp.dot(p3, m, preferred_element_type=F32))


def _split(a):
    hi = a.astype(BF16)
    return hi, (a - hi.astype(F32)).astype(BF16)


def _dot3(a, b):
    (ah, al), (bh, bl) = a, b
    return (jnp.dot(ah, bh, preferred_element_type=F32) + jnp.dot(ah, bl, preferred_element_type=F32)
            + jnp.dot(al, bh, preferred_element_type=F32))


def _sigmoid(x):
    return 1.0 / (1.0 + jnp.exp(-x))


def _silu(x):
    return x * _sigmoid(x)


def _softplus(x):
    return jnp.maximum(x, 0.0) + jnp.log(1.0 + jnp.exp(-jnp.abs(x)))


def _iota2(shape, dim):
    return lax.broadcasted_iota(jnp.int32, shape, dim)


def _params(sem):
    return pltpu.CompilerParams(dimension_semantics=sem, vmem_limit_bytes=VMEM_LIMIT)


def _matmul_kernel(x_ref, w_ref, o_ref):
    o_ref[...] = jnp.dot(x_ref[...], w_ref[...], preferred_element_type=F32)


def _matmul(x, w, tm, tn):
    t, k = x.shape
    n = w.shape[1]
    return pl.pallas_call(
        _matmul_kernel,
        out_shape=jax.ShapeDtypeStruct((t, n), F32),
        grid=(n // tn, t // tm),
        in_specs=[pl.BlockSpec((tm, k), lambda j, i: (i, 0)),
                  pl.BlockSpec((k, tn), lambda j, i: (0, j))],
        out_specs=pl.BlockSpec((tm, tn), lambda j, i: (i, j)),
        compiler_params=_params(("parallel", "arbitrary")),
        name="in_proj",
    )(x, w)


def _conv_silu_stages(x_ref, xpad_ref, xa_ref, cw_ref, cb_ref):
    rows, cols = x_ref.shape
    k = cw_ref.shape[0]
    base = SUBLANES
    for c0 in range(0, cols, CONV_SLAB):
        cs = slice(c0, c0 + CONV_SLAB)
        x = x_ref[:, cs]
        acc = x * cw_ref[k - 1:k, cs]
        if cb_ref is not None:
            acc = acc + cb_ref[:, cs]
        for s in range(1, k):
            buf = xpad_ref.at[s - 1]
            buf[base:base + SUBLANES, cs] = buf[base + rows:base + rows + SUBLANES, cs]
            buf[base + s:base + s + rows, cs] = x
            acc = acc + buf[base:base + rows, cs] * cw_ref[k - 1 - s:k - s, cs]
        xa_ref[:, cs] = _silu(acc)
        yield


def _ssd_stages(z_ref, xbc_ref, sm_ref, cw_ref, cb_ref, hp_ref, hpt_ref, dsk_ref, nw_ref, expand_ref,
                y_ref, xpad_ref, xa_ref, st_ref):
    yield from _conv_silu_stages(xbc_ref, xpad_ref, xa_ref, cw_ref, cb_ref)

    row = _iota2((ROWS, ROWS), 0)
    col = _iota2((ROWS, ROWS), 1)
    causal = row >= col
    tril = causal.astype(F32)
    triu = (row <= col).astype(F32)

    sm = sm_ref[...]
    dt = _softplus(sm + hp_ref[0:1, :])
    da = dt * (-jnp.exp(hp_ref[1:2, :]))
    acum = _dot_01(tril, da)
    smt = sm.T
    dtt = _softplus(smt[0:SSD_HEADS, :] + hpt_ref[0:SSD_HEADS, 0:1])
    dat = dtt * (-jnp.exp(hpt_ref[0:SSD_HEADS, 1:2]))
    acumt = _dot_x01(dat, triu)

    heads3 = jnp.concatenate([dt, jnp.exp(acum), jnp.exp(acum[ROWS - 1:ROWS, :] - acum)], axis=0)
    h_hi, h_lo = _split(heads3)
    heads6 = jnp.concatenate([h_hi, h_lo], axis=0)
    lane_lo = _iota2((ROWS, LANES), 1) < SSD_HEAD_DIM
    yield

    hg = SSD_HEADS // SSD_GROUPS
    gw = SSD_WIDTH // SSD_GROUPS
    for g in range(SSD_GROUPS):
        gs = slice(g * gw, (g + 1) * gw)
        ex = jnp.dot(heads6, expand_ref[g], preferred_element_type=F32)
        ex = ex[0:3 * ROWS, :] + ex[3 * ROWS:6 * ROWS, :]
        dtx = ex[0:ROWS, :]
        eax = ex[ROWS:2 * ROWS, :]
        tex = ex[2 * ROWS:3 * ROWS, :]
        yield
        xs = xa_ref[:, gs]
        xdt = xs * dtx
        xdt_b = xdt.astype(BF16)
        xend_b = (xdt * tex).astype(BF16)
        bm = xa_ref[:, SSD_WIDTH + g * SSD_STATE:SSD_WIDTH + (g + 1) * SSD_STATE]
        cm = xa_ref[:, SSD_WIDTH + SSD_BC + g * SSD_STATE:SSD_WIDTH + SSD_BC + (g + 1) * SSD_STATE]
        cm_b = cm.astype(BF16)
        cb = _dot_nt(cm_b, bm)
        yield
        yd = []
        for pair in range(hg // 2):
            res = []
            for sub in range(2):
                h = g * hg + pair * 2 + sub
                seg = acum[:, h:h + 1] - acumt[h:h + 1, :]
                lmat = cb * jnp.exp(jnp.where(causal, seg, -jnp.inf))
                c0 = pair * 2 * SSD_HEAD_DIM
                res.append(jnp.dot(lmat.astype(BF16), xdt_b[:, c0:c0 + LANES],
                                   preferred_element_type=F32))
            yd.append(jnp.where(lane_lo, res[0], res[1]))
            yield
        yd = jnp.concatenate(yd, axis=1)
        st = st_ref[g]
        yoff = jnp.dot(cm_b, st.astype(BF16), preferred_element_type=F32) * eax
        st_ref[g] = (st * eax[ROWS - 1:ROWS, :]
                     + jnp.dot(bm.T.astype(BF16), xend_b, preferred_element_type=F32))
        yield
        y = yd + yoff + xs * dsk_ref[:, gs]
        y = y * _silu(z_ref[:, gs])
        ms = jnp.mean(y * y, axis=-1, keepdims=True)
        y_ref[:, MIX_SSD + g * gw:MIX_SSD + (g + 1) * gw] = (
            y * lax.rsqrt(ms + 1e-6) * nw_ref[:, gs]).astype(y_ref.dtype)
        yield


def _gdn_stages(qkv_ref, gate_ref, sm_ref, cw_ref, hp_ref, hpt_ref, nw_ref, y_ref, xpad_ref, xa_ref, st_ref):
    yield from _conv_silu_stages(qkv_ref, xpad_ref, xa_ref, cw_ref, None)

    row = _iota2((ROWS, ROWS), 0)
    col = _iota2((ROWS, ROWS), 1)
    incl = row >= col
    strict = row > col
    tril = incl.astype(F32)
    triu = (row <= col).astype(F32)

    sm = sm_ref[...]
    la = -jnp.exp(hp_ref[1:2, :]) * _softplus(sm + hp_ref[0:1, :])
    gcum = _dot_01(tril, la)
    smt = sm.T
    lat = (-jnp.exp(hpt_ref[SM_B:SM_B + SUBLANES, 1:2])
           * _softplus(smt[SM_B:SM_B + SUBLANES, :] + hpt_ref[SM_B:SM_B + SUBLANES, 0:1]))
    gcumt = _dot_x01(lat, triu)
    beta_all = _sigmoid(sm)
    yield

    dh = GDN_HEAD_DIM
    heads = range(GDN_HEADS)
    qs, ks, xs, rhss, decays, gs = [], [], [], [], [], []
    for h in heads:
        q = xa_ref[:, h * dh:(h + 1) * dh]
        k = xa_ref[:, GDN_WIDTH + h * dh:GDN_WIDTH + (h + 1) * dh]
        v = xa_ref[:, 2 * GDN_WIDTH + h * dh:2 * GDN_WIDTH + (h + 1) * dh]
        q = q * lax.rsqrt(jnp.sum(q * q, axis=-1, keepdims=True) + 1e-6) * (dh ** -0.5)
        k = k * lax.rsqrt(jnp.sum(k * k, axis=-1, keepdims=True) + 1e-6)
        g = gcum[:, SM_A + h:SM_A + h + 1]
        gt = gcumt[SM_A - SM_B + h:SM_A - SM_B + h + 1, :]
        beta = beta_all[:, SM_B + h:SM_B + h + 1]
        decay = jnp.exp(jnp.where(incl, g - gt, -jnp.inf))
        kb = k * beta
        xs.append(-jnp.where(strict, _dot_nt(kb, k) * decay, 0.0))
        rhss.append(jnp.concatenate([v * beta, kb * jnp.exp(g)], axis=1))
        qs.append(q)
        ks.append(k)
        decays.append(decay)
        gs.append(g)
        yield
    ps = xs
    ns = xs
    levels = ROWS.bit_length() - 2
    for j in range(levels):
        if j < GDN_HI_LEVELS:
            pp = [_split(p) for p in ps]
            ps = [_dot3(p, p) for p in pp]
            yield
            pp = [_split(p) for p in ps]
            ns = [n + p + _dot3(_split(n), p2) for n, p, p2 in zip(ns, ps, pp)]
        else:
            ps = [_dot(p, p) for p in ps]
            yield
            ns = [n + p + _dot(n, p) for n, p in zip(ns, ps)]
        yield
    sols = [r + _dot(n, r) for n, r in zip(ns, rhss)]
    yield
    qks = [_dot_nt(q, k) * d for q, k, d in zip(qs, ks, decays)]
    sts = [st_ref[h] for h in heads]
    v_news = [s[:, 0:dh] - _dot(s[:, dh:2 * dh], st) for s, st in zip(sols, sts)]
    yield
    os_ = [_dot(q * jnp.exp(g), st) + _dot(qk, vn) for q, g, st, qk, vn in zip(qs, gs, sts, qks, v_news)]
    yield
    for h in heads:
        glast = gs[h][ROWS - 1:ROWS, :]
        kd = ks[h] * jnp.exp(glast - gs[h])
        st_ref[h] = sts[h] * jnp.exp(glast) + _dot(kd.T, v_news[h])
    yield
    for h in heads:
        o = os_[h]
        o = o * lax.rsqrt(jnp.mean(o * o, axis=-1, keepdims=True) + 1e-6)
        y_ref[:, MIX_GDN + h * dh:MIX_GDN + (h + 1) * dh] = (
            o * nw_ref[...] * _silu(gate_ref[:, h * dh:(h + 1) * dh])).astype(y_ref.dtype)
        yield


HG_LEVELS = ROWS.bit_length() - 1


def _hgrn_level_masks(mask_ref):
    row = _iota2((ROWS, ROWS), 0)
    col = _iota2((ROWS, ROWS), 1)
    for li in range(HG_LEVELS):
        m = 1 << li
        right = (row & m) != 0
        parent = ~(2 * m - 1)
        keep = (((row & m) ^ m) | (col & m) | ((row ^ col) & parent)) == 0
        mask_ref[0, li] = right.astype(F32)
        mask_ref[1, li] = jnp.where(right, LOG2E, -LOG2E)
        mask_ref[2, li] = keep.astype(F32)


def _hgrn_stages(q_ref, f_ref, i_ref, gate_ref, lb_ref, nw_ref, y_ref, st_ref, mask_ref):
    row = _iota2((ROWS, ROWS), 0)
    col = _iota2((ROWS, ROWS), 1)
    tril = (row >= col).astype(F32)
    rmod = _iota2((ROWS, LANES), 0) & (SUBLANES - 1)
    dk = HG_HEAD_DIM
    for h in range(HG_HEADS):
        sl = slice(h * dk, (h + 1) * dk)
        lb = lb_ref[:, sl]
        fr = f_ref[:, sl]
        log_sig = jnp.minimum(fr, 0.0) - jnp.log(1.0 + jnp.exp(-jnp.abs(fr)))
        a = jnp.log(lb)
        y = jnp.log1p(-lb) + log_sig
        log_f = jnp.maximum(a, y) + jnp.log(1.0 + jnp.exp(-jnp.abs(a - y)))
        k = (1.0 - lb) * _sigmoid(-fr)
        q = _silu(q_ref[:, sl])
        v = i_ref[:, sl]
        b = _dot_01(tril, log_f)
        yield

        st = st_ref[h]
        o = _dot_nt(q * jnp.exp(b), st)
        blast = b[ROWS - 1:ROWS, :]
        st_ref[h] = st * jnp.exp(blast) + _dot_tn(v, k * jnp.exp(blast - b))
        yield

        pmat = jnp.zeros((ROWS, ROWS), F32)
        b3 = b.reshape(ROWS // SUBLANES, SUBLANES, dk)
        m = 1
        while m < ROWS:
            if m < SUBLANES:
                ref = None
                for p0 in range(0, SUBLANES, 2 * m):
                    cand = jnp.broadcast_to(b3[:, p0 + m - 1:p0 + m, :], b3.shape).reshape(ROWS, dk)
                    ref = cand if ref is None else jnp.where(rmod >= p0, cand, ref)
            else:
                ref = jnp.concatenate(
                    [jnp.broadcast_to(b[p0 + m - 1:p0 + m, :], (2 * m, dk)) for p0 in range(0, ROWS, 2 * m)],
                    axis=0)
            li = m.bit_length() - 1
            z = jnp.where(mask_ref[0, li] != 0.0, q, k) * jnp.exp2((b - ref) * mask_ref[1, li])
            zb = z.astype(BF16)
            pmat = pmat + mask_ref[2, li] * _dot_nt(zb, zb)
            m *= 2
            yield
        o = o + _dot(pmat, v) + jnp.sum(q * k, axis=-1, keepdims=True) * v
        yield

        o = o * lax.rsqrt(jnp.mean(o * o, axis=-1, keepdims=True) + 1e-6)
        y_ref[:, MIX_HG + h * dk:MIX_HG + (h + 1) * dk] = (
            o * nw_ref[...] * _silu(gate_ref[:, sl])).astype(y_ref.dtype)
        yield


def _in_proj_stages(hb_ref, w_ref, out_ref):
    hb = hb_ref[...]
    for c0 in range(0, PROJ_COLS, PROJ_SLAB):
        out_ref[:, c0:c0 + PROJ_SLAB] = jnp.dot(hb, w_ref[:, c0:c0 + PROJ_SLAB], preferred_element_type=F32)
        yield


def _mixer_kernel(hb0_ref, hbn_ref, w_ref,
                  scw_ref, scb_ref, shp_ref, shpt_ref, dsk_ref, snw_ref,
                  gcw_ref, ghp_ref, ghpt_ref, gnw_ref, lb_ref, hnw_ref, sexp_ref,
                  y_ref, proj_ref, sxpad_ref, sxa_ref, sst_ref, gxpad_ref, gxa_ref, gst_ref, hst_ref, hmask_ref):
    c = pl.program_id(1)

    @pl.when(c == 0)
    def _():
        sst_ref[...] = jnp.zeros(sst_ref.shape, F32)
        gst_ref[...] = jnp.zeros(gst_ref.shape, F32)
        hst_ref[...] = jnp.zeros(hst_ref.shape, F32)
        for pad_ref in (sxpad_ref, gxpad_ref):
            taps, _, cols = pad_ref.shape
            pad_ref[:, SUBLANES + ROWS:2 * SUBLANES + ROWS, :] = jnp.zeros((taps, SUBLANES, cols), F32)
        _hgrn_level_masks(hmask_ref)
        for _ in _in_proj_stages(hb0_ref, w_ref, proj_ref.at[0]):
            pass

    cur = proj_ref.at[c % 2]
    nxt = proj_ref.at[(c + 1) % 2]

    def seg(off, width):
        return cur.at[:, off:off + width]

    sm_ref = seg(OFF_SMALL, LANES)
    streams = [
        _gdn_stages(seg(OFF_QKV, 3 * GDN_WIDTH), seg(OFF_GGATE, GDN_WIDTH), sm_ref, gcw_ref, ghp_ref, ghpt_ref,
                    gnw_ref, y_ref, gxpad_ref, gxa_ref, gst_ref),
        _ssd_stages(seg(OFF_Z, SSD_WIDTH), seg(OFF_XBC, SSD_XBC), sm_ref, scw_ref, scb_ref, shp_ref, shpt_ref,
                    dsk_ref, snw_ref, sexp_ref, y_ref, sxpad_ref, sxa_ref, sst_ref),
        _hgrn_stages(seg(OFF_HQ, HG_WIDTH), seg(OFF_HF, HG_WIDTH), seg(OFF_HI, HG_WIDTH),
                     seg(OFF_HGATE, HG_WIDTH), lb_ref, hnw_ref, y_ref, hst_ref, hmask_ref),
        _in_proj_stages(hbn_ref, w_ref, nxt),
    ]
    live = [True] * len(streams)
    rnd = 0
    while any(live):
        advance = [1, 1, 1 if rnd < HG_SLOW_ROUNDS else 2, 1]
        rnd += 1
        for s, stream in enumerate(streams):
            for _ in range(advance[s]):
                if live[s]:
                    try:
                        next(stream)
                    except StopIteration:
                        live[s] = False


def _mixers(hb, w_proj, layer, nb, seq, ssd_conv_w, ssd_conv_b, ssd_dt_bias, ssd_a_log, ssd_d, ssd_norm_w,
            gdn_conv_w, gdn_dt_bias, gdn_a_log, gdn_norm_w, lb, hg_norm_w):
    nblk = seq // ROWS
    shp = jnp.zeros((SUBLANES, LANES), F32)
    shp = shp.at[0, SM_DT:SM_DT + SSD_HEADS].set(ssd_dt_bias).at[1, SM_DT:SM_DT + SSD_HEADS].set(ssd_a_log)
    ghp = jnp.zeros((SUBLANES, LANES), F32)
    ghp = ghp.at[0, SM_A:SM_A + GDN_HEADS].set(gdn_dt_bias).at[1, SM_A:SM_A + GDN_HEADS].set(gdn_a_log)
    dsk = jnp.repeat(ssd_d, SSD_HEAD_DIM)[None, :]
    gw = SSD_WIDTH // SSD_GROUPS
    head_of = jnp.arange(SSD_GROUPS)[:, None, None] * (SSD_HEADS // SSD_GROUPS) + jnp.arange(gw)[None, None, :] // SSD_HEAD_DIM
    expand = (head_of == jnp.arange(LANES)[None, :, None]).astype(BF16)
    const = lambda b, c: (0, 0)

    def whole(shape):
        return pl.BlockSpec(shape, const)

    return pl.pallas_call(
        _mixer_kernel,
        out_shape=jax.ShapeDtypeStruct((nb * seq, D_MIX), BF16),
        grid=(nb, nblk),
        in_specs=[
            pl.BlockSpec((ROWS, D_MODEL), lambda b, c: (b * nblk + c, 0)),
            pl.BlockSpec((ROWS, D_MODEL), lambda b, c: (b * nblk + jnp.minimum(c + 1, nblk - 1), 0)),
            pl.BlockSpec((None, D_MODEL, PROJ_COLS), lambda b, c: (layer, 0, 0), pipeline_mode=pl.Buffered(1)),
            whole((SSD_CONV, SSD_XBC)), whole((1, SSD_XBC)), whole((SUBLANES, LANES)), whole((LANES, SUBLANES)),
            whole((1, SSD_WIDTH)), whole((1, SSD_WIDTH)),
            whole((GDN_CONV, 3 * GDN_WIDTH)), whole((SUBLANES, LANES)), whole((LANES, SUBLANES)),
            whole((1, GDN_HEAD_DIM)), whole((1, HG_WIDTH)), whole((1, HG_HEAD_DIM)),
            pl.BlockSpec(expand.shape, lambda b, c: (0, 0, 0)),
        ],
        out_specs=pl.BlockSpec((ROWS, D_MIX), lambda b, c: (b * nblk + c, 0)),
        scratch_shapes=[pltpu.VMEM((2, ROWS, PROJ_COLS), F32),
                        pltpu.VMEM((SSD_CONV - 1, ROWS + 2 * SUBLANES, SSD_XBC), F32),
                        pltpu.VMEM((ROWS, SSD_XBC), F32),
                        pltpu.VMEM((SSD_GROUPS, SSD_STATE, SSD_WIDTH // SSD_GROUPS), F32),
                        pltpu.VMEM((GDN_CONV - 1, ROWS + 2 * SUBLANES, 3 * GDN_WIDTH), F32),
                        pltpu.VMEM((ROWS, 3 * GDN_WIDTH), F32),
                        pltpu.VMEM((GDN_HEADS, GDN_HEAD_DIM, GDN_HEAD_DIM), F32),
                        pltpu.VMEM((HG_HEADS, HG_HEAD_DIM, HG_HEAD_DIM), F32),
                        pltpu.VMEM((3, HG_LEVELS, ROWS, ROWS), F32)],
        compiler_params=_params(("parallel", "arbitrary")),
        name="in_proj_mixers",
    )(hb, hb, w_proj,
      ssd_conv_w, ssd_conv_b[None, :], shp, shp.T, dsk, ssd_norm_w[None, :],
      gdn_conv_w, ghp, ghp.T, gdn_norm_w[None, :], lb[None, :], hg_norm_w[None, :], expand)


def _layer_norm(x, g, b):
    mu = jnp.mean(x, axis=-1, keepdims=True)
    xc = x - mu
    var = jnp.mean(xc * xc, axis=-1, keepdims=True)
    return xc * lax.rsqrt(var + 1e-5) * g + b


ROUTER_ROWS = 32
OUT_SLAB = 128


def _outproj_rows(mix, h, g_ref, b_ref, wr_ref, br_ref):
    h1 = _layer_norm(DN_ALPHA * h + mix, g_ref[...], b_ref[...])

    h_hi, h_lo = _split(h1)
    wr = wr_ref[...]
    lt = _dot_nt(wr, h_hi) + _dot_nt(wr, h_lo)
    lt = lt[0:ROUTER_ROWS, :] + lt[ROUTER_ROWS:2 * ROUTER_ROWS, :] + br_ref[...]
    glog = [lt[g:g + 1, :] for g in range(MOE_GROUPS)]
    elog = [lt[MOE_GROUPS + e:MOE_GROUPS + e + 1, :] for e in range(N_EXPERTS)]

    def softmax(rows):
        m = rows[0]
        for r in rows[1:]:
            m = jnp.maximum(m, r)
        ex = [jnp.exp(r - m) for r in rows]
        tot = ex[0]
        for r in ex[1:]:
            tot = tot + r
        return [r / tot for r in ex]

    def top1(rows):
        best = rows[0]
        for r in rows[1:]:
            best = jnp.maximum(best, r)
        idx = jnp.full_like(best, float(len(rows) - 1))
        for j in range(len(rows) - 2, -1, -1):
            idx = jnp.where(rows[j] == best, float(j), idx)
        return best, idx

    g_p, g_idx = top1(softmax(glog))
    chosen = []
    for k in range(EXPERTS_PER_GROUP):
        acc = jnp.zeros_like(g_p)
        for g in range(MOE_GROUPS):
            acc = jnp.where(g_idx == float(g), elog[g * EXPERTS_PER_GROUP + k], acc)
        chosen.append(acc)
    eprob = softmax(chosen)
    p1, i1 = top1(eprob)
    p2, i2 = top1([jnp.where(i1 == float(k), -1.0, eprob[k]) for k in range(EXPERTS_PER_GROUP)])
    denom = p1 + p2
    w1 = g_p * p1 / denom
    w2 = g_p * p2 / denom
    wk = [jnp.where(i1 == float(k), w1, 0.0) + jnp.where(i2 == float(k), w2, 0.0)
          for k in range(EXPERTS_PER_GROUP)]
    rid = _iota2((ROUTER_ROWS, lt.shape[1]), 0)
    comb_t = jnp.where(rid == N_EXPERTS, g_idx, 0.0)
    for g in range(MOE_GROUPS):
        for k in range(EXPERTS_PER_GROUP):
            comb_t = jnp.where(rid == g * EXPERTS_PER_GROUP + k,
                               jnp.where(g_idx == float(g), wk[k], 0.0), comb_t)
    comb_t = jnp.concatenate([comb_t, jnp.zeros((LANES - ROUTER_ROWS, lt.shape[1]), F32)], axis=0)
    return h1, comb_t.T


def _outproj_kernel(y_ref, h_ref, w_ref, g_ref, b_ref, wr_ref, br_ref, h1_ref, h1b_ref, comb_ref):
    slab = min(OUT_SLAB, y_ref.shape[0])
    slabs = [slice(r0, r0 + slab) for r0 in range(0, y_ref.shape[0], slab)]
    mixes = [jnp.dot(y_ref[rows, :], w_ref[...], preferred_element_type=F32) for rows in slabs]
    for rows, mix in zip(slabs, mixes):
        h1, comb = _outproj_rows(mix, h_ref[rows, :], g_ref, b_ref, wr_ref, br_ref)
        h1_ref[rows, :] = h1
        h1b_ref[rows, :] = h1.astype(BF16)
        comb_ref[rows, :] = comb


def _router_params(w_group, b_group, w_expert, b_expert):
    pad = ROUTER_ROWS - MOE_GROUPS - N_EXPERTS
    w_t = jnp.pad(jnp.concatenate([w_group, w_expert], axis=1).T, ((0, pad), (0, 0)))
    b_col = jnp.pad(jnp.concatenate([b_group, b_expert]), (0, pad))[:, None]
    return jnp.concatenate(_split(w_t), axis=0), b_col


def _outproj(y, h, w_out, layer, ln_g, ln_b, w_router, b_router, tm):
    t = h.shape[0]
    rowmap = lambda i: (i, 0)
    const = lambda i: (0, 0)
    return pl.pallas_call(
        _outproj_kernel,
        out_shape=(jax.ShapeDtypeStruct((t, D_MODEL), F32),
                   jax.ShapeDtypeStruct((t, D_MODEL), BF16),
                   jax.ShapeDtypeStruct((t, LANES), F32)),
        grid=(t // tm,),
        in_specs=[pl.BlockSpec((tm, D_MIX), rowmap),
                  pl.BlockSpec((tm, D_MODEL), rowmap),
                  pl.BlockSpec((None, D_MIX, D_MODEL), lambda i: (layer, 0, 0)),
                  pl.BlockSpec((1, D_MODEL), const),
                  pl.BlockSpec((1, D_MODEL), const),
                  pl.BlockSpec((2 * ROUTER_ROWS, D_MODEL), const),
                  pl.BlockSpec((ROUTER_ROWS, 1), const)],
        out_specs=(pl.BlockSpec((tm, D_MODEL), rowmap),
                   pl.BlockSpec((tm, D_MODEL), rowmap),
                   pl.BlockSpec((tm, LANES), rowmap)),
        compiler_params=_params(("parallel",)),
        name="out_proj_ln_router",
    )(y, h, w_out, ln_g[None, :], ln_b[None, :], w_router, b_router)


def _prefix_lanes(v, idx):
    axis = 1 if v.shape[0] == 1 else 0
    out = jnp.zeros_like(v)
    for g in range(MOE_GROUPS - 1):
        vg = v[:, g:g + 1] if axis == 1 else v[g:g + 1, :]
        out = out + jnp.where(idx > g, vg, 0.0)
    return out


def _moe_kernel(hb_ref, h_ref, comb_ref, wgu_ref, wdn_ref, g_ref, b_ref, o_ref, ob_ref,
                xs_ref, cw_ref, ys_ref):
    tm = hb_ref.shape[0]
    comb = comb_ref[...]
    lane = _iota2((tm, LANES), 1).astype(F32)
    gid = comb[:, N_EXPERTS:N_EXPERTS + 1]
    gsel = (lane == gid).astype(F32)
    tr = _iota2((tm, tm), 0)
    tc = _iota2((tm, tm), 1)

    ns = xs_ref.shape[0] - MOE_CHUNK

    def aligned(cnt):
        return jnp.floor((cnt + (MOE_ALIGN - 1)) * (1.0 / MOE_ALIGN)) * MOE_ALIGN

    rank_c = _dot((tc < tr).astype(F32), gsel)
    cnt_r = jnp.sum(gsel, axis=0, keepdims=True)
    start_r = _prefix_lanes(aligned(cnt_r), _iota2((1, LANES), 1))
    dest_c = jnp.sum(gsel * (start_r + rank_c), axis=1, keepdims=True)
    eye8 = (_iota2((SUBLANES, LANES), 0) == _iota2((SUBLANES, LANES), 1)).astype(F32)
    gsel_t = _dot_nt(eye8, gsel)
    rank_r = _dot(gsel_t, (tr < tc).astype(F32))
    cnt_c = jnp.sum(gsel_t, axis=1, keepdims=True)
    start_c = _prefix_lanes(aligned(cnt_c), _iota2((SUBLANES, 1), 0))
    dest_r = jnp.sum(gsel_t * (start_c + rank_r), axis=0, keepdims=True)

    perm = (dest_r == _iota2((ns, tm), 0).astype(F32)).astype(BF16)
    xs_ref[0:ns, :] = jnp.dot(perm, hb_ref[...], preferred_element_type=F32).astype(BF16)
    xs_ref[ns:ns + MOE_CHUNK, :] = jnp.zeros((MOE_CHUNK, D_MODEL), BF16)
    c_hi, c_lo = _split(comb)
    cw2 = jnp.dot(perm, jnp.concatenate([c_hi, c_lo], axis=1), preferred_element_type=F32)
    cw_ref[0:ns, :] = cw2[:, 0:LANES] + cw2[:, LANES:2 * LANES]
    cw_ref[ns:ns + MOE_CHUNK, :] = jnp.zeros((MOE_CHUNK, LANES), F32)
    ys_ref[...] = jnp.zeros(ys_ref.shape, F32)

    lane1 = _iota2((1, LANES), 1)
    starts = [jnp.sum(jnp.where(lane1 == g, start_r, 0.0)).astype(jnp.int32) for g in range(MOE_GROUPS)]
    cnts = [jnp.sum(jnp.where(lane1 == g, cnt_r, 0.0)) for g in range(MOE_GROUPS)]
    def block(g, j):
        rows = pl.ds(pl.multiple_of(starts[g] + j * MOE_CHUNK, MOE_ALIGN), MOE_CHUNK)
        x = xs_ref[rows, :]
        cw = cw_ref[rows, :]
        hm = []
        for e in range(EXPERTS_PER_GROUP):
            ex = g * EXPERTS_PER_GROUP + e
            gu = jnp.dot(x, wgu_ref[ex], preferred_element_type=F32)
            hm.append((_silu(gu[:, 0:D_EXPERT]) * gu[:, D_EXPERT:2 * D_EXPERT] * cw[:, ex:ex + 1]).astype(BF16))
        ys_ref[rows, :] += jnp.dot(jnp.concatenate(hm, axis=1), wdn_ref[g], preferred_element_type=F32)

    for g in range(MOE_GROUPS):
        block(g, 0)

    most = cnts[0]
    for g in range(1, MOE_GROUPS):
        most = jnp.maximum(most, cnts[g])

    @pl.when(most > MOE_CHUNK)
    def _():
        for j in range(1, -(-tm // MOE_CHUNK)):
            for g in range(MOE_GROUPS):
                pl.when(cnts[g] > j * MOE_CHUNK)(lambda g=g, j=j: block(g, j))

    unperm = (dest_c == _iota2((tm, ns), 1).astype(F32)).astype(BF16)
    y = jnp.dot(unperm, ys_ref[0:ns, :].astype(BF16), preferred_element_type=F32)
    h2 = _layer_norm(DN_ALPHA * h_ref[...] + y, g_ref[...], b_ref[...])
    o_ref[...] = h2
    ob_ref[...] = h2.astype(BF16)


def _moe(hb, h, comb, w_gu, w_dn, layer, ln_g, ln_b, tm):
    t = h.shape[0]
    rowmap = lambda i: (i, 0)
    const = lambda i: (0, 0)
    const3 = lambda i: (layer, 0, 0, 0)
    sorted_rows = -(-(tm + MOE_GROUPS * MOE_ALIGN) // LANES) * LANES + MOE_CHUNK
    return pl.pallas_call(
        _moe_kernel,
        out_shape=(jax.ShapeDtypeStruct((t, D_MODEL), F32),
                   jax.ShapeDtypeStruct((t, D_MODEL), BF16)),
        grid=(t // tm,),
        in_specs=[pl.BlockSpec((tm, D_MODEL), rowmap),
                  pl.BlockSpec((tm, D_MODEL), rowmap),
                  pl.BlockSpec((tm, LANES), rowmap),
                  pl.BlockSpec((None, N_EXPERTS, D_MODEL, 2 * D_EXPERT), const3, pipeline_mode=pl.Buffered(1)),
                  pl.BlockSpec((None, MOE_GROUPS, EXPERTS_PER_GROUP * D_EXPERT, D_MODEL), const3,
                               pipeline_mode=pl.Buffered(1)),
                  pl.BlockSpec((1, D_MODEL), const),
                  pl.BlockSpec((1, D_MODEL), const)],
        out_specs=(pl.BlockSpec((tm, D_MODEL), rowmap),
                   pl.BlockSpec((tm, D_MODEL), rowmap)),
        scratch_shapes=[pltpu.VMEM((sorted_rows, D_MODEL), BF16),
                        pltpu.VMEM((sorted_rows, LANES), F32),
                        pltpu.VMEM((sorted_rows, D_MODEL), F32)],
        compiler_params=pltpu.CompilerParams(dimension_semantics=("parallel",),
                                             vmem_limit_bytes=MOE_VMEM_LIMIT),
        name="moe_ln",
    )(hb, h, comb, w_gu, w_dn, ln_g[None, :], ln_b[None, :])


W_IN_SEGMENTS = ((SSD_WIDTH, OFF_Z), (SSD_XBC, OFF_XBC), (SSD_HEADS, OFF_SMALL + SM_DT),
                 (3 * GDN_WIDTH, OFF_QKV), (GDN_WIDTH, OFF_GGATE), (GDN_HEADS, OFF_SMALL + SM_B),
                 (GDN_HEADS, OFF_SMALL + SM_A), (HG_WIDTH, OFF_HQ), (HG_WIDTH, OFF_HF),
                 (HG_WIDTH, OFF_HI), (HG_WIDTH, OFF_HGATE))
W_T_BLOCK = 256
W_MAIN_BLOCKS = OFF_SMALL // W_T_BLOCK


def _w_in_kernel(tbl_ref, w_ref, small_ref, o_ref):
    j = pl.program_id(1)

    @pl.when(j < W_MAIN_BLOCKS)
    def _():
        o_ref[...] = w_ref[0].T.astype(BF16)

    @pl.when(j >= W_MAIN_BLOCKS)
    def _():
        o_ref[...] = small_ref[...].T.astype(BF16)


def _split_w_in(w_in):
    depth = w_in.shape[0]
    w_t = jnp.swapaxes(w_in, 1, 2)
    src_of, small_rows, src = {}, [], 0
    for width, dst in W_IN_SEGMENTS:
        if width >= W_T_BLOCK:
            for off in range(0, width, W_T_BLOCK):
                src_of[(dst + off) // W_T_BLOCK] = src + off
        else:
            small_rows.append(w_t[:, src:src + width, :])
        src += width
    table = jnp.array([src_of.get(j, 0) for j in range(PROJ_COLS // W_T_BLOCK)], jnp.int32)
    small = jnp.concatenate(small_rows, axis=1)
    small = jnp.pad(small, ((0, 0), (0, PROJ_COLS - OFF_SMALL - small.shape[1]), (0, 0)))
    return pl.pallas_call(
        _w_in_kernel,
        out_shape=jax.ShapeDtypeStruct((depth, D_MODEL, PROJ_COLS), BF16),
        grid_spec=pltpu.PrefetchScalarGridSpec(
            num_scalar_prefetch=1,
            grid=(depth, PROJ_COLS // W_T_BLOCK),
            in_specs=[pl.BlockSpec((pl.Element(1), pl.Element(W_T_BLOCK), pl.Element(D_MODEL)),
                                   lambda l, j, tbl: (l, pl.multiple_of(tbl[j], SUBLANES), 0)),
                      pl.BlockSpec((None, PROJ_COLS - OFF_SMALL, D_MODEL), lambda l, j, tbl: (l, 0, 0))],
            out_specs=pl.BlockSpec((None, D_MODEL, W_T_BLOCK), lambda l, j, tbl: (l, 0, j))),
        compiler_params=_params(("parallel", "arbitrary")),
        name="w_in_relayout",
    )(table, w_t, small)


def kernel(x, w_in, ssd_conv_w, ssd_conv_b, ssd_dt_bias, ssd_a_log, ssd_d, ssd_norm_w, gdn_conv_w, gdn_dt_bias, gdn_a_log, gdn_norm_w, hg_lb_logits, hg_norm_w, w_out, ln1_g, ln1_b, w_router_group, b_router_group, w_router_expert, b_router_expert, w_expert_gate_up, w_expert_down, ln2_g, ln2_b):
    nb, seq, d = x.shape
    t = nb * seq
    lb_cum = jnp.cumsum(jax.nn.softmax(hg_lb_logits.astype(F32), axis=0), axis=0)
    lb_all = lb_cum - lb_cum[0:1]
    h = x.reshape(t, d)
    hb = h.astype(BF16)
    w_proj = _split_w_in(w_in)
    w_out_b = w_out.astype(BF16)
    w_gu_b = w_expert_gate_up.astype(BF16)
    w_dn_b = w_expert_down.astype(BF16).reshape(DEPTH, MOE_GROUPS, EXPERTS_PER_GROUP * D_EXPERT, D_MODEL)
    for l in range(DEPTH):
        y = _mixers(hb, w_proj, l, nb, seq, ssd_conv_w[l], ssd_conv_b[l], ssd_dt_bias[l], ssd_a_log[l], ssd_d[l],
                    ssd_norm_w[l], gdn_conv_w[l], gdn_dt_bias[l], gdn_a_log[l], gdn_norm_w[l],
                    lb_all[l], hg_norm_w[l])
        w_router, b_router = _router_params(w_router_group[l], b_router_group[l],
                                            w_router_expert[l], b_router_expert[l])
        h1, h1b, comb = _outproj(y, h, w_out_b, l, ln1_g[l], ln1_b[l], w_router, b_router, 1024)
        h, hb = _moe(h1b, h1, comb, w_gu_b, w_dn_b, l, ln2_g[l], ln2_b[l], 512)
    return h.reshape(nb, seq, d)
```
